```python
import jax, jax.numpy as jnp
from jax import lax
import numpy as np

D_MODEL = 1024
BATCH = 2
SEQ = 8192
DEPTH = 2
DEC_BATCH = 128
DEC_SEQ = 1
PAST_LEN = 8192
PAGE_SIZE = 128

N_META = 16
D_CONV = D_MODEL
CONV_W = 3
N_HEADS = 16
N_KV_HEADS = 2
HEAD_DIM = 64
GQA = N_HEADS // N_KV_HEADS
ROT_DIM = HEAD_DIM // 4
ROPE_THETA = 500000.0
WINDOW = 128
BLOCK = 128
N_GROUPS = 4
EXPERTS_PER_GROUP = 4
N_EXPERTS = N_GROUPS * EXPERTS_PER_GROUP
TOP_K = 2
D_EXPERT = 512
EPS = 1e-6
NEG = -1e30
D_Q = N_HEADS * HEAD_DIM
D_KV = N_KV_HEADS * HEAD_DIM
D_IN = 3 * D_CONV + D_Q + 2 * D_KV + 2 * D_MODEL

kernel_name = 'hybrid_conv_swa_sink_hmoe_step'


def rms_norm(x, g):
    xf = x.astype(jnp.float32)
    y = xf * lax.rsqrt(jnp.mean(xf * xf, axis=-1, keepdims=True) + EPS)
    return (y * g.astype(jnp.float32)).astype(x.dtype)


def partial_rope(x, pos):
    half = ROT_DIM // 2
    inv_freq = jnp.float32(ROPE_THETA) ** (-jnp.arange(half, dtype=jnp.float32) * (2.0 / ROT_DIM))
    ang = pos.astype(jnp.float32)[:, None] * inv_freq[None, :]
    cos = jnp.cos(ang)[:, None, :]
    sin = jnp.sin(ang)[:, None, :]
    xr = x[..., :ROT_DIM].astype(jnp.float32)
    x1, x2 = xr[..., :half], xr[..., half:]
    rot = jnp.concatenate([x1 * cos - x2 * sin, x2 * cos + x1 * sin], axis=-1)
    return jnp.concatenate([rot.astype(x.dtype), x[..., ROT_DIM:]], axis=-1)


def project_in(h, w_in):
    sizes = (D_CONV, D_CONV, D_CONV, D_Q, D_KV, D_KV, D_MODEL, D_MODEL)
    bounds = np.cumsum(sizes)[:-1].tolist()
    return jnp.split(h @ w_in, bounds, axis=-1)


def qkv_heads(q, k, v, pos, q_norm, k_norm):
    b, s = q.shape[:2]
    q = partial_rope(rms_norm(q.reshape(b, s, N_HEADS, HEAD_DIM), q_norm), pos)
    k = partial_rope(rms_norm(k.reshape(b, s, N_KV_HEADS, HEAD_DIM), k_norm), pos)
    v = v.reshape(b, s, N_KV_HEADS, HEAD_DIM)
    return q, k, v


def sink_attend(s, valid, sinks, v, eq):
    s = jnp.where(valid, s, NEG)
    sk = sinks.astype(jnp.float32).reshape(N_KV_HEADS, GQA)[:, :, None, None]
    m = jnp.maximum(jnp.max(s, axis=-1, keepdims=True), sk)
    p = jnp.exp(s - m)
    den = jnp.sum(p, axis=-1, keepdims=True) + jnp.exp(sk - m)
    return jnp.einsum(eq, (p / den).astype(v.dtype), v)


def banded_window_attention(q, k, v, sinks):
    b, L = q.shape[:2]
    pad = (-N_META) % BLOCK
    lp = L + pad
    nb = lp // BLOCK
    fpad = lambda t: jnp.pad(t, ((0, 0), (pad, 0)) + ((0, 0),) * (t.ndim - 2))
    qb = fpad(q).reshape(b, nb, BLOCK, N_KV_HEADS, GQA, HEAD_DIM)
    kb = fpad(k).reshape(b, nb, BLOCK, N_KV_HEADS, HEAD_DIM)
    vb = fpad(v).reshape(b, nb, BLOCK, N_KV_HEADS, HEAD_DIM)
    with_prev = lambda t: jnp.concatenate(
        [jnp.pad(t[:, :-1], ((0, 0), (1, 0), (0, 0), (0, 0), (0, 0))), t], axis=2)
    kw, vw = with_prev(kb), with_prev(vb)
    s = jnp.einsum('bnqhgd,bnkhd->bnhgqk', qb, kw).astype(jnp.float32) * (HEAD_DIM ** -0.5)
    blk = jnp.arange(nb, dtype=jnp.int32)[:, None] * BLOCK
    qpos = blk + jnp.arange(BLOCK, dtype=jnp.int32)[None, :] - pad
    kpos = blk + jnp.arange(2 * BLOCK, dtype=jnp.int32)[None, :] - BLOCK - pad
    diff = qpos[:, :, None] - kpos[:, None, :]
    valid = (kpos[:, None, :] >= 0) & (diff >= 0) & (diff < WINDOW)
    o = sink_attend(s, valid[None, :, None, None], sinks, vw, 'bnhgqk,bnkhd->bnqhgd')
    return o.reshape(b, lp, D_Q)[:, pad:]


def merge_branches(conv_y, attn_o, g_conv, g_attn, w_conv_out, w_attn_out, w_o):
    ya = conv_y @ w_conv_out
    yb = attn_o @ w_attn_out
    return (jax.nn.sigmoid(g_conv) * ya + jax.nn.sigmoid(g_attn) * yb) @ w_o


def hier_moe(x, w_rg, b_rg, w_re, b_re, w_g, w_u, w_d):
    shape = x.shape
    xt = x.reshape(-1, D_MODEL)
    t = xt.shape[0]
    g_logits = (xt @ w_rg).astype(jnp.float32) + b_rg.astype(jnp.float32)
    grp = jnp.argmax(g_logits, axis=-1)
    p_grp = jnp.take_along_axis(jax.nn.softmax(g_logits, axis=-1), grp[:, None], axis=-1)
    e_logits = ((xt @ w_re).astype(jnp.float32) + b_re.astype(jnp.float32)).reshape(t, N_GROUPS, EXPERTS_PER_GROUP)
    e_in_grp = jnp.take_along_axis(e_logits, grp[:, None, None], axis=1)[:, 0]
    top_v, top_i = lax.top_k(e_in_grp, TOP_K)
    wts = jax.nn.softmax(top_v, axis=-1) * p_grp
    eidx = grp[:, None] * EXPERTS_PER_GROUP + top_i
    gate = jnp.einsum('tk,tke->te', wts, jax.nn.one_hot(eidx, N_EXPERTS, dtype=jnp.float32))
    a = jnp.einsum('td,edf->tef', xt, w_g)
    u = jnp.einsum('td,edf->tef', xt, w_u)
    hdn = jax.nn.silu(a) * u * gate.astype(xt.dtype)[:, :, None]
    return jnp.einsum('tef,efd->td', hdn, w_d).reshape(shape)


def prompt_layer(x, lw):
    (norm_mix, w_in, conv_w, q_norm, k_norm, sinks, w_conv_out, w_attn_out, w_o,
     norm_ffn, w_rg, b_rg, w_re, b_re, w_g, w_u, w_d) = lw
    L = x.shape[1]
    pos = jnp.arange(L, dtype=jnp.int32)
    h = rms_norm(x, norm_mix)
    gate_b, gate_c, hc, q, k, v, g_conv, g_attn = project_in(h, w_in)
    u = gate_c * hc
    up = jnp.pad(u, ((0, 0), (CONV_W - 1, 0), (0, 0)))
    conv_y = sum(up[:, j:j + L] * conv_w[j] for j in range(CONV_W))
    q, k, v = qkv_heads(q, k, v, pos, q_norm, k_norm)
    attn_o = banded_window_attention(q, k, v, sinks)
    x = x + merge_branches(gate_b * conv_y, attn_o, g_conv, g_attn, w_conv_out, w_attn_out, w_o)
    x = x + hier_moe(rms_norm(x, norm_ffn), w_rg, b_rg, w_re, b_re, w_g, w_u, w_d)
    return x, k[:, L - WINDOW:], v[:, L - WINDOW:], u[:, L - (CONV_W - 1):]


def sample_layer(x, ck, cv, cconv, lw):
    (norm_mix, w_in, conv_w, q_norm, k_norm, sinks, w_conv_out, w_attn_out, w_o,
     norm_ffn, w_rg, b_rg, w_re, b_re, w_g, w_u, w_d) = lw
    bd, s = x.shape[:2]
    pos = PAST_LEN + jnp.arange(s, dtype=jnp.int32)
    h = rms_norm(x, norm_mix)
    gate_b, gate_c, hc, q, k, v, g_conv, g_attn = project_in(h, w_in)
    u = gate_c * hc
    ext = jnp.concatenate([cconv.astype(u.dtype), u], axis=1)
    conv_y = sum(ext[:, j:j + s] * conv_w[j] for j in range(CONV_W))
    q, k, v = qkv_heads(q, k, v, pos, q_norm, k_norm)
    kk = jnp.concatenate([ck.astype(k.dtype), k], axis=1)
    vv = jnp.concatenate([cv.astype(v.dtype), v], axis=1)
    qg = q.reshape(bd, s, N_KV_HEADS, GQA, HEAD_DIM)
    sc = jnp.einsum('bqhgd,bkhd->bhgqk', qg, kk).astype(jnp.float32) * (HEAD_DIM ** -0.5)
    kpos = PAST_LEN - WINDOW + jnp.arange(WINDOW + s, dtype=jnp.int32)
    diff = pos[:, None] - kpos[None, :]
    valid = (kpos[None, :] >= 0) & (diff >= 0) & (diff < WINDOW)
    attn_o = sink_attend(sc, valid, sinks, vv, 'bhgqk,bkhd->bqhgd').reshape(bd, s, D_Q)
    x = x + merge_branches(gate_b * conv_y, attn_o, g_conv, g_attn, w_conv_out, w_attn_out, w_o)
    x = x + hier_moe(rms_norm(x, norm_ffn), w_rg, b_rg, w_re, b_re, w_g, w_u, w_d)
    n = kk.shape[1]
    return x, kk[:, n - WINDOW:], vv[:, n - WINDOW:], ext[:, ext.shape[1] - (CONV_W - 1):]


def setup_inputs(seed: int = 0) -> dict:
    key = jax.random.key(seed)
    ks = jax.random.split(key, 24)
    nrm = lambda k, shape, scale: jax.random.normal(k, shape, jnp.float32) * scale
    return {
        'x_prompt': nrm(ks[0], (BATCH, SEQ, D_MODEL), 1.0),
        'x_sample': nrm(ks[1], (DEC_BATCH, DEC_SEQ, D_MODEL), 1.0),
        'cache_k': nrm(ks[2], (DEPTH, DEC_BATCH, WINDOW, N_KV_HEADS, HEAD_DIM), 1.0),
        'cache_v': nrm(ks[3], (DEPTH, DEC_BATCH, WINDOW, N_KV_HEADS, HEAD_DIM), 1.0),
        'state_conv': nrm(ks[4], (DEPTH, DEC_BATCH, CONV_W - 1, D_CONV), 1.0),
        'meta_tokens': nrm(ks[5], (N_META, D_MODEL), 1.0),
        'norm_mix': 1.0 + nrm(ks[6], (DEPTH, D_MODEL), 0.02),
        'w_in': nrm(ks[7], (DEPTH, D_MODEL, D_IN), D_MODEL ** -0.5),
        'conv_w': nrm(ks[8], (DEPTH, CONV_W, D_CONV), CONV_W ** -0.5),
        'q_norm': 1.0 + nrm(ks[9], (DEPTH, HEAD_DIM), 0.02),
        'k_norm': 1.0 + nrm(ks[10], (DEPTH, HEAD_DIM), 0.02),
        'attn_sinks': nrm(ks[11], (DEPTH, N_HEADS), 0.5),
        'w_conv_out': nrm(ks[12], (DEPTH, D_CONV, D_MODEL), D_CONV ** -0.5),
        'w_attn_out': nrm(ks[13], (DEPTH, D_Q, D_MODEL), D_Q ** -0.5),
        'w_o': nrm(ks[14], (DEPTH, D_MODEL, D_MODEL), D_MODEL ** -0.5),
        'norm_ffn': 1.0 + nrm(ks[15], (DEPTH, D_MODEL), 0.02),
        'w_router_group': nrm(ks[16], (DEPTH, D_MODEL, N_GROUPS), D_MODEL ** -0.5),
        'b_router_group': nrm(ks[17], (DEPTH, N_GROUPS), 0.01),
        'w_router_expert': nrm(ks[18], (DEPTH, D_MODEL, N_EXPERTS), D_MODEL ** -0.5),
        'b_router_expert': nrm(ks[19], (DEPTH, N_EXPERTS), 0.01),
        'w_exp_gate': nrm(ks[20], (DEPTH, N_EXPERTS, D_MODEL, D_EXPERT), D_MODEL ** -0.5),
        'w_exp_up': nrm(ks[21], (DEPTH, N_EXPERTS, D_MODEL, D_EXPERT), D_MODEL ** -0.5),
        'w_exp_down': nrm(ks[22], (DEPTH, N_EXPERTS, D_EXPERT, D_MODEL), D_EXPERT ** -0.5),
    }


def reference(x_prompt, x_sample, cache_k, cache_v, state_conv, meta_tokens, norm_mix, w_in, conv_w,
              q_norm, k_norm, attn_sinks, w_conv_out, w_attn_out, w_o, norm_ffn, w_router_group,
              b_router_group, w_router_expert, b_router_expert, w_exp_gate, w_exp_up, w_exp_down):
    meta = jnp.broadcast_to(meta_tokens[None].astype(x_prompt.dtype), (x_prompt.shape[0], N_META, D_MODEL))
    xp = jnp.concatenate([meta, x_prompt], axis=1)
    xs = x_sample
    kp, vp, cp, ksl, vsl, csl = [], [], [], [], [], []
    for l in range(DEPTH):
        lw = (norm_mix[l], w_in[l], conv_w[l], q_norm[l], k_norm[l], attn_sinks[l], w_conv_out[l],
              w_attn_out[l], w_o[l], norm_ffn[l], w_router_group[l], b_router_group[l],
              w_router_expert[l], b_router_expert[l], w_exp_gate[l], w_exp_up[l], w_exp_down[l])
        xp, k_new, v_new, c_new = prompt_layer(xp, lw)
        kp.append(k_new)
        vp.append(v_new)
        cp.append(c_new)
        xs, k_new, v_new, c_new = sample_layer(xs, cache_k[l], cache_v[l], state_conv[l], lw)
        ksl.append(k_new)
        vsl.append(v_new)
        csl.append(c_new)
    y_prompt = xp[:, N_META:]
    return (y_prompt, xs, jnp.stack(kp), jnp.stack(vp), jnp.stack(cp), jnp.stack(ksl), jnp.stack(vsl), jnp.stack(csl))
```

```python
import functools

import jax
import jax.numpy as jnp
import numpy as np
from jax import lax
from jax.experimental import pallas as pl
from jax.experimental.pallas import tpu as pltpu

D_MODEL = 1024
N_META = 16
D_CONV = D_MODEL
CONV_W = 3
N_HEADS = 16
N_KV_HEADS = 2
HEAD_DIM = 64
GQA = N_HEADS // N_KV_HEADS
ROT_DIM = HEAD_DIM // 4
ROPE_THETA = 500000.0
WINDOW = 128
PAST_LEN = 8192
BLOCK = 128
N_GROUPS = 4
EXPERTS_PER_GROUP = 4
N_EXPERTS = N_GROUPS * EXPERTS_PER_GROUP
D_EXPERT = 512
EPS = 1e-6
NEG = -1e30
D_Q = N_HEADS * HEAD_DIM
D_KV = N_KV_HEADS * HEAD_DIM
C_B, C_C, C_HC = 0, D_CONV, 2 * D_CONV
C_Q = 3 * D_CONV
C_K = C_Q + D_Q
C_V = C_K + D_KV
C_G = C_V + D_KV
D_MIX = C_G
D_IN = C_G + 2 * D_MODEL

LANES = 128
PAD = (-N_META) % BLOCK
N_PAIRS = 6
N_BUCKETS = N_GROUPS * N_PAIRS
MOE_TM = 256
PACK_W = D_MODEL // 2 + LANES

F32 = jnp.float32
BF16 = jnp.bfloat16
VMEM_LIMIT = 56 * 1024 * 1024


def _cp(sem, vmem=VMEM_LIMIT):
    return pltpu.CompilerParams(dimension_semantics=sem, vmem_limit_bytes=vmem)


def _const_spec(shape):
    nd = len(shape)
    return pl.BlockSpec(shape, lambda *_: (0,) * nd, pipeline_mode=pl.Buffered(1))


def _dot(a, b):
    return jnp.dot(a, b, preferred_element_type=F32)


def _seg_mean_sq(x, seg_ones):
    sq = x * x
    hi = sq.astype(BF16)
    lo = (sq - hi.astype(F32)).astype(BF16)
    return (_dot(hi, seg_ones) + _dot(lo, seg_ones)) * (1.0 / HEAD_DIM)


def _rope128(t, cos, sneg, spos):
    return t * cos + pltpu.roll(t, LANES - ROT_DIM // 2, 1) * sneg + pltpu.roll(t, ROT_DIM // 2, 1) * spos


def _rms_rows(x, g):
    ms = jnp.mean(x * x, axis=-1, keepdims=True)
    return (x * lax.rsqrt(ms + EPS)) * g


def _qk_heads(hb, w_ref, qn, kn, cos, sneg, spos, s256, s128, store_q):
    for c in range(D_Q // 256):
        qc = _dot(hb, w_ref[:, C_Q + c * 256:C_Q + (c + 1) * 256])
        qc = (qc * lax.rsqrt(_seg_mean_sq(qc, s256) + EPS)) * qn[:, c * 256:(c + 1) * 256]
        for s in range(2):
            r = _rope128(qc[:, s * LANES:(s + 1) * LANES], cos, sneg, spos)
            store_q(2 * c + s, (r * (HEAD_DIM ** -0.5)).astype(BF16))
    kc = _dot(hb, w_ref[:, C_K:C_K + D_KV])
    kc = (kc * lax.rsqrt(_seg_mean_sq(kc, s128) + EPS)) * kn
    return _rope128(kc, cos, sneg, spos)


def _in_body(x_ref, nm_ref, w_ref, cw_ref, qn_ref, kn_ref, cos_ref, sneg_ref, spos_ref,
             s256_ref, s128_ref, cy_ref, q_ref, k_ref, v_ref, ul_ref, us_ref, *, tm):
    i = pl.program_id(1)
    hb = _rms_rows(x_ref[0], nm_ref[...]).astype(BF16)

    u = _dot(hb, w_ref[:, C_C:C_C + D_CONV]) * _dot(hb, w_ref[:, C_HC:C_HC + D_CONV])
    row = lax.broadcasted_iota(jnp.int32, (tm, 1), 0) + i * tm
    u = jnp.where(row >= PAD, u, 0.0)

    @pl.when(i == 0)
    def _():
        us_ref[0:8, :] = jnp.zeros((8, D_CONV), F32)

    us_ref[8:8 + tm, :] = u
    conv = (us_ref[6:6 + tm, :] * cw_ref[0:1, :] + us_ref[7:7 + tm, :] * cw_ref[1:2, :]) + u * cw_ref[2:3, :]
    cy_ref[0] = (_dot(hb, w_ref[:, C_B:C_B + D_CONV]) * conv).astype(BF16)
    last = us_ref[tm:tm + 8, :]
    ul_ref[0] = last
    us_ref[0:8, :] = last

    def store_q(slab, val):
        q_ref[0, :, slab * LANES:(slab + 1) * LANES] = val

    k_ref[0] = _qk_heads(hb, w_ref, qn_ref[...], kn_ref[...], cos_ref[...], sneg_ref[...],
                         spos_ref[...], s256_ref[...], s128_ref[...], store_q)
    v_ref[0] = _dot(hb, w_ref[:, C_V:C_V + D_KV])


def _prompt_in(x, nm, w_mix, cw, qn, kn, rope, s256, s128, tm):
    b, lp, _ = x.shape
    nt = lp // tm
    cos, sneg, spos = rope
    tok = lambda w: pl.BlockSpec((1, tm, w), lambda bi, i: (bi, i, 0))
    tab = pl.BlockSpec((tm, LANES), lambda bi, i: (i, 0))
    return pl.pallas_call(
        functools.partial(_in_body, tm=tm),
        grid=(b, nt),
        in_specs=[tok(D_MODEL), _const_spec((1, D_MODEL)), _const_spec((D_MODEL, D_MIX)),
                  _const_spec((CONV_W, D_CONV)), _const_spec((1, D_Q)), _const_spec((1, D_KV)),
                  tab, tab, tab, _const_spec((256, 256)), _const_spec((LANES, LANES))],
        out_specs=[tok(D_CONV), tok(D_Q), tok(D_KV), tok(D_KV),
                   pl.BlockSpec((1, 8, D_CONV), lambda bi, i: (bi, 0, 0))],
        out_shape=[jax.ShapeDtypeStruct((b, lp, D_CONV), BF16), jax.ShapeDtypeStruct((b, lp, D_Q), BF16),
                   jax.ShapeDtypeStruct((b, lp, D_KV), F32), jax.ShapeDtypeStruct((b, lp, D_KV), F32),
                   jax.ShapeDtypeStruct((b, 8, D_CONV), F32)],
        scratch_shapes=[pltpu.VMEM((tm + 8, D_CONV), F32)],
        compiler_params=_cp(("arbitrary", "arbitrary")),
        name="prompt_in",
    )(x, nm, w_mix, cw, qn, kn, cos, sneg, spos, s256, s128)


def _sin_body(x_ref, c0_ref, c1_ref, nm_ref, w_ref, cw_ref, qn_ref, kn_ref, cos_ref, sneg_ref,
              spos_ref, s256_ref, s128_ref, cy_ref, qx_ref, k_ref, v_ref, u_ref):
    hb = _rms_rows(x_ref[...], nm_ref[...]).astype(BF16)
    u = _dot(hb, w_ref[:, C_C:C_C + D_CONV]) * _dot(hb, w_ref[:, C_HC:C_HC + D_CONV])
    u_ref[...] = u
    conv = (c0_ref[...] * cw_ref[0:1, :] + c1_ref[...] * cw_ref[1:2, :]) + u * cw_ref[2:3, :]
    cy_ref[...] = (_dot(hb, w_ref[:, C_B:C_B + D_CONV]) * conv).astype(BF16)

    lane = lax.broadcasted_iota(jnp.int32, (x_ref.shape[0], LANES), 1)
    low = lane < HEAD_DIM

    def store_q(slab, val):
        valf = val.astype(F32)
        swapped = pltpu.roll(valf, HEAD_DIM, 1)
        zero = jnp.zeros_like(valf)
        for h in (2 * slab, 2 * slab + 1):
            src = valf if (h % 2) == (h // GQA) else swapped
            keep = low if (h // GQA) == 0 else jnp.logical_not(low)
            qx_ref[h] = jnp.where(keep, src, zero).astype(BF16)

    cos = jnp.broadcast_to(cos_ref[...], (x_ref.shape[0], LANES))
    sneg = jnp.broadcast_to(sneg_ref[...], (x_ref.shape[0], LANES))
    spos = jnp.broadcast_to(spos_ref[...], (x_ref.shape[0], LANES))
    k_ref[...] = _qk_heads(hb, w_ref, qn_ref[...], kn_ref[...], cos, sneg, spos,
                           s256_ref[...], s128_ref[...], store_q)
    v_ref[...] = _dot(hb, w_ref[:, C_V:C_V + D_KV])


def _sample_in(x, c0, c1, nm, w_mix, cw, qn, kn, rope, s256, s128):
    n = x.shape[0]
    cos, sneg, spos = rope
    full = lambda *s: pl.BlockSpec(s, lambda i: (0,) * len(s))
    return pl.pallas_call(
        _sin_body,
        grid=(1,),
        in_specs=[full(n, D_MODEL), full(n, D_CONV), full(n, D_CONV), full(1, D_MODEL),
                  full(D_MODEL, D_MIX), full(CONV_W, D_CONV), full(1, D_Q), full(1, D_KV),
                  full(1, LANES), full(1, LANES), full(1, LANES), full(256, 256), full(LANES, LANES)],
        out_specs=[full(n, D_CONV), full(N_HEADS, n, LANES), full(n, D_KV), full(n, D_KV), full(n, D_CONV)],
        out_shape=[jax.ShapeDtypeStruct((n, D_CONV), BF16), jax.ShapeDtypeStruct((N_HEADS, n, LANES), BF16),
                   jax.ShapeDtypeStruct((n, D_KV), F32), jax.ShapeDtypeStruct((n, D_KV), F32),
                   jax.ShapeDtypeStruct((n, D_CONV), F32)],
        compiler_params=_cp(("arbitrary",)),
        name="sample_in",
    )(x, c0, c1, nm, w_mix, cw, qn, kn, cos, sneg, spos, s256, s128)


def _softmax_sink(s, valid, sink):
    s = jnp.where(valid, s, NEG)
    m = jnp.maximum(jnp.max(s, axis=-1, keepdims=True), sink)
    p = jnp.exp(s - m)
    den = jnp.sum(p, axis=-1, keepdims=True) + jnp.exp(sink - m)
    return p.astype(BF16), 1.0 / den


def _attn_body(sink_ref, q_ref, kp_ref, kc_ref, vp_ref, vc_ref, o_ref, ke_ref, ko_ref, va_ref, vb_ref, *, qb):
    i = pl.program_id(1)
    lane = lax.broadcasted_iota(jnp.int32, (BLOCK, LANES), 1)
    low = lane < HEAD_DIM

    def prep(src, r0, n, e_ref, o_ref_):
        for t in range(n // BLOCK):
            blk = src[0, t * BLOCK:(t + 1) * BLOCK, :]
            swp = pltpu.roll(blk, HEAD_DIM, 1)
            zero = jnp.zeros_like(blk)
            rows = slice(r0 + t * BLOCK, r0 + (t + 1) * BLOCK)
            e_ref[0, rows, :] = jnp.where(low, blk, zero).astype(BF16)
            o_ref_[0, rows, :] = jnp.where(low, zero, swp).astype(BF16)
            e_ref[1, rows, :] = jnp.where(low, swp, zero).astype(BF16)
            o_ref_[1, rows, :] = jnp.where(low, zero, blk).astype(BF16)

    prep(kp_ref, 0, BLOCK, ke_ref, ko_ref)
    prep(kc_ref, BLOCK, qb * BLOCK, ke_ref, ko_ref)
    prep(vp_ref, 0, BLOCK, va_ref, vb_ref)
    prep(vc_ref, BLOCK, qb * BLOCK, va_ref, vb_ref)

    r = lax.broadcasted_iota(jnp.int32, (BLOCK, 2 * BLOCK), 0)
    c = lax.broadcasted_iota(jnp.int32, (BLOCK, 2 * BLOCK), 1)
    diff = r - (c - BLOCK)
    band = (diff >= 0) & (diff < WINDOW)
    nt = (((1,), (1,)), ((), ()))

    def one_block(b, carry):
        r0 = pl.multiple_of(b * BLOCK, BLOCK)
        kpos = (i * qb + b) * BLOCK + c - BLOCK - PAD
        valid = band & (kpos >= 0)
        for m in range(N_HEADS // 2):
            j = (2 * m) // GQA
            q2 = q_ref[0, pl.ds(r0, BLOCK), m * LANES:(m + 1) * LANES]
            se = lax.dot_general(q2, ke_ref[j, pl.ds(r0, 2 * BLOCK), :], nt, preferred_element_type=F32)
            so = lax.dot_general(q2, ko_ref[j, pl.ds(r0, 2 * BLOCK), :], nt, preferred_element_type=F32)
            pe, re = _softmax_sink(se, valid, sink_ref[2 * m])
            po, ro = _softmax_sink(so, valid, sink_ref[2 * m + 1])
            o = _dot(pe, va_ref[j, pl.ds(r0, 2 * BLOCK), :]) + _dot(po, vb_ref[j, pl.ds(r0, 2 * BLOCK), :])
            o_ref[0, pl.ds(r0, BLOCK), m * LANES:(m + 1) * LANES] = (o * jnp.where(low, re, ro)).astype(BF16)
        return carry

    lax.fori_loop(0, qb, one_block, 0)


def _prompt_attn(q, k, v, sinks, qb):
    b, lp, _ = q.shape
    nsteps = lp // (qb * BLOCK)
    cur = lambda w: pl.BlockSpec((1, qb * BLOCK, w), lambda bi, i: (bi, i, 0))
    prev = pl.BlockSpec((1, BLOCK, D_KV), lambda bi, i: (bi, jnp.maximum(i * qb - 1, 0), 0))
    ext = ((qb + 1) * BLOCK, LANES)
    return pl.pallas_call(
        functools.partial(_attn_body, qb=qb),
        grid=(b, nsteps),
        in_specs=[pl.BlockSpec(memory_space=pltpu.SMEM), cur(D_Q), prev, cur(D_KV), prev, cur(D_KV)],
        out_specs=cur(D_Q),
        out_shape=jax.ShapeDtypeStruct((b, lp, D_Q), BF16),
        scratch_shapes=[pltpu.VMEM((N_KV_HEADS,) + ext, BF16) for _ in range(4)],
        compiler_params=_cp(("arbitrary", "arbitrary")),
        name="prompt_attn",
    )(sinks, q, k, k, v, v)


def _sattn_body(qx_ref, sink_ref, ck_ref, cv_ref, kn_ref, vn_ref, ox_ref, nk_ref, nv_ref, *, tb):
    for t in range(tb):
        nk_ref[t, 0:WINDOW - 1, :] = ck_ref[t, 1:WINDOW, :]
        nk_ref[t, WINDOW - 1:WINDOW, :] = kn_ref[t:t + 1, :]
        nv_ref[t, 0:WINDOW - 1, :] = cv_ref[t, 1:WINDOW, :]
        nv_ref[t, WINDOW - 1:WINDOW, :] = vn_ref[t:t + 1, :]
        s = lax.dot_general(qx_ref[t], nk_ref[t].astype(BF16), (((1,), (1,)), ((), ())),
                            preferred_element_type=F32)
        sink = sink_ref[...][:, 0:1]
        m = jnp.maximum(jnp.max(s, axis=-1, keepdims=True), sink)
        p = jnp.exp(s - m)
        den = jnp.sum(p, axis=-1, keepdims=True) + jnp.exp(sink - m)
        ox_ref[t] = _dot(p.astype(BF16), nv_ref[t].astype(BF16)) * (1.0 / den)


def _sample_attn(qx, sinkb, ck, cv, kn, vn, tb=16):
    n = qx.shape[0]
    blk3 = lambda a, c: pl.BlockSpec((tb, a, c), lambda i: (i, 0, 0))
    row = pl.BlockSpec((tb, D_KV), lambda i: (i, 0))
    return pl.pallas_call(
        functools.partial(_sattn_body, tb=tb),
        grid=(n // tb,),
        in_specs=[blk3(N_HEADS, LANES), pl.BlockSpec((N_HEADS, LANES), lambda i: (0, 0)),
                  blk3(WINDOW, D_KV), blk3(WINDOW, D_KV), row, row],
        out_specs=[blk3(N_HEADS, LANES), blk3(WINDOW, D_KV), blk3(WINDOW, D_KV)],
        out_shape=[jax.ShapeDtypeStruct((n, N_HEADS, LANES), F32),
                   jax.ShapeDtypeStruct((n, WINDOW, D_KV), F32), jax.ShapeDtypeStruct((n, WINDOW, D_KV), F32)],
        compiler_params=_cp(("arbitrary",)),
        name="sample_attn",
    )(qx, sinkb, ck, cv, kn, vn)


def _out_body(x_ref, cy_ref, ao_ref, nm_ref, wg_ref, wco_ref, wao_ref, wo_ref, nf_ref, wr_ref, br_ref,
              tri_ref, cin_ref, x1_ref, pk_ref, rank_ref, bkt_ref, cnt_ref, run_ref, *, tm):
    i = pl.program_id(0)
    x = x_ref[...]
    hb = _rms_rows(x, nm_ref[...]).astype(BF16)
    ya = _dot(cy_ref[...], wco_ref[...])
    yb = _dot(ao_ref[...], wao_ref[...])
    mix = jax.nn.sigmoid(_dot(hb, wg_ref[:, 0:D_MODEL])) * ya + jax.nn.sigmoid(_dot(hb, wg_ref[:, D_MODEL:])) * yb
    x1 = x + _dot(mix.astype(BF16), wo_ref[...])
    x1_ref[...] = x1

    xn = _rms_rows(x1, nf_ref[...])
    xnb = xn.astype(BF16)
    logits = _dot(xnb, wr_ref[...]) + br_ref[...]

    lanef = lax.broadcasted_iota(jnp.int32, (tm, LANES), 1).astype(F32)
    big = jnp.float32(3e38)
    far = jnp.float32(LANES)
    rmax = lambda a: jnp.max(a, axis=-1, keepdims=True)
    rmin = lambda a: jnp.min(a, axis=-1, keepdims=True)

    gmask = lanef < N_GROUPS
    gl = jnp.where(gmask, logits, -big)
    gmax = rmax(gl)
    grp = rmin(jnp.where(gmask & (gl == gmax), lanef, far))
    p_grp = 1.0 / jnp.sum(jnp.where(gmask, jnp.exp(gl - gmax), 0.0), axis=-1, keepdims=True)

    e_lo = N_GROUPS + EXPERTS_PER_GROUP * grp
    emask = (lanef >= e_lo) & (lanef < e_lo + EXPERTS_PER_GROUP)
    el = jnp.where(emask, logits, -big)
    v1 = rmax(el)
    i1 = rmin(jnp.where(emask & (el == v1), lanef, far))
    rest = emask & (lanef != i1)
    el2 = jnp.where(rest, logits, -big)
    v2 = rmax(el2)
    i2 = rmin(jnp.where(rest & (el2 == v2), lanef, far))
    e = jnp.exp(v2 - v1)
    w1 = (1.0 / (1.0 + e)) * p_grp
    w2 = (e / (1.0 + e)) * p_grp
    first_low = i1 < i2
    ea = jnp.where(first_low, i1, i2) - e_lo
    eb = jnp.where(first_low, i2, i1) - e_lo
    w_a = jnp.where(first_low, w1, w2)
    w_b = jnp.where(first_low, w2, w1)
    pair = jnp.where(ea == 0.0, 0.0, jnp.where(ea == 1.0, 3.0, 5.0)) + (eb - ea - 1.0)
    bucket = grp * N_PAIRS + pair

    bits = lax.bitcast_convert_type(xnb.astype(F32), jnp.uint32)
    half = D_MODEL // 2
    pk_ref[:, 0:half] = bits[:, 0:half] | (bits[:, half:] >> 16)
    meta = jnp.where(lanef == 0.0, w_a, jnp.where(lanef == 1.0, w_b, 0.0))
    pk_ref[:, half:] = lax.bitcast_convert_type(meta, jnp.uint32)

    @pl.when(i == 0)
    def _():
        run_ref[...] = cin_ref[...]

    oht = jnp.transpose((lanef == bucket).astype(F32))
    before = _dot(oht.astype(BF16), tri_ref[...]) + run_ref[:, 0:1]
    rank_ref[0] = jnp.sum(oht * before, axis=0, keepdims=True).astype(jnp.int32)
    sub = lax.broadcasted_iota(jnp.int32, (LANES, tm), 0).astype(F32)
    bkt_ref[0] = jnp.sum(oht * sub, axis=0, keepdims=True).astype(jnp.int32)
    run_ref[...] = run_ref[...] + jnp.sum(oht, axis=-1, keepdims=True)
    cnt_ref[...] = run_ref[...]


def _mix_out(x, cy, ao, nm, w_gate, wco, wao, wo, nf, wr, br, tri, cnt_in, tm):
    t = x.shape[0]
    nt = t // tm
    tok = lambda w: pl.BlockSpec((tm, w), lambda i: (i, 0))
    rowi = pl.BlockSpec((1, 1, tm), lambda i: (i, 0, 0))
    sq = (D_MODEL, D_MODEL)
    return pl.pallas_call(
        functools.partial(_out_body, tm=tm),
        grid=(nt,),
        in_specs=[tok(D_MODEL), tok(D_CONV), tok(D_Q), _const_spec((1, D_MODEL)),
                  _const_spec((D_MODEL, 2 * D_MODEL)), _const_spec(sq), _const_spec(sq), _const_spec(sq),
                  _const_spec((1, D_MODEL)), _const_spec((D_MODEL, LANES)), _const_spec((1, LANES)),
                  _const_spec((tm, tm)), _const_spec((LANES, LANES))],
        out_specs=[tok(D_MODEL), tok(PACK_W), rowi, rowi, pl.BlockSpec((LANES, LANES), lambda i: (0, 0))],
        out_shape=[jax.ShapeDtypeStruct((t, D_MODEL), F32), jax.ShapeDtypeStruct((t, PACK_W), jnp.uint32),
                   jax.ShapeDtypeStruct((nt, 1, tm), jnp.int32), jax.ShapeDtypeStruct((nt, 1, tm), jnp.int32),
                   jax.ShapeDtypeStruct((LANES, LANES), F32)],
        scratch_shapes=[pltpu.VMEM((LANES, LANES), F32)],
        compiler_params=_cp(("arbitrary",)),
        name="mix_out",
    )(x, cy, ao, nm, w_gate, wco, wao, wo, nf, wr, br, tri, cnt_in)


def _row_copy(src, dst, r_src, r_dst, sem):
    return pltpu.make_async_copy(src.at[pl.ds(r_src, 1), :], dst.at[pl.ds(r_dst, 1), :], sem)


def _scatter_body(pos_ref, src_ref, dst_in_ref, dst_ref, sem, *, tm):
    del dst_in_ref

    def issue(r, c):
        _row_copy(src_ref, dst_ref, r, pos_ref[0, 0, r], sem).start()
        return c

    lax.fori_loop(0, tm, issue, 0)

    def drain(r, c):
        _row_copy(src_ref, dst_ref, r, pos_ref[0, 0, r], sem).wait()
        return c

    lax.fori_loop(0, tm, drain, 0)


def _scatter_rows(pos, src, dst, tm):
    nt = src.shape[0] // tm
    return pl.pallas_call(
        functools.partial(_scatter_body, tm=tm),
        grid=(nt,),
        in_specs=[pl.BlockSpec((1, 1, tm), lambda i: (i, 0, 0), memory_space=pltpu.SMEM),
                  pl.BlockSpec((tm, src.shape[1]), lambda i: (i, 0)),
                  pl.BlockSpec(memory_space=pl.ANY)],
        out_specs=pl.BlockSpec(memory_space=pl.ANY),
        out_shape=jax.ShapeDtypeStruct(dst.shape, dst.dtype),
        scratch_shapes=[pltpu.SemaphoreType.DMA(())],
        input_output_aliases={2: 0},
        compiler_params=_cp(("arbitrary",)),
        name="dispatch_scatter",
    )(pos, src, dst)


def _moe_body(ta_ref, tb_ref, na_ref, xs_ref, wga_ref, wua_ref, wda_ref, wgb_ref, wub_ref, wdb_ref, y_ref):
    del ta_ref, tb_ref

    @pl.when(pl.program_id(0) < na_ref[0])
    def _():
        half = D_MODEL // 2
        words = xs_ref[:, 0:half]
        x_lo = lax.bitcast_convert_type(words & jnp.uint32(0xFFFF0000), F32).astype(BF16)
        x_hi = lax.bitcast_convert_type(words << 16, F32).astype(BF16)
        meta = lax.bitcast_convert_type(xs_ref[:, half:], F32)

        def expert(wg, wu, wd, gate):
            a = _dot(x_lo, wg[0, 0:half, :]) + _dot(x_hi, wg[0, half:, :])
            u = _dot(x_lo, wu[0, 0:half, :]) + _dot(x_hi, wu[0, half:, :])
            hdn = (jax.nn.silu(a) * u) * gate
            return _dot(hdn.astype(BF16), wd[0])

        y_ref[...] = expert(wga_ref, wua_ref, wda_ref, meta[:, 0:1]) + expert(wgb_ref, wub_ref, wdb_ref, meta[:, 1:2])

    @pl.when(pl.program_id(0) >= na_ref[0])
    def _():
        y_ref[...] = jnp.zeros(y_ref.shape, F32)


def _moe_experts(tile_a, tile_b, n_act, xs, wg, wu, wd):
    nt = xs.shape[0] // MOE_TM
    last = lambda i, na: jnp.minimum(i, na[0] - 1)
    w_up = lambda sel: pl.BlockSpec((1, D_MODEL, D_EXPERT), lambda i, ta, tb, na: (sel(ta, tb)[last(i, na)], 0, 0))
    w_dn = lambda sel: pl.BlockSpec((1, D_EXPERT, D_MODEL), lambda i, ta, tb, na: (sel(ta, tb)[last(i, na)], 0, 0))
    sa = lambda ta, tb: ta
    sb = lambda ta, tb: tb
    grid_spec = pltpu.PrefetchScalarGridSpec(
        num_scalar_prefetch=3,
        grid=(nt,),
        in_specs=[pl.BlockSpec((MOE_TM, PACK_W), lambda i, ta, tb, na: (last(i, na), 0)),
                  w_up(sa), w_up(sa), w_dn(sa), w_up(sb), w_up(sb), w_dn(sb)],
        out_specs=pl.BlockSpec((MOE_TM, D_MODEL), lambda i, ta, tb, na: (i, 0)),
    )
    return pl.pallas_call(
        _moe_body,
        grid_spec=grid_spec,
        out_shape=jax.ShapeDtypeStruct((xs.shape[0], D_MODEL), F32),
        compiler_params=_cp(("arbitrary",)),
        name="moe_experts",
    )(tile_a, tile_b, n_act, xs, wg, wu, wd, wg, wu, wd)


def _combine_body(pos_ref, x1_ref, ys_ref, o_ref, buf_ref, sem, *, tm):
    def issue(r, c):
        _row_copy(ys_ref, buf_ref, pos_ref[0, 0, r], r, sem).start()
        return c

    lax.fori_loop(0, tm, issue, 0)

    def drain(r, c):
        _row_copy(ys_ref, buf_ref, pos_ref[0, 0, r], r, sem).wait()
        return c

    lax.fori_loop(0, tm, drain, 0)
    o_ref[...] = x1_ref[...] + buf_ref[...]


def _combine(pos, x1, ys, tm):
    t = x1.shape[0]
    return pl.pallas_call(
        functools.partial(_combine_body, tm=tm),
        grid=(t // tm,),
        in_specs=[pl.BlockSpec((1, 1, tm), lambda i: (i, 0, 0), memory_space=pltpu.SMEM),
                  pl.BlockSpec((tm, D_MODEL), lambda i: (i, 0)),
                  pl.BlockSpec(memory_space=pl.ANY)],
        out_specs=pl.BlockSpec((tm, D_MODEL), lambda i: (i, 0)),
        out_shape=jax.ShapeDtypeStruct((t, D_MODEL), F32),
        scratch_shapes=[pltpu.VMEM((tm, D_MODEL), F32), pltpu.SemaphoreType.DMA(())],
        compiler_params=_cp(("arbitrary",)),
        name="moe_combine",
    )(pos, x1, ys)


def _rope_tables(pos):
    half = ROT_DIM // 2
    inv_freq = jnp.float32(ROPE_THETA) ** (-jnp.arange(half, dtype=jnp.float32) * (2.0 / ROT_DIM))
    ang = pos.astype(jnp.float32)[:, None] * inv_freq[None, :]
    cos, sin = jnp.cos(ang), jnp.sin(ang)
    n = pos.shape[0]
    one = jnp.ones((n, HEAD_DIM - ROT_DIM), F32)
    zero = jnp.zeros((n, HEAD_DIM - ROT_DIM), F32)
    zh = jnp.zeros((n, half), F32)
    two = lambda t: jnp.concatenate([t, t], axis=1)
    return (two(jnp.concatenate([cos, cos, one], axis=1)),
            two(jnp.concatenate([-sin, zh, zero], axis=1)),
            two(jnp.concatenate([zh, sin, zero], axis=1)))


def _seg_ones(n):
    idx = np.arange(n) // HEAD_DIM
    return jnp.asarray(idx[:, None] == idx[None, :], BF16)


def _bucket_experts():
    ea, eb = [], []
    for g in range(N_GROUPS):
        for a in range(EXPERTS_PER_GROUP):
            for b in range(a + 1, EXPERTS_PER_GROUP):
                ea.append(g * EXPERTS_PER_GROUP + a)
                eb.append(g * EXPERTS_PER_GROUP + b)
    return np.asarray(ea, np.int32), np.asarray(eb, np.int32)


def _dispatch_plan(counts, n_tiles):
    padded = ((counts + MOE_TM - 1) // MOE_TM) * MOE_TM
    ends = jnp.cumsum(padded)
    offs = ends - padded
    n_act = jnp.maximum(ends[-1] // MOE_TM, 1)
    starts = jnp.arange(n_tiles, dtype=jnp.int32) * MOE_TM
    tile_bucket = jnp.minimum(jnp.sum(starts[:, None] >= ends[None, :], axis=1), N_BUCKETS - 1)
    ea, eb = _bucket_experts()
    onehot = tile_bucket[:, None] == jnp.arange(N_BUCKETS)[None, :]
    tile_a = jnp.sum(jnp.where(onehot, ea[None, :], 0), axis=1).astype(jnp.int32)
    tile_b = jnp.sum(jnp.where(onehot, eb[None, :], 0), axis=1).astype(jnp.int32)
    return offs, tile_a, tile_b, n_act.astype(jnp.int32).reshape(1)


def _positions(offs, bucket, rank):
    onehot = bucket[..., None] == jnp.arange(N_BUCKETS, dtype=jnp.int32)
    return (jnp.sum(jnp.where(onehot, offs.astype(jnp.int32), 0), axis=-1) + rank).astype(jnp.int32)


def kernel(x_prompt, x_sample, cache_k, cache_v, state_conv, meta_tokens, norm_mix, w_in, conv_w, q_norm, k_norm,
           attn_sinks, w_conv_out, w_attn_out, w_o, norm_ffn, w_router_group, b_router_group, w_router_expert,
           b_router_expert, w_exp_gate, w_exp_up, w_exp_down):
    batch, seq, _ = x_prompt.shape
    depth = w_in.shape[0]
    n_dec = x_sample.shape[0]
    past_len = PAST_LEN
    lp = PAD + N_META + seq
    tm_in, tm_out, qb = 640, 640, 5
    assert lp % tm_in == 0 and (batch * lp) % tm_out == 0 and lp % (qb * BLOCK) == 0
    assert x_sample.shape[1] == 1 and cache_k.shape[2] == WINDOW and past_len >= WINDOW

    t_prompt = batch * lp
    t_all = t_prompt + n_dec
    n_tiles = -(-(t_all + N_BUCKETS * (MOE_TM - 1)) // MOE_TM)

    front = jnp.zeros((batch, PAD, D_MODEL), F32)
    meta = jnp.broadcast_to(meta_tokens[None].astype(F32), (batch, N_META, D_MODEL))
    xp = jnp.concatenate([front, meta, x_prompt], axis=1)
    xs = x_sample.reshape(n_dec, D_MODEL)

    rope_p = _rope_tables(jnp.arange(lp, dtype=jnp.int32) - PAD)
    rope_s = _rope_tables(jnp.full((1,), past_len, jnp.int32))
    s256, s128 = _seg_ones(256), _seg_ones(LANES)
    tri_p = jnp.asarray(np.triu(np.ones((tm_out, tm_out)), 1), BF16)
    tri_s = jnp.asarray(np.triu(np.ones((n_dec, n_dec)), 1), BF16)
    zero_cnt = jnp.zeros((LANES, LANES), F32)

    outs = {k: [] for k in ("kp", "vp", "cp", "ks", "vs", "cs")}
    for l in range(depth):
        w_mix = w_in[l, :, :D_MIX].astype(BF16)
        w_gate = w_in[l, :, D_MIX:].astype(BF16)
        wco, wao, wo = w_conv_out[l].astype(BF16), w_attn_out[l].astype(BF16), w_o[l].astype(BF16)
        wg, wu, wd = w_exp_gate[l].astype(BF16), w_exp_up[l].astype(BF16), w_exp_down[l].astype(BF16)
        nm, nf = norm_mix[l].reshape(1, D_MODEL), norm_ffn[l].reshape(1, D_MODEL)
        qn = jnp.tile(q_norm[l], N_HEADS).reshape(1, D_Q)
        kn = jnp.tile(k_norm[l], N_KV_HEADS).reshape(1, D_KV)
        wr = jnp.zeros((D_MODEL, LANES), F32)
        wr = wr.at[:, :N_GROUPS].set(w_router_group[l]).at[:, N_GROUPS:N_GROUPS + N_EXPERTS].set(w_router_expert[l])
        br = jnp.zeros((1, LANES), F32)
        br = br.at[0, :N_GROUPS].set(b_router_group[l]).at[0, N_GROUPS:N_GROUPS + N_EXPERTS].set(b_router_expert[l])
        wr = wr.astype(BF16)
        sinks = attn_sinks[l].astype(F32)

        cy, q, k, v, ulast = _prompt_in(xp, nm, w_mix, conv_w[l], qn, kn, rope_p, s256, s128, tm_in)
        ao = _prompt_attn(q, k, v, sinks, qb)
        x1p, pkp, rankp, bktp, cnt = _mix_out(
            xp.reshape(t_prompt, D_MODEL), cy.reshape(t_prompt, D_CONV), ao.reshape(t_prompt, D_Q),
            nm, w_gate, wco, wao, wo, nf, wr, br, tri_p, zero_cnt, tm_out)
        outs["kp"].append(k[:, lp - WINDOW:].reshape(batch, WINDOW, N_KV_HEADS, HEAD_DIM))
        outs["vp"].append(v[:, lp - WINDOW:].reshape(batch, WINDOW, N_KV_HEADS, HEAD_DIM))
        outs["cp"].append(ulast[:, 8 - (CONV_W - 1):])

        c0, c1 = state_conv[l, :, 0, :], state_conv[l, :, 1, :]
        cys, qx, ksn, vsn, us = _sample_in(xs, c0, c1, nm, w_mix, conv_w[l], qn, kn, rope_s, s256, s128)
        sinkb = jnp.broadcast_to(sinks[:, None], (N_HEADS, LANES))
        ox, nk, nv = _sample_attn(jnp.transpose(qx, (1, 0, 2)), sinkb,
                                  cache_k[l].reshape(n_dec, WINDOW, D_KV), cache_v[l].reshape(n_dec, WINDOW, D_KV),
                                  ksn, vsn)
        ox = ox.reshape(n_dec, N_KV_HEADS, GQA, N_KV_HEADS, HEAD_DIM)
        aos = jnp.stack([ox[:, j, :, j, :] for j in range(N_KV_HEADS)], axis=1).reshape(n_dec, D_Q).astype(BF16)
        x1s, pks, ranks, bkts, cnt = _mix_out(xs, cys, aos, nm, w_gate, wco, wao, wo, nf, wr, br, tri_s, cnt, n_dec)
        outs["ks"].append(nk.reshape(n_dec, WINDOW, N_KV_HEADS, HEAD_DIM))
        outs["vs"].append(nv.reshape(n_dec, WINDOW, N_KV_HEADS, HEAD_DIM))
        outs["cs"].append(jnp.stack([c1, us], axis=1))

        counts = cnt[:N_BUCKETS, 0].astype(jnp.int32)
        offs, tile_a, tile_b, n_act = _dispatch_plan(counts, n_tiles)
        posp = _positions(offs, bktp, rankp)
        poss = _positions(offs, bkts, ranks)
        sorted_rows = jnp.zeros((n_tiles * MOE_TM, PACK_W), jnp.uint32)
        sorted_rows = _scatter_rows(posp, pkp, sorted_rows, tm_out)
        sorted_rows = _scatter_rows(poss, pks, sorted_rows, n_dec)
        ys = _moe_experts(tile_a, tile_b, n_act, sorted_rows, wg, wu, wd)
        xp = _combine(posp, x1p, ys, tm_out).reshape(batch, lp, D_MODEL)
        xs = _combine(poss, x1s, ys, n_dec)

    y_prompt = xp[:, PAD + N_META:]
    y_sample = xs.reshape(n_dec, 1, D_MODEL)
    st = lambda k: jnp.stack(outs[k])
    return (y_prompt, y_sample, st("kp"), st("vp"), st("cp"), st("ks"), st("vs"), st("cs"))
```

```python
import functools

import jax
import jax.numpy as jnp
import numpy as np
from jax import lax
from jax.experimental import pallas as pl
from jax.experimental.pallas import tpu as pltpu

D_MODEL = 1024
N_META = 16
D_CONV = D_MODEL
CONV_W = 3
N_HEADS = 16
N_KV_HEADS = 2
HEAD_DIM = 64
GQA = N_HEADS // N_KV_HEADS
ROT_DIM = HEAD_DIM // 4
ROPE_THETA = 500000.0
WINDOW = 128
PAST_LEN = 8192
BLOCK = 128
N_GROUPS = 4
EXPERTS_PER_GROUP = 4
N_EXPERTS = N_GROUPS * EXPERTS_PER_GROUP
D_EXPERT = 512
EPS = 1e-6
NEG = -1e30
D_Q = N_HEADS * HEAD_DIM
D_KV = N_KV_HEADS * HEAD_DIM
C_B, C_C, C_HC = 0, D_CONV, 2 * D_CONV
C_Q = 3 * D_CONV
C_K = C_Q + D_Q
C_V = C_K + D_KV
C_G = C_V + D_KV
D_MIX = C_G
D_IN = C_G + 2 * D_MODEL

LANES = 128
PAD = (-N_META) % BLOCK
N_PAIRS = 6
N_BUCKETS = N_GROUPS * N_PAIRS
MOE_TM = 256
ROW_W = D_MODEL + LANES

F32 = jnp.float32
BF16 = jnp.bfloat16
VMEM_LIMIT = 56 * 1024 * 1024


def _cp(sem, vmem=VMEM_LIMIT):
    return pltpu.CompilerParams(dimension_semantics=sem, vmem_limit_bytes=vmem)


def _const_spec(shape):
    nd = len(shape)
    return pl.BlockSpec(shape, lambda *_: (0,) * nd, pipeline_mode=pl.Buffered(1))


def _dot(a, b):
    return jnp.dot(a, b, preferred_element_type=F32)


def _seg_mean_sq(x, seg_ones):
    sq = x * x
    hi = sq.astype(BF16)
    lo = (sq - hi.astype(F32)).astype(BF16)
    return (_dot(hi, seg_ones) + _dot(lo, seg_ones)) * (1.0 / HEAD_DIM)


def _rope128(t, cos, sneg, spos):
    return t * cos + pltpu.roll(t, LANES - ROT_DIM // 2, 1) * sneg + pltpu.roll(t, ROT_DIM // 2, 1) * spos


def _rms_rows(x, g):
    ms = jnp.mean(x * x, axis=-1, keepdims=True)
    return (x * lax.rsqrt(ms + EPS)) * g


def _qk_heads(hb, w_ref, qn, kn, cos, sneg, spos, s256, s128, store_q):
    for c in range(D_Q // 256):
        qc = _dot(hb, w_ref[:, C_Q + c * 256:C_Q + (c + 1) * 256])
        qc = (qc * lax.rsqrt(_seg_mean_sq(qc, s256) + EPS)) * qn[:, c * 256:(c + 1) * 256]
        for s in range(2):
            r = _rope128(qc[:, s * LANES:(s + 1) * LANES], cos, sneg, spos)
            store_q(2 * c + s, (r * (HEAD_DIM ** -0.5)).astype(BF16))
    kc = _dot(hb, w_ref[:, C_K:C_K + D_KV])
    kc = (kc * lax.rsqrt(_seg_mean_sq(kc, s128) + EPS)) * kn
    return _rope128(kc, cos, sneg, spos)


def _in_body(x_ref, nm_ref, w_ref, cw_ref, qn_ref, kn_ref, cos_ref, sneg_ref, spos_ref,
             s256_ref, s128_ref, cy_ref, q_ref, k_ref, v_ref, ul_ref, us_ref, *, tm):
    i = pl.program_id(1)
    hb = _rms_rows(x_ref[0], nm_ref[...]).astype(BF16)

    u = _dot(hb, w_ref[:, C_C:C_C + D_CONV]) * _dot(hb, w_ref[:, C_HC:C_HC + D_CONV])
    row = lax.broadcasted_iota(jnp.int32, (tm, 1), 0) + i * tm
    u = jnp.where(row >= PAD, u, 0.0)

    @pl.when(i == 0)
    def _():
        us_ref[0:8, :] = jnp.zeros((8, D_CONV), F32)

    us_ref[8:8 + tm, :] = u
    conv = (us_ref[6:6 + tm, :] * cw_ref[0:1, :] + us_ref[7:7 + tm, :] * cw_ref[1:2, :]) + u * cw_ref[2:3, :]
    cy_ref[0] = (_dot(hb, w_ref[:, C_B:C_B + D_CONV]) * conv).astype(BF16)
    last = us_ref[tm:tm + 8, :]
    ul_ref[0] = last
    us_ref[0:8, :] = last

    def store_q(slab, val):
        q_ref[0, :, slab * LANES:(slab + 1) * LANES] = val

    k_ref[0] = _qk_heads(hb, w_ref, qn_ref[...], kn_ref[...], cos_ref[...], sneg_ref[...],
                         spos_ref[...], s256_ref[...], s128_ref[...], store_q)
    v_ref[0] = _dot(hb, w_ref[:, C_V:C_V + D_KV])


def _prompt_in(x, nm, w_mix, cw, qn, kn, rope, s256, s128, tm):
    b, lp, _ = x.shape
    nt = lp // tm
    cos, sneg, spos = rope
    tok = lambda w: pl.BlockSpec((1, tm, w), lambda bi, i: (bi, i, 0))
    tab = pl.BlockSpec((tm, LANES), lambda bi, i: (i, 0))
    return pl.pallas_call(
        functools.partial(_in_body, tm=tm),
        grid=(b, nt),
        in_specs=[tok(D_MODEL), _const_spec((1, D_MODEL)), _const_spec((D_MODEL, D_MIX)),
                  _const_spec((CONV_W, D_CONV)), _const_spec((1, D_Q)), _const_spec((1, D_KV)),
                  tab, tab, tab, _const_spec((256, 256)), _const_spec((LANES, LANES))],
        out_specs=[tok(D_CONV), tok(D_Q), tok(D_KV), tok(D_KV),
                   pl.BlockSpec((1, 8, D_CONV), lambda bi, i: (bi, 0, 0))],
        out_shape=[jax.ShapeDtypeStruct((b, lp, D_CONV), BF16), jax.ShapeDtypeStruct((b, lp, D_Q), BF16),
                   jax.ShapeDtypeStruct((b, lp, D_KV), F32), jax.ShapeDtypeStruct((b, lp, D_KV), F32),
                   jax.ShapeDtypeStruct((b, 8, D_CONV), F32)],
        scratch_shapes=[pltpu.VMEM((tm + 8, D_CONV), F32)],
        compiler_params=_cp(("arbitrary", "arbitrary")),
        name="prompt_in",
    )(x, nm, w_mix, cw, qn, kn, cos, sneg, spos, s256, s128)


def _sin_body(x_ref, c0_ref, c1_ref, nm_ref, w_ref, cw_ref, qn_ref, kn_ref, cos_ref, sneg_ref,
              spos_ref, s256_ref, s128_ref, cy_ref, qx_ref, k_ref, v_ref, u_ref):
    hb = _rms_rows(x_ref[...], nm_ref[...]).astype(BF16)
    u = _dot(hb, w_ref[:, C_C:C_C + D_CONV]) * _dot(hb, w_ref[:, C_HC:C_HC + D_CONV])
    u_ref[...] = u
    conv = (c0_ref[...] * cw_ref[0:1, :] + c1_ref[...] * cw_ref[1:2, :]) + u * cw_ref[2:3, :]
    cy_ref[...] = (_dot(hb, w_ref[:, C_B:C_B + D_CONV]) * conv).astype(BF16)

    lane = lax.broadcasted_iota(jnp.int32, (x_ref.shape[0], LANES), 1)
    low = lane < HEAD_DIM

    def store_q(slab, val):
        valf = val.astype(F32)
        swapped = pltpu.roll(valf, HEAD_DIM, 1)
        zero = jnp.zeros_like(valf)
        for h in (2 * slab, 2 * slab + 1):
            src = valf if (h % 2) == (h // GQA) else swapped
            keep = low if (h // GQA) == 0 else jnp.logical_not(low)
            qx_ref[h] = jnp.where(keep, src, zero).astype(BF16)

    cos = jnp.broadcast_to(cos_ref[...], (x_ref.shape[0], LANES))
    sneg = jnp.broadcast_to(sneg_ref[...], (x_ref.shape[0], LANES))
    spos = jnp.broadcast_to(spos_ref[...], (x_ref.shape[0], LANES))
    k_ref[...] = _qk_heads(hb, w_ref, qn_ref[...], kn_ref[...], cos, sneg, spos,
                           s256_ref[...], s128_ref[...], store_q)
    v_ref[...] = _dot(hb, w_ref[:, C_V:C_V + D_KV])


def _sample_in(x, c0, c1, nm, w_mix, cw, qn, kn, rope, s256, s128):
    n = x.shape[0]
    cos, sneg, spos = rope
    full = lambda *s: pl.BlockSpec(s, lambda i: (0,) * len(s))
    return pl.pallas_call(
        _sin_body,
        grid=(1,),
        in_specs=[full(n, D_MODEL), full(n, D_CONV), full(n, D_CONV), full(1, D_MODEL),
                  full(D_MODEL, D_MIX), full(CONV_W, D_CONV), full(1, D_Q), full(1, D_KV),
                  full(1, LANES), full(1, LANES), full(1, LANES), full(256, 256), full(LANES, LANES)],
        out_specs=[full(n, D_CONV), full(N_HEADS, n, LANES), full(n, D_KV), full(n, D_KV), full(n, D_CONV)],
        out_shape=[jax.ShapeDtypeStruct((n, D_CONV), BF16), jax.ShapeDtypeStruct((N_HEADS, n, LANES), BF16),
                   jax.ShapeDtypeStruct((n, D_KV), F32), jax.ShapeDtypeStruct((n, D_KV), F32),
                   jax.ShapeDtypeStruct((n, D_CONV), F32)],
        compiler_params=_cp(("arbitrary",)),
        name="sample_in",
    )(x, c0, c1, nm, w_mix, cw, qn, kn, cos, sneg, spos, s256, s128)


def _attn_body(sink_ref, q_ref, kp_ref, kc_ref, vp_ref, vc_ref, o_ref, ke_ref, ko_ref, vt_ref, *, qb):
    i = pl.program_id(1)
    lane = lax.broadcasted_iota(jnp.int32, (BLOCK, LANES), 1)
    low = lane < HEAD_DIM

    def prep_k(src, blk0, nblk):
        for t in range(nblk):
            blk = src[0, t * BLOCK:(t + 1) * BLOCK, :]
            swp = pltpu.roll(blk, HEAD_DIM, 1)
            zero = jnp.zeros_like(blk)
            rows = slice((blk0 + t) * BLOCK, (blk0 + t + 1) * BLOCK)
            ke_ref[0, rows, :] = jnp.where(low, blk, zero).astype(BF16)
            ko_ref[0, rows, :] = jnp.where(low, zero, swp).astype(BF16)
            ke_ref[1, rows, :] = jnp.where(low, swp, zero).astype(BF16)
            ko_ref[1, rows, :] = jnp.where(low, zero, blk).astype(BF16)

    def prep_v(src, blk0, nblk):
        for t in range(nblk):
            vt = jnp.transpose(src[0, t * BLOCK:(t + 1) * BLOCK, :]).astype(BF16)
            for j in range(N_KV_HEADS):
                vt_ref[blk0 + t, j] = vt[j * HEAD_DIM:(j + 1) * HEAD_DIM, :]

    prep_k(kp_ref, 0, 1)
    prep_k(kc_ref, 1, qb)
    prep_v(vp_ref, 0, 1)
    prep_v(vc_ref, 1, qb)

    c = lax.broadcasted_iota(jnp.int32, (2 * BLOCK, BLOCK), 0)
    r = lax.broadcasted_iota(jnp.int32, (2 * BLOCK, BLOCK), 1)
    diff = r - (c - BLOCK)
    band = (diff >= 0) & (diff < WINDOW)
    nt = (((1,), (1,)), ((), ()))

    def one_block(b, carry):
        r0 = pl.multiple_of(b * BLOCK, BLOCK)
        kpos = (i * qb + b) * BLOCK + c - BLOCK - PAD
        valid = band & (kpos >= 0)
        for m in range(N_HEADS // 2):
            j = (2 * m) // GQA
            q2 = q_ref[0, pl.ds(r0, BLOCK), m * LANES:(m + 1) * LANES]
            halves = []
            for par, k_ref in ((0, ke_ref), (1, ko_ref)):
                st = lax.dot_general(k_ref[j, pl.ds(r0, 2 * BLOCK), :], q2, nt, preferred_element_type=F32)
                st = jnp.where(valid, st, NEG)
                sink = sink_ref[2 * m + par]
                mx = jnp.maximum(jnp.max(st, axis=0, keepdims=True), sink)
                p = jnp.exp(st - mx)
                den = jnp.sum(p, axis=0, keepdims=True) + jnp.exp(sink - mx)
                pb = p.astype(BF16)
                ot = _dot(vt_ref[b, j], pb[0:BLOCK]) + _dot(vt_ref[b + 1, j], pb[BLOCK:])
                halves.append(ot * (1.0 / den))
            o2 = jnp.transpose(jnp.concatenate(halves, axis=0))
            o_ref[0, pl.ds(r0, BLOCK), m * LANES:(m + 1) * LANES] = o2.astype(BF16)
        return carry

    lax.fori_loop(0, qb, one_block, 0)


def _prompt_attn(q, k, v, sinks, qb):
    b, lp, _ = q.shape
    nsteps = lp // (qb * BLOCK)
    cur = lambda w: pl.BlockSpec((1, qb * BLOCK, w), lambda bi, i: (bi, i, 0))
    prev = pl.BlockSpec((1, BLOCK, D_KV), lambda bi, i: (bi, jnp.maximum(i * qb - 1, 0), 0))
    ext = ((qb + 1) * BLOCK, LANES)
    return pl.pallas_call(
        functools.partial(_attn_body, qb=qb),
        grid=(b, nsteps),
        in_specs=[pl.BlockSpec(memory_space=pltpu.SMEM), cur(D_Q), prev, cur(D_KV), prev, cur(D_KV)],
        out_specs=cur(D_Q),
        out_shape=jax.ShapeDtypeStruct((b, lp, D_Q), BF16),
        scratch_shapes=[pltpu.VMEM((N_KV_HEADS,) + ext, BF16), pltpu.VMEM((N_KV_HEADS,) + ext, BF16),
                        pltpu.VMEM((qb + 1, N_KV_HEADS, HEAD_DIM, BLOCK), BF16)],
        compiler_params=_cp(("arbitrary", "arbitrary")),
        name="prompt_attn",
    )(sinks, q, k, k, v, v)


def _sattn_body(qx_ref, sink_ref, ck_ref, cv_ref, kn_ref, vn_ref, ox_ref, nk_ref, nv_ref, *, tb):
    for t in range(tb):
        nk_ref[t, 0:WINDOW - 1, :] = ck_ref[t, 1:WINDOW, :]
        nk_ref[t, WINDOW - 1:WINDOW, :] = kn_ref[t:t + 1, :]
        nv_ref[t, 0:WINDOW - 1, :] = cv_ref[t, 1:WINDOW, :]
        nv_ref[t, WINDOW - 1:WINDOW, :] = vn_ref[t:t + 1, :]
        s = lax.dot_general(qx_ref[t], nk_ref[t].astype(BF16), (((1,), (1,)), ((), ())),
                            preferred_element_type=F32)
        sink = sink_ref[...][:, 0:1]
        m = jnp.maximum(jnp.max(s, axis=-1, keepdims=True), sink)
        p = jnp.exp(s - m)
        den = jnp.sum(p, axis=-1, keepdims=True) + jnp.exp(sink - m)
        ox_ref[t] = _dot(p.astype(BF16), nv_ref[t].astype(BF16)) * (1.0 / den)


def _sample_attn(qx, sinkb, ck, cv, kn, vn, tb=16):
    n = qx.shape[0]
    blk3 = lambda a, c: pl.BlockSpec((tb, a, c), lambda i: (i, 0, 0))
    row = pl.BlockSpec((tb, D_KV), lambda i: (i, 0))
    return pl.pallas_call(
        functools.partial(_sattn_body, tb=tb),
        grid=(n // tb,),
        in_specs=[blk3(N_HEADS, LANES), pl.BlockSpec((N_HEADS, LANES), lambda i: (0, 0)),
                  blk3(WINDOW, D_KV), blk3(WINDOW, D_KV), row, row],
        out_specs=[blk3(N_HEADS, LANES), blk3(WINDOW, D_KV), blk3(WINDOW, D_KV)],
        out_shape=[jax.ShapeDtypeStruct((n, N_HEADS, LANES), F32),
                   jax.ShapeDtypeStruct((n, WINDOW, D_KV), F32), jax.ShapeDtypeStruct((n, WINDOW, D_KV), F32)],
        compiler_params=_cp(("arbitrary",)),
        name="sample_attn",
    )(qx, sinkb, ck, cv, kn, vn)


def _out_body(x_ref, cy_ref, ao_ref, nm_ref, wg_ref, wco_ref, wao_ref, wo_ref, nf_ref, wr_ref, br_ref,
              tri_ref, cin_ref, xe_ref, rank_ref, bkt_ref, cnt_ref, run_ref, *, tm):
    i = pl.program_id(0)
    x = x_ref[...]
    hb = _rms_rows(x, nm_ref[...]).astype(BF16)
    ya = _dot(cy_ref[...], wco_ref[...])
    yb = _dot(ao_ref[...], wao_ref[...])
    mix = jax.nn.sigmoid(_dot(hb, wg_ref[:, 0:D_MODEL])) * ya + jax.nn.sigmoid(_dot(hb, wg_ref[:, D_MODEL:])) * yb
    x1 = x + _dot(mix.astype(BF16), wo_ref[...])
    xe_ref[:, 0:D_MODEL] = x1

    xnb = _rms_rows(x1, nf_ref[...]).astype(BF16)
    logits = _dot(xnb, wr_ref[...]) + br_ref[...]

    lanef = lax.broadcasted_iota(jnp.int32, (tm, LANES), 1).astype(F32)
    big = jnp.float32(3e38)
    far = jnp.float32(LANES)
    rmax = lambda a: jnp.max(a, axis=-1, keepdims=True)
    rmin = lambda a: jnp.min(a, axis=-1, keepdims=True)

    gmask = lanef < N_GROUPS
    gl = jnp.where(gmask, logits, -big)
    gmax = rmax(gl)
    grp = rmin(jnp.where(gmask & (gl == gmax), lanef, far))
    p_grp = 1.0 / jnp.sum(jnp.where(gmask, jnp.exp(gl - gmax), 0.0), axis=-1, keepdims=True)

    e_lo = N_GROUPS + EXPERTS_PER_GROUP * grp
    emask = (lanef >= e_lo) & (lanef < e_lo + EXPERTS_PER_GROUP)
    el = jnp.where(emask, logits, -big)
    v1 = rmax(el)
    i1 = rmin(jnp.where(emask & (el == v1), lanef, far))
    rest = emask & (lanef != i1)
    el2 = jnp.where(rest, logits, -big)
    v2 = rmax(el2)
    i2 = rmin(jnp.where(rest & (el2 == v2), lanef, far))
    e = jnp.exp(v2 - v1)
    w1 = (1.0 / (1.0 + e)) * p_grp
    w2 = (e / (1.0 + e)) * p_grp
    first_low = i1 < i2
    ea = jnp.where(first_low, i1, i2) - e_lo
    eb = jnp.where(first_low, i2, i1) - e_lo
    w_a = jnp.where(first_low, w1, w2)
    w_b = jnp.where(first_low, w2, w1)
    pair = jnp.where(ea == 0.0, 0.0, jnp.where(ea == 1.0, 3.0, 5.0)) + (eb - ea - 1.0)
    bucket = grp * N_PAIRS + pair

    xe_ref[:, D_MODEL:] = jnp.where(lanef == 0.0, w_a, jnp.where(lanef == 1.0, w_b, 0.0))

    @pl.when(i == 0)
    def _():
        run_ref[...] = cin_ref[...]

    oht = jnp.transpose((lanef == bucket).astype(F32))
    before = _dot(oht.astype(BF16), tri_ref[...]) + run_ref[:, 0:1]
    rank_ref[0] = jnp.sum(oht * before, axis=0, keepdims=True).astype(jnp.int32)
    sub = lax.broadcasted_iota(jnp.int32, (LANES, tm), 0).astype(F32)
    bkt_ref[0] = jnp.sum(oht * sub, axis=0, keepdims=True).astype(jnp.int32)
    run_ref[...] = run_ref[...] + jnp.sum(oht, axis=-1, keepdims=True)
    cnt_ref[...] = run_ref[...]


def _mix_out(x, cy, ao, nm, w_gate, wco, wao, wo, nf, wr, br, tri, cnt_in, tm):
    t = x.shape[0]
    nt = t // tm
    tok = lambda w: pl.BlockSpec((tm, w), lambda i: (i, 0))
    rowi = pl.BlockSpec((1, 1, tm), lambda i: (i, 0, 0))
    sq = (D_MODEL, D_MODEL)
    return pl.pallas_call(
        functools.partial(_out_body, tm=tm),
        grid=(nt,),
        in_specs=[tok(D_MODEL), tok(D_CONV), tok(D_Q), _const_spec((1, D_MODEL)),
                  _const_spec((D_MODEL, 2 * D_MODEL)), _const_spec(sq), _const_spec(sq), _const_spec(sq),
                  _const_spec((1, D_MODEL)), _const_spec((D_MODEL, LANES)), _const_spec((1, LANES)),
                  _const_spec((tm, tm)), _const_spec((LANES, LANES))],
        out_specs=[tok(ROW_W), rowi, rowi, pl.BlockSpec((LANES, LANES), lambda i: (0, 0))],
        out_shape=[jax.ShapeDtypeStruct((t, ROW_W), F32),
                   jax.ShapeDtypeStruct((nt, 1, tm), jnp.int32), jax.ShapeDtypeStruct((nt, 1, tm), jnp.int32),
                   jax.ShapeDtypeStruct((LANES, LANES), F32)],
        scratch_shapes=[pltpu.VMEM((LANES, LANES), F32)],
        compiler_params=_cp(("arbitrary",)),
        name="mix_out",
    )(x, cy, ao, nm, w_gate, wco, wao, wo, nf, wr, br, tri, cnt_in)


ROW_UNROLL = 8


def _scatter_body(pos_ref, src_ref, dst_in_ref, dst_ref, sem, *, tm):
    del dst_in_ref

    def issue(r, c):
        pltpu.make_async_copy(src_ref.at[pl.ds(r, 1), :], dst_ref.at[pl.ds(pos_ref[0, 0, r], 1), :], sem).start()
        return c

    lax.fori_loop(0, tm, issue, 0, unroll=ROW_UNROLL)
    pltpu.make_async_copy(src_ref, dst_ref.at[pl.ds(0, tm), :], sem).wait()


def _scatter_rows(pos, src, dst, tm):
    nt = src.shape[0] // tm
    return pl.pallas_call(
        functools.partial(_scatter_body, tm=tm),
        grid=(nt,),
        in_specs=[pl.BlockSpec((1, 1, tm), lambda i: (i, 0, 0), memory_space=pltpu.SMEM),
                  pl.BlockSpec((tm, src.shape[1]), lambda i: (i, 0)),
                  pl.BlockSpec(memory_space=pl.ANY)],
        out_specs=pl.BlockSpec(memory_space=pl.ANY),
        out_shape=jax.ShapeDtypeStruct(dst.shape, dst.dtype),
        scratch_shapes=[pltpu.SemaphoreType.DMA(())],
        input_output_aliases={2: 0},
        compiler_params=_cp(("arbitrary",)),
        name="dispatch_scatter",
    )(pos, src, dst)


def _moe_body(ta_ref, tb_ref, na_ref, xs_ref, nf_ref, wga_ref, wua_ref, wda_ref, wgb_ref, wub_ref, wdb_ref, y_ref):
    del ta_ref, tb_ref

    @pl.when(pl.program_id(0) < na_ref[0])
    def _():
        x1 = xs_ref[:, 0:D_MODEL]
        xb = _rms_rows(x1, nf_ref[...]).astype(BF16)

        def expert(wg, wu, wd, gate):
            hdn = (jax.nn.silu(_dot(xb, wg[0])) * _dot(xb, wu[0])) * gate
            return _dot(hdn.astype(BF16), wd[0])

        y = expert(wga_ref, wua_ref, wda_ref, xs_ref[:, D_MODEL:D_MODEL + 1])
        y = y + expert(wgb_ref, wub_ref, wdb_ref, xs_ref[:, D_MODEL + 1:D_MODEL + 2])
        y_ref[...] = x1 + y

    @pl.when(pl.program_id(0) >= na_ref[0])
    def _():
        y_ref[...] = jnp.zeros(y_ref.shape, F32)


def _moe_experts(tile_a, tile_b, n_act, xs, nf, wg, wu, wd):
    nt = xs.shape[0] // MOE_TM
    last = lambda i, na: jnp.minimum(i, na[0] - 1)
    w_up = lambda sel: pl.BlockSpec((1, D_MODEL, D_EXPERT), lambda i, ta, tb, na: (sel(ta, tb)[last(i, na)], 0, 0))
    w_dn = lambda sel: pl.BlockSpec((1, D_EXPERT, D_MODEL), lambda i, ta, tb, na: (sel(ta, tb)[last(i, na)], 0, 0))
    sa = lambda ta, tb: ta
    sb = lambda ta, tb: tb
    grid_spec = pltpu.PrefetchScalarGridSpec(
        num_scalar_prefetch=3,
        grid=(nt,),
        in_specs=[pl.BlockSpec((MOE_TM, ROW_W), lambda i, ta, tb, na: (last(i, na), 0)),
                  pl.BlockSpec((1, D_MODEL), lambda i, ta, tb, na: (0, 0)),
                  w_up(sa), w_up(sa), w_dn(sa), w_up(sb), w_up(sb), w_dn(sb)],
        out_specs=pl.BlockSpec((MOE_TM, D_MODEL), lambda i, ta, tb, na: (i, 0)),
    )
    return pl.pallas_call(
        _moe_body,
        grid_spec=grid_spec,
        out_shape=jax.ShapeDtypeStruct((xs.shape[0], D_MODEL), F32),
        compiler_params=_cp(("arbitrary",)),
        name="moe_experts",
    )(tile_a, tile_b, n_act, xs, nf, wg, wu, wd, wg, wu, wd)


def _unpermute_body(pos_ref, ys_ref, o_ref, sem, *, tm):
    def issue(r, c):
        pltpu.make_async_copy(ys_ref.at[pl.ds(pos_ref[0, 0, r], 1), :], o_ref.at[pl.ds(r, 1), :], sem).start()
        return c

    lax.fori_loop(0, tm, issue, 0, unroll=ROW_UNROLL)
    pltpu.make_async_copy(ys_ref.at[pl.ds(0, tm), :], o_ref, sem).wait()


def _unpermute(pos, ys, tm):
    t = pos.shape[0] * tm
    return pl.pallas_call(
        functools.partial(_unpermute_body, tm=tm),
        grid=(t // tm,),
        in_specs=[pl.BlockSpec((1, 1, tm), lambda i: (i, 0, 0), memory_space=pltpu.SMEM),
                  pl.BlockSpec(memory_space=pl.ANY)],
        out_specs=pl.BlockSpec((tm, D_MODEL), lambda i: (i, 0)),
        out_shape=jax.ShapeDtypeStruct((t, D_MODEL), F32),
        scratch_shapes=[pltpu.SemaphoreType.DMA(())],
        compiler_params=_cp(("arbitrary",)),
        name="moe_unpermute",
    )(pos, ys)


def _rope_tables(pos):
    half = ROT_DIM // 2
    inv_freq = jnp.float32(ROPE_THETA) ** (-jnp.arange(half, dtype=jnp.float32) * (2.0 / ROT_DIM))
    ang = pos.astype(jnp.float32)[:, None] * inv_freq[None, :]
    cos, sin = jnp.cos(ang), jnp.sin(ang)
    n = pos.shape[0]
    one = jnp.ones((n, HEAD_DIM - ROT_DIM), F32)
    zero = jnp.zeros((n, HEAD_DIM - ROT_DIM), F32)
    zh = jnp.zeros((n, half), F32)
    two = lambda t: jnp.concatenate([t, t], axis=1)
    return (two(jnp.concatenate([cos, cos, one], axis=1)),
            two(jnp.concatenate([-sin, zh, zero], axis=1)),
            two(jnp.concatenate([zh, sin, zero], axis=1)))


def _seg_ones(n):
    idx = np.arange(n) // HEAD_DIM
    return jnp.asarray(idx[:, None] == idx[None, :], BF16)


def _bucket_experts():
    ea, eb = [], []
    for g in range(N_GROUPS):
        for a in range(EXPERTS_PER_GROUP):
            for b in range(a + 1, EXPERTS_PER_GROUP):
                ea.append(g * EXPERTS_PER_GROUP + a)
                eb.append(g * EXPERTS_PER_GROUP + b)
    return np.asarray(ea, np.int32), np.asarray(eb, np.int32)


def _dispatch_plan(counts, n_tiles):
    padded = ((counts + MOE_TM - 1) // MOE_TM) * MOE_TM
    ends = jnp.cumsum(padded)
    offs = ends - padded
    n_act = jnp.maximum(ends[-1] // MOE_TM, 1)
    starts = jnp.arange(n_tiles, dtype=jnp.int32) * MOE_TM
    tile_bucket = jnp.minimum(jnp.sum(starts[:, None] >= ends[None, :], axis=1), N_BUCKETS - 1)
    ea, eb = _bucket_experts()
    onehot = tile_bucket[:, None] == jnp.arange(N_BUCKETS)[None, :]
    tile_a = jnp.sum(jnp.where(onehot, ea[None, :], 0), axis=1).astype(jnp.int32)
    tile_b = jnp.sum(jnp.where(onehot, eb[None, :], 0), axis=1).astype(jnp.int32)
    return offs, tile_a, tile_b, n_act.astype(jnp.int32).reshape(1)


def _positions(offs, bucket, rank):
    onehot = bucket[..., None] == jnp.arange(N_BUCKETS, dtype=jnp.int32)
    return (jnp.sum(jnp.where(onehot, offs.astype(jnp.int32), 0), axis=-1) + rank).astype(jnp.int32)


def kernel(x_prompt, x_sample, cache_k, cache_v, state_conv, meta_tokens, norm_mix, w_in, conv_w, q_norm, k_norm,
           attn_sinks, w_conv_out, w_attn_out, w_o, norm_ffn, w_router_group, b_router_group, w_router_expert,
           b_router_expert, w_exp_gate, w_exp_up, w_exp_down):
    batch, seq, _ = x_prompt.shape
    depth = w_in.shape[0]
    n_dec = x_sample.shape[0]
    past_len = PAST_LEN
    lp = PAD + N_META + seq
    tm_in, tm_out, qb = 640, 640, 5
    assert lp % tm_in == 0 and (batch * lp) % tm_out == 0 and lp % (qb * BLOCK) == 0
    assert x_sample.shape[1] == 1 and cache_k.shape[2] == WINDOW and past_len >= WINDOW

    t_prompt = batch * lp
    t_all = t_prompt + n_dec
    n_tiles = -(-(t_all + N_BUCKETS * (MOE_TM - 1)) // MOE_TM)

    front = jnp.zeros((batch, PAD, D_MODEL), F32)
    meta = jnp.broadcast_to(meta_tokens[None].astype(F32), (batch, N_META, D_MODEL))
    xp = jnp.concatenate([front, meta, x_prompt], axis=1)
    xs = x_sample.reshape(n_dec, D_MODEL)

    rope_p = _rope_tables(jnp.arange(lp, dtype=jnp.int32) - PAD)
    rope_s = _rope_tables(jnp.full((1,), past_len, jnp.int32))
    s256, s128 = _seg_ones(256), _seg_ones(LANES)
    tri_p = jnp.asarray(np.triu(np.ones((tm_out, tm_out)), 1), BF16)
    tri_s = jnp.asarray(np.triu(np.ones((n_dec, n_dec)), 1), BF16)
    zero_cnt = jnp.zeros((LANES, LANES), F32)

    outs = {k: [] for k in ("kp", "vp", "cp", "ks", "vs", "cs")}
    for l in range(depth):
        w_mix = w_in[l, :, :D_MIX].astype(BF16)
        w_gate = w_in[l, :, D_MIX:].astype(BF16)
        wco, wao, wo = w_conv_out[l].astype(BF16), w_attn_out[l].astype(BF16), w_o[l].astype(BF16)
        wg, wu, wd = w_exp_gate[l].astype(BF16), w_exp_up[l].astype(BF16), w_exp_down[l].astype(BF16)
        nm, nf = norm_mix[l].reshape(1, D_MODEL), norm_ffn[l].reshape(1, D_MODEL)
        qn = jnp.tile(q_norm[l], N_HEADS).reshape(1, D_Q)
        kn = jnp.tile(k_norm[l], N_KV_HEADS).reshape(1, D_KV)
        wr = jnp.zeros((D_MODEL, LANES), F32)
        wr = wr.at[:, :N_GROUPS].set(w_router_group[l]).at[:, N_GROUPS:N_GROUPS + N_EXPERTS].set(w_router_expert[l])
        br = jnp.zeros((1, LANES), F32)
        br = br.at[0, :N_GROUPS].set(b_router_group[l]).at[0, N_GROUPS:N_GROUPS + N_EXPERTS].set(b_router_expert[l])
        wr = wr.astype(BF16)
        sinks = attn_sinks[l].astype(F32)

        cy, q, k, v, ulast = _prompt_in(xp, nm, w_mix, conv_w[l], qn, kn, rope_p, s256, s128, tm_in)
        ao = _prompt_attn(q, k, v, sinks, qb)
        xep, rankp, bktp, cnt = _mix_out(
            xp.reshape(t_prompt, D_MODEL), cy.reshape(t_prompt, D_CONV), ao.reshape(t_prompt, D_Q),
            nm, w_gate, wco, wao, wo, nf, wr, br, tri_p, zero_cnt, tm_out)
        outs["kp"].append(k[:, lp - WINDOW:].reshape(batch, WINDOW, N_KV_HEADS, HEAD_DIM))
        outs["vp"].append(v[:, lp - WINDOW:].reshape(batch, WINDOW, N_KV_HEADS, HEAD_DIM))
        outs["cp"].append(ulast[:, 8 - (CONV_W - 1):])

        c0, c1 = state_conv[l, :, 0, :], state_conv[l, :, 1, :]
        cys, qx, ksn, vsn, us = _sample_in(xs, c0, c1, nm, w_mix, conv_w[l], qn, kn, rope_s, s256, s128)
        sinkb = jnp.broadcast_to(sinks[:, None], (N_HEADS, LANES))
        ox, nk, nv = _sample_attn(jnp.transpose(qx, (1, 0, 2)), sinkb,
                                  cache_k[l].reshape(n_dec, WINDOW, D_KV), cache_v[l].reshape(n_dec, WINDOW, D_KV),
                                  ksn, vsn)
        ox = ox.reshape(n_dec, N_KV_HEADS, GQA, N_KV_HEADS, HEAD_DIM)
        aos = jnp.stack([ox[:, j, :, j, :] for j in range(N_KV_HEADS)], axis=1).reshape(n_dec, D_Q).astype(BF16)
        xes, ranks, bkts, cnt = _mix_out(xs, cys, aos, nm, w_gate, wco, wao, wo, nf, wr, br, tri_s, cnt, n_dec)
        outs["ks"].append(nk.reshape(n_dec, WINDOW, N_KV_HEADS, HEAD_DIM))
        outs["vs"].append(nv.reshape(n_dec, WINDOW, N_KV_HEADS, HEAD_DIM))
        outs["cs"].append(jnp.stack([c1, us], axis=1))

        counts = cnt[:N_BUCKETS, 0].astype(jnp.int32)
        offs, tile_a, tile_b, n_act = _dispatch_plan(counts, n_tiles)
        posp = _positions(offs, bktp, rankp)
        poss = _positions(offs, bkts, ranks)
        sorted_rows = jnp.zeros((n_tiles * MOE_TM, ROW_W), F32)
        sorted_rows = _scatter_rows(posp, xep, sorted_rows, tm_out)
        sorted_rows = _scatter_rows(poss, xes, sorted_rows, n_dec)
        ys = _moe_experts(tile_a, tile_b, n_act, sorted_rows, nf, wg, wu, wd)
        xp = _unpermute(posp, ys, tm_out).reshape(batch, lp, D_MODEL)
        xs = _unpermute(poss, ys, n_dec)

    y_prompt = xp[:, PAD + N_META:]
    y_sample = xs.reshape(n_dec, 1, D_MODEL)
    st = lambda k: jnp.stack(outs[k])
    return (y_prompt, y_sample, st("kp"), st("vp"), st("cp"), st("ks"), st("vs"), st("cs"))
```

```python
import functools

import jax
import jax.numpy as jnp
import numpy as np
from jax import lax
from jax.experimental import pallas as pl
from jax.experimental.pallas import tpu as pltpu

D_MODEL = 1024
N_META = 16
D_CONV = D_MODEL
CONV_W = 3
N_HEADS = 16
N_KV_HEADS = 2
HEAD_DIM = 64
GQA = N_HEADS // N_KV_HEADS
ROT_DIM = HEAD_DIM // 4
ROPE_THETA = 500000.0
WINDOW = 128
PAST_LEN = 8192
BLOCK = 128
N_GROUPS = 4
EXPERTS_PER_GROUP = 4
N_EXPERTS = N_GROUPS * EXPERTS_PER_GROUP
D_EXPERT = 512
EPS = 1e-6
NEG = -1e30
D_Q = N_HEADS * HEAD_DIM
D_KV = N_KV_HEADS * HEAD_DIM
C_B, C_C, C_HC = 0, D_CONV, 2 * D_CONV
C_Q = 3 * D_CONV
C_K = C_Q + D_Q
C_V = C_K + D_KV
C_G = C_V + D_KV
D_MIX = C_G
D_IN = C_G + 2 * D_MODEL

LANES = 128
PAD = (-N_META) % BLOCK
N_PAIRS = 6
N_BUCKETS = N_GROUPS * N_PAIRS
MOE_TM = 256
ROW_W = D_MODEL + LANES

F32 = jnp.float32
BF16 = jnp.bfloat16
VMEM_LIMIT = 56 * 1024 * 1024


def _cp(sem, vmem=VMEM_LIMIT):
    return pltpu.CompilerParams(dimension_semantics=sem, vmem_limit_bytes=vmem)


def _const_spec(shape):
    nd = len(shape)
    return pl.BlockSpec(shape, lambda *_: (0,) * nd, pipeline_mode=pl.Buffered(1))


def _layer_spec(l, *shape):
    n = len(shape)
    return pl.BlockSpec((None,) + shape, lambda *_: (l,) + (0,) * n, pipeline_mode=pl.Buffered(1))


def _dot(a, b):
    return jnp.dot(a, b, preferred_element_type=F32)


def _seg_mean_sq(x, seg_ones):
    return _dot((x * x).astype(BF16), seg_ones) * (1.0 / HEAD_DIM)


def _rope128(t, cos, sneg, spos):
    return t * cos + pltpu.roll(t, LANES - ROT_DIM // 2, 1) * sneg + pltpu.roll(t, ROT_DIM // 2, 1) * spos


def _rms_rows(x, g):
    ms = jnp.mean(x * x, axis=-1, keepdims=True)
    return (x * lax.rsqrt(ms + EPS)) * g


def _qk_heads(hb, w_ref, qn, kn, cos, sneg, spos, s256, s128, store_q):
    for c in range(D_Q // 256):
        qc = _dot(hb, w_ref[:, C_Q + c * 256:C_Q + (c + 1) * 256])
        qc = (qc * lax.rsqrt(_seg_mean_sq(qc, s256) + EPS)) * qn[:, c * 256:(c + 1) * 256]
        for s in range(2):
            r = _rope128(qc[:, s * LANES:(s + 1) * LANES], cos, sneg, spos)
            store_q(2 * c + s, (r * (HEAD_DIM ** -0.5)).astype(BF16))
    kc = _dot(hb, w_ref[:, C_K:C_K + D_KV])
    kc = (kc * lax.rsqrt(_seg_mean_sq(kc, s128) + EPS)) * kn
    return _rope128(kc, cos, sneg, spos)


def _in_body(x_ref, nm_ref, w_ref, cw_ref, qn_ref, kn_ref, cos_ref, sneg_ref, spos_ref,
             s256_ref, s128_ref, cy_ref, q_ref, k_ref, v_ref, ul_ref, us_ref, *, tm):
    i = pl.program_id(1)
    hb = _rms_rows(x_ref[0], nm_ref[...]).astype(BF16)

    u = _dot(hb, w_ref[:, C_C:C_C + D_CONV]) * _dot(hb, w_ref[:, C_HC:C_HC + D_CONV])
    row = lax.broadcasted_iota(jnp.int32, (tm, 1), 0) + i * tm
    u = jnp.where(row >= PAD, u, 0.0)

    @pl.when(i == 0)
    def _():
        us_ref[0:8, :] = jnp.zeros((8, D_CONV), F32)

    us_ref[8:8 + tm, :] = u
    conv = (us_ref[6:6 + tm, :] * cw_ref[0:1, :] + us_ref[7:7 + tm, :] * cw_ref[1:2, :]) + u * cw_ref[2:3, :]
    cy_ref[0] = (_dot(hb, w_ref[:, C_B:C_B + D_CONV]) * conv).astype(BF16)
    last = us_ref[tm:tm + 8, :]
    ul_ref[0] = last
    us_ref[0:8, :] = last

    def store_q(slab, val):
        q_ref[0, :, slab * LANES:(slab + 1) * LANES] = val

    k_ref[0] = _qk_heads(hb, w_ref, qn_ref[...], kn_ref[...], cos_ref[...], sneg_ref[...],
                         spos_ref[...], s256_ref[...], s128_ref[...], store_q)
    v_ref[0] = _dot(hb, w_ref[:, C_V:C_V + D_KV])


def _prompt_in(l, x, nm, w_mix, cw, qn, kn, rope, s256, s128, tm):
    b, lp, _ = x.shape
    nt = lp // tm
    cos, sneg, spos = rope
    tok = lambda w: pl.BlockSpec((1, tm, w), lambda bi, i: (bi, i, 0))
    tab = pl.BlockSpec((tm, LANES), lambda bi, i: (i, 0))
    return pl.pallas_call(
        functools.partial(_in_body, tm=tm),
        grid=(b, nt),
        in_specs=[tok(D_MODEL), _layer_spec(l, 1, D_MODEL), _layer_spec(l, D_MODEL, D_MIX),
                  _layer_spec(l, CONV_W, D_CONV), _layer_spec(l, 1, D_Q), _layer_spec(l, 1, D_KV),
                  tab, tab, tab, _const_spec((256, 256)), _const_spec((LANES, LANES))],
        out_specs=[tok(D_CONV), tok(D_Q), tok(D_KV), tok(D_KV),
                   pl.BlockSpec((1, 8, D_CONV), lambda bi, i: (bi, 0, 0))],
        out_shape=[jax.ShapeDtypeStruct((b, lp, D_CONV), BF16), jax.ShapeDtypeStruct((b, lp, D_Q), BF16),
                   jax.ShapeDtypeStruct((b, lp, D_KV), F32), jax.ShapeDtypeStruct((b, lp, D_KV), F32),
                   jax.ShapeDtypeStruct((b, 8, D_CONV), F32)],
        scratch_shapes=[pltpu.VMEM((tm + 8, D_CONV), F32)],
        compiler_params=_cp(("arbitrary", "arbitrary")),
        name="prompt_in",
    )(x, nm, w_mix, cw, qn, kn, cos, sneg, spos, s256, s128)


def _sin_body(x_ref, c0_ref, c1_ref, nm_ref, w_ref, cw_ref, qn_ref, kn_ref, cos_ref, sneg_ref,
              spos_ref, s256_ref, s128_ref, cy_ref, qx_ref, k_ref, v_ref, u_ref):
    hb = _rms_rows(x_ref[...], nm_ref[...]).astype(BF16)
    u = _dot(hb, w_ref[:, C_C:C_C + D_CONV]) * _dot(hb, w_ref[:, C_HC:C_HC + D_CONV])
    u_ref[...] = u
    conv = (c0_ref[...] * cw_ref[0:1, :] + c1_ref[...] * cw_ref[1:2, :]) + u * cw_ref[2:3, :]
    cy_ref[...] = (_dot(hb, w_ref[:, C_B:C_B + D_CONV]) * conv).astype(BF16)

    lane = lax.broadcasted_iota(jnp.int32, (x_ref.shape[0], LANES), 1)
    low = lane < HEAD_DIM

    def store_q(slab, val):
        valf = val.astype(F32)
        swapped = pltpu.roll(valf, HEAD_DIM, 1)
        zero = jnp.zeros_like(valf)
        for h in (2 * slab, 2 * slab + 1):
            src = valf if (h % 2) == (h // GQA) else swapped
            keep = low if (h // GQA) == 0 else jnp.logical_not(low)
            qx_ref[h] = jnp.where(keep, src, zero).astype(BF16)

    cos = jnp.broadcast_to(cos_ref[...], (x_ref.shape[0], LANES))
    sneg = jnp.broadcast_to(sneg_ref[...], (x_ref.shape[0], LANES))
    spos = jnp.broadcast_to(spos_ref[...], (x_ref.shape[0], LANES))
    k_ref[...] = _qk_heads(hb, w_ref, qn_ref[...], kn_ref[...], cos, sneg, spos,
                           s256_ref[...], s128_ref[...], store_q)
    v_ref[...] = _dot(hb, w_ref[:, C_V:C_V + D_KV])


def _sample_in(l, x, c0, c1, nm, w_mix, cw, qn, kn, rope, s256, s128):
    n = x.shape[0]
    cos, sneg, spos = rope
    full = lambda *s: pl.BlockSpec(s, lambda i: (0,) * len(s))
    return pl.pallas_call(
        _sin_body,
        grid=(1,),
        in_specs=[full(n, D_MODEL), full(n, D_CONV), full(n, D_CONV), _layer_spec(l, 1, D_MODEL),
                  _layer_spec(l, D_MODEL, D_MIX), _layer_spec(l, CONV_W, D_CONV), _layer_spec(l, 1, D_Q),
                  _layer_spec(l, 1, D_KV), full(1, LANES), full(1, LANES), full(1, LANES), full(256, 256),
                  full(LANES, LANES)],
        out_specs=[full(n, D_CONV), full(N_HEADS, n, LANES), full(n, D_KV), full(n, D_KV), full(n, D_CONV)],
        out_shape=[jax.ShapeDtypeStruct((n, D_CONV), BF16), jax.ShapeDtypeStruct((N_HEADS, n, LANES), BF16),
                   jax.ShapeDtypeStruct((n, D_KV), F32), jax.ShapeDtypeStruct((n, D_KV), F32),
                   jax.ShapeDtypeStruct((n, D_CONV), F32)],
        compiler_params=_cp(("arbitrary",)),
        name="sample_in",
    )(x, c0, c1, nm, w_mix, cw, qn, kn, cos, sneg, spos, s256, s128)


def _attn_body(sink_ref, q_ref, kp_ref, kc_ref, vp_ref, vc_ref, o_ref, ke_ref, ko_ref, vt_ref, *, qb, l):
    i = pl.program_id(1)
    lane = lax.broadcasted_iota(jnp.int32, (BLOCK, LANES), 1)
    low = lane < HEAD_DIM

    def prep_k(src, blk0, nblk):
        for t in range(nblk):
            blk = src[0, t * BLOCK:(t + 1) * BLOCK, :]
            swp = pltpu.roll(blk, HEAD_DIM, 1)
            zero = jnp.zeros_like(blk)
            rows = slice((blk0 + t) * BLOCK, (blk0 + t + 1) * BLOCK)
            ke_ref[0, rows, :] = jnp.where(low, blk, zero).astype(BF16)
            ko_ref[0, rows, :] = jnp.where(low, zero, swp).astype(BF16)
            ke_ref[1, rows, :] = jnp.where(low, swp, zero).astype(BF16)
            ko_ref[1, rows, :] = jnp.where(low, zero, blk).astype(BF16)

    def prep_v(src, blk0, nblk):
        for t in range(nblk):
            vt = jnp.transpose(src[0, t * BLOCK:(t + 1) * BLOCK, :]).astype(BF16)
            for j in range(N_KV_HEADS):
                vt_ref[blk0 + t, j] = vt[j * HEAD_DIM:(j + 1) * HEAD_DIM, :]

    prep_k(kp_ref, 0, 1)
    prep_k(kc_ref, 1, qb)
    prep_v(vp_ref, 0, 1)
    prep_v(vc_ref, 1, qb)

    c = lax.broadcasted_iota(jnp.int32, (2 * BLOCK, BLOCK), 0)
    r = lax.broadcasted_iota(jnp.int32, (2 * BLOCK, BLOCK), 1)
    diff = r - (c - BLOCK)
    band = (diff >= 0) & (diff < WINDOW)
    nt = (((1,), (1,)), ((), ()))

    def one_block(b, carry):
        r0 = pl.multiple_of(b * BLOCK, BLOCK)
        kpos = (i * qb + b) * BLOCK + c - BLOCK - PAD
        valid = band & (kpos >= 0)
        for m in range(N_HEADS // 2):
            j = (2 * m) // GQA
            q2 = q_ref[0, pl.ds(r0, BLOCK), m * LANES:(m + 1) * LANES]
            halves = []
            for par, k_ref in ((0, ke_ref), (1, ko_ref)):
                st = lax.dot_general(k_ref[j, pl.ds(r0, 2 * BLOCK), :], q2, nt, preferred_element_type=F32)
                st = jnp.where(valid, st, NEG)
                sink = sink_ref[l, 2 * m + par]
                mx = jnp.maximum(jnp.max(st, axis=0, keepdims=True), sink)
                p = jnp.exp(st - mx)
                den = jnp.sum(p, axis=0, keepdims=True) + jnp.exp(sink - mx)
                pb = p.astype(BF16)
                ot = _dot(vt_ref[b, j], pb[0:BLOCK]) + _dot(vt_ref[b + 1, j], pb[BLOCK:])
                halves.append(ot * (1.0 / den))
            o2 = jnp.transpose(jnp.concatenate(halves, axis=0))
            o_ref[0, pl.ds(r0, BLOCK), m * LANES:(m + 1) * LANES] = o2.astype(BF16)
        return carry

    lax.fori_loop(0, qb, one_block, 0)


def _prompt_attn(l, q, k, v, sinks, qb):
    b, lp, _ = q.shape
    nsteps = lp // (qb * BLOCK)
    cur = lambda w: pl.BlockSpec((1, qb * BLOCK, w), lambda bi, i: (bi, i, 0))
    prev = pl.BlockSpec((1, BLOCK, D_KV), lambda bi, i: (bi, jnp.maximum(i * qb - 1, 0), 0))
    ext = ((qb + 1) * BLOCK, LANES)
    return pl.pallas_call(
        functools.partial(_attn_body, qb=qb, l=l),
        grid=(b, nsteps),
        in_specs=[pl.BlockSpec(memory_space=pltpu.SMEM), cur(D_Q), prev, cur(D_KV), prev, cur(D_KV)],
        out_specs=cur(D_Q),
        out_shape=jax.ShapeDtypeStruct((b, lp, D_Q), BF16),
        scratch_shapes=[pltpu.VMEM((N_KV_HEADS,) + ext, BF16), pltpu.VMEM((N_KV_HEADS,) + ext, BF16),
                        pltpu.VMEM((qb + 1, N_KV_HEADS, HEAD_DIM, BLOCK), BF16)],
        compiler_params=_cp(("arbitrary", "arbitrary")),
        name="prompt_attn",
    )(sinks, q, k, k, v, v)


def _sattn_body(qx_ref, sink_ref, ck_ref, cv_ref, kn_ref, vn_ref, ox_ref, nk_ref, nv_ref, *, tb):
    for t in range(tb):
        nk_ref[t, 0:WINDOW - 1, :] = ck_ref[t, 1:WINDOW, :]
        nk_ref[t, WINDOW - 1:WINDOW, :] = kn_ref[t:t + 1, :]
        nv_ref[t, 0:WINDOW - 1, :] = cv_ref[t, 1:WINDOW, :]
        nv_ref[t, WINDOW - 1:WINDOW, :] = vn_ref[t:t + 1, :]
        s = lax.dot_general(qx_ref[t], nk_ref[t].astype(BF16), (((1,), (1,)), ((), ())),
                            preferred_element_type=F32)
        sink = sink_ref[...][:, 0:1]
        m = jnp.maximum(jnp.max(s, axis=-1, keepdims=True), sink)
        p = jnp.exp(s - m)
        den = jnp.sum(p, axis=-1, keepdims=True) + jnp.exp(sink - m)
        ox_ref[t] = _dot(p.astype(BF16), nv_ref[t].astype(BF16)) * (1.0 / den)


def _sample_attn(l, qx, sinkb, ck, cv, kn, vn, tb=16):
    n = qx.shape[0]
    blk3 = lambda a, c: pl.BlockSpec((tb, a, c), lambda i: (i, 0, 0))
    cache = pl.BlockSpec((None, tb, WINDOW, D_KV), lambda i: (l, i, 0, 0))
    row = pl.BlockSpec((tb, D_KV), lambda i: (i, 0))
    return pl.pallas_call(
        functools.partial(_sattn_body, tb=tb),
        grid=(n // tb,),
        in_specs=[blk3(N_HEADS, LANES), _layer_spec(l, N_HEADS, LANES), cache, cache, row, row],
        out_specs=[blk3(N_HEADS, LANES), blk3(WINDOW, D_KV), blk3(WINDOW, D_KV)],
        out_shape=[jax.ShapeDtypeStruct((n, N_HEADS, LANES), F32),
                   jax.ShapeDtypeStruct((n, WINDOW, D_KV), F32), jax.ShapeDtypeStruct((n, WINDOW, D_KV), F32)],
        compiler_params=_cp(("arbitrary",)),
        name="sample_attn",
    )(qx, sinkb, ck, cv, kn, vn)


def _out_body(x_ref, cy_ref, ao_ref, nm_ref, wg_ref, wco_ref, wao_ref, wo_ref, nf_ref, wr_ref, br_ref,
              tri_ref, cin_ref, xe_ref, rank_ref, bkt_ref, cnt_ref, run_ref, *, tm):
    i = pl.program_id(0)
    x = x_ref[...]
    hb = _rms_rows(x, nm_ref[...]).astype(BF16)
    ya = _dot(cy_ref[...], wco_ref[...])
    yb = _dot(ao_ref[...], wao_ref[...])
    mix = jax.nn.sigmoid(_dot(hb, wg_ref[:, 0:D_MODEL])) * ya + jax.nn.sigmoid(_dot(hb, wg_ref[:, D_MODEL:])) * yb
    x1 = x + _dot(mix.astype(BF16), wo_ref[...])
    xe_ref[:, 0:D_MODEL] = x1

    xnb = _rms_rows(x1, nf_ref[...]).astype(BF16)
    logits = _dot(xnb, wr_ref[...]) + br_ref[...]

    lanef = lax.broadcasted_iota(jnp.int32, (tm, LANES), 1).astype(F32)
    big = jnp.float32(3e38)
    far = jnp.float32(LANES)
    rmax = lambda a: jnp.max(a, axis=-1, keepdims=True)
    rmin = lambda a: jnp.min(a, axis=-1, keepdims=True)

    gmask = lanef < N_GROUPS
    gl = jnp.where(gmask, logits, -big)
    gmax = rmax(gl)
    grp = rmin(jnp.where(gmask & (gl == gmax), lanef, far))
    p_grp = 1.0 / jnp.sum(jnp.where(gmask, jnp.exp(gl - gmax), 0.0), axis=-1, keepdims=True)

    e_lo = N_GROUPS + EXPERTS_PER_GROUP * grp
    emask = (lanef >= e_lo) & (lanef < e_lo + EXPERTS_PER_GROUP)
    el = jnp.where(emask, logits, -big)
    v1 = rmax(el)
    i1 = rmin(jnp.where(emask & (el == v1), lanef, far))
    rest = emask & (lanef != i1)
    el2 = jnp.where(rest, logits, -big)
    v2 = rmax(el2)
    i2 = rmin(jnp.where(rest & (el2 == v2), lanef, far))
    e = jnp.exp(v2 - v1)
    w1 = (1.0 / (1.0 + e)) * p_grp
    w2 = (e / (1.0 + e)) * p_grp
    first_low = i1 < i2
    ea = jnp.where(first_low, i1, i2) - e_lo
    eb = jnp.where(first_low, i2, i1) - e_lo
    w_a = jnp.where(first_low, w1, w2)
    w_b = jnp.where(first_low, w2, w1)
    pair = jnp.where(ea == 0.0, 0.0, jnp.where(ea == 1.0, 3.0, 5.0)) + (eb - ea - 1.0)
    bucket = grp * N_PAIRS + pair

    xe_ref[:, D_MODEL:] = jnp.where(lanef == 0.0, w_a, jnp.where(lanef == 1.0, w_b, 0.0))

    @pl.when(i == 0)
    def _():
        run_ref[...] = cin_ref[...]

    oht = jnp.transpose((lanef == bucket).astype(F32))
    before = _dot(oht.astype(BF16), tri_ref[...]) + run_ref[:, 0:1]
    rank_ref[0] = jnp.sum(oht * before, axis=0, keepdims=True).astype(jnp.int32)
    sub = lax.broadcasted_iota(jnp.int32, (LANES, tm), 0).astype(F32)
    bkt_ref[0] = jnp.sum(oht * sub, axis=0, keepdims=True).astype(jnp.int32)
    run_ref[...] = run_ref[...] + jnp.sum(oht, axis=-1, keepdims=True)
    cnt_ref[...] = run_ref[...]


def _mix_out(l, x, cy, ao, nm, w_gate, wco, wao, wo, nf, wr, br, tri, cnt_in, tm):
    t = x.shape[0]
    nt = t // tm
    tok = lambda w: pl.BlockSpec((tm, w), lambda i: (i, 0))
    rowi = pl.BlockSpec((1, 1, tm), lambda i: (i, 0, 0))
    sq = (D_MODEL, D_MODEL)
    return pl.pallas_call(
        functools.partial(_out_body, tm=tm),
        grid=(nt,),
        in_specs=[tok(D_MODEL), tok(D_CONV), tok(D_Q), _layer_spec(l, 1, D_MODEL),
                  _layer_spec(l, D_MODEL, 2 * D_MODEL), _layer_spec(l, *sq), _layer_spec(l, *sq), _layer_spec(l, *sq),
                  _layer_spec(l, 1, D_MODEL), _layer_spec(l, D_MODEL, LANES), _layer_spec(l, 1, LANES),
                  _const_spec((tm, tm)), _const_spec((LANES, LANES))],
        out_specs=[tok(ROW_W), rowi, rowi, pl.BlockSpec((LANES, LANES), lambda i: (0, 0))],
        out_shape=[jax.ShapeDtypeStruct((t, ROW_W), F32),
                   jax.ShapeDtypeStruct((nt, 1, tm), jnp.int32), jax.ShapeDtypeStruct((nt, 1, tm), jnp.int32),
                   jax.ShapeDtypeStruct((LANES, LANES), F32)],
        scratch_shapes=[pltpu.VMEM((LANES, LANES), F32)],
        compiler_params=_cp(("arbitrary",)),
        name="mix_out",
    )(x, cy, ao, nm, w_gate, wco, wao, wo, nf, wr, br, tri, cnt_in)


ROW_UNROLL = 8


def _scatter_tile_rows(pos_ref, src_ref, dst_ref, sem, tm):
    def issue(r, c):
        pltpu.make_async_copy(src_ref.at[pl.ds(r, 1), :], dst_ref.at[pl.ds(pos_ref[0, 0, r], 1), :], sem).start()
        return c

    lax.fori_loop(0, tm, issue, 0, unroll=ROW_UNROLL)
    pltpu.make_async_copy(src_ref, dst_ref.at[pl.ds(0, tm), :], sem).wait()


def _scatter_body(fill_ref, na_ref, pos_ref, src_ref, dst_ref, zero_ref, sem, *, tm, n_tiles):
    tile_rows = lambda t: dst_ref.at[pl.ds(pl.multiple_of(t * MOE_TM, MOE_TM), MOE_TM), :]

    @pl.when(pl.program_id(0) == 0)
    def _():
        zero_ref[...] = jnp.zeros(zero_ref.shape, F32)
        for b in range(N_BUCKETS):
            pltpu.make_async_copy(zero_ref, tile_rows(fill_ref[b]), sem).start()

        def tail_start(t, c):
            pltpu.make_async_copy(zero_ref, tile_rows(t), sem).start()
            return c

        lax.fori_loop(na_ref[0], n_tiles, tail_start, 0)
        for b in range(N_BUCKETS):
            pltpu.make_async_copy(zero_ref, tile_rows(fill_ref[b]), sem).wait()

        def tail_wait(t, c):
            pltpu.make_async_copy(zero_ref, tile_rows(t), sem).wait()
            return c

        lax.fori_loop(na_ref[0], n_tiles, tail_wait, 0)

    _scatter_tile_rows(pos_ref, src_ref, dst_ref, sem, tm)


def _scatter_rows(fill_tile, n_act, pos, src, n_tiles, tm):
    nt = src.shape[0] // tm
    grid_spec = pltpu.PrefetchScalarGridSpec(
        num_scalar_prefetch=2,
        grid=(nt,),
        in_specs=[pl.BlockSpec((1, 1, tm), lambda i, *_: (i, 0, 0), memory_space=pltpu.SMEM),
                  pl.BlockSpec((tm, ROW_W), lambda i, *_: (i, 0))],
        out_specs=pl.BlockSpec(memory_space=pl.ANY),
        scratch_shapes=[pltpu.VMEM((MOE_TM, ROW_W), F32), pltpu.SemaphoreType.DMA(())],
    )
    return pl.pallas_call(
        functools.partial(_scatter_body, tm=tm, n_tiles=n_tiles),
        grid_spec=grid_spec,
        out_shape=jax.ShapeDtypeStruct((n_tiles * MOE_TM, ROW_W), F32),
        compiler_params=_cp(("arbitrary",)),
        name="dispatch_scatter",
    )(fill_tile, n_act, pos, src)


def _scatter_more_body(pos_ref, src_ref, dst_in_ref, dst_ref, sem, *, tm):
    del dst_in_ref
    _scatter_tile_rows(pos_ref, src_ref, dst_ref, sem, tm)


def _scatter_more_rows(pos, src, dst, tm):
    nt = src.shape[0] // tm
    return pl.pallas_call(
        functools.partial(_scatter_more_body, tm=tm),
        grid=(nt,),
        in_specs=[pl.BlockSpec((1, 1, tm), lambda i: (i, 0, 0), memory_space=pltpu.SMEM),
                  pl.BlockSpec((tm, ROW_W), lambda i: (i, 0)),
                  pl.BlockSpec(memory_space=pl.ANY)],
        out_specs=pl.BlockSpec(memory_space=pl.ANY),
        out_shape=jax.ShapeDtypeStruct(dst.shape, dst.dtype),
        scratch_shapes=[pltpu.SemaphoreType.DMA(())],
        input_output_aliases={2: 0},
        compiler_params=_cp(("arbitrary",)),
        name="dispatch_scatter_more",
    )(pos, src, dst)


def _moe_body(ta_ref, tb_ref, na_ref, xs_ref, nf_ref, wga_ref, wua_ref, wda_ref, wgb_ref, wub_ref, wdb_ref, y_ref):
    del ta_ref, tb_ref

    @pl.when(pl.program_id(0) < na_ref[0])
    def _():
        x1 = xs_ref[:, 0:D_MODEL]
        xb = _rms_rows(x1, nf_ref[...]).astype(BF16)

        def expert(wg, wu, wd, gate):
            hdn = (jax.nn.silu(_dot(xb, wg[0])) * _dot(xb, wu[0])) * gate
            return _dot(hdn.astype(BF16), wd[0])

        y = expert(wga_ref, wua_ref, wda_ref, xs_ref[:, D_MODEL:D_MODEL + 1])
        y = y + expert(wgb_ref, wub_ref, wdb_ref, xs_ref[:, D_MODEL + 1:D_MODEL + 2])
        y_ref[...] = x1 + y

    @pl.when(pl.program_id(0) >= na_ref[0])
    def _():
        y_ref[...] = jnp.zeros(y_ref.shape, F32)


def _moe_experts(l, tile_a, tile_b, n_act, xs, nf, wg, wu, wd):
    nt = xs.shape[0] // MOE_TM
    last = lambda i, na: jnp.minimum(i, na[0] - 1)
    expert = lambda sel, i, ta, tb, na: (l * N_EXPERTS + sel(ta, tb)[last(i, na)], 0, 0)
    w_up = lambda sel: pl.BlockSpec((1, D_MODEL, D_EXPERT), functools.partial(expert, sel))
    w_dn = lambda sel: pl.BlockSpec((1, D_EXPERT, D_MODEL), functools.partial(expert, sel))
    sa = lambda ta, tb: ta
    sb = lambda ta, tb: tb
    grid_spec = pltpu.PrefetchScalarGridSpec(
        num_scalar_prefetch=3,
        grid=(nt,),
        in_specs=[pl.BlockSpec((MOE_TM, ROW_W), lambda i, ta, tb, na: (last(i, na), 0)),
                  _layer_spec(l, 1, D_MODEL),
                  w_up(sa), w_up(sa), w_dn(sa), w_up(sb), w_up(sb), w_dn(sb)],
        out_specs=pl.BlockSpec((MOE_TM, D_MODEL), lambda i, ta, tb, na: (i, 0)),
    )
    return pl.pallas_call(
        _moe_body,
        grid_spec=grid_spec,
        out_shape=jax.ShapeDtypeStruct((xs.shape[0], D_MODEL), F32),
        compiler_params=_cp(("arbitrary",)),
        name="moe_experts",
    )(tile_a, tile_b, n_act, xs, nf, wg, wu, wd, wg, wu, wd)


def _unpermute_body(pos_ref, ys_ref, o_ref, sem, *, tm):
    def issue(r, c):
        pltpu.make_async_copy(ys_ref.at[pl.ds(pos_ref[0, 0, r], 1), :], o_ref.at[pl.ds(r, 1), :], sem).start()
        return c

    lax.fori_loop(0, tm, issue, 0, unroll=ROW_UNROLL)
    pltpu.make_async_copy(ys_ref.at[pl.ds(0, tm), :], o_ref, sem).wait()


def _unpermute(pos, ys, tm):
    t = pos.shape[0] * tm
    return pl.pallas_call(
        functools.partial(_unpermute_body, tm=tm),
        grid=(t // tm,),
        in_specs=[pl.BlockSpec((1, 1, tm), lambda i: (i, 0, 0), memory_space=pltpu.SMEM),
                  pl.BlockSpec(memory_space=pl.ANY)],
        out_specs=pl.BlockSpec((tm, D_MODEL), lambda i: (i, 0)),
        out_shape=jax.ShapeDtypeStruct((t, D_MODEL), F32),
        scratch_shapes=[pltpu.SemaphoreType.DMA(())],
        compiler_params=_cp(("arbitrary",)),
        name="moe_unpermute",
    )(pos, ys)


def _rope_tables(pos):
    half = ROT_DIM // 2
    inv_freq = jnp.float32(ROPE_THETA) ** (-jnp.arange(half, dtype=jnp.float32) * (2.0 / ROT_DIM))
    dim = np.arange(LANES) % HEAD_DIM
    ang = pos.astype(jnp.float32)[:, None] * inv_freq[dim % half][None, :]
    cos, sin = jnp.cos(ang), jnp.sin(ang)
    first, second = dim < half, (dim >= half) & (dim < ROT_DIM)
    return (jnp.where(first | second, cos, 1.0), jnp.where(first, -sin, 0.0), jnp.where(second, sin, 0.0))


def _seg_ones(n):
    idx = np.arange(n) // HEAD_DIM
    return jnp.asarray(idx[:, None] == idx[None, :], BF16)


def _bucket_experts():
    ea, eb = [], []
    for g in range(N_GROUPS):
        for a in range(EXPERTS_PER_GROUP):
            for b in range(a + 1, EXPERTS_PER_GROUP):
                ea.append(g * EXPERTS_PER_GROUP + a)
                eb.append(g * EXPERTS_PER_GROUP + b)
    return np.asarray(ea, np.int32), np.asarray(eb, np.int32)


def _dispatch_plan(counts, n_tiles):
    padded = ((counts + MOE_TM - 1) // MOE_TM) * MOE_TM
    ends = jnp.cumsum(padded)
    offs = ends - padded
    n_act = jnp.maximum(ends[-1] // MOE_TM, 1)
    starts = jnp.arange(n_tiles, dtype=jnp.int32) * MOE_TM
    tile_bucket = jnp.minimum(jnp.sum(starts[:, None] >= ends[None, :], axis=1), N_BUCKETS - 1)
    ea, eb = _bucket_experts()
    onehot = tile_bucket[:, None] == jnp.arange(N_BUCKETS)[None, :]
    tile_a = jnp.sum(jnp.where(onehot, ea[None, :], 0), axis=1).astype(jnp.int32)
    tile_b = jnp.sum(jnp.where(onehot, eb[None, :], 0), axis=1).astype(jnp.int32)
    fill_tile = jnp.maximum(ends // MOE_TM - 1, 0).astype(jnp.int32)
    return offs, tile_a, tile_b, n_act.astype(jnp.int32).reshape(1), fill_tile


def _positions(offs, bucket, rank):
    onehot = bucket[..., None] == jnp.arange(N_BUCKETS, dtype=jnp.int32)
    return (jnp.sum(jnp.where(onehot, offs.astype(jnp.int32), 0), axis=-1) + rank).astype(jnp.int32)


def kernel(x_prompt, x_sample, cache_k, cache_v, state_conv, meta_tokens, norm_mix, w_in, conv_w, q_norm, k_norm,
           attn_sinks, w_conv_out, w_attn_out, w_o, norm_ffn, w_router_group, b_router_group, w_router_expert,
           b_router_expert, w_exp_gate, w_exp_up, w_exp_down):
    batch, seq, _ = x_prompt.shape
    depth = w_in.shape[0]
    n_dec = x_sample.shape[0]
    past_len = PAST_LEN
    lp = PAD + N_META + seq
    tm_in, tm_out, qb, tm_last = 640, 640, 5, 512
    assert lp % tm_in == 0 and (batch * lp) % tm_out == 0 and lp % (qb * BLOCK) == 0 and seq % tm_last == 0
    assert x_sample.shape[1] == 1 and cache_k.shape[2] == WINDOW and past_len >= WINDOW

    t_prompt = batch * lp
    t_all = t_prompt + n_dec
    n_tiles = -(-(t_all + N_BUCKETS * (MOE_TM - 1)) // MOE_TM)

    front = jnp.zeros((batch, PAD, D_MODEL), F32)
    meta = jnp.broadcast_to(meta_tokens[None].astype(F32), (batch, N_META, D_MODEL))
    xp = jnp.concatenate([front, meta, x_prompt], axis=1)
    xs = x_sample.reshape(n_dec, D_MODEL)

    rope_p = _rope_tables(jnp.arange(lp, dtype=jnp.int32) - PAD)
    rope_s = _rope_tables(jnp.full((1,), past_len, jnp.int32))
    s256, s128 = _seg_ones(256), _seg_ones(LANES)
    tri_p = jnp.asarray(np.triu(np.ones((tm_out, tm_out)), 1), BF16)
    tri_s = jnp.asarray(np.triu(np.ones((n_dec, n_dec)), 1), BF16)
    zero_cnt = jnp.zeros((LANES, LANES), F32)

    w_mix = w_in[:, :, :D_MIX].astype(BF16)
    w_gate = w_in[:, :, D_MIX:].astype(BF16)
    wco, wao, wo = w_conv_out.astype(BF16), w_attn_out.astype(BF16), w_o.astype(BF16)
    wg = w_exp_gate.astype(BF16).reshape(depth * N_EXPERTS, D_MODEL, D_EXPERT)
    wu = w_exp_up.astype(BF16).reshape(depth * N_EXPERTS, D_MODEL, D_EXPERT)
    wd = w_exp_down.astype(BF16).reshape(depth * N_EXPERTS, D_EXPERT, D_MODEL)
    nm, nf = norm_mix.reshape(depth, 1, D_MODEL), norm_ffn.reshape(depth, 1, D_MODEL)
    qn = jnp.tile(q_norm, (1, N_HEADS)).reshape(depth, 1, D_Q)
    kn = jnp.tile(k_norm, (1, N_KV_HEADS)).reshape(depth, 1, D_KV)
    r_pad = LANES - N_GROUPS - N_EXPERTS
    wr = jnp.concatenate([w_router_group, w_router_expert, jnp.zeros((depth, D_MODEL, r_pad), F32)], axis=-1).astype(BF16)
    br = jnp.concatenate([b_router_group, b_router_expert, jnp.zeros((depth, r_pad), F32)], axis=-1).reshape(depth, 1, LANES)
    sinks = attn_sinks.astype(F32)
    sinkb = jnp.broadcast_to(sinks[:, :, None], (depth, N_HEADS, LANES))
    ck = cache_k.reshape(depth, n_dec, WINDOW, D_KV)
    cv = cache_v.reshape(depth, n_dec, WINDOW, D_KV)

    outs = {k: [] for k in ("kp", "vp", "cp", "ks", "vs", "cs")}
    for l in range(depth):
        cy, q, k, v, ulast = _prompt_in(l, xp, nm, w_mix, conv_w, qn, kn, rope_p, s256, s128, tm_in)
        ao = _prompt_attn(l, q, k, v, sinks, qb)
        xep, rankp, bktp, cnt = _mix_out(
            l, xp.reshape(t_prompt, D_MODEL), cy.reshape(t_prompt, D_CONV), ao.reshape(t_prompt, D_Q),
            nm, w_gate, wco, wao, wo, nf, wr, br, tri_p, zero_cnt, tm_out)
        outs["kp"].append(k[:, lp - WINDOW:].reshape(batch, WINDOW, N_KV_HEADS, HEAD_DIM))
        outs["vp"].append(v[:, lp - WINDOW:].reshape(batch, WINDOW, N_KV_HEADS, HEAD_DIM))
        outs["cp"].append(ulast[:, 8 - (CONV_W - 1):])

        c0, c1 = state_conv[l, :, 0, :], state_conv[l, :, 1, :]
        cys, qx, ksn, vsn, us = _sample_in(l, xs, c0, c1, nm, w_mix, conv_w, qn, kn, rope_s, s256, s128)
        ox, nk, nv = _sample_attn(l, jnp.transpose(qx, (1, 0, 2)), sinkb, ck, cv, ksn, vsn)
        ox = ox.reshape(n_dec, N_KV_HEADS, GQA, N_KV_HEADS, HEAD_DIM)
        aos = jnp.stack([ox[:, j, :, j, :] for j in range(N_KV_HEADS)], axis=1).reshape(n_dec, D_Q).astype(BF16)
        xes, ranks, bkts, cnt = _mix_out(l, xs, cys, aos, nm, w_gate, wco, wao, wo, nf, wr, br, tri_s, cnt, n_dec)
        outs["ks"].append(nk.reshape(n_dec, WINDOW, N_KV_HEADS, HEAD_DIM))
        outs["vs"].append(nv.reshape(n_dec, WINDOW, N_KV_HEADS, HEAD_DIM))
        outs["cs"].append(jnp.stack([c1, us], axis=1))

        counts = cnt[:N_BUCKETS, 0].astype(jnp.int32)
        offs, tile_a, tile_b, n_act, fill_tile = _dispatch_plan(counts, n_tiles)
        posp = _positions(offs, bktp, rankp)
        poss = _positions(offs, bkts, ranks)
        sorted_rows = _scatter_rows(fill_tile, n_act, posp, xep, n_tiles, tm_out)
        sorted_rows = _scatter_more_rows(poss, xes, sorted_rows, n_dec)
        ys = _moe_experts(l, tile_a, tile_b, n_act, sorted_rows, nf, wg, wu, wd)
        xs = _unpermute(poss, ys, n_dec)
        if l + 1 < depth:
            xp = _unpermute(posp, ys, tm_out).reshape(batch, lp, D_MODEL)
        else:
            pos_tok = posp.reshape(batch, lp)[:, PAD + N_META:].reshape(batch * seq // tm_last, 1, tm_last)
            y_prompt = _unpermute(pos_tok, ys, tm_last).reshape(batch, seq, D_MODEL)

    y_sample = xs.reshape(n_dec, 1, D_MODEL)
    st = lambda k: jnp.stack(outs[k])
    return (y_prompt, y_sample, st("kp"), st("vp"), st("cp"), st("ks"), st("vs"), st("cs"))
```

```python
import functools

import jax
import jax.numpy as jnp
import numpy as np
from jax import lax
from jax.experimental import pallas as pl
from jax.experimental.pallas import tpu as pltpu

D_MODEL = 1024
N_META = 16
D_CONV = D_MODEL
CONV_W = 3
N_HEADS = 16
N_KV_HEADS = 2
HEAD_DIM = 64
GQA = N_HEADS // N_KV_HEADS
ROT_DIM = HEAD_DIM // 4
ROPE_THETA = 500000.0
WINDOW = 128
PAST_LEN = 8192
BLOCK = 128
N_GROUPS = 4
EXPERTS_PER_GROUP = 4
N_EXPERTS = N_GROUPS * EXPERTS_PER_GROUP
D_EXPERT = 512
EPS = 1e-6
NEG = -1e30
D_Q = N_HEADS * HEAD_DIM
D_KV = N_KV_HEADS * HEAD_DIM
C_B, C_C, C_HC = 0, D_CONV, 2 * D_CONV
C_Q = 3 * D_CONV
C_K = C_Q + D_Q
C_V = C_K + D_KV
C_G = C_V + D_KV
D_MIX = C_G
D_IN = C_G + 2 * D_MODEL

LANES = 128
PAD = (-N_META) % BLOCK
N_PAIRS = 6
N_BUCKETS = N_GROUPS * N_PAIRS
MOE_TM = 256
ROW_W = D_MODEL + LANES

F32 = jnp.float32
BF16 = jnp.bfloat16
VMEM_LIMIT = 56 * 1024 * 1024


def _cp(sem, vmem=VMEM_LIMIT):
    return pltpu.CompilerParams(dimension_semantics=sem, vmem_limit_bytes=vmem)


def _const_spec(shape):
    nd = len(shape)
    return pl.BlockSpec(shape, lambda *_: (0,) * nd, pipeline_mode=pl.Buffered(1))


def _layer_spec(l, *shape):
    n = len(shape)
    return pl.BlockSpec((None,) + shape, lambda *_: (l,) + (0,) * n, pipeline_mode=pl.Buffered(1))


def _dot(a, b):
    return jnp.dot(a, b, preferred_element_type=F32)


def _seg_mean_sq(x, seg_ones):
    return _dot((x * x).astype(BF16), seg_ones) * (1.0 / HEAD_DIM)


def _rope128(t, cos, sneg, spos):
    return t * cos + pltpu.roll(t, LANES - ROT_DIM // 2, 1) * sneg + pltpu.roll(t, ROT_DIM // 2, 1) * spos


def _rms_rows(x, g):
    ms = jnp.mean(x * x, axis=-1, keepdims=True)
    return (x * lax.rsqrt(ms + EPS)) * g


def _qk_heads(hb, w_ref, qn, kn, cos, sneg, spos, s256, s128, store_q):
    for c in range(D_Q // 256):
        qc = _dot(hb, w_ref[:, C_Q + c * 256:C_Q + (c + 1) * 256])
        qc = (qc * lax.rsqrt(_seg_mean_sq(qc, s256) + EPS)) * qn[:, c * 256:(c + 1) * 256]
        for s in range(2):
            r = _rope128(qc[:, s * LANES:(s + 1) * LANES], cos, sneg, spos)
            store_q(2 * c + s, (r * (HEAD_DIM ** -0.5)).astype(BF16))
    kc = _dot(hb, w_ref[:, C_K:C_K + D_KV])
    kc = (kc * lax.rsqrt(_seg_mean_sq(kc, s128) + EPS)) * kn
    return _rope128(kc, cos, sneg, spos)


def _in_body(x_ref, nm_ref, w_ref, cw_ref, qn_ref, kn_ref, cos_ref, sneg_ref, spos_ref,
             s256_ref, s128_ref, cy_ref, q_ref, k_ref, v_ref, ul_ref, us_ref, *, tm):
    i = pl.program_id(1)
    hb = _rms_rows(x_ref[0], nm_ref[...]).astype(BF16)

    u = _dot(hb, w_ref[:, C_C:C_C + D_CONV]) * _dot(hb, w_ref[:, C_HC:C_HC + D_CONV])
    row = lax.broadcasted_iota(jnp.int32, (tm, 1), 0) + i * tm
    u = jnp.where(row >= PAD, u, 0.0)

    @pl.when(i == 0)
    def _():
        us_ref[0:8, :] = jnp.zeros((8, D_CONV), F32)

    us_ref[8:8 + tm, :] = u
    conv = (us_ref[6:6 + tm, :] * cw_ref[0:1, :] + us_ref[7:7 + tm, :] * cw_ref[1:2, :]) + u * cw_ref[2:3, :]
    cy_ref[0] = (_dot(hb, w_ref[:, C_B:C_B + D_CONV]) * conv).astype(BF16)
    last = us_ref[tm:tm + 8, :]
    ul_ref[0] = last
    us_ref[0:8, :] = last

    def store_q(slab, val):
        q_ref[0, :, slab * LANES:(slab + 1) * LANES] = val

    k_ref[0] = _qk_heads(hb, w_ref, qn_ref[...], kn_ref[...], cos_ref[...], sneg_ref[...],
                         spos_ref[...], s256_ref[...], s128_ref[...], store_q)
    v_ref[0] = _dot(hb, w_ref[:, C_V:C_V + D_KV])


def _prompt_in(l, x, nm, w_mix, cw, qn, kn, rope, s256, s128, tm):
    b, lp, _ = x.shape
    nt = lp // tm
    cos, sneg, spos = rope
    tok = lambda w: pl.BlockSpec((1, tm, w), lambda bi, i: (bi, i, 0))
    tab = pl.BlockSpec((tm, LANES), lambda bi, i: (i, 0))
    return pl.pallas_call(
        functools.partial(_in_body, tm=tm),
        grid=(b, nt),
        in_specs=[tok(D_MODEL), _layer_spec(l, 1, D_MODEL), _layer_spec(l, D_MODEL, D_MIX),
                  _layer_spec(l, CONV_W, D_CONV), _layer_spec(l, 1, D_Q), _layer_spec(l, 1, D_KV),
                  tab, tab, tab, _const_spec((256, 256)), _const_spec((LANES, LANES))],
        out_specs=[tok(D_CONV), tok(D_Q), tok(D_KV), tok(D_KV),
                   pl.BlockSpec((1, 8, D_CONV), lambda bi, i: (bi, 0, 0))],
        out_shape=[jax.ShapeDtypeStruct((b, lp, D_CONV), BF16), jax.ShapeDtypeStruct((b, lp, D_Q), BF16),
                   jax.ShapeDtypeStruct((b, lp, D_KV), F32), jax.ShapeDtypeStruct((b, lp, D_KV), F32),
                   jax.ShapeDtypeStruct((b, 8, D_CONV), F32)],
        scratch_shapes=[pltpu.VMEM((tm + 8, D_CONV), F32)],
        compiler_params=_cp(("arbitrary", "arbitrary")),
        name="prompt_in",
    )(x, nm, w_mix, cw, qn, kn, cos, sneg, spos, s256, s128)


def _sin_body(x_ref, c0_ref, c1_ref, nm_ref, w_ref, cw_ref, qn_ref, kn_ref, cos_ref, sneg_ref,
              spos_ref, s256_ref, s128_ref, cy_ref, qx_ref, k_ref, v_ref, u_ref):
    hb = _rms_rows(x_ref[...], nm_ref[...]).astype(BF16)
    u = _dot(hb, w_ref[:, C_C:C_C + D_CONV]) * _dot(hb, w_ref[:, C_HC:C_HC + D_CONV])
    u_ref[...] = u
    conv = (c0_ref[...] * cw_ref[0:1, :] + c1_ref[...] * cw_ref[1:2, :]) + u * cw_ref[2:3, :]
    cy_ref[...] = (_dot(hb, w_ref[:, C_B:C_B + D_CONV]) * conv).astype(BF16)

    lane = lax.broadcasted_iota(jnp.int32, (x_ref.shape[0], LANES), 1)
    low = lane < HEAD_DIM

    def store_q(slab, val):
        valf = val.astype(F32)
        swapped = pltpu.roll(valf, HEAD_DIM, 1)
        zero = jnp.zeros_like(valf)
        for h in (2 * slab, 2 * slab + 1):
            src = valf if (h % 2) == (h // GQA) else swapped
            keep = low if (h // GQA) == 0 else jnp.logical_not(low)
            qx_ref[h] = jnp.where(keep, src, zero).astype(BF16)

    cos = jnp.broadcast_to(cos_ref[...], (x_ref.shape[0], LANES))
    sneg = jnp.broadcast_to(sneg_ref[...], (x_ref.shape[0], LANES))
    spos = jnp.broadcast_to(spos_ref[...], (x_ref.shape[0], LANES))
    k_ref[...] = _qk_heads(hb, w_ref, qn_ref[...], kn_ref[...], cos, sneg, spos,
                           s256_ref[...], s128_ref[...], store_q)
    v_ref[...] = _dot(hb, w_ref[:, C_V:C_V + D_KV])


def _sample_in(l, x, c0, c1, nm, w_mix, cw, qn, kn, rope, s256, s128):
    n = x.shape[0]
    cos, sneg, spos = rope
    full = lambda *s: pl.BlockSpec(s, lambda i: (0,) * len(s))
    return pl.pallas_call(
        _sin_body,
        grid=(1,),
        in_specs=[full(n, D_MODEL), full(n, D_CONV), full(n, D_CONV), _layer_spec(l, 1, D_MODEL),
                  _layer_spec(l, D_MODEL, D_MIX), _layer_spec(l, CONV_W, D_CONV), _layer_spec(l, 1, D_Q),
                  _layer_spec(l, 1, D_KV), full(1, LANES), full(1, LANES), full(1, LANES), full(256, 256),
                  full(LANES, LANES)],
        out_specs=[full(n, D_CONV), full(N_HEADS, n, LANES), full(n, D_KV), full(n, D_KV), full(n, D_CONV)],
        out_shape=[jax.ShapeDtypeStruct((n, D_CONV), BF16), jax.ShapeDtypeStruct((N_HEADS, n, LANES), BF16),
                   jax.ShapeDtypeStruct((n, D_KV), F32), jax.ShapeDtypeStruct((n, D_KV), F32),
                   jax.ShapeDtypeStruct((n, D_CONV), F32)],
        compiler_params=_cp(("arbitrary",)),
        name="sample_in",
    )(x, c0, c1, nm, w_mix, cw, qn, kn, cos, sneg, spos, s256, s128)


def _attn_body(sink_ref, q_ref, kp_ref, kc_ref, vp_ref, vc_ref, o_ref, ke_ref, ko_ref, vt_ref, *, qb, l):
    i = pl.program_id(1)
    lane = lax.broadcasted_iota(jnp.int32, (BLOCK, LANES), 1)
    low = lane < HEAD_DIM

    def prep_k(src, blk0, nblk):
        for t in range(nblk):
            blk = src[0, t * BLOCK:(t + 1) * BLOCK, :]
            swp = pltpu.roll(blk, HEAD_DIM, 1)
            zero = jnp.zeros_like(blk)
            rows = slice((blk0 + t) * BLOCK, (blk0 + t + 1) * BLOCK)
            ke_ref[0, rows, :] = jnp.where(low, blk, zero).astype(BF16)
            ko_ref[0, rows, :] = jnp.where(low, zero, swp).astype(BF16)
            ke_ref[1, rows, :] = jnp.where(low, swp, zero).astype(BF16)
            ko_ref[1, rows, :] = jnp.where(low, zero, blk).astype(BF16)

    def prep_v(src, blk0, nblk):
        for t in range(nblk):
            vt = jnp.transpose(src[0, t * BLOCK:(t + 1) * BLOCK, :]).astype(BF16)
            for j in range(N_KV_HEADS):
                vt_ref[blk0 + t, j] = vt[j * HEAD_DIM:(j + 1) * HEAD_DIM, :]

    prep_k(kp_ref, 0, 1)
    prep_k(kc_ref, 1, qb)
    prep_v(vp_ref, 0, 1)
    prep_v(vc_ref, 1, qb)

    c = lax.broadcasted_iota(jnp.int32, (2 * BLOCK, BLOCK), 0)
    r = lax.broadcasted_iota(jnp.int32, (2 * BLOCK, BLOCK), 1)
    diff = r - (c - BLOCK)
    band = (diff >= 0) & (diff < WINDOW)
    nt = (((1,), (1,)), ((), ()))

    def one_block(b, carry):
        r0 = pl.multiple_of(b * BLOCK, BLOCK)
        kpos = (i * qb + b) * BLOCK + c - BLOCK - PAD
        bias = jnp.where(band & (kpos >= 0), 0.0, NEG)
        for m in range(N_HEADS // 2):
            j = (2 * m) // GQA
            q2 = q_ref[0, pl.ds(r0, BLOCK), m * LANES:(m + 1) * LANES]
            halves = []
            for par, k_ref in ((0, ke_ref), (1, ko_ref)):
                st = lax.dot_general(k_ref[j, pl.ds(r0, 2 * BLOCK), :], q2, nt, preferred_element_type=F32)
                st = st + bias
                sink = sink_ref[l, 2 * m + par]
                mx = jnp.maximum(jnp.max(st, axis=0, keepdims=True), sink)
                p = jnp.exp(st - mx)
                den = jnp.sum(p, axis=0, keepdims=True) + jnp.exp(sink - mx)
                pb = p.astype(BF16)
                ot = _dot(vt_ref[b, j], pb[0:BLOCK]) + _dot(vt_ref[b + 1, j], pb[BLOCK:])
                halves.append(ot * (1.0 / den))
            o2 = jnp.transpose(jnp.concatenate(halves, axis=0))
            o_ref[0, pl.ds(r0, BLOCK), m * LANES:(m + 1) * LANES] = o2.astype(BF16)
        return carry

    lax.fori_loop(0, qb, one_block, 0)


def _prompt_attn(l, q, k, v, sinks, qb):
    b, lp, _ = q.shape
    nsteps = lp // (qb * BLOCK)
    cur = lambda w: pl.BlockSpec((1, qb * BLOCK, w), lambda bi, i: (bi, i, 0))
    prev = pl.BlockSpec((1, BLOCK, D_KV), lambda bi, i: (bi, jnp.maximum(i * qb - 1, 0), 0))
    ext = ((qb + 1) * BLOCK, LANES)
    return pl.pallas_call(
        functools.partial(_attn_body, qb=qb, l=l),
        grid=(b, nsteps),
        in_specs=[pl.BlockSpec(memory_space=pltpu.SMEM), cur(D_Q), prev, cur(D_KV), prev, cur(D_KV)],
        out_specs=cur(D_Q),
        out_shape=jax.ShapeDtypeStruct((b, lp, D_Q), BF16),
        scratch_shapes=[pltpu.VMEM((N_KV_HEADS,) + ext, BF16), pltpu.VMEM((N_KV_HEADS,) + ext, BF16),
                        pltpu.VMEM((qb + 1, N_KV_HEADS, HEAD_DIM, BLOCK), BF16)],
        compiler_params=_cp(("arbitrary", "arbitrary")),
        name="prompt_attn",
    )(sinks, q, k, k, v, v)


def _sattn_body(qx_ref, sink_ref, ck_ref, cv_ref, kn_ref, vn_ref, ox_ref, nk_ref, nv_ref, *, tb):
    def window(c_ref, n_ref, t):
        return jnp.concatenate([c_ref[t, 1:WINDOW, :], n_ref[t:t + 1, :]], axis=0)

    for t in range(tb):
        nk_ref[t] = window(ck_ref, kn_ref, t)
        nv_ref[t] = window(cv_ref, vn_ref, t)
    nt = (((1,), (1,)), ((), ()))
    s = jnp.concatenate([lax.dot_general(qx_ref[t], window(ck_ref, kn_ref, t).astype(BF16), nt,
                                         preferred_element_type=F32) for t in range(tb)], axis=0)
    sink = jnp.concatenate([sink_ref[...][:, 0:1]] * tb, axis=0)
    m = jnp.maximum(jnp.max(s, axis=-1, keepdims=True), sink)
    p = jnp.exp(s - m)
    rden = 1.0 / (jnp.sum(p, axis=-1, keepdims=True) + jnp.exp(sink - m))
    pb = p.astype(BF16)
    for t in range(tb):
        rows = slice(t * N_HEADS, (t + 1) * N_HEADS)
        ox_ref[t] = _dot(pb[rows], window(cv_ref, vn_ref, t).astype(BF16)) * rden[rows]


def _sample_attn(l, qx, sinkb, ck, cv, kn, vn, tb=16):
    n = qx.shape[0]
    blk3 = lambda a, c: pl.BlockSpec((tb, a, c), lambda i: (i, 0, 0))
    cache = pl.BlockSpec((None, tb, WINDOW, D_KV), lambda i: (l, i, 0, 0))
    row = pl.BlockSpec((tb, D_KV), lambda i: (i, 0))
    return pl.pallas_call(
        functools.partial(_sattn_body, tb=tb),
        grid=(n // tb,),
        in_specs=[blk3(N_HEADS, LANES), _layer_spec(l, N_HEADS, LANES), cache, cache, row, row],
        out_specs=[blk3(N_HEADS, LANES), blk3(WINDOW, D_KV), blk3(WINDOW, D_KV)],
        out_shape=[jax.ShapeDtypeStruct((n, N_HEADS, LANES), F32),
                   jax.ShapeDtypeStruct((n, WINDOW, D_KV), F32), jax.ShapeDtypeStruct((n, WINDOW, D_KV), F32)],
        compiler_params=_cp(("arbitrary",)),
        name="sample_attn",
    )(qx, sinkb, ck, cv, kn, vn)


def _out_body(x_ref, cy_ref, ao_ref, nm_ref, wg_ref, wco_ref, wao_ref, wo_ref, nf_ref, wr_ref, br_ref,
              tri_ref, cin_ref, xe_ref, rank_ref, bkt_ref, cnt_ref, run_ref, *, tm):
    i = pl.program_id(0)
    x = x_ref[...]
    hb = _rms_rows(x, nm_ref[...]).astype(BF16)
    ya = _dot(cy_ref[...], wco_ref[...])
    yb = _dot(ao_ref[...], wao_ref[...])
    mix = jax.nn.sigmoid(_dot(hb, wg_ref[:, 0:D_MODEL])) * ya + jax.nn.sigmoid(_dot(hb, wg_ref[:, D_MODEL:])) * yb
    x1 = x + _dot(mix.astype(BF16), wo_ref[...])
    xe_ref[:, 0:D_MODEL] = x1

    xnb = _rms_rows(x1, nf_ref[...]).astype(BF16)
    logits = _dot(xnb, wr_ref[...]) + br_ref[...]

    lanef = lax.broadcasted_iota(jnp.int32, (tm, LANES), 1).astype(F32)
    big = jnp.float32(3e38)
    far = jnp.float32(LANES)
    rmax = lambda a: jnp.max(a, axis=-1, keepdims=True)
    rmin = lambda a: jnp.min(a, axis=-1, keepdims=True)

    gmask = lanef < N_GROUPS
    gl = jnp.where(gmask, logits, -big)
    gmax = rmax(gl)
    grp = rmin(jnp.where(gmask & (gl == gmax), lanef, far))
    p_grp = 1.0 / jnp.sum(jnp.where(gmask, jnp.exp(gl - gmax), 0.0), axis=-1, keepdims=True)

    e_lo = N_GROUPS + EXPERTS_PER_GROUP * grp
    emask = (lanef >= e_lo) & (lanef < e_lo + EXPERTS_PER_GROUP)
    el = jnp.where(emask, logits, -big)
    v1 = rmax(el)
    i1 = rmin(jnp.where(emask & (el == v1), lanef, far))
    rest = emask & (lanef != i1)
    el2 = jnp.where(rest, logits, -big)
    v2 = rmax(el2)
    i2 = rmin(jnp.where(rest & (el2 == v2), lanef, far))
    e = jnp.exp(v2 - v1)
    w1 = (1.0 / (1.0 + e)) * p_grp
    w2 = (e / (1.0 + e)) * p_grp
    first_low = i1 < i2
    ea = jnp.where(first_low, i1, i2) - e_lo
    eb = jnp.where(first_low, i2, i1) - e_lo
    w_a = jnp.where(first_low, w1, w2)
    w_b = jnp.where(first_low, w2, w1)
    pair = jnp.where(ea == 0.0, 0.0, jnp.where(ea == 1.0, 3.0, 5.0)) + (eb - ea - 1.0)
    bucket = grp * N_PAIRS + pair

    xe_ref[:, D_MODEL:] = jnp.where(lanef == 0.0, w_a, jnp.where(lanef == 1.0, w_b, 0.0))

    @pl.when(i == 0)
    def _():
        run_ref[...] = cin_ref[...]

    oht = jnp.transpose((lanef == bucket).astype(F32))
    before = _dot(oht.astype(BF16), tri_ref[...]) + run_ref[:, 0:1]
    rank_ref[0] = jnp.sum(oht * before, axis=0, keepdims=True).astype(jnp.int32)
    sub = lax.broadcasted_iota(jnp.int32, (LANES, tm), 0).astype(F32)
    bkt_ref[0] = jnp.sum(oht * sub, axis=0, keepdims=True).astype(jnp.int32)
    run_ref[...] = run_ref[...] + jnp.sum(oht, axis=-1, keepdims=True)
    cnt_ref[...] = run_ref[...]


def _mix_out(l, x, cy, ao, nm, w_gate, wco, wao, wo, nf, wr, br, tri, cnt_in, tm):
    t = x.shape[0]
    nt = t // tm
    tok = lambda w: pl.BlockSpec((tm, w), lambda i: (i, 0))
    rowi = pl.BlockSpec((1, 1, tm), lambda i: (i, 0, 0))
    sq = (D_MODEL, D_MODEL)
    return pl.pallas_call(
        functools.partial(_out_body, tm=tm),
        grid=(nt,),
        in_specs=[tok(D_MODEL), tok(D_CONV), tok(D_Q), _layer_spec(l, 1, D_MODEL),
                  _layer_spec(l, D_MODEL, 2 * D_MODEL), _layer_spec(l, *sq), _layer_spec(l, *sq), _layer_spec(l, *sq),
                  _layer_spec(l, 1, D_MODEL), _layer_spec(l, D_MODEL, LANES), _layer_spec(l, 1, LANES),
                  _const_spec((tm, tm)), _const_spec((LANES, LANES))],
        out_specs=[tok(ROW_W), rowi, rowi, pl.BlockSpec((LANES, LANES), lambda i: (0, 0))],
        out_shape=[jax.ShapeDtypeStruct((t, ROW_W), F32),
                   jax.ShapeDtypeStruct((nt, 1, tm), jnp.int32), jax.ShapeDtypeStruct((nt, 1, tm), jnp.int32),
                   jax.ShapeDtypeStruct((LANES, LANES), F32)],
        scratch_shapes=[pltpu.VMEM((LANES, LANES), F32)],
        compiler_params=_cp(("arbitrary",)),
        name="mix_out",
    )(x, cy, ao, nm, w_gate, wco, wao, wo, nf, wr, br, tri, cnt_in)


ROW_UNROLL = 8


def _scatter_tile_rows(pos_ref, src_ref, dst_ref, sem, tm):
    def issue(r, c):
        pltpu.make_async_copy(src_ref.at[pl.ds(r, 1), :], dst_ref.at[pl.ds(pos_ref[0, 0, r], 1), :], sem).start()
        return c

    lax.fori_loop(0, tm, issue, 0, unroll=ROW_UNROLL)
    pltpu.make_async_copy(src_ref, dst_ref.at[pl.ds(0, tm), :], sem).wait()


def _scatter_body(fill_ref, na_ref, pos_ref, src_ref, dst_ref, zero_ref, sem, *, tm, n_tiles):
    tile_rows = lambda t: dst_ref.at[pl.ds(pl.multiple_of(t * MOE_TM, MOE_TM), MOE_TM), :]

    @pl.when(pl.program_id(0) == 0)
    def _():
        zero_ref[...] = jnp.zeros(zero_ref.shape, F32)
        for b in range(N_BUCKETS):
            pltpu.make_async_copy(zero_ref, tile_rows(fill_ref[b]), sem).start()

        def tail_start(t, c):
            pltpu.make_async_copy(zero_ref, tile_rows(t), sem).start()
            return c

        lax.fori_loop(na_ref[0], n_tiles, tail_start, 0)
        for b in range(N_BUCKETS):
            pltpu.make_async_copy(zero_ref, tile_rows(fill_ref[b]), sem).wait()

        def tail_wait(t, c):
            pltpu.make_async_copy(zero_ref, tile_rows(t), sem).wait()
            return c

        lax.fori_loop(na_ref[0], n_tiles, tail_wait, 0)

    _scatter_tile_rows(pos_ref, src_ref, dst_ref, sem, tm)


def _scatter_rows(fill_tile, n_act, pos, src, n_tiles, tm):
    nt = src.shape[0] // tm
    grid_spec = pltpu.PrefetchScalarGridSpec(
        num_scalar_prefetch=2,
        grid=(nt,),
        in_specs=[pl.BlockSpec((1, 1, tm), lambda i, *_: (i, 0, 0), memory_space=pltpu.SMEM),
                  pl.BlockSpec((tm, ROW_W), lambda i, *_: (i, 0))],
        out_specs=pl.BlockSpec(memory_space=pl.ANY),
        scratch_shapes=[pltpu.VMEM((MOE_TM, ROW_W), F32), pltpu.SemaphoreType.DMA(())],
    )
    return pl.pallas_call(
        functools.partial(_scatter_body, tm=tm, n_tiles=n_tiles),
        grid_spec=grid_spec,
        out_shape=jax.ShapeDtypeStruct((n_tiles * MOE_TM, ROW_W), F32),
        compiler_params=_cp(("arbitrary",)),
        name="dispatch_scatter",
    )(fill_tile, n_act, pos, src)


def _scatter_more_body(pos_ref, src_ref, dst_in_ref, dst_ref, sem, *, tm):
    del dst_in_ref
    _scatter_tile_rows(pos_ref, src_ref, dst_ref, sem, tm)


def _scatter_more_rows(pos, src, dst, tm):
    nt = src.shape[0] // tm
    return pl.pallas_call(
        functools.partial(_scatter_more_body, tm=tm),
        grid=(nt,),
        in_specs=[pl.BlockSpec((1, 1, tm), lambda i: (i, 0, 0), memory_space=pltpu.SMEM),
                  pl.BlockSpec((tm, ROW_W), lambda i: (i, 0)),
                  pl.BlockSpec(memory_space=pl.ANY)],
        out_specs=pl.BlockSpec(memory_space=pl.ANY),
        out_shape=jax.ShapeDtypeStruct(dst.shape, dst.dtype),
        scratch_shapes=[pltpu.SemaphoreType.DMA(())],
        input_output_aliases={2: 0},
        compiler_params=_cp(("arbitrary",)),
        name="dispatch_scatter_more",
    )(pos, src, dst)


def _moe_body(ta_ref, tb_ref, na_ref, xs_ref, nf_ref, wga_ref, wua_ref, wda_ref, wgb_ref, wub_ref, wdb_ref, y_ref):
    del ta_ref, tb_ref

    @pl.when(pl.program_id(0) < na_ref[0])
    def _():
        x1 = xs_ref[:, 0:D_MODEL]
        xb = _rms_rows(x1, nf_ref[...]).astype(BF16)

        def expert(wg, wu, wd, gate):
            hdn = (jax.nn.silu(_dot(xb, wg[0])) * _dot(xb, wu[0])) * gate
            return _dot(hdn.astype(BF16), wd[0])

        y = expert(wga_ref, wua_ref, wda_ref, xs_ref[:, D_MODEL:D_MODEL + 1])
        y = y + expert(wgb_ref, wub_ref, wdb_ref, xs_ref[:, D_MODEL + 1:D_MODEL + 2])
        y_ref[...] = x1 + y

    @pl.when(pl.program_id(0) >= na_ref[0])
    def _():
        y_ref[...] = jnp.zeros(y_ref.shape, F32)


def _moe_experts(l, tile_a, tile_b, n_act, xs, nf, wg, wu, wd):
    nt = xs.shape[0] // MOE_TM
    last = lambda i, na: jnp.minimum(i, na[0] - 1)
    expert = lambda sel, i, ta, tb, na: (l * N_EXPERTS + sel(ta, tb)[last(i, na)], 0, 0)
    w_up = lambda sel: pl.BlockSpec((1, D_MODEL, D_EXPERT), functools.partial(expert, sel))
    w_dn = lambda sel: pl.BlockSpec((1, D_EXPERT, D_MODEL), functools.partial(expert, sel))
    sa = lambda ta, tb: ta
    sb = lambda ta, tb: tb
    grid_spec = pltpu.PrefetchScalarGridSpec(
        num_scalar_prefetch=3,
        grid=(nt,),
        in_specs=[pl.BlockSpec((MOE_TM, ROW_W), lambda i, ta, tb, na: (last(i, na), 0)),
                  _layer_spec(l, 1, D_MODEL),
                  w_up(sa), w_up(sa), w_dn(sa), w_up(sb), w_up(sb), w_dn(sb)],
        out_specs=pl.BlockSpec((MOE_TM, D_MODEL), lambda i, ta, tb, na: (i, 0)),
    )
    return pl.pallas_call(
        _moe_body,
        grid_spec=grid_spec,
        out_shape=jax.ShapeDtypeStruct((xs.shape[0], D_MODEL), F32),
        compiler_params=_cp(("arbitrary",)),
        name="moe_experts",
    )(tile_a, tile_b, n_act, xs, nf, wg, wu, wd, wg, wu, wd)


def _unpermute_body(pos_ref, ys_ref, o_ref, sem, *, tm):
    def issue(r, c):
        pltpu.make_async_copy(ys_ref.at[pl.ds(pos_ref[0, 0, r], 1), :], o_ref.at[pl.ds(r, 1), :], sem).start()
        return c

    lax.fori_loop(0, tm, issue, 0, unroll=ROW_UNROLL)
    pltpu.make_async_copy(ys_ref.at[pl.ds(0, tm), :], o_ref, sem).wait()


def _unpermute(pos, ys, tm):
    t = pos.shape[0] * tm
    return pl.pallas_call(
        functools.partial(_unpermute_body, tm=tm),
        grid=(t // tm,),
        in_specs=[pl.BlockSpec((1, 1, tm), lambda i: (i, 0, 0), memory_space=pltpu.SMEM),
                  pl.BlockSpec(memory_space=pl.ANY)],
        out_specs=pl.BlockSpec((tm, D_MODEL), lambda i: (i, 0)),
        out_shape=jax.ShapeDtypeStruct((t, D_MODEL), F32),
        scratch_shapes=[pltpu.SemaphoreType.DMA(())],
        compiler_params=_cp(("arbitrary",)),
        name="moe_unpermute",
    )(pos, ys)


def _rope_tables(pos):
    half = ROT_DIM // 2
    inv_freq = jnp.float32(ROPE_THETA) ** (-jnp.arange(half, dtype=jnp.float32) * (2.0 / ROT_DIM))
    dim = np.arange(LANES) % HEAD_DIM
    ang = pos.astype(jnp.float32)[:, None] * inv_freq[dim % half][None, :]
    cos, sin = jnp.cos(ang), jnp.sin(ang)
    first, second = dim < half, (dim >= half) & (dim < ROT_DIM)
    return (jnp.where(first | second, cos, 1.0), jnp.where(first, -sin, 0.0), jnp.where(second, sin, 0.0))


def _seg_ones(n):
    idx = np.arange(n) // HEAD_DIM
    return jnp.asarray(idx[:, None] == idx[None, :], BF16)


def _bucket_experts():
    ea, eb = [], []
    for g in range(N_GROUPS):
        for a in range(EXPERTS_PER_GROUP):
            for b in range(a + 1, EXPERTS_PER_GROUP):
                ea.append(g * EXPERTS_PER_GROUP + a)
                eb.append(g * EXPERTS_PER_GROUP + b)
    return np.asarray(ea, np.int32), np.asarray(eb, np.int32)


def _dispatch_plan(counts, n_tiles):
    padded = ((counts + MOE_TM - 1) // MOE_TM) * MOE_TM
    ends = jnp.cumsum(padded)
    offs = ends - padded
    n_act = jnp.maximum(ends[-1] // MOE_TM, 1)
    starts = jnp.arange(n_tiles, dtype=jnp.int32) * MOE_TM
    tile_bucket = jnp.minimum(jnp.sum(starts[:, None] >= ends[None, :], axis=1), N_BUCKETS - 1)
    ea, eb = _bucket_experts()
    onehot = tile_bucket[:, None] == jnp.arange(N_BUCKETS)[None, :]
    tile_a = jnp.sum(jnp.where(onehot, ea[None, :], 0), axis=1).astype(jnp.int32)
    tile_b = jnp.sum(jnp.where(onehot, eb[None, :], 0), axis=1).astype(jnp.int32)
    fill_tile = jnp.maximum(ends // MOE_TM - 1, 0).astype(jnp.int32)
    return offs, tile_a, tile_b, n_act.astype(jnp.int32).reshape(1), fill_tile


def _positions(offs, bucket, rank):
    onehot = bucket[..., None] == jnp.arange(N_BUCKETS, dtype=jnp.int32)
    return (jnp.sum(jnp.where(onehot, offs.astype(jnp.int32), 0), axis=-1) + rank).astype(jnp.int32)


def kernel(x_prompt, x_sample, cache_k, cache_v, state_conv, meta_tokens, norm_mix, w_in, conv_w, q_norm, k_norm,
           attn_sinks, w_conv_out, w_attn_out, w_o, norm_ffn, w_router_group, b_router_group, w_router_expert,
           b_router_expert, w_exp_gate, w_exp_up, w_exp_down):
    batch, seq, _ = x_prompt.shape
    depth = w_in.shape[0]
    n_dec = x_sample.shape[0]
    past_len = PAST_LEN
    lp = PAD + N_META + seq
    tm_in, tm_out, qb = 640, 640, 5
    tm_move, tm_last = 3328, 4096
    assert lp % tm_in == 0 and (batch * lp) % tm_out == 0 and lp % (qb * BLOCK) == 0
    assert (batch * lp) % tm_move == 0 and (batch * seq) % tm_last == 0
    assert x_sample.shape[1] == 1 and cache_k.shape[2] == WINDOW and past_len >= WINDOW

    t_prompt = batch * lp
    t_all = t_prompt + n_dec
    n_tiles = -(-(t_all + N_BUCKETS * (MOE_TM - 1)) // MOE_TM)

    front = jnp.zeros((batch, PAD, D_MODEL), F32)
    meta = jnp.broadcast_to(meta_tokens[None].astype(F32), (batch, N_META, D_MODEL))
    xp = jnp.concatenate([front, meta, x_prompt], axis=1)
    xs = x_sample.reshape(n_dec, D_MODEL)

    rope_p = _rope_tables(jnp.arange(lp, dtype=jnp.int32) - PAD)
    rope_s = _rope_tables(jnp.full((1,), past_len, jnp.int32))
    s256, s128 = _seg_ones(256), _seg_ones(LANES)
    tri_p = jnp.asarray(np.triu(np.ones((tm_out, tm_out)), 1), BF16)
    tri_s = jnp.asarray(np.triu(np.ones((n_dec, n_dec)), 1), BF16)
    zero_cnt = jnp.zeros((LANES, LANES), F32)

    w_mix = w_in.astype(BF16)
    w_gate = w_mix[:, :, D_MIX:]
    wco, wao, wo = w_conv_out.astype(BF16), w_attn_out.astype(BF16), w_o.astype(BF16)
    wg = w_exp_gate.astype(BF16).reshape(depth * N_EXPERTS, D_MODEL, D_EXPERT)
    wu = w_exp_up.astype(BF16).reshape(depth * N_EXPERTS, D_MODEL, D_EXPERT)
    wd = w_exp_down.astype(BF16).reshape(depth * N_EXPERTS, D_EXPERT, D_MODEL)
    nm, nf = norm_mix.reshape(depth, 1, D_MODEL), norm_ffn.reshape(depth, 1, D_MODEL)
    qn = jnp.tile(q_norm, (1, N_HEADS)).reshape(depth, 1, D_Q)
    kn = jnp.tile(k_norm, (1, N_KV_HEADS)).reshape(depth, 1, D_KV)
    r_pad = LANES - N_GROUPS - N_EXPERTS
    wr = jnp.concatenate([w_router_group, w_router_expert, jnp.zeros((depth, D_MODEL, r_pad), F32)], axis=-1).astype(BF16)
    br = jnp.concatenate([b_router_group, b_router_expert, jnp.zeros((depth, r_pad), F32)], axis=-1).reshape(depth, 1, LANES)
    sinks = attn_sinks.astype(F32)
    sinkb = jnp.broadcast_to(sinks[:, :, None], (depth, N_HEADS, LANES))
    ck = cache_k.reshape(depth, n_dec, WINDOW, D_KV)
    cv = cache_v.reshape(depth, n_dec, WINDOW, D_KV)

    outs = {k: [] for k in ("kp", "vp", "cp", "ks", "vs", "cs")}
    for l in range(depth):
        cy, q, k, v, ulast = _prompt_in(l, xp, nm, w_mix, conv_w, qn, kn, rope_p, s256, s128, tm_in)
        ao = _prompt_attn(l, q, k, v, sinks, qb)
        xep, rankp, bktp, cnt = _mix_out(
            l, xp.reshape(t_prompt, D_MODEL), cy.reshape(t_prompt, D_CONV), ao.reshape(t_prompt, D_Q),
            nm, w_gate, wco, wao, wo, nf, wr, br, tri_p, zero_cnt, tm_out)
        outs["kp"].append(k[:, lp - WINDOW:].reshape(batch, WINDOW, N_KV_HEADS, HEAD_DIM))
        outs["vp"].append(v[:, lp - WINDOW:].reshape(batch, WINDOW, N_KV_HEADS, HEAD_DIM))
        outs["cp"].append(ulast[:, 8 - (CONV_W - 1):])

        c0, c1 = state_conv[l, :, 0, :], state_conv[l, :, 1, :]
        cys, qx, ksn, vsn, us = _sample_in(l, xs, c0, c1, nm, w_mix, conv_w, qn, kn, rope_s, s256, s128)
        ox, nk, nv = _sample_attn(l, jnp.transpose(qx, (1, 0, 2)), sinkb, ck, cv, ksn, vsn)
        ox = ox.reshape(n_dec, N_KV_HEADS, GQA, N_KV_HEADS, HEAD_DIM)
        aos = jnp.stack([ox[:, j, :, j, :] for j in range(N_KV_HEADS)], axis=1).reshape(n_dec, D_Q).astype(BF16)
        xes, ranks, bkts, cnt = _mix_out(l, xs, cys, aos, nm, w_gate, wco, wao, wo, nf, wr, br, tri_s, cnt, n_dec)
        outs["ks"].append(nk.reshape(n_dec, WINDOW, N_KV_HEADS, HEAD_DIM))
        outs["vs"].append(nv.reshape(n_dec, WINDOW, N_KV_HEADS, HEAD_DIM))
        outs["cs"].append(jnp.stack([c1, us], axis=1))

        counts = cnt[:N_BUCKETS, 0].astype(jnp.int32)
        offs, tile_a, tile_b, n_act, fill_tile = _dispatch_plan(counts, n_tiles)
        posp = _positions(offs, bktp, rankp).reshape(t_prompt // tm_move, 1, tm_move)
        poss = _positions(offs, bkts, ranks)
        sorted_rows = _scatter_rows(fill_tile, n_act, posp, xep, n_tiles, tm_move)
        sorted_rows = _scatter_more_rows(poss, xes, sorted_rows, n_dec)
        ys = _moe_experts(l, tile_a, tile_b, n_act, sorted_rows, nf, wg, wu, wd)
        xs = _unpermute(poss, ys, n_dec)
        if l + 1 < depth:
            xp = _unpermute(posp, ys, tm_move).reshape(batch, lp, D_MODEL)
        else:
            pos_tok = posp.reshape(batch, lp)[:, PAD + N_META:].reshape(batch * seq // tm_last, 1, tm_last)
            y_prompt = _unpermute(pos_tok, ys, tm_last).reshape(batch, seq, D_MODEL)

    y_sample = xs.reshape(n_dec, 1, D_MODEL)
    st = lambda k: jnp.stack(outs[k])
    return (y_prompt, y_sample, st("kp"), st("vp"), st("cp"), st("ks"), st("vs"), st("cs"))
```

```python
import functools

import jax
import jax.numpy as jnp
import numpy as np
from jax import lax
from jax.experimental import pallas as pl
from jax.experimental.pallas import tpu as pltpu

D_MODEL = 1024
N_META = 16
D_CONV = D_MODEL
CONV_W = 3
N_HEADS = 16
N_KV_HEADS = 2
HEAD_DIM = 64
GQA = N_HEADS // N_KV_HEADS
ROT_DIM = HEAD_DIM // 4
ROPE_THETA = 500000.0
WINDOW = 128
PAST_LEN = 8192
BLOCK = 128
N_GROUPS = 4
EXPERTS_PER_GROUP = 4
N_EXPERTS = N_GROUPS * EXPERTS_PER_GROUP
D_EXPERT = 512
EPS = 1e-6
NEG = -1e30
D_Q = N_HEADS * HEAD_DIM
D_KV = N_KV_HEADS * HEAD_DIM
C_B, C_C, C_HC = 0, D_CONV, 2 * D_CONV
C_Q = 3 * D_CONV
C_K = C_Q + D_Q
C_V = C_K + D_KV
C_G = C_V + D_KV
D_MIX = C_G
D_IN = C_G + 2 * D_MODEL

LANES = 128
PAD = (-N_META) % BLOCK
N_PAIRS = 6
N_BUCKETS = N_GROUPS * N_PAIRS
MOE_TM = 256
ROW_W = D_MODEL + LANES
Q_SCALE = HEAD_DIM ** -0.5
assert Q_SCALE == 0.125

F32 = jnp.float32
BF16 = jnp.bfloat16
VMEM_LIMIT = 56 * 1024 * 1024


def _cp(sem, vmem=VMEM_LIMIT):
    return pltpu.CompilerParams(dimension_semantics=sem, vmem_limit_bytes=vmem)


def _const_spec(shape):
    nd = len(shape)
    return pl.BlockSpec(shape, lambda *_: (0,) * nd, pipeline_mode=pl.Buffered(1))


def _layer_spec(l, *shape):
    n = len(shape)
    return pl.BlockSpec((None,) + shape, lambda *_: (l,) + (0,) * n, pipeline_mode=pl.Buffered(1))


def _dot(a, b):
    return jnp.dot(a, b, preferred_element_type=F32)


def _seg_mean_sq(x, seg_ones):
    return _dot((x * x).astype(BF16), seg_ones) * (1.0 / HEAD_DIM)


def _rope128(t, cos, sin_pm, first):
    partner = jnp.where(first, pltpu.roll(t, LANES - ROT_DIM // 2, 1), pltpu.roll(t, ROT_DIM // 2, 1))
    return t * cos + partner * sin_pm


def _rms_scale(x):
    return lax.rsqrt(jnp.mean(x * x, axis=-1, keepdims=True) + EPS)


def _rms_rows(x, g):
    return (x * _rms_scale(x)) * g


def _qk_project(hb, w_ref, s256, s128):
    qs = [_dot(hb, w_ref[:, C_Q + c * 256:C_Q + (c + 1) * 256]) for c in range(D_Q // 256)]
    kc = _dot(hb, w_ref[:, C_K:C_K + D_KV])
    return qs, [_seg_mean_sq(qc, s256) for qc in qs], kc, _seg_mean_sq(kc, s128)


def _qk_finish(proj, qn, kn, cos, sneg, spos, store_q):
    qs, q_ms, kc, k_ms = proj
    sin_pm = sneg + spos
    lane = lax.broadcasted_iota(jnp.int32, cos.shape, 1)
    first = lax.bitwise_and(lane, HEAD_DIM - 1) < ROT_DIM // 2
    for c, (qc, ms) in enumerate(zip(qs, q_ms)):
        qc = (qc * lax.rsqrt(ms + EPS)) * qn[:, c * 256:(c + 1) * 256]
        for s in range(2):
            r = _rope128(qc[:, s * LANES:(s + 1) * LANES], cos, sin_pm, first)
            store_q(2 * c + s, r.astype(BF16))
    kc = (kc * lax.rsqrt(k_ms + EPS)) * kn
    return _rope128(kc, cos, sin_pm, first)


def _in_body(x_ref, nm_ref, w_ref, cw_ref, qn_ref, kn_ref, cos_ref, sneg_ref, spos_ref,
             s256_ref, s128_ref, cy_ref, q_ref, k_ref, v_ref, ul_ref, us_ref, *, tm):
    i = pl.program_id(1)
    hb = _rms_rows(x_ref[0], nm_ref[...]).astype(BF16)

    proj = _qk_project(hb, w_ref, s256_ref[...], s128_ref[...])
    u = _dot(hb, w_ref[:, C_C:C_C + D_CONV]) * _dot(hb, w_ref[:, C_HC:C_HC + D_CONV])
    row = lax.broadcasted_iota(jnp.int32, (tm, 1), 0) + i * tm
    u = jnp.where(row >= PAD, u, 0.0)

    @pl.when(i == 0)
    def _():
        us_ref[0:8, :] = jnp.zeros((8, D_CONV), F32)

    us_ref[8:8 + tm, :] = u
    conv = (us_ref[6:6 + tm, :] * cw_ref[0:1, :] + us_ref[7:7 + tm, :] * cw_ref[1:2, :]) + u * cw_ref[2:3, :]
    cy_ref[0] = (_dot(hb, w_ref[:, C_B:C_B + D_CONV]) * conv).astype(BF16)
    last = us_ref[tm:tm + 8, :]
    ul_ref[0] = last
    us_ref[0:8, :] = last

    v_ref[0] = _dot(hb, w_ref[:, C_V:C_V + D_KV])

    def store_q(slab, val):
        q_ref[0, :, slab * LANES:(slab + 1) * LANES] = val

    k_ref[0] = _qk_finish(proj, qn_ref[...], kn_ref[...], cos_ref[...], sneg_ref[...], spos_ref[...], store_q)


def _prompt_in(l, x, nm, w_mix, cw, qn, kn, rope, s256, s128, tm):
    b, lp, _ = x.shape
    nt = lp // tm
    cos, sneg, spos = rope
    tok = lambda w: pl.BlockSpec((1, tm, w), lambda bi, i: (bi, i, 0))
    tab = pl.BlockSpec((tm, LANES), lambda bi, i: (i, 0))
    return pl.pallas_call(
        functools.partial(_in_body, tm=tm),
        grid=(b, nt),
        in_specs=[tok(D_MODEL), _layer_spec(l, 1, D_MODEL), _layer_spec(l, D_MODEL, D_MIX),
                  _layer_spec(l, CONV_W, D_CONV), _layer_spec(l, 1, D_Q), _layer_spec(l, 1, D_KV),
                  tab, tab, tab, _const_spec((256, 256)), _const_spec((LANES, LANES))],
        out_specs=[tok(D_CONV), tok(D_Q), tok(D_KV), tok(D_KV),
                   pl.BlockSpec((1, 8, D_CONV), lambda bi, i: (bi, 0, 0))],
        out_shape=[jax.ShapeDtypeStruct((b, lp, D_CONV), BF16), jax.ShapeDtypeStruct((b, lp, D_Q), BF16),
                   jax.ShapeDtypeStruct((b, lp, D_KV), F32), jax.ShapeDtypeStruct((b, lp, D_KV), F32),
                   jax.ShapeDtypeStruct((b, 8, D_CONV), F32)],
        scratch_shapes=[pltpu.VMEM((tm + 8, D_CONV), F32)],
        compiler_params=_cp(("arbitrary", "arbitrary")),
        name="prompt_in",
    )(x, nm, w_mix, cw, qn, kn, cos, sneg, spos, s256, s128)


def _sin_body(x_ref, c0_ref, c1_ref, nm_ref, w_ref, cw_ref, qn_ref, kn_ref, cos_ref, sneg_ref,
              spos_ref, s256_ref, s128_ref, cy_ref, qx_ref, k_ref, v_ref, u_ref):
    hb = _rms_rows(x_ref[...], nm_ref[...]).astype(BF16)
    u = _dot(hb, w_ref[:, C_C:C_C + D_CONV]) * _dot(hb, w_ref[:, C_HC:C_HC + D_CONV])
    u_ref[...] = u
    conv = (c0_ref[...] * cw_ref[0:1, :] + c1_ref[...] * cw_ref[1:2, :]) + u * cw_ref[2:3, :]
    cy_ref[...] = (_dot(hb, w_ref[:, C_B:C_B + D_CONV]) * conv).astype(BF16)

    lane = lax.broadcasted_iota(jnp.int32, (x_ref.shape[0], LANES), 1)
    low = lane < HEAD_DIM

    def store_q(slab, val):
        valf = val.astype(F32)
        swapped = pltpu.roll(valf, HEAD_DIM, 1)
        zero = jnp.zeros_like(valf)
        for h in (2 * slab, 2 * slab + 1):
            src = valf if (h % 2) == (h // GQA) else swapped
            keep = low if (h // GQA) == 0 else jnp.logical_not(low)
            qx_ref[h] = jnp.where(keep, src, zero).astype(BF16)

    cos = jnp.broadcast_to(cos_ref[...], (x_ref.shape[0], LANES))
    sneg = jnp.broadcast_to(sneg_ref[...], (x_ref.shape[0], LANES))
    spos = jnp.broadcast_to(spos_ref[...], (x_ref.shape[0], LANES))
    proj = _qk_project(hb, w_ref, s256_ref[...], s128_ref[...])
    k_ref[...] = _qk_finish(proj, qn_ref[...], kn_ref[...], cos, sneg, spos, store_q)
    v_ref[...] = _dot(hb, w_ref[:, C_V:C_V + D_KV])


def _sample_in(l, x, c0, c1, nm, w_mix, cw, qn, kn, rope, s256, s128):
    n = x.shape[0]
    cos, sneg, spos = rope
    full = lambda *s: pl.BlockSpec(s, lambda i: (0,) * len(s))
    return pl.pallas_call(
        _sin_body,
        grid=(1,),
        in_specs=[full(n, D_MODEL), full(n, D_CONV), full(n, D_CONV), _layer_spec(l, 1, D_MODEL),
                  _layer_spec(l, D_MODEL, D_MIX), _layer_spec(l, CONV_W, D_CONV), _layer_spec(l, 1, D_Q),
                  _layer_spec(l, 1, D_KV), full(1, LANES), full(1, LANES), full(1, LANES), full(256, 256),
                  full(LANES, LANES)],
        out_specs=[full(n, D_CONV), full(N_HEADS, n, LANES), full(n, D_KV), full(n, D_KV), full(n, D_CONV)],
        out_shape=[jax.ShapeDtypeStruct((n, D_CONV), BF16), jax.ShapeDtypeStruct((N_HEADS, n, LANES), BF16),
                   jax.ShapeDtypeStruct((n, D_KV), F32), jax.ShapeDtypeStruct((n, D_KV), F32),
                   jax.ShapeDtypeStruct((n, D_CONV), F32)],
        compiler_params=_cp(("arbitrary",)),
        name="sample_in",
    )(x, c0, c1, nm, w_mix, cw, qn, kn, cos, sneg, spos, s256, s128)


def _attn_body(sink_ref, q_ref, kp_ref, kc_ref, vp_ref, vc_ref, o_ref, ke_ref, ko_ref, vt_ref, *, qb, l):
    i = pl.program_id(1)
    lane = lax.broadcasted_iota(jnp.int32, (BLOCK, LANES), 1)
    low = lane < HEAD_DIM

    def prep_k(src, blk0, nblk):
        for t in range(nblk):
            blk = src[0, t * BLOCK:(t + 1) * BLOCK, :]
            swp = pltpu.roll(blk, HEAD_DIM, 1)
            zero = jnp.zeros_like(blk)
            rows = slice((blk0 + t) * BLOCK, (blk0 + t + 1) * BLOCK)
            ke_ref[0, rows, :] = jnp.where(low, blk, zero).astype(BF16)
            ko_ref[0, rows, :] = jnp.where(low, zero, swp).astype(BF16)
            ke_ref[1, rows, :] = jnp.where(low, swp, zero).astype(BF16)
            ko_ref[1, rows, :] = jnp.where(low, zero, blk).astype(BF16)

    def prep_v(src, blk0, nblk):
        for t in range(nblk):
            vt = jnp.transpose(src[0, t * BLOCK:(t + 1) * BLOCK, :]).astype(BF16)
            for j in range(N_KV_HEADS):
                vt_ref[blk0 + t, j] = vt[j * HEAD_DIM:(j + 1) * HEAD_DIM, :]

    prep_k(kp_ref, 0, 1)
    prep_k(kc_ref, 1, qb)
    prep_v(vp_ref, 0, 1)
    prep_v(vc_ref, 1, qb)

    c = lax.broadcasted_iota(jnp.int32, (2 * BLOCK, BLOCK), 0)
    r = lax.broadcasted_iota(jnp.int32, (2 * BLOCK, BLOCK), 1)
    diff = r - (c - BLOCK)
    band = (diff >= 0) & (diff < WINDOW)
    nt = (((1,), (1,)), ((), ()))

    def one_block(b, carry):
        r0 = pl.multiple_of(b * BLOCK, BLOCK)
        kpos = (i * qb + b) * BLOCK + c - BLOCK - PAD
        bias = jnp.where(band & (kpos >= 0), 0.0, NEG)
        for m in range(N_HEADS // 2):
            j = (2 * m) // GQA
            q2 = q_ref[0, pl.ds(r0, BLOCK), m * LANES:(m + 1) * LANES]
            halves = []
            for par, k_ref in ((0, ke_ref), (1, ko_ref)):
                st = lax.dot_general(k_ref[j, pl.ds(r0, 2 * BLOCK), :], q2, nt, preferred_element_type=F32)
                st = st + bias
                sink = sink_ref[l, 2 * m + par]
                mx = jnp.maximum(jnp.max(st, axis=0, keepdims=True), sink)
                p = jnp.exp(st - mx)
                den = jnp.sum(p, axis=0, keepdims=True) + jnp.exp(sink - mx)
                pb = p.astype(BF16)
                ot = _dot(vt_ref[b, j], pb[0:BLOCK]) + _dot(vt_ref[b + 1, j], pb[BLOCK:])
                halves.append(ot * (1.0 / den))
            o2 = jnp.transpose(jnp.concatenate(halves, axis=0))
            o_ref[0, pl.ds(r0, BLOCK), m * LANES:(m + 1) * LANES] = o2.astype(BF16)
        return carry

    lax.fori_loop(0, qb, one_block, 0)


def _prompt_attn(l, q, k, v, sinks, qb):
    b, lp, _ = q.shape
    nsteps = lp // (qb * BLOCK)
    cur = lambda w: pl.BlockSpec((1, qb * BLOCK, w), lambda bi, i: (bi, i, 0))
    prev = pl.BlockSpec((1, BLOCK, D_KV), lambda bi, i: (bi, jnp.maximum(i * qb - 1, 0), 0))
    ext = ((qb + 1) * BLOCK, LANES)
    return pl.pallas_call(
        functools.partial(_attn_body, qb=qb, l=l),
        grid=(b, nsteps),
        in_specs=[pl.BlockSpec(memory_space=pltpu.SMEM), cur(D_Q), prev, cur(D_KV), prev, cur(D_KV)],
        out_specs=cur(D_Q),
        out_shape=jax.ShapeDtypeStruct((b, lp, D_Q), BF16),
        scratch_shapes=[pltpu.VMEM((N_KV_HEADS,) + ext, BF16), pltpu.VMEM((N_KV_HEADS,) + ext, BF16),
                        pltpu.VMEM((qb + 1, N_KV_HEADS, HEAD_DIM, BLOCK), BF16)],
        compiler_params=_cp(("arbitrary", "arbitrary")),
        name="prompt_attn",
    )(sinks, q, k, k, v, v)


def _sattn_body(qx_ref, sink_ref, ck_ref, cv_ref, kn_ref, vn_ref, ox_ref, nk_ref, nv_ref, *, tb):
    def window(c_ref, n_ref, t):
        return jnp.concatenate([c_ref[t, 1:WINDOW, :], n_ref[t:t + 1, :]], axis=0)

    for t in range(tb):
        nk_ref[t] = window(ck_ref, kn_ref, t)
        nv_ref[t] = window(cv_ref, vn_ref, t)
    nt = (((1,), (1,)), ((), ()))
    s = jnp.concatenate([lax.dot_general(qx_ref[t], window(ck_ref, kn_ref, t).astype(BF16), nt,
                                         preferred_element_type=F32) for t in range(tb)], axis=0)
    sink = jnp.concatenate([sink_ref[...][:, 0:1]] * tb, axis=0)
    m = jnp.maximum(jnp.max(s, axis=-1, keepdims=True), sink)
    p = jnp.exp(s - m)
    rden = 1.0 / (jnp.sum(p, axis=-1, keepdims=True) + jnp.exp(sink - m))
    pb = p.astype(BF16)
    for t in range(tb):
        rows = slice(t * N_HEADS, (t + 1) * N_HEADS)
        ox_ref[t] = _dot(pb[rows], window(cv_ref, vn_ref, t).astype(BF16)) * rden[rows]


def _sample_attn(l, qx, sinkb, ck, cv, kn, vn, tb=16):
    n = qx.shape[0]
    blk3 = lambda a, c: pl.BlockSpec((tb, a, c), lambda i: (i, 0, 0))
    cache = pl.BlockSpec((None, tb, WINDOW, D_KV), lambda i: (l, i, 0, 0))
    row = pl.BlockSpec((tb, D_KV), lambda i: (i, 0))
    return pl.pallas_call(
        functools.partial(_sattn_body, tb=tb),
        grid=(n // tb,),
        in_specs=[blk3(N_HEADS, LANES), _layer_spec(l, N_HEADS, LANES), cache, cache, row, row],
        out_specs=[blk3(N_HEADS, LANES), blk3(WINDOW, D_KV), blk3(WINDOW, D_KV)],
        out_shape=[jax.ShapeDtypeStruct((n, N_HEADS, LANES), F32),
                   jax.ShapeDtypeStruct((n, WINDOW, D_KV), F32), jax.ShapeDtypeStruct((n, WINDOW, D_KV), F32)],
        compiler_params=_cp(("arbitrary",)),
        name="sample_attn",
    )(qx, sinkb, ck, cv, kn, vn)


def _out_body(x_ref, cy_ref, ao_ref, nm_ref, wg_ref, wco_ref, wao_ref, wo_ref, nf_ref, wr_ref, br_ref,
              tri_ref, cin_ref, xe_ref, rank_ref, bkt_ref, cnt_ref, run_ref, *, tm):
    i = pl.program_id(0)
    x = x_ref[...]
    hb = _rms_rows(x, nm_ref[...]).astype(BF16)
    ya = _dot(cy_ref[...], wco_ref[...])
    yb = _dot(ao_ref[...], wao_ref[...])
    mix = jax.nn.sigmoid(_dot(hb, wg_ref[:, 0:D_MODEL])) * ya + jax.nn.sigmoid(_dot(hb, wg_ref[:, D_MODEL:])) * yb
    x1 = x + _dot(mix.astype(BF16), wo_ref[...])
    xe_ref[:, 0:D_MODEL] = x1

    rinv = _rms_scale(x1)
    xnb = ((x1 * rinv) * nf_ref[...]).astype(BF16)
    logits = _dot(xnb, wr_ref[...]) + br_ref[...]

    lanef = lax.broadcasted_iota(jnp.int32, (tm, LANES), 1).astype(F32)
    big = jnp.float32(3e38)
    far = jnp.float32(LANES)
    rmax = lambda a: jnp.max(a, axis=-1, keepdims=True)
    rmin = lambda a: jnp.min(a, axis=-1, keepdims=True)

    gmask = lanef < N_GROUPS
    gl = jnp.where(gmask, logits, -big)
    gmax = rmax(gl)
    grp = rmin(jnp.where(gmask & (gl == gmax), lanef, far))
    p_grp = 1.0 / jnp.sum(jnp.where(gmask, jnp.exp(gl - gmax), 0.0), axis=-1, keepdims=True)

    e_lo = N_GROUPS + EXPERTS_PER_GROUP * grp
    emask = (lanef >= e_lo) & (lanef < e_lo + EXPERTS_PER_GROUP)
    el = jnp.where(emask, logits, -big)
    v1 = rmax(el)
    i1 = rmin(jnp.where(emask & (el == v1), lanef, far))
    rest = emask & (lanef != i1)
    el2 = jnp.where(rest, logits, -big)
    v2 = rmax(el2)
    i2 = rmin(jnp.where(rest & (el2 == v2), lanef, far))
    e = jnp.exp(v2 - v1)
    w1 = (1.0 / (1.0 + e)) * p_grp
    w2 = (e / (1.0 + e)) * p_grp
    first_low = i1 < i2
    ea = jnp.where(first_low, i1, i2) - e_lo
    eb = jnp.where(first_low, i2, i1) - e_lo
    w_a = jnp.where(first_low, w1, w2)
    w_b = jnp.where(first_low, w2, w1)
    pair = jnp.where(ea == 0.0, 0.0, jnp.where(ea == 1.0, 3.0, 5.0)) + (eb - ea - 1.0)
    bucket = grp * N_PAIRS + pair

    xe_ref[:, D_MODEL:] = jnp.where(lanef == 0.0, w_a, jnp.where(lanef == 1.0, w_b, jnp.where(lanef == 2.0, rinv, 0.0)))

    @pl.when(i == 0)
    def _():
        run_ref[...] = cin_ref[...]

    oht = jnp.transpose((lanef == bucket).astype(F32))
    before = _dot(oht.astype(BF16), tri_ref[...]) + run_ref[:, 0:1]
    rank_ref[0] = jnp.sum(oht * before, axis=0, keepdims=True).astype(jnp.int32)
    sub = lax.broadcasted_iota(jnp.int32, (LANES, tm), 0).astype(F32)
    bkt_ref[0] = jnp.sum(oht * sub, axis=0, keepdims=True).astype(jnp.int32)
    run_ref[...] = run_ref[...] + jnp.sum(oht, axis=-1, keepdims=True)
    cnt_ref[...] = run_ref[...]


def _mix_out(l, x, cy, ao, nm, w_gate, wco, wao, wo, nf, wr, br, tri, cnt_in, tm):
    t = x.shape[0]
    nt = t // tm
    tok = lambda w: pl.BlockSpec((tm, w), lambda i: (i, 0))
    rowi = pl.BlockSpec((1, 1, tm), lambda i: (i, 0, 0))
    sq = (D_MODEL, D_MODEL)
    return pl.pallas_call(
        functools.partial(_out_body, tm=tm),
        grid=(nt,),
        in_specs=[tok(D_MODEL), tok(D_CONV), tok(D_Q), _layer_spec(l, 1, D_MODEL),
                  _layer_spec(l, D_MODEL, 2 * D_MODEL), _layer_spec(l, *sq), _layer_spec(l, *sq), _layer_spec(l, *sq),
                  _layer_spec(l, 1, D_MODEL), _layer_spec(l, D_MODEL, LANES), _layer_spec(l, 1, LANES),
                  _const_spec((tm, tm)), _const_spec((LANES, LANES))],
        out_specs=[tok(ROW_W), rowi, rowi, pl.BlockSpec((LANES, LANES), lambda i: (0, 0))],
        out_shape=[jax.ShapeDtypeStruct((t, ROW_W), F32),
                   jax.ShapeDtypeStruct((nt, 1, tm), jnp.int32), jax.ShapeDtypeStruct((nt, 1, tm), jnp.int32),
                   jax.ShapeDtypeStruct((LANES, LANES), F32)],
        scratch_shapes=[pltpu.VMEM((LANES, LANES), F32)],
        compiler_params=_cp(("arbitrary",)),
        name="mix_out",
    )(x, cy, ao, nm, w_gate, wco, wao, wo, nf, wr, br, tri, cnt_in)


SUB = 8


def _wait_rows(block_ref, sem):
    pltpu.make_async_copy(block_ref, block_ref, sem).wait()


def _scatter_tile_rows(pos_ref, src_ref, dst_ref, sem, tm):
    def group(g, c):
        for u in range(SUB):
            row = dst_ref.at[pl.ds(pos_ref[0, 0, g * SUB + u], 1), :]
            pltpu.make_async_copy(src_ref.at[g, pl.ds(u, 1), :], row, sem).start()
        return c

    lax.fori_loop(0, tm // SUB, group, 0)
    _wait_rows(src_ref, sem)


def _scatter_body(fill_ref, na_ref, pos_ref, src_ref, dst_ref, zero_ref, sem, *, tm, n_tiles):
    tile_rows = lambda t: dst_ref.at[pl.ds(pl.multiple_of(t * MOE_TM, MOE_TM), MOE_TM), :]

    @pl.when(pl.program_id(0) == 0)
    def _():
        zero_ref[...] = jnp.zeros(zero_ref.shape, F32)
        for b in range(N_BUCKETS):
            pltpu.make_async_copy(zero_ref, tile_rows(fill_ref[b]), sem).start()

        def tail_start(t, c):
            pltpu.make_async_copy(zero_ref, tile_rows(t), sem).start()
            return c

        lax.fori_loop(na_ref[0], n_tiles, tail_start, 0)
        for b in range(N_BUCKETS):
            pltpu.make_async_copy(zero_ref, tile_rows(fill_ref[b]), sem).wait()

        def tail_wait(t, c):
            pltpu.make_async_copy(zero_ref, tile_rows(t), sem).wait()
            return c

        lax.fori_loop(na_ref[0], n_tiles, tail_wait, 0)

    _scatter_tile_rows(pos_ref, src_ref, dst_ref, sem, tm)


def _scatter_rows(fill_tile, n_act, pos, src, n_tiles, tm):
    nt = src.shape[0] * SUB // tm
    grid_spec = pltpu.PrefetchScalarGridSpec(
        num_scalar_prefetch=2,
        grid=(nt,),
        in_specs=[pl.BlockSpec((1, 1, tm), lambda i, *_: (i, 0, 0), memory_space=pltpu.SMEM),
                  pl.BlockSpec((tm // SUB, SUB, ROW_W), lambda i, *_: (i, 0, 0))],
        out_specs=pl.BlockSpec(memory_space=pl.ANY),
        scratch_shapes=[pltpu.VMEM((MOE_TM, ROW_W), F32), pltpu.SemaphoreType.DMA(())],
    )
    return pl.pallas_call(
        functools.partial(_scatter_body, tm=tm, n_tiles=n_tiles),
        grid_spec=grid_spec,
        out_shape=jax.ShapeDtypeStruct((n_tiles * MOE_TM, ROW_W), F32),
        compiler_params=_cp(("arbitrary",)),
        name="dispatch_scatter",
    )(fill_tile, n_act, pos, src)


def _scatter_more_body(pos_ref, src_ref, dst_in_ref, dst_ref, sem, *, tm):
    del dst_in_ref
    _scatter_tile_rows(pos_ref, src_ref, dst_ref, sem, tm)


def _scatter_more_rows(pos, src, dst, tm):
    nt = src.shape[0] * SUB // tm
    return pl.pallas_call(
        functools.partial(_scatter_more_body, tm=tm),
        grid=(nt,),
        in_specs=[pl.BlockSpec((1, 1, tm), lambda i: (i, 0, 0), memory_space=pltpu.SMEM),
                  pl.BlockSpec((tm // SUB, SUB, ROW_W), lambda i: (i, 0, 0)),
                  pl.BlockSpec(memory_space=pl.ANY)],
        out_specs=pl.BlockSpec(memory_space=pl.ANY),
        out_shape=jax.ShapeDtypeStruct(dst.shape, dst.dtype),
        scratch_shapes=[pltpu.SemaphoreType.DMA(())],
        input_output_aliases={2: 0},
        compiler_params=_cp(("arbitrary",)),
        name="dispatch_scatter_more",
    )(pos, src, dst)


def _moe_body(ta_ref, tb_ref, na_ref, xs_ref, nf_ref, wga_ref, wua_ref, wda_ref, wgb_ref, wub_ref, wdb_ref, y_ref):
    del ta_ref, tb_ref

    @pl.when(pl.program_id(0) < na_ref[0])
    def _():
        x1 = xs_ref[:, 0:D_MODEL]
        xb = ((x1 * xs_ref[:, D_MODEL + 2:D_MODEL + 3]) * nf_ref[...]).astype(BF16)

        def expert(wg, wu, wd, gate):
            hdn = (jax.nn.silu(_dot(xb, wg[0])) * _dot(xb, wu[0])) * gate
            return _dot(hdn.astype(BF16), wd[0])

        y = expert(wga_ref, wua_ref, wda_ref, xs_ref[:, D_MODEL:D_MODEL + 1])
        y = y + expert(wgb_ref, wub_ref, wdb_ref, xs_ref[:, D_MODEL + 1:D_MODEL + 2])
        y_ref[...] = x1 + y

    @pl.when(pl.program_id(0) >= na_ref[0])
    def _():
        y_ref[...] = jnp.zeros(y_ref.shape, F32)


def _moe_experts(l, tile_a, tile_b, n_act, xs, nf, wg, wu, wd):
    nt = xs.shape[0] // MOE_TM
    last = lambda i, na: jnp.minimum(i, na[0] - 1)
    expert = lambda sel, i, ta, tb, na: (l * N_EXPERTS + sel(ta, tb)[last(i, na)], 0, 0)
    w_up = lambda sel: pl.BlockSpec((1, D_MODEL, D_EXPERT), functools.partial(expert, sel))
    w_dn = lambda sel: pl.BlockSpec((1, D_EXPERT, D_MODEL), functools.partial(expert, sel))
    sa = lambda ta, tb: ta
    sb = lambda ta, tb: tb
    grid_spec = pltpu.PrefetchScalarGridSpec(
        num_scalar_prefetch=3,
        grid=(nt,),
        in_specs=[pl.BlockSpec((MOE_TM, ROW_W), lambda i, ta, tb, na: (last(i, na), 0)),
                  _layer_spec(l, 1, D_MODEL),
                  w_up(sa), w_up(sa), w_dn(sa), w_up(sb), w_up(sb), w_dn(sb)],
        out_specs=pl.BlockSpec((MOE_TM, D_MODEL), lambda i, ta, tb, na: (i, 0)),
    )
    return pl.pallas_call(
        _moe_body,
        grid_spec=grid_spec,
        out_shape=jax.ShapeDtypeStruct((xs.shape[0], D_MODEL), F32),
        compiler_params=_cp(("arbitrary",)),
        name="moe_experts",
    )(tile_a, tile_b, n_act, xs, nf, wg, wu, wd, wg, wu, wd)


def _unpermute_body(pos_ref, ys_ref, o_ref, sem, *, tm):
    def group(g, c):
        for u in range(SUB):
            row = ys_ref.at[pl.ds(pos_ref[0, 0, g * SUB + u], 1), :]
            pltpu.make_async_copy(row, o_ref.at[g, pl.ds(u, 1), :], sem).start()
        return c

    lax.fori_loop(0, tm // SUB, group, 0)
    _wait_rows(o_ref, sem)


def _unpermute(pos, ys, tm):
    t = pos.shape[0] * tm
    return pl.pallas_call(
        functools.partial(_unpermute_body, tm=tm),
        grid=(t // tm,),
        in_specs=[pl.BlockSpec((1, 1, tm), lambda i: (i, 0, 0), memory_space=pltpu.SMEM),
                  pl.BlockSpec(memory_space=pl.ANY)],
        out_specs=pl.BlockSpec((tm // SUB, SUB, D_MODEL), lambda i: (i, 0, 0)),
        out_shape=jax.ShapeDtypeStruct((t // SUB, SUB, D_MODEL), F32),
        scratch_shapes=[pltpu.SemaphoreType.DMA(())],
        compiler_params=_cp(("arbitrary",)),
        name="moe_unpermute",
    )(pos, ys)


def _rope_tables(pos):
    half = ROT_DIM // 2
    inv_freq = jnp.float32(ROPE_THETA) ** (-jnp.arange(half, dtype=jnp.float32) * (2.0 / ROT_DIM))
    dim = np.arange(LANES) % HEAD_DIM
    ang = pos.astype(jnp.float32)[:, None] * inv_freq[dim % half][None, :]
    cos, sin = jnp.cos(ang), jnp.sin(ang)
    first, second = dim < half, (dim >= half) & (dim < ROT_DIM)
    return (jnp.where(first | second, cos, 1.0), jnp.where(first, -sin, 0.0), jnp.where(second, sin, 0.0))


def _seg_ones(n):
    idx = np.arange(n) // HEAD_DIM
    return jnp.asarray(idx[:, None] == idx[None, :], BF16)


def _bucket_experts():
    ea, eb = [], []
    for g in range(N_GROUPS):
        for a in range(EXPERTS_PER_GROUP):
            for b in range(a + 1, EXPERTS_PER_GROUP):
                ea.append(g * EXPERTS_PER_GROUP + a)
                eb.append(g * EXPERTS_PER_GROUP + b)
    return np.asarray(ea, np.int32), np.asarray(eb, np.int32)


def _dispatch_plan(counts, n_tiles):
    padded = ((counts + MOE_TM - 1) // MOE_TM) * MOE_TM
    ends = jnp.cumsum(padded)
    offs = ends - padded
    n_act = jnp.maximum(ends[-1] // MOE_TM, 1)
    starts = jnp.arange(n_tiles, dtype=jnp.int32) * MOE_TM
    tile_bucket = jnp.minimum(jnp.sum(starts[:, None] >= ends[None, :], axis=1), N_BUCKETS - 1)
    ea, eb = _bucket_experts()
    onehot = tile_bucket[:, None] == jnp.arange(N_BUCKETS)[None, :]
    tile_a = jnp.sum(jnp.where(onehot, ea[None, :], 0), axis=1).astype(jnp.int32)
    tile_b = jnp.sum(jnp.where(onehot, eb[None, :], 0), axis=1).astype(jnp.int32)
    fill_tile = jnp.maximum(ends // MOE_TM - 1, 0).astype(jnp.int32)
    return offs, tile_a, tile_b, n_act.astype(jnp.int32).reshape(1), fill_tile


def _positions(offs, bucket, rank):
    onehot = bucket[..., None] == jnp.arange(N_BUCKETS, dtype=jnp.int32)
    return (jnp.sum(jnp.where(onehot, offs.astype(jnp.int32), 0), axis=-1) + rank).astype(jnp.int32)


def kernel(x_prompt, x_sample, cache_k, cache_v, state_conv, meta_tokens, norm_mix, w_in, conv_w, q_norm, k_norm,
           attn_sinks, w_conv_out, w_attn_out, w_o, norm_ffn, w_router_group, b_router_group, w_router_expert,
           b_router_expert, w_exp_gate, w_exp_up, w_exp_down):
    batch, seq, _ = x_prompt.shape
    depth = w_in.shape[0]
    n_dec = x_sample.shape[0]
    past_len = PAST_LEN
    lp = PAD + N_META + seq
    tm_in, tm_out, qb = 640, 640, 5
    tm_move, tm_last = 3328, 4096
    assert lp % tm_in == 0 and (batch * lp) % tm_out == 0 and lp % (qb * BLOCK) == 0
    assert (batch * lp) % tm_move == 0 and (batch * seq) % tm_last == 0
    assert x_sample.shape[1] == 1 and cache_k.shape[2] == WINDOW and past_len >= WINDOW

    t_prompt = batch * lp
    t_all = t_prompt + n_dec
    n_tiles = -(-(t_all + N_BUCKETS * (MOE_TM - 1)) // MOE_TM)

    front = jnp.zeros((batch, PAD, D_MODEL), F32)
    meta = jnp.broadcast_to(meta_tokens[None].astype(F32), (batch, N_META, D_MODEL))
    xp = jnp.concatenate([front, meta, x_prompt], axis=1)
    xs = x_sample.reshape(n_dec, D_MODEL)

    rope_p = _rope_tables(jnp.arange(lp, dtype=jnp.int32) - PAD)
    rope_s = _rope_tables(jnp.full((1,), past_len, jnp.int32))
    s256, s128 = _seg_ones(256), _seg_ones(LANES)
    tri_p = jnp.asarray(np.triu(np.ones((tm_out, tm_out)), 1), BF16)
    tri_s = jnp.asarray(np.triu(np.ones((n_dec, n_dec)), 1), BF16)
    zero_cnt = jnp.zeros((LANES, LANES), F32)

    w_mix = w_in.astype(BF16)
    w_gate = w_mix[:, :, D_MIX:]
    wco, wao, wo = w_conv_out.astype(BF16), w_attn_out.astype(BF16), w_o.astype(BF16)
    wg = w_exp_gate.astype(BF16).reshape(depth * N_EXPERTS, D_MODEL, D_EXPERT)
    wu = w_exp_up.astype(BF16).reshape(depth * N_EXPERTS, D_MODEL, D_EXPERT)
    wd = w_exp_down.astype(BF16).reshape(depth * N_EXPERTS, D_EXPERT, D_MODEL)
    nm, nf = norm_mix.reshape(depth, 1, D_MODEL), norm_ffn.reshape(depth, 1, D_MODEL)
    qn = (jnp.tile(q_norm, (1, N_HEADS)) * Q_SCALE).reshape(depth, 1, D_Q)
    kn = jnp.tile(k_norm, (1, N_KV_HEADS)).reshape(depth, 1, D_KV)
    r_pad = LANES - N_GROUPS - N_EXPERTS
    wr = jnp.concatenate([w_router_group, w_router_expert, jnp.zeros((depth, D_MODEL, r_pad), F32)], axis=-1).astype(BF16)
    br = jnp.concatenate([b_router_group, b_router_expert, jnp.zeros((depth, r_pad), F32)], axis=-1).reshape(depth, 1, LANES)
    sinks = attn_sinks.astype(F32)
    sinkb = jnp.broadcast_to(sinks[:, :, None], (depth, N_HEADS, LANES))
    ck = cache_k.reshape(depth, n_dec, WINDOW, D_KV)
    cv = cache_v.reshape(depth, n_dec, WINDOW, D_KV)

    outs = {k: [] for k in ("kp", "vp", "cp", "ks", "vs", "cs")}
    for l in range(depth):
        cy, q, k, v, ulast = _prompt_in(l, xp, nm, w_mix, conv_w, qn, kn, rope_p, s256, s128, tm_in)
        ao = _prompt_attn(l, q, k, v, sinks, qb)
        xep, rankp, bktp, cnt = _mix_out(
            l, xp.reshape(t_prompt, D_MODEL), cy.reshape(t_prompt, D_CONV), ao.reshape(t_prompt, D_Q),
            nm, w_gate, wco, wao, wo, nf, wr, br, tri_p, zero_cnt, tm_out)
        outs["kp"].append(k[:, lp - WINDOW:].reshape(batch, WINDOW, N_KV_HEADS, HEAD_DIM))
        outs["vp"].append(v[:, lp - WINDOW:].reshape(batch, WINDOW, N_KV_HEADS, HEAD_DIM))
        outs["cp"].append(ulast[:, 8 - (CONV_W - 1):])

        c0, c1 = state_conv[l, :, 0, :], state_conv[l, :, 1, :]
        cys, qx, ksn, vsn, us = _sample_in(l, xs, c0, c1, nm, w_mix, conv_w, qn, kn, rope_s, s256, s128)
        ox, nk, nv = _sample_attn(l, jnp.transpose(qx, (1, 0, 2)), sinkb, ck, cv, ksn, vsn)
        ox = ox.reshape(n_dec, N_KV_HEADS, GQA, N_KV_HEADS, HEAD_DIM)
        aos = jnp.stack([ox[:, j, :, j, :] for j in range(N_KV_HEADS)], axis=1).reshape(n_dec, D_Q).astype(BF16)
        xes, ranks, bkts, cnt = _mix_out(l, xs, cys, aos, nm, w_gate, wco, wao, wo, nf, wr, br, tri_s, cnt, n_dec)
        outs["ks"].append(nk.reshape(n_dec, WINDOW, N_KV_HEADS, HEAD_DIM))
        outs["vs"].append(nv.reshape(n_dec, WINDOW, N_KV_HEADS, HEAD_DIM))
        outs["cs"].append(jnp.stack([c1, us], axis=1))

        counts = cnt[:N_BUCKETS, 0].astype(jnp.int32)
        offs, tile_a, tile_b, n_act, fill_tile = _dispatch_plan(counts, n_tiles)
        posp = _positions(offs, bktp, rankp).reshape(t_prompt // tm_move, 1, tm_move)
        poss = _positions(offs, bkts, ranks)
        by8 = lambda a: a.reshape(a.shape[0] // SUB, SUB, a.shape[1])
        sorted_rows = _scatter_rows(fill_tile, n_act, posp, by8(xep), n_tiles, tm_move)
        sorted_rows = _scatter_more_rows(poss, by8(xes), sorted_rows, n_dec)
        ys = _moe_experts(l, tile_a, tile_b, n_act, sorted_rows, nf, wg, wu, wd)
        xs = _unpermute(poss, ys, n_dec).reshape(n_dec, D_MODEL)
        if l + 1 < depth:
            xp = _unpermute(posp, ys, tm_move).reshape(batch, lp, D_MODEL)
        else:
            pos_tok = posp.reshape(batch, lp)[:, PAD + N_META:].reshape(batch * seq // tm_last, 1, tm_last)
            y_prompt = _unpermute(pos_tok, ys, tm_last).reshape(batch, seq, D_MODEL)

    y_sample = xs.reshape(n_dec, 1, D_MODEL)
    st = lambda k: jnp.stack(outs[k])
    return (y_prompt, y_sample, st("kp"), st("vp"), st("cp"), st("ks"), st("vs"), st("cs"))
```

```python
import functools

import jax
import jax.numpy as jnp
import numpy as np
from jax import lax
from jax.experimental import pallas as pl
from jax.experimental.pallas import tpu as pltpu

D_MODEL = 1024
N_META = 16
D_CONV = D_MODEL
CONV_W = 3
N_HEADS = 16
N_KV_HEADS = 2
HEAD_DIM = 64
GQA = N_HEADS // N_KV_HEADS
ROT_DIM = HEAD_DIM // 4
ROPE_THETA = 500000.0
WINDOW = 128
PAST_LEN = 8192
BLOCK = 128
N_GROUPS = 4
EXPERTS_PER_GROUP = 4
N_EXPERTS = N_GROUPS * EXPERTS_PER_GROUP
D_EXPERT = 512
EPS = 1e-6
NEG = -1e30
D_Q = N_HEADS * HEAD_DIM
D_KV = N_KV_HEADS * HEAD_DIM
C_B, C_C, C_HC = 0, D_CONV, 2 * D_CONV
C_Q = 3 * D_CONV
C_K = C_Q + D_Q
C_V = C_K + D_KV
C_G = C_V + D_KV
D_MIX = C_G
D_IN = C_G + 2 * D_MODEL

LANES = 128
SUBLANES = 8
PAD = (-N_META) % BLOCK
N_PAIRS = 6
N_BUCKETS = N_GROUPS * N_PAIRS
MOE_TM = 256
ROW_W = D_MODEL + LANES
Q_SCALE = HEAD_DIM ** -0.5
assert Q_SCALE == 0.125

F32 = jnp.float32
BF16 = jnp.bfloat16
VMEM_LIMIT = 56 * 1024 * 1024


def _cp(sem, vmem=VMEM_LIMIT):
    return pltpu.CompilerParams(dimension_semantics=sem, vmem_limit_bytes=vmem)


def _const_spec(shape):
    nd = len(shape)
    return pl.BlockSpec(shape, lambda *_: (0,) * nd, pipeline_mode=pl.Buffered(1))


def _layer_spec(l, *shape):
    n = len(shape)
    return pl.BlockSpec((None,) + shape, lambda *_: (l,) + (0,) * n, pipeline_mode=pl.Buffered(1))


def _dot(a, b):
    return jnp.dot(a, b, preferred_element_type=F32)


def _seg_mean_sq(x, seg_ones):
    return _dot((x * x).astype(BF16), seg_ones) * (1.0 / HEAD_DIM)


def _rope128(t, cos, sin_pm, first):
    partner = jnp.where(first, pltpu.roll(t, LANES - ROT_DIM // 2, 1), pltpu.roll(t, ROT_DIM // 2, 1))
    return t * cos + partner * sin_pm


def _rms_scale(x):
    return lax.rsqrt(jnp.mean(x * x, axis=-1, keepdims=True) + EPS)


def _rms_rows(x, g):
    return (x * _rms_scale(x)) * g


def _qk_project(hb, w_ref, s256, s128):
    qs = [_dot(hb, w_ref[:, C_Q + c * 256:C_Q + (c + 1) * 256]) for c in range(D_Q // 256)]
    kc = _dot(hb, w_ref[:, C_K:C_K + D_KV])
    return qs, [_seg_mean_sq(qc, s256) for qc in qs], kc, _seg_mean_sq(kc, s128)


def _qk_finish(proj, qn, kn, cos, sneg, spos, store_q):
    qs, q_ms, kc, k_ms = proj
    sin_pm = sneg + spos
    lane = lax.broadcasted_iota(jnp.int32, cos.shape, 1)
    first = lax.bitwise_and(lane, HEAD_DIM - 1) < ROT_DIM // 2
    for c, (qc, ms) in enumerate(zip(qs, q_ms)):
        qc = (qc * lax.rsqrt(ms + EPS)) * qn[:, c * 256:(c + 1) * 256]
        for s in range(2):
            r = _rope128(qc[:, s * LANES:(s + 1) * LANES], cos, sin_pm, first)
            store_q(2 * c + s, r.astype(BF16))
    kc = (kc * lax.rsqrt(k_ms + EPS)) * kn
    return _rope128(kc, cos, sin_pm, first)


def _in_body(x_ref, nm_ref, w_ref, cw_ref, qn_ref, kn_ref, cos_ref, sneg_ref, spos_ref,
             s256_ref, s128_ref, cy_ref, q_ref, k_ref, v_ref, ul_ref, us_ref, *, tm, parts):
    i = pl.program_id(1)
    th = tm // parts

    @pl.when(i == 0)
    def _():
        us_ref[0:8, :] = jnp.zeros((8, D_CONV), F32)

    projs = []
    for h in range(parts):
        r0 = h * th
        hb = _rms_rows(x_ref[0, r0:r0 + th, :], nm_ref[...]).astype(BF16)
        projs.append(_qk_project(hb, w_ref, s256_ref[...], s128_ref[...]))
        u = _dot(hb, w_ref[:, C_C:C_C + D_CONV]) * _dot(hb, w_ref[:, C_HC:C_HC + D_CONV])
        row = lax.broadcasted_iota(jnp.int32, (th, 1), 0) + (i * tm + r0)
        u = jnp.where(row >= PAD, u, 0.0)
        us_ref[8 + r0:8 + r0 + th, :] = u
        conv = (us_ref[6 + r0:6 + r0 + th, :] * cw_ref[0:1, :] + us_ref[7 + r0:7 + r0 + th, :] * cw_ref[1:2, :]) \
            + u * cw_ref[2:3, :]
        cy_ref[0, r0:r0 + th, :] = (_dot(hb, w_ref[:, C_B:C_B + D_CONV]) * conv).astype(BF16)
        v_ref[0, r0:r0 + th, :] = _dot(hb, w_ref[:, C_V:C_V + D_KV])

    last = us_ref[tm:tm + 8, :]
    ul_ref[0] = last
    us_ref[0:8, :] = last

    for h in range(parts):
        rows = slice(h * th, (h + 1) * th)

        def store_q(slab, val, rows=rows):
            q_ref[0, rows, slab * LANES:(slab + 1) * LANES] = val

        k_ref[0, rows, :] = _qk_finish(projs[h], qn_ref[...], kn_ref[...], cos_ref[rows, :], sneg_ref[rows, :],
                                       spos_ref[rows, :], store_q)


def _prompt_in(l, x, nm, w_mix, cw, qn, kn, rope, s256, s128, tm):
    b, lp, _ = x.shape
    nt = lp // tm
    cos, sneg, spos = rope
    tok = lambda w: pl.BlockSpec((1, tm, w), lambda bi, i: (bi, i, 0))
    tab = pl.BlockSpec((tm, LANES), lambda bi, i: (i, 0))
    return pl.pallas_call(
        functools.partial(_in_body, tm=tm, parts=2),
        grid=(b, nt),
        in_specs=[tok(D_MODEL), _layer_spec(l, 1, D_MODEL), _layer_spec(l, D_MODEL, D_MIX),
                  _layer_spec(l, CONV_W, D_CONV), _layer_spec(l, 1, D_Q), _layer_spec(l, 1, D_KV),
                  tab, tab, tab, _const_spec((256, 256)), _const_spec((LANES, LANES))],
        out_specs=[tok(D_CONV), tok(D_Q), tok(D_KV), tok(D_KV),
                   pl.BlockSpec((1, 8, D_CONV), lambda bi, i: (bi, 0, 0))],
        out_shape=[jax.ShapeDtypeStruct((b, lp, D_CONV), BF16), jax.ShapeDtypeStruct((b, lp, D_Q), BF16),
                   jax.ShapeDtypeStruct((b, lp, D_KV), F32), jax.ShapeDtypeStruct((b, lp, D_KV), F32),
                   jax.ShapeDtypeStruct((b, 8, D_CONV), F32)],
        scratch_shapes=[pltpu.VMEM((tm + 8, D_CONV), F32)],
        compiler_params=_cp(("arbitrary", "arbitrary")),
        name="prompt_in",
    )(x, nm, w_mix, cw, qn, kn, cos, sneg, spos, s256, s128)


def _sin_body(x_ref, c0_ref, c1_ref, nm_ref, w_ref, cw_ref, qn_ref, kn_ref, cos_ref, sneg_ref,
              spos_ref, s256_ref, s128_ref, cy_ref, qx_ref, k_ref, v_ref, u_ref):
    hb = _rms_rows(x_ref[...], nm_ref[...]).astype(BF16)
    u = _dot(hb, w_ref[:, C_C:C_C + D_CONV]) * _dot(hb, w_ref[:, C_HC:C_HC + D_CONV])
    u_ref[...] = u
    conv = (c0_ref[...] * cw_ref[0:1, :] + c1_ref[...] * cw_ref[1:2, :]) + u * cw_ref[2:3, :]
    cy_ref[...] = (_dot(hb, w_ref[:, C_B:C_B + D_CONV]) * conv).astype(BF16)

    lane = lax.broadcasted_iota(jnp.int32, (x_ref.shape[0], LANES), 1)
    low = lane < HEAD_DIM

    def store_q(slab, val):
        valf = val.astype(F32)
        swapped = pltpu.roll(valf, HEAD_DIM, 1)
        zero = jnp.zeros_like(valf)
        for h in (2 * slab, 2 * slab + 1):
            src = valf if (h % 2) == (h // GQA) else swapped
            keep = low if (h // GQA) == 0 else jnp.logical_not(low)
            qx_ref[h] = jnp.where(keep, src, zero).astype(BF16)

    cos = jnp.broadcast_to(cos_ref[...], (x_ref.shape[0], LANES))
    sneg = jnp.broadcast_to(sneg_ref[...], (x_ref.shape[0], LANES))
    spos = jnp.broadcast_to(spos_ref[...], (x_ref.shape[0], LANES))
    proj = _qk_project(hb, w_ref, s256_ref[...], s128_ref[...])
    k_ref[...] = _qk_finish(proj, qn_ref[...], kn_ref[...], cos, sneg, spos, store_q)
    v_ref[...] = _dot(hb, w_ref[:, C_V:C_V + D_KV])


def _sample_in(l, x, c0, c1, nm, w_mix, cw, qn, kn, rope, s256, s128):
    n = x.shape[0]
    cos, sneg, spos = rope
    full = lambda *s: pl.BlockSpec(s, lambda i: (0,) * len(s))
    return pl.pallas_call(
        _sin_body,
        grid=(1,),
        in_specs=[full(n, D_MODEL), full(n, D_CONV), full(n, D_CONV), _layer_spec(l, 1, D_MODEL),
                  _layer_spec(l, D_MODEL, D_MIX), _layer_spec(l, CONV_W, D_CONV), _layer_spec(l, 1, D_Q),
                  _layer_spec(l, 1, D_KV), full(1, LANES), full(1, LANES), full(1, LANES), full(256, 256),
                  full(LANES, LANES)],
        out_specs=[full(n, D_CONV), full(N_HEADS, n, LANES), full(n, D_KV), full(n, D_KV), full(n, D_CONV)],
        out_shape=[jax.ShapeDtypeStruct((n, D_CONV), BF16), jax.ShapeDtypeStruct((N_HEADS, n, LANES), BF16),
                   jax.ShapeDtypeStruct((n, D_KV), F32), jax.ShapeDtypeStruct((n, D_KV), F32),
                   jax.ShapeDtypeStruct((n, D_CONV), F32)],
        compiler_params=_cp(("arbitrary",)),
        name="sample_in",
    )(x, c0, c1, nm, w_mix, cw, qn, kn, cos, sneg, spos, s256, s128)


def _attn_body(sink_ref, q_ref, kp_ref, kc_ref, vp_ref, vc_ref, o_ref, ke_ref, ko_ref, vt_ref, *, qb, l):
    i = pl.program_id(1)
    lane = lax.broadcasted_iota(jnp.int32, (BLOCK, LANES), 1)
    low = lane < HEAD_DIM

    def prep_k(src, blk0, nblk):
        for t in range(nblk):
            blk = src[0, t * BLOCK:(t + 1) * BLOCK, :]
            swp = pltpu.roll(blk, HEAD_DIM, 1)
            zero = jnp.zeros_like(blk)
            rows = slice((blk0 + t) * BLOCK, (blk0 + t + 1) * BLOCK)
            ke_ref[0, rows, :] = jnp.where(low, blk, zero).astype(BF16)
            ko_ref[0, rows, :] = jnp.where(low, zero, swp).astype(BF16)
            ke_ref[1, rows, :] = jnp.where(low, swp, zero).astype(BF16)
            ko_ref[1, rows, :] = jnp.where(low, zero, blk).astype(BF16)

    def prep_v(src, blk0, nblk):
        for t in range(nblk):
            vt = jnp.transpose(src[0, t * BLOCK:(t + 1) * BLOCK, :]).astype(BF16)
            for j in range(N_KV_HEADS):
                vt_ref[blk0 + t, j] = vt[j * HEAD_DIM:(j + 1) * HEAD_DIM, :]

    prep_k(kp_ref, 0, 1)
    prep_k(kc_ref, 1, qb)
    prep_v(vp_ref, 0, 1)
    prep_v(vc_ref, 1, qb)

    c = lax.broadcasted_iota(jnp.int32, (2 * BLOCK, BLOCK), 0)
    r = lax.broadcasted_iota(jnp.int32, (2 * BLOCK, BLOCK), 1)
    diff = r - (c - BLOCK)
    band = (diff >= 0) & (diff < WINDOW)
    nt = (((1,), (1,)), ((), ()))

    def one_block(b, carry):
        r0 = pl.multiple_of(b * BLOCK, BLOCK)
        kpos = (i * qb + b) * BLOCK + c - BLOCK - PAD
        bias = jnp.where(band & (kpos >= 0), 0.0, NEG)
        for m in range(N_HEADS // 2):
            j = (2 * m) // GQA
            q2 = q_ref[0, pl.ds(r0, BLOCK), m * LANES:(m + 1) * LANES]
            halves = []
            for par, k_ref in ((0, ke_ref), (1, ko_ref)):
                st = lax.dot_general(k_ref[j, pl.ds(r0, 2 * BLOCK), :], q2, nt, preferred_element_type=F32)
                st = st + bias
                sink = sink_ref[l, 2 * m + par]
                mx = jnp.maximum(jnp.max(st, axis=0, keepdims=True), sink)
                p = jnp.exp(st - mx)
                den = jnp.sum(p, axis=0, keepdims=True) + jnp.exp(sink - mx)
                pb = p.astype(BF16)
                ot = _dot(vt_ref[b, j], pb[0:BLOCK]) + _dot(vt_ref[b + 1, j], pb[BLOCK:])
                halves.append(ot * (1.0 / den))
            o2 = jnp.transpose(jnp.concatenate(halves, axis=0))
            o_ref[0, pl.ds(r0, BLOCK), m * LANES:(m + 1) * LANES] = o2.astype(BF16)
        return carry

    lax.fori_loop(0, qb, one_block, 0)


def _prompt_attn(l, q, k, v, sinks, qb):
    b, lp, _ = q.shape
    nsteps = lp // (qb * BLOCK)
    cur = lambda w: pl.BlockSpec((1, qb * BLOCK, w), lambda bi, i: (bi, i, 0))
    prev = pl.BlockSpec((1, BLOCK, D_KV), lambda bi, i: (bi, jnp.maximum(i * qb - 1, 0), 0))
    ext = ((qb + 1) * BLOCK, LANES)
    return pl.pallas_call(
        functools.partial(_attn_body, qb=qb, l=l),
        grid=(b, nsteps),
        in_specs=[pl.BlockSpec(memory_space=pltpu.SMEM), cur(D_Q), prev, cur(D_KV), prev, cur(D_KV)],
        out_specs=cur(D_Q),
        out_shape=jax.ShapeDtypeStruct((b, lp, D_Q), BF16),
        scratch_shapes=[pltpu.VMEM((N_KV_HEADS,) + ext, BF16), pltpu.VMEM((N_KV_HEADS,) + ext, BF16),
                        pltpu.VMEM((qb + 1, N_KV_HEADS, HEAD_DIM, BLOCK), BF16)],
        compiler_params=_cp(("arbitrary", "arbitrary")),
        name="prompt_attn",
    )(sinks, q, k, k, v, v)


def _sattn_body(qx_ref, sink_ref, ck_ref, cv_ref, kn_ref, vn_ref, ox_ref, nk_ref, nv_ref, *, tb):
    def window(c_ref, n_ref, t):
        return jnp.concatenate([c_ref[t, 1:WINDOW, :], n_ref[t:t + 1, :]], axis=0)

    for t in range(tb):
        nk_ref[t] = window(ck_ref, kn_ref, t)
        nv_ref[t] = window(cv_ref, vn_ref, t)
    nt = (((1,), (1,)), ((), ()))
    s = jnp.concatenate([lax.dot_general(qx_ref[t], window(ck_ref, kn_ref, t).astype(BF16), nt,
                                         preferred_element_type=F32) for t in range(tb)], axis=0)
    sink = jnp.concatenate([sink_ref[...][:, 0:1]] * tb, axis=0)
    m = jnp.maximum(jnp.max(s, axis=-1, keepdims=True), sink)
    p = jnp.exp(s - m)
    rden = 1.0 / (jnp.sum(p, axis=-1, keepdims=True) + jnp.exp(sink - m))
    pb = p.astype(BF16)
    for t in range(tb):
        rows = slice(t * N_HEADS, (t + 1) * N_HEADS)
        ox_ref[t] = _dot(pb[rows], window(cv_ref, vn_ref, t).astype(BF16)) * rden[rows]


def _sample_attn(l, qx, sinkb, ck, cv, kn, vn, tb=16):
    n = qx.shape[0]
    blk3 = lambda a, c: pl.BlockSpec((tb, a, c), lambda i: (i, 0, 0))
    cache = pl.BlockSpec((None, tb, WINDOW, D_KV), lambda i: (l, i, 0, 0))
    row = pl.BlockSpec((tb, D_KV), lambda i: (i, 0))
    return pl.pallas_call(
        functools.partial(_sattn_body, tb=tb),
        grid=(n // tb,),
        in_specs=[blk3(N_HEADS, LANES), _layer_spec(l, N_HEADS, LANES), cache, cache, row, row],
        out_specs=[blk3(N_HEADS, LANES), blk3(WINDOW, D_KV), blk3(WINDOW, D_KV)],
        out_shape=[jax.ShapeDtypeStruct((n, N_HEADS, LANES), F32),
                   jax.ShapeDtypeStruct((n, WINDOW, D_KV), F32), jax.ShapeDtypeStruct((n, WINDOW, D_KV), F32)],
        compiler_params=_cp(("arbitrary",)),
        name="sample_attn",
    )(qx, sinkb, ck, cv, kn, vn)


ROUTE_ROWS = 24
RINV_LANE = LANES - 1


def _route(lt):
    top = lt[0:ROUTE_ROWS, :]
    rows = top.shape[1]
    rowf = lax.broadcasted_iota(jnp.int32, top.shape, 0).astype(F32)
    big = jnp.float32(3e38)
    far = jnp.float32(LANES)
    cmax = lambda a: jnp.max(a, axis=0, keepdims=True)
    cmin = lambda a: jnp.min(a, axis=0, keepdims=True)

    gmask = rowf < N_GROUPS
    gl = jnp.where(gmask, top, -big)
    gmax = cmax(gl)
    grp = cmin(jnp.where(gmask & (gl == gmax), rowf, far))
    p_grp = 1.0 / jnp.sum(jnp.where(gmask, jnp.exp(gl - gmax), 0.0), axis=0, keepdims=True)

    e_lo = N_GROUPS + EXPERTS_PER_GROUP * grp
    emask = (rowf >= e_lo) & (rowf < e_lo + EXPERTS_PER_GROUP)
    el = jnp.where(emask, top, -big)
    v1 = cmax(el)
    i1 = cmin(jnp.where(emask & (el == v1), rowf, far))
    rest = emask & (rowf != i1)
    el2 = jnp.where(rest, top, -big)
    v2 = cmax(el2)
    i2 = cmin(jnp.where(rest & (el2 == v2), rowf, far))
    e = jnp.exp(v2 - v1)
    w1 = (1.0 / (1.0 + e)) * p_grp
    w2 = (e / (1.0 + e)) * p_grp
    first_low = i1 < i2
    ea = jnp.where(first_low, i1, i2) - e_lo
    eb = jnp.where(first_low, i2, i1) - e_lo
    w_a = jnp.where(first_low, w1, w2)
    w_b = jnp.where(first_low, w2, w1)
    pair = jnp.where(ea == 0.0, 0.0, jnp.where(ea == 1.0, 3.0, 5.0)) + (eb - ea - 1.0)
    bucket = grp * N_PAIRS + pair

    r8 = lax.broadcasted_iota(jnp.int32, (SUBLANES, rows), 0)
    head = jnp.where(r8 == 0, w_a, jnp.where(r8 == 1, w_b, jnp.where(r8 == 2, lt[RINV_LANE:RINV_LANE + 1, :], 0.0)))
    return bucket, jnp.concatenate([head, jnp.zeros((LANES - SUBLANES, rows), F32)], axis=0)


def _out_body(x_ref, cy_ref, ao_ref, nm_ref, wg_ref, wco_ref, wao_ref, wo_ref, nf_ref, wr_ref, br_ref,
              tri_ref, cin_ref, xe_ref, rank_ref, bkt_ref, cnt_ref, run_ref, *, tm):
    i = pl.program_id(0)
    x = x_ref[...]
    hb = _rms_rows(x, nm_ref[...]).astype(BF16)
    ya = _dot(cy_ref[...], wco_ref[...])
    yb = _dot(ao_ref[...], wao_ref[...])
    mix = jax.nn.sigmoid(_dot(hb, wg_ref[:, 0:D_MODEL])) * ya + jax.nn.sigmoid(_dot(hb, wg_ref[:, D_MODEL:])) * yb
    x1 = x + _dot(mix.astype(BF16), wo_ref[...])
    xe_ref[:, 0:D_MODEL] = x1

    rinv = _rms_scale(x1)
    xnb = ((x1 * rinv) * nf_ref[...]).astype(BF16)
    logits = _dot(xnb, wr_ref[...]) + br_ref[...]
    lane = lax.broadcasted_iota(jnp.int32, (tm, LANES), 1)
    bucket, meta_t = _route(jnp.transpose(jnp.where(lane == RINV_LANE, rinv, logits)))
    xe_ref[:, D_MODEL:] = jnp.transpose(meta_t)

    @pl.when(i == 0)
    def _():
        run_ref[...] = cin_ref[...]

    sub = lax.broadcasted_iota(jnp.int32, (LANES, tm), 0).astype(F32)
    oht = (sub == bucket).astype(F32)
    before = _dot(oht.astype(BF16), tri_ref[...]) + run_ref[:, 0:1]
    rank_ref[0] = jnp.sum(oht * before, axis=0, keepdims=True).astype(jnp.int32)
    bkt_ref[0] = bucket.astype(jnp.int32)
    run_ref[...] = run_ref[...] + jnp.sum(oht, axis=-1, keepdims=True)
    cnt_ref[...] = run_ref[...]


def _mix_out(l, x, cy, ao, nm, w_gate, wco, wao, wo, nf, wr, br, tri, cnt_in, tm):
    t = x.shape[0]
    nt = t // tm
    tok = lambda w: pl.BlockSpec((tm, w), lambda i: (i, 0))
    rowi = pl.BlockSpec((1, 1, tm), lambda i: (i, 0, 0))
    sq = (D_MODEL, D_MODEL)
    return pl.pallas_call(
        functools.partial(_out_body, tm=tm),
        grid=(nt,),
        in_specs=[tok(D_MODEL), tok(D_CONV), tok(D_Q), _layer_spec(l, 1, D_MODEL),
                  _layer_spec(l, D_MODEL, 2 * D_MODEL), _layer_spec(l, *sq), _layer_spec(l, *sq), _layer_spec(l, *sq),
                  _layer_spec(l, 1, D_MODEL), _layer_spec(l, D_MODEL, LANES), _layer_spec(l, 1, LANES),
                  _const_spec((tm, tm)), _const_spec((LANES, LANES))],
        out_specs=[tok(ROW_W), rowi, rowi, pl.BlockSpec((LANES, LANES), lambda i: (0, 0))],
        out_shape=[jax.ShapeDtypeStruct((t, ROW_W), F32),
                   jax.ShapeDtypeStruct((nt, 1, tm), jnp.int32), jax.ShapeDtypeStruct((nt, 1, tm), jnp.int32),
                   jax.ShapeDtypeStruct((LANES, LANES), F32)],
        scratch_shapes=[pltpu.VMEM((LANES, LANES), F32)],
        compiler_params=_cp(("arbitrary",)),
        name="mix_out",
    )(x, cy, ao, nm, w_gate, wco, wao, wo, nf, wr, br, tri, cnt_in)


SUB = SUBLANES


def _wait_rows(block_ref, sem):
    pltpu.make_async_copy(block_ref, block_ref, sem).wait()


def _scatter_tile_rows(pos_ref, src_ref, dst_ref, sem, tm):
    def group(g, c):
        for u in range(SUB):
            row = dst_ref.at[pl.ds(pos_ref[0, 0, g * SUB + u], 1), :]
            pltpu.make_async_copy(src_ref.at[g, pl.ds(u, 1), :], row, sem).start()
        return c

    lax.fori_loop(0, tm // SUB, group, 0)
    _wait_rows(src_ref, sem)


def _scatter_body(fill_ref, na_ref, pos_ref, src_ref, dst_ref, zero_ref, sem, *, tm, n_tiles):
    tile_rows = lambda t: dst_ref.at[pl.ds(pl.multiple_of(t * MOE_TM, MOE_TM), MOE_TM), :]

    @pl.when(pl.program_id(0) == 0)
    def _():
        zero_ref[...] = jnp.zeros(zero_ref.shape, F32)
        for b in range(N_BUCKETS):
            pltpu.make_async_copy(zero_ref, tile_rows(fill_ref[b]), sem).start()

        def tail_start(t, c):
            pltpu.make_async_copy(zero_ref, tile_rows(t), sem).start()
            return c

        lax.fori_loop(na_ref[0], n_tiles, tail_start, 0)
        for b in range(N_BUCKETS):
            pltpu.make_async_copy(zero_ref, tile_rows(fill_ref[b]), sem).wait()

        def tail_wait(t, c):
            pltpu.make_async_copy(zero_ref, tile_rows(t), sem).wait()
            return c

        lax.fori_loop(na_ref[0], n_tiles, tail_wait, 0)

    _scatter_tile_rows(pos_ref, src_ref, dst_ref, sem, tm)


def _scatter_rows(fill_tile, n_act, pos, src, n_tiles, tm):
    nt = src.shape[0] * SUB // tm
    grid_spec = pltpu.PrefetchScalarGridSpec(
        num_scalar_prefetch=2,
        grid=(nt,),
        in_specs=[pl.BlockSpec((1, 1, tm), lambda i, *_: (i, 0, 0), memory_space=pltpu.SMEM),
                  pl.BlockSpec((tm // SUB, SUB, ROW_W), lambda i, *_: (i, 0, 0))],
        out_specs=pl.BlockSpec(memory_space=pl.ANY),
        scratch_shapes=[pltpu.VMEM((MOE_TM, ROW_W), F32), pltpu.SemaphoreType.DMA(())],
    )
    return pl.pallas_call(
        functools.partial(_scatter_body, tm=tm, n_tiles=n_tiles),
        grid_spec=grid_spec,
        out_shape=jax.ShapeDtypeStruct((n_tiles * MOE_TM, ROW_W), F32),
        compiler_params=_cp(("arbitrary",)),
        name="dispatch_scatter",
    )(fill_tile, n_act, pos, src)


def _scatter_more_body(pos_ref, src_ref, dst_in_ref, dst_ref, sem, *, tm):
    del dst_in_ref
    _scatter_tile_rows(pos_ref, src_ref, dst_ref, sem, tm)


def _scatter_more_rows(pos, src, dst, tm):
    nt = src.shape[0] * SUB // tm
    return pl.pallas_call(
        functools.partial(_scatter_more_body, tm=tm),
        grid=(nt,),
        in_specs=[pl.BlockSpec((1, 1, tm), lambda i: (i, 0, 0), memory_space=pltpu.SMEM),
                  pl.BlockSpec((tm // SUB, SUB, ROW_W), lambda i: (i, 0, 0)),
                  pl.BlockSpec(memory_space=pl.ANY)],
        out_specs=pl.BlockSpec(memory_space=pl.ANY),
        out_shape=jax.ShapeDtypeStruct(dst.shape, dst.dtype),
        scratch_shapes=[pltpu.SemaphoreType.DMA(())],
        input_output_aliases={2: 0},
        compiler_params=_cp(("arbitrary",)),
        name="dispatch_scatter_more",
    )(pos, src, dst)


def _moe_body(ta_ref, tb_ref, na_ref, xs_ref, nf_ref, wga_ref, wua_ref, wda_ref, wgb_ref, wub_ref, wdb_ref, y_ref):
    del ta_ref, tb_ref

    @pl.when(pl.program_id(0) < na_ref[0])
    def _():
        x1 = xs_ref[:, 0:D_MODEL]
        xb = ((x1 * xs_ref[:, D_MODEL + 2:D_MODEL + 3]) * nf_ref[...]).astype(BF16)

        def expert(wg, wu, wd, gate):
            hdn = (jax.nn.silu(_dot(xb, wg[0])) * _dot(xb, wu[0])) * gate
            return _dot(hdn.astype(BF16), wd[0])

        y = expert(wga_ref, wua_ref, wda_ref, xs_ref[:, D_MODEL:D_MODEL + 1])
        y = y + expert(wgb_ref, wub_ref, wdb_ref, xs_ref[:, D_MODEL + 1:D_MODEL + 2])
        y_ref[...] = x1 + y

    @pl.when(pl.program_id(0) >= na_ref[0])
    def _():
        y_ref[...] = jnp.zeros(y_ref.shape, F32)


def _moe_experts(l, tile_a, tile_b, n_act, xs, nf, wg, wu, wd):
    nt = xs.shape[0] // MOE_TM
    last = lambda i, na: jnp.minimum(i, na[0] - 1)
    expert = lambda sel, i, ta, tb, na: (l * N_EXPERTS + sel(ta, tb)[last(i, na)], 0, 0)
    w_up = lambda sel: pl.BlockSpec((1, D_MODEL, D_EXPERT), functools.partial(expert, sel))
    w_dn = lambda sel: pl.BlockSpec((1, D_EXPERT, D_MODEL), functools.partial(expert, sel))
    sa = lambda ta, tb: ta
    sb = lambda ta, tb: tb
    grid_spec = pltpu.PrefetchScalarGridSpec(
        num_scalar_prefetch=3,
        grid=(nt,),
        in_specs=[pl.BlockSpec((MOE_TM, ROW_W), lambda i, ta, tb, na: (last(i, na), 0)),
                  _layer_spec(l, 1, D_MODEL),
                  w_up(sa), w_up(sa), w_dn(sa), w_up(sb), w_up(sb), w_dn(sb)],
        out_specs=pl.BlockSpec((MOE_TM, D_MODEL), lambda i, ta, tb, na: (i, 0)),
    )
    return pl.pallas_call(
        _moe_body,
        grid_spec=grid_spec,
        out_shape=jax.ShapeDtypeStruct((xs.shape[0], D_MODEL), F32),
        compiler_params=_cp(("arbitrary",)),
        name="moe_experts",
    )(tile_a, tile_b, n_act, xs, nf, wg, wu, wd, wg, wu, wd)


def _unpermute_body(pos_ref, ys_ref, o_ref, sem, *, tm):
    def group(g, c):
        for u in range(SUB):
            row = ys_ref.at[pl.ds(pos_ref[0, 0, g * SUB + u], 1), :]
            pltpu.make_async_copy(row, o_ref.at[g, pl.ds(u, 1), :], sem).start()
        return c

    lax.fori_loop(0, tm // SUB, group, 0)
    _wait_rows(o_ref, sem)


def _unpermute(pos, ys, tm):
    t = pos.shape[0] * tm
    return pl.pallas_call(
        functools.partial(_unpermute_body, tm=tm),
        grid=(t // tm,),
        in_specs=[pl.BlockSpec((1, 1, tm), lambda i: (i, 0, 0), memory_space=pltpu.SMEM),
                  pl.BlockSpec(memory_space=pl.ANY)],
        out_specs=pl.BlockSpec((tm // SUB, SUB, D_MODEL), lambda i: (i, 0, 0)),
        out_shape=jax.ShapeDtypeStruct((t // SUB, SUB, D_MODEL), F32),
        scratch_shapes=[pltpu.SemaphoreType.DMA(())],
        compiler_params=_cp(("arbitrary",)),
        name="moe_unpermute",
    )(pos, ys)


def _rope_tables(pos):
    half = ROT_DIM // 2
    inv_freq = jnp.float32(ROPE_THETA) ** (-jnp.arange(half, dtype=jnp.float32) * (2.0 / ROT_DIM))
    dim = np.arange(LANES) % HEAD_DIM
    ang = pos.astype(jnp.float32)[:, None] * inv_freq[dim % half][None, :]
    cos, sin = jnp.cos(ang), jnp.sin(ang)
    first, second = dim < half, (dim >= half) & (dim < ROT_DIM)
    return (jnp.where(first | second, cos, 1.0), jnp.where(first, -sin, 0.0), jnp.where(second, sin, 0.0))


def _seg_ones(n):
    idx = np.arange(n) // HEAD_DIM
    return jnp.asarray(idx[:, None] == idx[None, :], BF16)


def _bucket_experts():
    ea, eb = [], []
    for g in range(N_GROUPS):
        for a in range(EXPERTS_PER_GROUP):
            for b in range(a + 1, EXPERTS_PER_GROUP):
                ea.append(g * EXPERTS_PER_GROUP + a)
                eb.append(g * EXPERTS_PER_GROUP + b)
    return np.asarray(ea, np.int32), np.asarray(eb, np.int32)


def _dispatch_plan(counts, n_tiles):
    padded = ((counts + MOE_TM - 1) // MOE_TM) * MOE_TM
    ends = jnp.cumsum(padded)
    offs = ends - padded
    n_act = jnp.maximum(ends[-1] // MOE_TM, 1)
    starts = jnp.arange(n_tiles, dtype=jnp.int32) * MOE_TM
    tile_bucket = jnp.minimum(jnp.sum(starts[:, None] >= ends[None, :], axis=1), N_BUCKETS - 1)
    ea, eb = _bucket_experts()
    onehot = tile_bucket[:, None] == jnp.arange(N_BUCKETS)[None, :]
    tile_a = jnp.sum(jnp.where(onehot, ea[None, :], 0), axis=1).astype(jnp.int32)
    tile_b = jnp.sum(jnp.where(onehot, eb[None, :], 0), axis=1).astype(jnp.int32)
    fill_tile = jnp.maximum(ends // MOE_TM - 1, 0).astype(jnp.int32)
    return offs, tile_a, tile_b, n_act.astype(jnp.int32).reshape(1), fill_tile


def _positions(offs, bucket, rank):
    onehot = bucket[..., None] == jnp.arange(N_BUCKETS, dtype=jnp.int32)
    return (jnp.sum(jnp.where(onehot, offs.astype(jnp.int32), 0), axis=-1) + rank).astype(jnp.int32)


def kernel(x_prompt, x_sample, cache_k, cache_v, state_conv, meta_tokens, norm_mix, w_in, conv_w, q_norm, k_norm,
           attn_sinks, w_conv_out, w_attn_out, w_o, norm_ffn, w_router_group, b_router_group, w_router_expert,
           b_router_expert, w_exp_gate, w_exp_up, w_exp_down):
    batch, seq, _ = x_prompt.shape
    depth = w_in.shape[0]
    n_dec = x_sample.shape[0]
    past_len = PAST_LEN
    lp = PAD + N_META + seq
    tm_in, tm_out, qb = 640, 640, 5
    tm_move, tm_last = 3328, 4096
    assert lp % tm_in == 0 and (batch * lp) % tm_out == 0 and lp % (qb * BLOCK) == 0
    assert (batch * lp) % tm_move == 0 and (batch * seq) % tm_last == 0
    assert x_sample.shape[1] == 1 and cache_k.shape[2] == WINDOW and past_len >= WINDOW

    t_prompt = batch * lp
    t_all = t_prompt + n_dec
    n_tiles = -(-(t_all + N_BUCKETS * (MOE_TM - 1)) // MOE_TM)

    front = jnp.zeros((batch, PAD, D_MODEL), F32)
    meta = jnp.broadcast_to(meta_tokens[None].astype(F32), (batch, N_META, D_MODEL))
    xp = jnp.concatenate([front, meta, x_prompt], axis=1)
    xs = x_sample.reshape(n_dec, D_MODEL)

    rope_p = _rope_tables(jnp.arange(lp, dtype=jnp.int32) - PAD)
    rope_s = _rope_tables(jnp.full((1,), past_len, jnp.int32))
    s256, s128 = _seg_ones(256), _seg_ones(LANES)
    tri_p = jnp.asarray(np.triu(np.ones((tm_out, tm_out)), 1), BF16)
    tri_s = jnp.asarray(np.triu(np.ones((n_dec, n_dec)), 1), BF16)
    zero_cnt = jnp.zeros((LANES, LANES), F32)

    w_mix = w_in.astype(BF16)
    w_gate = w_mix[:, :, D_MIX:]
    wco, wao, wo = w_conv_out.astype(BF16), w_attn_out.astype(BF16), w_o.astype(BF16)
    wg = w_exp_gate.astype(BF16).reshape(depth * N_EXPERTS, D_MODEL, D_EXPERT)
    wu = w_exp_up.astype(BF16).reshape(depth * N_EXPERTS, D_MODEL, D_EXPERT)
    wd = w_exp_down.astype(BF16).reshape(depth * N_EXPERTS, D_EXPERT, D_MODEL)
    nm, nf = norm_mix.reshape(depth, 1, D_MODEL), norm_ffn.reshape(depth, 1, D_MODEL)
    qn = (jnp.tile(q_norm, (1, N_HEADS)) * Q_SCALE).reshape(depth, 1, D_Q)
    kn = jnp.tile(k_norm, (1, N_KV_HEADS)).reshape(depth, 1, D_KV)
    r_pad = LANES - N_GROUPS - N_EXPERTS
    wr = jnp.concatenate([w_router_group, w_router_expert, jnp.zeros((depth, D_MODEL, r_pad), F32)], axis=-1).astype(BF16)
    br = jnp.concatenate([b_router_group, b_router_expert, jnp.zeros((depth, r_pad), F32)], axis=-1).reshape(depth, 1, LANES)
    sinks = attn_sinks.astype(F32)
    sinkb = jnp.broadcast_to(sinks[:, :, None], (depth, N_HEADS, LANES))
    ck = cache_k.reshape(depth, n_dec, WINDOW, D_KV)
    cv = cache_v.reshape(depth, n_dec, WINDOW, D_KV)

    outs = {k: [] for k in ("kp", "vp", "cp", "ks", "vs", "cs")}
    for l in range(depth):
        cy, q, k, v, ulast = _prompt_in(l, xp, nm, w_mix, conv_w, qn, kn, rope_p, s256, s128, tm_in)
        ao = _prompt_attn(l, q, k, v, sinks, qb)
        xep, rankp, bktp, cnt = _mix_out(
            l, xp.reshape(t_prompt, D_MODEL), cy.reshape(t_prompt, D_CONV), ao.reshape(t_prompt, D_Q),
            nm, w_gate, wco, wao, wo, nf, wr, br, tri_p, zero_cnt, tm_out)
        outs["kp"].append(k[:, lp - WINDOW:].reshape(batch, WINDOW, N_KV_HEADS, HEAD_DIM))
        outs["vp"].append(v[:, lp - WINDOW:].reshape(batch, WINDOW, N_KV_HEADS, HEAD_DIM))
        outs["cp"].append(ulast[:, 8 - (CONV_W - 1):])

        c0, c1 = state_conv[l, :, 0, :], state_conv[l, :, 1, :]
        cys, qx, ksn, vsn, us = _sample_in(l, xs, c0, c1, nm, w_mix, conv_w, qn, kn, rope_s, s256, s128)
        ox, nk, nv = _sample_attn(l, jnp.transpose(qx, (1, 0, 2)), sinkb, ck, cv, ksn, vsn)
        ox = ox.reshape(n_dec, N_KV_HEADS, GQA, N_KV_HEADS, HEAD_DIM)
        aos = jnp.stack([ox[:, j, :, j, :] for j in range(N_KV_HEADS)], axis=1).reshape(n_dec, D_Q).astype(BF16)
        xes, ranks, bkts, cnt = _mix_out(l, xs, cys, aos, nm, w_gate, wco, wao, wo, nf, wr, br, tri_s, cnt, n_dec)
        outs["ks"].append(nk.reshape(n_dec, WINDOW, N_KV_HEADS, HEAD_DIM))
        outs["vs"].append(nv.reshape(n_dec, WINDOW, N_KV_HEADS, HEAD_DIM))
        outs["cs"].append(jnp.stack([c1, us], axis=1))

        counts = cnt[:N_BUCKETS, 0].astype(jnp.int32)
        offs, tile_a, tile_b, n_act, fill_tile = _dispatch_plan(counts, n_tiles)
        posp = _positions(offs, bktp, rankp).reshape(t_prompt // tm_move, 1, tm_move)
        poss = _positions(offs, bkts, ranks)
        by8 = lambda a: a.reshape(a.shape[0] // SUB, SUB, a.shape[1])
        sorted_rows = _scatter_rows(fill_tile, n_act, posp, by8(xep), n_tiles, tm_move)
        sorted_rows = _scatter_more_rows(poss, by8(xes), sorted_rows, n_dec)
        ys = _moe_experts(l, tile_a, tile_b, n_act, sorted_rows, nf, wg, wu, wd)
        xs = _unpermute(poss, ys, n_dec).reshape(n_dec, D_MODEL)
        if l + 1 < depth:
            xp = _unpermute(posp, ys, tm_move).reshape(batch, lp, D_MODEL)
        else:
            pos_tok = posp.reshape(batch, lp)[:, PAD + N_META:].reshape(batch * seq // tm_last, 1, tm_last)
            y_prompt = _unpermute(pos_tok, ys, tm_last).reshape(batch, seq, D_MODEL)

    y_sample = xs.reshape(n_dec, 1, D_MODEL)
    st = lambda k: jnp.stack(outs[k])
    return (y_prompt, y_sample, st("kp"), st("vp"), st("cp"), st("ks"), st("vs"), st("cs"))
```

```python
import functools

import jax
import jax.numpy as jnp
import numpy as np
from jax import lax
from jax.experimental import pallas as pl
from jax.experimental.pallas import tpu as pltpu

D_MODEL = 1024
N_META = 16
D_CONV = D_MODEL
CONV_W = 3
N_HEADS = 16
N_KV_HEADS = 2
HEAD_DIM = 64
GQA = N_HEADS // N_KV_HEADS
ROT_DIM = HEAD_DIM // 4
ROPE_THETA = 500000.0
WINDOW = 128
PAST_LEN = 8192
BLOCK = 128
N_GROUPS = 4
EXPERTS_PER_GROUP = 4
N_EXPERTS = N_GROUPS * EXPERTS_PER_GROUP
D_EXPERT = 512
EPS = 1e-6
NEG = -1e30
D_Q = N_HEADS * HEAD_DIM
D_KV = N_KV_HEADS * HEAD_DIM
C_B, C_C, C_HC = 0, D_CONV, 2 * D_CONV
C_Q = 3 * D_CONV
C_K = C_Q + D_Q
C_V = C_K + D_KV
C_G = C_V + D_KV
D_MIX = C_G
D_IN = C_G + 2 * D_MODEL

LANES = 128
SUBLANES = 8
PAD = (-N_META) % BLOCK
N_PAIRS = 6
N_BUCKETS = N_GROUPS * N_PAIRS
MOE_TM = 256
ROW_W = D_MODEL + LANES
Q_SCALE = HEAD_DIM ** -0.5
assert Q_SCALE == 0.125

F32 = jnp.float32
BF16 = jnp.bfloat16
VMEM_LIMIT = 56 * 1024 * 1024


def _cp(sem, vmem=VMEM_LIMIT):
    return pltpu.CompilerParams(dimension_semantics=sem, vmem_limit_bytes=vmem)


def _const_spec(shape):
    nd = len(shape)
    return pl.BlockSpec(shape, lambda *_: (0,) * nd, pipeline_mode=pl.Buffered(1))


def _layer_spec(l, *shape):
    n = len(shape)
    return pl.BlockSpec((None,) + shape, lambda *_: (l,) + (0,) * n, pipeline_mode=pl.Buffered(1))


def _dot(a, b):
    return jnp.dot(a, b, preferred_element_type=F32)


def _seg_mean_sq(x, seg_ones):
    return _dot((x * x).astype(BF16), seg_ones) * (1.0 / HEAD_DIM)


def _rope128(t, cos, sin_pm, first):
    partner = jnp.where(first, pltpu.roll(t, LANES - ROT_DIM // 2, 1), pltpu.roll(t, ROT_DIM // 2, 1))
    return t * cos + partner * sin_pm


def _rms_scale(x):
    return lax.rsqrt(jnp.mean(x * x, axis=-1, keepdims=True) + EPS)


def _rms_rows(x, g):
    return (x * _rms_scale(x)) * g


def _qk_project(hb, w_ref, s256, s128):
    qs = [_dot(hb, w_ref[:, C_Q + c * 256:C_Q + (c + 1) * 256]) for c in range(D_Q // 256)]
    kc = _dot(hb, w_ref[:, C_K:C_K + D_KV])
    return qs, [_seg_mean_sq(qc, s256) for qc in qs], kc, _seg_mean_sq(kc, s128)


def _qk_finish(proj, qn, kn, cos, sneg, spos, store_q):
    qs, q_ms, kc, k_ms = proj
    sin_pm = sneg + spos
    lane = lax.broadcasted_iota(jnp.int32, cos.shape, 1)
    first = lax.bitwise_and(lane, HEAD_DIM - 1) < ROT_DIM // 2
    for c, (qc, ms) in enumerate(zip(qs, q_ms)):
        qc = (qc * lax.rsqrt(ms + EPS)) * qn[:, c * 256:(c + 1) * 256]
        for s in range(2):
            r = _rope128(qc[:, s * LANES:(s + 1) * LANES], cos, sin_pm, first)
            store_q(2 * c + s, r.astype(BF16))
    kc = (kc * lax.rsqrt(k_ms + EPS)) * kn
    return _rope128(kc, cos, sin_pm, first)


def _in_body(x_ref, nm_ref, w_ref, cw_ref, qn_ref, kn_ref, cos_ref, sneg_ref, spos_ref,
             s256_ref, s128_ref, cy_ref, q_ref, k_ref, v_ref, ul_ref, us_ref, *, tm, parts):
    i = pl.program_id(1)
    th = tm // parts

    @pl.when(i == 0)
    def _():
        us_ref[0:8, :] = jnp.zeros((8, D_CONV), F32)

    projs = []
    for h in range(parts):
        r0 = h * th
        hb = _rms_rows(x_ref[0, r0:r0 + th, :], nm_ref[...]).astype(BF16)
        projs.append(_qk_project(hb, w_ref, s256_ref[...], s128_ref[...]))
        u = _dot(hb, w_ref[:, C_C:C_C + D_CONV]) * _dot(hb, w_ref[:, C_HC:C_HC + D_CONV])
        row = lax.broadcasted_iota(jnp.int32, (th, 1), 0) + (i * tm + r0)
        u = jnp.where(row >= PAD, u, 0.0)
        us_ref[8 + r0:8 + r0 + th, :] = u
        conv = (us_ref[6 + r0:6 + r0 + th, :] * cw_ref[0:1, :] + us_ref[7 + r0:7 + r0 + th, :] * cw_ref[1:2, :]) \
            + u * cw_ref[2:3, :]
        cy_ref[0, r0:r0 + th, :] = (_dot(hb, w_ref[:, C_B:C_B + D_CONV]) * conv).astype(BF16)
        v_ref[0, r0:r0 + th, :] = _dot(hb, w_ref[:, C_V:C_V + D_KV])

    last = us_ref[tm:tm + 8, :]
    ul_ref[0] = last
    us_ref[0:8, :] = last

    for h in range(parts):
        rows = slice(h * th, (h + 1) * th)

        def store_q(slab, val, rows=rows):
            q_ref[0, rows, slab * LANES:(slab + 1) * LANES] = val

        k_ref[0, rows, :] = _qk_finish(projs[h], qn_ref[...], kn_ref[...], cos_ref[rows, :], sneg_ref[rows, :],
                                       spos_ref[rows, :], store_q)


def _in_first_body(head_ref, *refs, tm, parts):
    nb = tm // BLOCK
    blocks, rest, xpad_ref, us_ref = refs[:nb], refs[nb:-2], refs[-2], refs[-1]
    first = pl.program_id(1) == 0
    xpad_ref[0, 0:BLOCK, :] = jnp.where(first, head_ref[0], blocks[0][0])
    for j in range(1, nb):
        xpad_ref[0, j * BLOCK:(j + 1) * BLOCK, :] = blocks[j][0]
    _in_body(xpad_ref, *rest, us_ref, tm=tm, parts=parts)


def _prompt_in(l, x, nm, w_mix, cw, qn, kn, rope, s256, s128, tm, head=None):
    b = x.shape[0]
    lp = x.shape[1] if head is None else x.shape[1] + BLOCK
    nt = lp // tm
    nb = tm // BLOCK
    cos, sneg, spos = rope
    tok = lambda w: pl.BlockSpec((1, tm, w), lambda bi, i: (bi, i, 0))
    tab = pl.BlockSpec((tm, LANES), lambda bi, i: (i, 0))
    params = [_layer_spec(l, 1, D_MODEL), _layer_spec(l, D_MODEL, D_MIX),
              _layer_spec(l, CONV_W, D_CONV), _layer_spec(l, 1, D_Q), _layer_spec(l, 1, D_KV),
              tab, tab, tab, _const_spec((256, 256)), _const_spec((LANES, LANES))]
    out_specs = [tok(D_CONV), tok(D_Q), tok(D_KV), tok(D_KV), pl.BlockSpec((1, 8, D_CONV), lambda bi, i: (bi, 0, 0))]
    out_shape = [jax.ShapeDtypeStruct((b, lp, D_CONV), BF16), jax.ShapeDtypeStruct((b, lp, D_Q), BF16),
                 jax.ShapeDtypeStruct((b, lp, D_KV), F32), jax.ShapeDtypeStruct((b, lp, D_KV), F32),
                 jax.ShapeDtypeStruct((b, 8, D_CONV), F32)]
    if head is None:
        body, x_specs, x_args = _in_body, [tok(D_MODEL)], (x,)
    else:
        blk = lambda j: pl.BlockSpec((1, BLOCK, D_MODEL), lambda bi, i: (bi, jnp.maximum(nb * i - 1 + j, 0), 0))
        body = _in_first_body
        x_specs = [pl.BlockSpec((1, BLOCK, D_MODEL), lambda bi, i: (bi, 0, 0))] + [blk(j) for j in range(nb)]
        x_args = (head,) + (x,) * nb
        out_specs.append(tok(D_MODEL))
        out_shape.append(jax.ShapeDtypeStruct((b, lp, D_MODEL), F32))
    return pl.pallas_call(
        functools.partial(body, tm=tm, parts=2),
        grid=(b, nt),
        in_specs=x_specs + params,
        out_specs=out_specs,
        out_shape=out_shape,
        scratch_shapes=[pltpu.VMEM((tm + 8, D_CONV), F32)],
        compiler_params=_cp(("arbitrary", "arbitrary")),
        name="prompt_in",
    )(*x_args, nm, w_mix, cw, qn, kn, cos, sneg, spos, s256, s128)


def _sin_body(x_ref, c0_ref, c1_ref, nm_ref, w_ref, cw_ref, qn_ref, kn_ref, cos_ref, sneg_ref,
              spos_ref, s256_ref, s128_ref, cy_ref, qx_ref, k_ref, v_ref, u_ref):
    hb = _rms_rows(x_ref[...], nm_ref[...]).astype(BF16)
    u = _dot(hb, w_ref[:, C_C:C_C + D_CONV]) * _dot(hb, w_ref[:, C_HC:C_HC + D_CONV])
    u_ref[...] = u
    conv = (c0_ref[...] * cw_ref[0:1, :] + c1_ref[...] * cw_ref[1:2, :]) + u * cw_ref[2:3, :]
    cy_ref[...] = (_dot(hb, w_ref[:, C_B:C_B + D_CONV]) * conv).astype(BF16)

    lane = lax.broadcasted_iota(jnp.int32, (x_ref.shape[0], LANES), 1)
    low = lane < HEAD_DIM

    def store_q(slab, val):
        valf = val.astype(F32)
        swapped = pltpu.roll(valf, HEAD_DIM, 1)
        zero = jnp.zeros_like(valf)
        for h in (2 * slab, 2 * slab + 1):
            src = valf if (h % 2) == (h // GQA) else swapped
            keep = low if (h // GQA) == 0 else jnp.logical_not(low)
            qx_ref[h] = jnp.where(keep, src, zero).astype(BF16)

    cos = jnp.broadcast_to(cos_ref[...], (x_ref.shape[0], LANES))
    sneg = jnp.broadcast_to(sneg_ref[...], (x_ref.shape[0], LANES))
    spos = jnp.broadcast_to(spos_ref[...], (x_ref.shape[0], LANES))
    proj = _qk_project(hb, w_ref, s256_ref[...], s128_ref[...])
    k_ref[...] = _qk_finish(proj, qn_ref[...], kn_ref[...], cos, sneg, spos, store_q)
    v_ref[...] = _dot(hb, w_ref[:, C_V:C_V + D_KV])


def _sample_in(l, x, c0, c1, nm, w_mix, cw, qn, kn, rope, s256, s128):
    n = x.shape[0]
    cos, sneg, spos = rope
    full = lambda *s: pl.BlockSpec(s, lambda i: (0,) * len(s))
    return pl.pallas_call(
        _sin_body,
        grid=(1,),
        in_specs=[full(n, D_MODEL), full(n, D_CONV), full(n, D_CONV), _layer_spec(l, 1, D_MODEL),
                  _layer_spec(l, D_MODEL, D_MIX), _layer_spec(l, CONV_W, D_CONV), _layer_spec(l, 1, D_Q),
                  _layer_spec(l, 1, D_KV), full(1, LANES), full(1, LANES), full(1, LANES), full(256, 256),
                  full(LANES, LANES)],
        out_specs=[full(n, D_CONV), full(N_HEADS, n, LANES), full(n, D_KV), full(n, D_KV), full(n, D_CONV)],
        out_shape=[jax.ShapeDtypeStruct((n, D_CONV), BF16), jax.ShapeDtypeStruct((N_HEADS, n, LANES), BF16),
                   jax.ShapeDtypeStruct((n, D_KV), F32), jax.ShapeDtypeStruct((n, D_KV), F32),
                   jax.ShapeDtypeStruct((n, D_CONV), F32)],
        compiler_params=_cp(("arbitrary",)),
        name="sample_in",
    )(x, c0, c1, nm, w_mix, cw, qn, kn, cos, sneg, spos, s256, s128)


def _attn_body(sink_ref, q_ref, kp_ref, kc_ref, vp_ref, vc_ref, o_ref, ke_ref, ko_ref, vt_ref, *, qb, l):
    i = pl.program_id(1)
    lane = lax.broadcasted_iota(jnp.int32, (BLOCK, LANES), 1)
    low = lane < HEAD_DIM

    def prep_k(src, blk0, nblk):
        for t in range(nblk):
            blk = src[0, t * BLOCK:(t + 1) * BLOCK, :]
            swp = pltpu.roll(blk, HEAD_DIM, 1)
            zero = jnp.zeros_like(blk)
            rows = slice((blk0 + t) * BLOCK, (blk0 + t + 1) * BLOCK)
            ke_ref[0, rows, :] = jnp.where(low, blk, zero).astype(BF16)
            ko_ref[0, rows, :] = jnp.where(low, zero, swp).astype(BF16)
            ke_ref[1, rows, :] = jnp.where(low, swp, zero).astype(BF16)
            ko_ref[1, rows, :] = jnp.where(low, zero, blk).astype(BF16)

    def prep_v(src, blk0, nblk):
        for t in range(nblk):
            vt = jnp.transpose(src[0, t * BLOCK:(t + 1) * BLOCK, :]).astype(BF16)
            for j in range(N_KV_HEADS):
                vt_ref[blk0 + t, j] = vt[j * HEAD_DIM:(j + 1) * HEAD_DIM, :]

    prep_k(kp_ref, 0, 1)
    prep_k(kc_ref, 1, qb)
    prep_v(vp_ref, 0, 1)
    prep_v(vc_ref, 1, qb)

    c = lax.broadcasted_iota(jnp.int32, (2 * BLOCK, BLOCK), 0)
    r = lax.broadcasted_iota(jnp.int32, (2 * BLOCK, BLOCK), 1)
    diff = r - (c - BLOCK)
    band = (diff >= 0) & (diff < WINDOW)
    nt = (((1,), (1,)), ((), ()))

    def one_block(b, carry):
        r0 = pl.multiple_of(b * BLOCK, BLOCK)
        kpos = (i * qb + b) * BLOCK + c - BLOCK - PAD
        bias = jnp.where(band & (kpos >= 0), 0.0, NEG)
        for m in range(N_HEADS // 2):
            j = (2 * m) // GQA
            q2 = q_ref[0, pl.ds(r0, BLOCK), m * LANES:(m + 1) * LANES]
            halves = []
            for par, k_ref in ((0, ke_ref), (1, ko_ref)):
                st = lax.dot_general(k_ref[j, pl.ds(r0, 2 * BLOCK), :], q2, nt, preferred_element_type=F32)
                st = st + bias
                sink = sink_ref[l, 2 * m + par]
                mx = jnp.maximum(jnp.max(st, axis=0, keepdims=True), sink)
                p = jnp.exp(st - mx)
                den = jnp.sum(p, axis=0, keepdims=True) + jnp.exp(sink - mx)
                pb = p.astype(BF16)
                ot = _dot(vt_ref[b, j], pb[0:BLOCK]) + _dot(vt_ref[b + 1, j], pb[BLOCK:])
                halves.append(ot * (1.0 / den))
            o2 = jnp.transpose(jnp.concatenate(halves, axis=0))
            o_ref[0, pl.ds(r0, BLOCK), m * LANES:(m + 1) * LANES] = o2.astype(BF16)
        return carry

    lax.fori_loop(0, qb, one_block, 0)


def _prompt_attn(l, q, k, v, sinks, qb):
    b, lp, _ = q.shape
    nsteps = lp // (qb * BLOCK)
    cur = lambda w: pl.BlockSpec((1, qb * BLOCK, w), lambda bi, i: (bi, i, 0))
    prev = pl.BlockSpec((1, BLOCK, D_KV), lambda bi, i: (bi, jnp.maximum(i * qb - 1, 0), 0))
    ext = ((qb + 1) * BLOCK, LANES)
    return pl.pallas_call(
        functools.partial(_attn_body, qb=qb, l=l),
        grid=(b, nsteps),
        in_specs=[pl.BlockSpec(memory_space=pltpu.SMEM), cur(D_Q), prev, cur(D_KV), prev, cur(D_KV)],
        out_specs=cur(D_Q),
        out_shape=jax.ShapeDtypeStruct((b, lp, D_Q), BF16),
        scratch_shapes=[pltpu.VMEM((N_KV_HEADS,) + ext, BF16), pltpu.VMEM((N_KV_HEADS,) + ext, BF16),
                        pltpu.VMEM((qb + 1, N_KV_HEADS, HEAD_DIM, BLOCK), BF16)],
        compiler_params=_cp(("arbitrary", "arbitrary")),
        name="prompt_attn",
    )(sinks, q, k, k, v, v)


def _sattn_body(qx_ref, sink_ref, ck_ref, cv_ref, kn_ref, vn_ref, ox_ref, nk_ref, nv_ref, *, tb):
    def window(c_ref, n_ref, t):
        return jnp.concatenate([c_ref[t, 1:WINDOW, :], n_ref[t:t + 1, :]], axis=0)

    for t in range(tb):
        nk_ref[t] = window(ck_ref, kn_ref, t)
        nv_ref[t] = window(cv_ref, vn_ref, t)
    nt = (((1,), (1,)), ((), ()))
    s = jnp.concatenate([lax.dot_general(qx_ref[t], window(ck_ref, kn_ref, t).astype(BF16), nt,
                                         preferred_element_type=F32) for t in range(tb)], axis=0)
    sink = jnp.concatenate([sink_ref[...][:, 0:1]] * tb, axis=0)
    m = jnp.maximum(jnp.max(s, axis=-1, keepdims=True), sink)
    p = jnp.exp(s - m)
    rden = 1.0 / (jnp.sum(p, axis=-1, keepdims=True) + jnp.exp(sink - m))
    pb = p.astype(BF16)
    for t in range(tb):
        rows = slice(t * N_HEADS, (t + 1) * N_HEADS)
        ox_ref[t] = _dot(pb[rows], window(cv_ref, vn_ref, t).astype(BF16)) * rden[rows]


def _sample_attn(l, qx, sinkb, ck, cv, kn, vn, tb=16):
    n = qx.shape[0]
    blk3 = lambda a, c: pl.BlockSpec((tb, a, c), lambda i: (i, 0, 0))
    cache = pl.BlockSpec((None, tb, WINDOW, D_KV), lambda i: (l, i, 0, 0))
    row = pl.BlockSpec((tb, D_KV), lambda i: (i, 0))
    return pl.pallas_call(
        functools.partial(_sattn_body, tb=tb),
        grid=(n // tb,),
        in_specs=[blk3(N_HEADS, LANES), _layer_spec(l, N_HEADS, LANES), cache, cache, row, row],
        out_specs=[blk3(N_HEADS, LANES), blk3(WINDOW, D_KV), blk3(WINDOW, D_KV)],
        out_shape=[jax.ShapeDtypeStruct((n, N_HEADS, LANES), F32),
                   jax.ShapeDtypeStruct((n, WINDOW, D_KV), F32), jax.ShapeDtypeStruct((n, WINDOW, D_KV), F32)],
        compiler_params=_cp(("arbitrary",)),
        name="sample_attn",
    )(qx, sinkb, ck, cv, kn, vn)


ROUTE_ROWS = 24
RINV_LANE = LANES - 1


def _route(lt):
    top = lt[0:ROUTE_ROWS, :]
    rows = top.shape[1]
    rowf = lax.broadcasted_iota(jnp.int32, top.shape, 0).astype(F32)
    big = jnp.float32(3e38)
    far = jnp.float32(LANES)
    cmax = lambda a: jnp.max(a, axis=0, keepdims=True)
    cmin = lambda a: jnp.min(a, axis=0, keepdims=True)

    gmask = rowf < N_GROUPS
    gl = jnp.where(gmask, top, -big)
    gmax = cmax(gl)
    grp = cmin(jnp.where(gmask & (gl == gmax), rowf, far))
    p_grp = 1.0 / jnp.sum(jnp.where(gmask, jnp.exp(gl - gmax), 0.0), axis=0, keepdims=True)

    e_lo = N_GROUPS + EXPERTS_PER_GROUP * grp
    emask = (rowf >= e_lo) & (rowf < e_lo + EXPERTS_PER_GROUP)
    el = jnp.where(emask, top, -big)
    v1 = cmax(el)
    i1 = cmin(jnp.where(emask & (el == v1), rowf, far))
    rest = emask & (rowf != i1)
    el2 = jnp.where(rest, top, -big)
    v2 = cmax(el2)
    i2 = cmin(jnp.where(rest & (el2 == v2), rowf, far))
    e = jnp.exp(v2 - v1)
    w1 = (1.0 / (1.0 + e)) * p_grp
    w2 = (e / (1.0 + e)) * p_grp
    first_low = i1 < i2
    ea = jnp.where(first_low, i1, i2) - e_lo
    eb = jnp.where(first_low, i2, i1) - e_lo
    w_a = jnp.where(first_low, w1, w2)
    w_b = jnp.where(first_low, w2, w1)
    pair = jnp.where(ea == 0.0, 0.0, jnp.where(ea == 1.0, 3.0, 5.0)) + (eb - ea - 1.0)
    bucket = grp * N_PAIRS + pair

    r8 = lax.broadcasted_iota(jnp.int32, (SUBLANES, rows), 0)
    head = jnp.where(r8 == 0, w_a, jnp.where(r8 == 1, w_b, jnp.where(r8 == 2, lt[RINV_LANE:RINV_LANE + 1, :], 0.0)))
    return bucket, jnp.concatenate([head, jnp.zeros((LANES - SUBLANES, rows), F32)], axis=0)


def _out_body(x_ref, cy_ref, ao_ref, nm_ref, wg_ref, wco_ref, wao_ref, wo_ref, nf_ref, wr_ref, br_ref,
              tri_ref, cin_ref, xe_ref, rank_ref, bkt_ref, cnt_ref, run_ref, *, tm):
    i = pl.program_id(0)
    x = x_ref[...]
    hb = _rms_rows(x, nm_ref[...]).astype(BF16)
    ya = _dot(cy_ref[...], wco_ref[...])
    yb = _dot(ao_ref[...], wao_ref[...])
    mix = jax.nn.sigmoid(_dot(hb, wg_ref[:, 0:D_MODEL])) * ya + jax.nn.sigmoid(_dot(hb, wg_ref[:, D_MODEL:])) * yb
    x1 = x + _dot(mix.astype(BF16), wo_ref[...])
    xe_ref[:, 0:D_MODEL] = x1

    rinv = _rms_scale(x1)
    xnb = ((x1 * rinv) * nf_ref[...]).astype(BF16)
    logits = _dot(xnb, wr_ref[...]) + br_ref[...]
    lane = lax.broadcasted_iota(jnp.int32, (tm, LANES), 1)
    bucket, meta_t = _route(jnp.transpose(jnp.where(lane == RINV_LANE, rinv, logits)))
    xe_ref[:, D_MODEL:] = jnp.transpose(meta_t)

    @pl.when(i == 0)
    def _():
        run_ref[...] = cin_ref[...]

    sub = lax.broadcasted_iota(jnp.int32, (LANES, tm), 0).astype(F32)
    oht = (sub == bucket).astype(F32)
    before = _dot(oht.astype(BF16), tri_ref[...]) + run_ref[:, 0:1]
    rank_ref[0] = jnp.sum(oht * before, axis=0, keepdims=True).astype(jnp.int32)
    bkt_ref[0] = bucket.astype(jnp.int32)
    run_ref[...] = run_ref[...] + jnp.sum(oht, axis=-1, keepdims=True)
    cnt_ref[...] = run_ref[...]


def _mix_out(l, x, cy, ao, nm, w_gate, wco, wao, wo, nf, wr, br, tri, cnt_in, tm):
    t = x.shape[0]
    nt = t // tm
    tok = lambda w: pl.BlockSpec((tm, w), lambda i: (i, 0))
    rowi = pl.BlockSpec((1, 1, tm), lambda i: (i, 0, 0))
    sq = (D_MODEL, D_MODEL)
    return pl.pallas_call(
        functools.partial(_out_body, tm=tm),
        grid=(nt,),
        in_specs=[tok(D_MODEL), tok(D_CONV), tok(D_Q), _layer_spec(l, 1, D_MODEL),
                  _layer_spec(l, D_MODEL, 2 * D_MODEL), _layer_spec(l, *sq), _layer_spec(l, *sq), _layer_spec(l, *sq),
                  _layer_spec(l, 1, D_MODEL), _layer_spec(l, D_MODEL, LANES), _layer_spec(l, 1, LANES),
                  _const_spec((tm, tm)), _const_spec((LANES, LANES))],
        out_specs=[tok(ROW_W), rowi, rowi, pl.BlockSpec((LANES, LANES), lambda i: (0, 0))],
        out_shape=[jax.ShapeDtypeStruct((t, ROW_W), F32),
                   jax.ShapeDtypeStruct((nt, 1, tm), jnp.int32), jax.ShapeDtypeStruct((nt, 1, tm), jnp.int32),
                   jax.ShapeDtypeStruct((LANES, LANES), F32)],
        scratch_shapes=[pltpu.VMEM((LANES, LANES), F32)],
        compiler_params=_cp(("arbitrary",)),
        name="mix_out",
    )(x, cy, ao, nm, w_gate, wco, wao, wo, nf, wr, br, tri, cnt_in)


SUB = SUBLANES


def _wait_rows(block_ref, sem):
    pltpu.make_async_copy(block_ref, block_ref, sem).wait()


def _scatter_tile_rows(pos_ref, src_ref, dst_ref, sem, tm):
    def group(g, c):
        for u in range(SUB):
            row = dst_ref.at[pl.ds(pos_ref[0, 0, g * SUB + u], 1), :]
            pltpu.make_async_copy(src_ref.at[g, pl.ds(u, 1), :], row, sem).start()
        return c

    lax.fori_loop(0, tm // SUB, group, 0)
    _wait_rows(src_ref, sem)


def _scatter_body(fill_ref, na_ref, pos_ref, src_ref, dst_ref, zero_ref, sem, *, tm, n_tiles):
    tile_rows = lambda t: dst_ref.at[pl.ds(pl.multiple_of(t * MOE_TM, MOE_TM), MOE_TM), :]

    @pl.when(pl.program_id(0) == 0)
    def _():
        zero_ref[...] = jnp.zeros(zero_ref.shape, F32)
        for b in range(N_BUCKETS):
            pltpu.make_async_copy(zero_ref, tile_rows(fill_ref[b]), sem).start()

        def tail_start(t, c):
            pltpu.make_async_copy(zero_ref, tile_rows(t), sem).start()
            return c

        lax.fori_loop(na_ref[0], n_tiles, tail_start, 0)
        for b in range(N_BUCKETS):
            pltpu.make_async_copy(zero_ref, tile_rows(fill_ref[b]), sem).wait()

        def tail_wait(t, c):
            pltpu.make_async_copy(zero_ref, tile_rows(t), sem).wait()
            return c

        lax.fori_loop(na_ref[0], n_tiles, tail_wait, 0)

    _scatter_tile_rows(pos_ref, src_ref, dst_ref, sem, tm)


def _scatter_rows(fill_tile, n_act, pos, src, n_tiles, tm):
    nt = src.shape[0] * SUB // tm
    grid_spec = pltpu.PrefetchScalarGridSpec(
        num_scalar_prefetch=2,
        grid=(nt,),
        in_specs=[pl.BlockSpec((1, 1, tm), lambda i, *_: (i, 0, 0), memory_space=pltpu.SMEM),
                  pl.BlockSpec((tm // SUB, SUB, ROW_W), lambda i, *_: (i, 0, 0))],
        out_specs=pl.BlockSpec(memory_space=pl.ANY),
        scratch_shapes=[pltpu.VMEM((MOE_TM, ROW_W), F32), pltpu.SemaphoreType.DMA(())],
    )
    return pl.pallas_call(
        functools.partial(_scatter_body, tm=tm, n_tiles=n_tiles),
        grid_spec=grid_spec,
        out_shape=jax.ShapeDtypeStruct((n_tiles * MOE_TM, ROW_W), F32),
        compiler_params=_cp(("arbitrary",)),
        name="dispatch_scatter",
    )(fill_tile, n_act, pos, src)


def _scatter_more_body(pos_ref, src_ref, dst_in_ref, dst_ref, sem, *, tm):
    del dst_in_ref
    _scatter_tile_rows(pos_ref, src_ref, dst_ref, sem, tm)


def _scatter_more_rows(pos, src, dst, tm):
    nt = src.shape[0] * SUB // tm
    return pl.pallas_call(
        functools.partial(_scatter_more_body, tm=tm),
        grid=(nt,),
        in_specs=[pl.BlockSpec((1, 1, tm), lambda i: (i, 0, 0), memory_space=pltpu.SMEM),
                  pl.BlockSpec((tm // SUB, SUB, ROW_W), lambda i: (i, 0, 0)),
                  pl.BlockSpec(memory_space=pl.ANY)],
        out_specs=pl.BlockSpec(memory_space=pl.ANY),
        out_shape=jax.ShapeDtypeStruct(dst.shape, dst.dtype),
        scratch_shapes=[pltpu.SemaphoreType.DMA(())],
        input_output_aliases={2: 0},
        compiler_params=_cp(("arbitrary",)),
        name="dispatch_scatter_more",
    )(pos, src, dst)


def _moe_body(ta_ref, tb_ref, na_ref, xs_ref, nf_ref, wga_ref, wua_ref, wda_ref, wgb_ref, wub_ref, wdb_ref, y_ref):
    del ta_ref, tb_ref

    @pl.when(pl.program_id(0) < na_ref[0])
    def _():
        x1 = xs_ref[:, 0:D_MODEL]
        xb = ((x1 * xs_ref[:, D_MODEL + 2:D_MODEL + 3]) * nf_ref[...]).astype(BF16)

        ups = [(_dot(xb, wg[0]), _dot(xb, wu[0])) for wg, wu in ((wga_ref, wua_ref), (wgb_ref, wub_ref))]
        y = x1
        for k, ((a, u), wd) in enumerate(zip(ups, (wda_ref, wdb_ref))):
            hdn = (jax.nn.silu(a) * u) * xs_ref[:, D_MODEL + k:D_MODEL + k + 1]
            y = y + _dot(hdn.astype(BF16), wd[0])
        y_ref[...] = y

    @pl.when(pl.program_id(0) >= na_ref[0])
    def _():
        y_ref[...] = jnp.zeros(y_ref.shape, F32)


def _moe_experts(l, tile_a, tile_b, n_act, xs, nf, wg, wu, wd):
    nt = xs.shape[0] // MOE_TM
    last = lambda i, na: jnp.minimum(i, na[0] - 1)
    expert = lambda sel, i, ta, tb, na: (l * N_EXPERTS + sel(ta, tb)[last(i, na)], 0, 0)
    w_up = lambda sel: pl.BlockSpec((1, D_MODEL, D_EXPERT), functools.partial(expert, sel))
    w_dn = lambda sel: pl.BlockSpec((1, D_EXPERT, D_MODEL), functools.partial(expert, sel))
    sa = lambda ta, tb: ta
    sb = lambda ta, tb: tb
    grid_spec = pltpu.PrefetchScalarGridSpec(
        num_scalar_prefetch=3,
        grid=(nt,),
        in_specs=[pl.BlockSpec((MOE_TM, ROW_W), lambda i, ta, tb, na: (last(i, na), 0)),
                  _layer_spec(l, 1, D_MODEL),
                  w_up(sa), w_up(sa), w_dn(sa), w_up(sb), w_up(sb), w_dn(sb)],
        out_specs=pl.BlockSpec((MOE_TM, D_MODEL), lambda i, ta, tb, na: (i, 0)),
    )
    return pl.pallas_call(
        _moe_body,
        grid_spec=grid_spec,
        out_shape=jax.ShapeDtypeStruct((xs.shape[0], D_MODEL), F32),
        compiler_params=_cp(("arbitrary",)),
        name="moe_experts",
    )(tile_a, tile_b, n_act, xs, nf, wg, wu, wd, wg, wu, wd)


def _unpermute_body(pos_ref, ys_ref, o_ref, sem, *, tm):
    def group(g, c):
        for u in range(SUB):
            row = ys_ref.at[pl.ds(pos_ref[0, 0, g * SUB + u], 1), :]
            pltpu.make_async_copy(row, o_ref.at[g, pl.ds(u, 1), :], sem).start()
        return c

    lax.fori_loop(0, tm // SUB, group, 0)
    _wait_rows(o_ref, sem)


def _unpermute(pos, ys, tm):
    t = pos.shape[0] * tm
    return pl.pallas_call(
        functools.partial(_unpermute_body, tm=tm),
        grid=(t // tm,),
        in_specs=[pl.BlockSpec((1, 1, tm), lambda i: (i, 0, 0), memory_space=pltpu.SMEM),
                  pl.BlockSpec(memory_space=pl.ANY)],
        out_specs=pl.BlockSpec((tm // SUB, SUB, D_MODEL), lambda i: (i, 0, 0)),
        out_shape=jax.ShapeDtypeStruct((t // SUB, SUB, D_MODEL), F32),
        scratch_shapes=[pltpu.SemaphoreType.DMA(())],
        compiler_params=_cp(("arbitrary",)),
        name="moe_unpermute",
    )(pos, ys)


def _rope_tables(pos):
    half = ROT_DIM // 2
    inv_freq = jnp.float32(ROPE_THETA) ** (-jnp.arange(half, dtype=jnp.float32) * (2.0 / ROT_DIM))
    dim = np.arange(LANES) % HEAD_DIM
    ang = pos.astype(jnp.float32)[:, None] * inv_freq[dim % half][None, :]
    cos, sin = jnp.cos(ang), jnp.sin(ang)
    first, second = dim < half, (dim >= half) & (dim < ROT_DIM)
    return (jnp.where(first | second, cos, 1.0), jnp.where(first, -sin, 0.0), jnp.where(second, sin, 0.0))


def _seg_ones(n):
    idx = np.arange(n) // HEAD_DIM
    return jnp.asarray(idx[:, None] == idx[None, :], BF16)


def _bucket_experts():
    ea, eb = [], []
    for g in range(N_GROUPS):
        for a in range(EXPERTS_PER_GROUP):
            for b in range(a + 1, EXPERTS_PER_GROUP):
                ea.append(g * EXPERTS_PER_GROUP + a)
                eb.append(g * EXPERTS_PER_GROUP + b)
    return np.asarray(ea, np.int32), np.asarray(eb, np.int32)


def _dispatch_plan(counts, n_tiles):
    padded = ((counts + MOE_TM - 1) // MOE_TM) * MOE_TM
    ends = jnp.cumsum(padded)
    offs = ends - padded
    n_act = jnp.maximum(ends[-1] // MOE_TM, 1)
    starts = jnp.arange(n_tiles, dtype=jnp.int32) * MOE_TM
    tile_bucket = jnp.minimum(jnp.sum(starts[:, None] >= ends[None, :], axis=1), N_BUCKETS - 1)
    ea, eb = _bucket_experts()
    onehot = tile_bucket[:, None] == jnp.arange(N_BUCKETS)[None, :]
    tile_a = jnp.sum(jnp.where(onehot, ea[None, :], 0), axis=1).astype(jnp.int32)
    tile_b = jnp.sum(jnp.where(onehot, eb[None, :], 0), axis=1).astype(jnp.int32)
    fill_tile = jnp.maximum(ends // MOE_TM - 1, 0).astype(jnp.int32)
    return offs, tile_a, tile_b, n_act.astype(jnp.int32).reshape(1), fill_tile


def _positions(offs, bucket, rank):
    onehot = bucket[..., None] == jnp.arange(N_BUCKETS, dtype=jnp.int32)
    return (jnp.sum(jnp.where(onehot, offs.astype(jnp.int32), 0), axis=-1) + rank).astype(jnp.int32)


def kernel(x_prompt, x_sample, cache_k, cache_v, state_conv, meta_tokens, norm_mix, w_in, conv_w, q_norm, k_norm,
           attn_sinks, w_conv_out, w_attn_out, w_o, norm_ffn, w_router_group, b_router_group, w_router_expert,
           b_router_expert, w_exp_gate, w_exp_up, w_exp_down):
    batch, seq, _ = x_prompt.shape
    depth = w_in.shape[0]
    n_dec = x_sample.shape[0]
    past_len = PAST_LEN
    lp = PAD + N_META + seq
    tm_in, tm_out, qb = 640, 640, 5
    tm_move, tm_last = 3328, 4096
    assert PAD + N_META == BLOCK and seq % BLOCK == 0 and tm_in % BLOCK == 0
    assert lp % tm_in == 0 and (batch * lp) % tm_out == 0 and lp % (qb * BLOCK) == 0
    assert (batch * lp) % tm_move == 0 and (batch * seq) % tm_last == 0
    assert x_sample.shape[1] == 1 and cache_k.shape[2] == WINDOW and past_len >= WINDOW

    t_prompt = batch * lp
    t_all = t_prompt + n_dec
    n_tiles = -(-(t_all + N_BUCKETS * (MOE_TM - 1)) // MOE_TM)

    meta = jnp.broadcast_to(meta_tokens[None].astype(F32), (batch, N_META, D_MODEL))
    head = jnp.concatenate([jnp.zeros((batch, PAD, D_MODEL), F32), meta], axis=1)
    xs = x_sample.reshape(n_dec, D_MODEL)

    rope_p = _rope_tables(jnp.arange(lp, dtype=jnp.int32) - PAD)
    rope_s = _rope_tables(jnp.full((1,), past_len, jnp.int32))
    s256, s128 = _seg_ones(256), _seg_ones(LANES)
    tri_p = jnp.asarray(np.triu(np.ones((tm_out, tm_out)), 1), BF16)
    tri_s = jnp.asarray(np.triu(np.ones((n_dec, n_dec)), 1), BF16)
    zero_cnt = jnp.zeros((LANES, LANES), F32)

    w_mix = w_in.astype(BF16)
    w_gate = w_mix[:, :, D_MIX:]
    wco, wao, wo = w_conv_out.astype(BF16), w_attn_out.astype(BF16), w_o.astype(BF16)
    wg = w_exp_gate.astype(BF16).reshape(depth * N_EXPERTS, D_MODEL, D_EXPERT)
    wu = w_exp_up.astype(BF16).reshape(depth * N_EXPERTS, D_MODEL, D_EXPERT)
    wd = w_exp_down.astype(BF16).reshape(depth * N_EXPERTS, D_EXPERT, D_MODEL)
    nm, nf = norm_mix.reshape(depth, 1, D_MODEL), norm_ffn.reshape(depth, 1, D_MODEL)
    qn = (jnp.tile(q_norm, (1, N_HEADS)) * Q_SCALE).reshape(depth, 1, D_Q)
    kn = jnp.tile(k_norm, (1, N_KV_HEADS)).reshape(depth, 1, D_KV)
    r_pad = LANES - N_GROUPS - N_EXPERTS
    wr = jnp.concatenate([w_router_group, w_router_expert, jnp.zeros((depth, D_MODEL, r_pad), F32)], axis=-1).astype(BF16)
    br = jnp.concatenate([b_router_group, b_router_expert, jnp.zeros((depth, r_pad), F32)], axis=-1).reshape(depth, 1, LANES)
    sinks = attn_sinks.astype(F32)
    sinkb = jnp.broadcast_to(sinks[:, :, None], (depth, N_HEADS, LANES))
    ck = cache_k.reshape(depth, n_dec, WINDOW, D_KV)
    cv = cache_v.reshape(depth, n_dec, WINDOW, D_KV)

    outs = {k: [] for k in ("kp", "vp", "cp", "ks", "vs", "cs")}
    for l in range(depth):
        if l == 0:
            cy, q, k, v, ulast, xp = _prompt_in(l, x_prompt, nm, w_mix, conv_w, qn, kn, rope_p, s256, s128, tm_in, head)
        else:
            cy, q, k, v, ulast = _prompt_in(l, xp, nm, w_mix, conv_w, qn, kn, rope_p, s256, s128, tm_in)
        ao = _prompt_attn(l, q, k, v, sinks, qb)
        xep, rankp, bktp, cnt = _mix_out(
            l, xp.reshape(t_prompt, D_MODEL), cy.reshape(t_prompt, D_CONV), ao.reshape(t_prompt, D_Q),
            nm, w_gate, wco, wao, wo, nf, wr, br, tri_p, zero_cnt, tm_out)
        outs["kp"].append(k[:, lp - WINDOW:].reshape(batch, WINDOW, N_KV_HEADS, HEAD_DIM))
        outs["vp"].append(v[:, lp - WINDOW:].reshape(batch, WINDOW, N_KV_HEADS, HEAD_DIM))
        outs["cp"].append(ulast[:, 8 - (CONV_W - 1):])

        c0, c1 = state_conv[l, :, 0, :], state_conv[l, :, 1, :]
        cys, qx, ksn, vsn, us = _sample_in(l, xs, c0, c1, nm, w_mix, conv_w, qn, kn, rope_s, s256, s128)
        ox, nk, nv = _sample_attn(l, jnp.transpose(qx, (1, 0, 2)), sinkb, ck, cv, ksn, vsn)
        ox = ox.reshape(n_dec, N_KV_HEADS, GQA, N_KV_HEADS, HEAD_DIM)
        aos = jnp.stack([ox[:, j, :, j, :] for j in range(N_KV_HEADS)], axis=1).reshape(n_dec, D_Q).astype(BF16)
        xes, ranks, bkts, cnt = _mix_out(l, xs, cys, aos, nm, w_gate, wco, wao, wo, nf, wr, br, tri_s, cnt, n_dec)
        outs["ks"].append(nk.reshape(n_dec, WINDOW, N_KV_HEADS, HEAD_DIM))
        outs["vs"].append(nv.reshape(n_dec, WINDOW, N_KV_HEADS, HEAD_DIM))
        outs["cs"].append(jnp.stack([c1, us], axis=1))

        counts = cnt[:N_BUCKETS, 0].astype(jnp.int32)
        offs, tile_a, tile_b, n_act, fill_tile = _dispatch_plan(counts, n_tiles)
        posp = _positions(offs, bktp, rankp).reshape(t_prompt // tm_move, 1, tm_move)
        poss = _positions(offs, bkts, ranks)
        by8 = lambda a: a.reshape(a.shape[0] // SUB, SUB, a.shape[1])
        sorted_rows = _scatter_rows(fill_tile, n_act, posp, by8(xep), n_tiles, tm_move)
        sorted_rows = _scatter_more_rows(poss, by8(xes), sorted_rows, n_dec)
        ys = _moe_experts(l, tile_a, tile_b, n_act, sorted_rows, nf, wg, wu, wd)
        xs = _unpermute(poss, ys, n_dec).reshape(n_dec, D_MODEL)
        if l + 1 < depth:
            xp = _unpermute(posp, ys, tm_move).reshape(batch, lp, D_MODEL)
        else:
            pos_tok = posp.reshape(batch, lp)[:, PAD + N_META:].reshape(batch * seq // tm_last, 1, tm_last)
            y_prompt = _unpermute(pos_tok, ys, tm_last).reshape(batch, seq, D_MODEL)

    y_sample = xs.reshape(n_dec, 1, D_MODEL)
    st = lambda k: jnp.stack(outs[k])
    return (y_prompt, y_sample, st("kp"), st("vp"), st("cp"), st("ks"), st("vs"), st("cs"))
```

```python
import functools

import jax
import jax.numpy as jnp
import numpy as np
from jax import lax
from jax.experimental import pallas as pl
from jax.experimental.pallas import tpu as pltpu

D_MODEL = 1024
N_META = 16
D_CONV = D_MODEL
CONV_W = 3
N_HEADS = 16
N_KV_HEADS = 2
HEAD_DIM = 64
GQA = N_HEADS // N_KV_HEADS
ROT_DIM = HEAD_DIM // 4
ROPE_THETA = 500000.0
WINDOW = 128
PAST_LEN = 8192
BLOCK = 128
N_GROUPS = 4
EXPERTS_PER_GROUP = 4
N_EXPERTS = N_GROUPS * EXPERTS_PER_GROUP
D_EXPERT = 512
EPS = 1e-6
NEG = -1e30
D_Q = N_HEADS * HEAD_DIM
D_KV = N_KV_HEADS * HEAD_DIM
C_B, C_C, C_HC = 0, D_CONV, 2 * D_CONV
C_Q = 3 * D_CONV
C_K = C_Q + D_Q
C_V = C_K + D_KV
C_G = C_V + D_KV
D_MIX = C_G
D_IN = C_G + 2 * D_MODEL

LANES = 128
SUBLANES = 8
PAD = (-N_META) % BLOCK
N_PAIRS = 6
N_BUCKETS = N_GROUPS * N_PAIRS
MOE_TM = 256
ROW_W = D_MODEL + LANES
Q_SCALE = HEAD_DIM ** -0.5
assert Q_SCALE == 0.125

F32 = jnp.float32
BF16 = jnp.bfloat16
VMEM_LIMIT = 56 * 1024 * 1024


def _cp(sem, vmem=VMEM_LIMIT):
    return pltpu.CompilerParams(dimension_semantics=sem, vmem_limit_bytes=vmem)


def _const_spec(shape):
    nd = len(shape)
    return pl.BlockSpec(shape, lambda *_: (0,) * nd, pipeline_mode=pl.Buffered(1))


def _layer_spec(l, *shape):
    n = len(shape)
    return pl.BlockSpec((None,) + shape, lambda *_: (l,) + (0,) * n, pipeline_mode=pl.Buffered(1))


def _dot(a, b):
    return jnp.dot(a, b, preferred_element_type=F32)


def _seg_mean_sq(x, seg_ones):
    return _dot((x * x).astype(BF16), seg_ones) * (1.0 / HEAD_DIM)


def _rope128(t, cos, sin_pm, first):
    partner = jnp.where(first, pltpu.roll(t, LANES - ROT_DIM // 2, 1), pltpu.roll(t, ROT_DIM // 2, 1))
    return t * cos + partner * sin_pm


def _rms_scale(x):
    return lax.rsqrt(jnp.mean(x * x, axis=-1, keepdims=True) + EPS)


def _rms_rows(x, g):
    return (x * _rms_scale(x)) * g


def _qk_project(hb, w_ref, s256, s128):
    qs = [_dot(hb, w_ref[:, C_Q + c * 256:C_Q + (c + 1) * 256]) for c in range(D_Q // 256)]
    kc = _dot(hb, w_ref[:, C_K:C_K + D_KV])
    return qs, [_seg_mean_sq(qc, s256) for qc in qs], kc, _seg_mean_sq(kc, s128)


def _qk_finish(proj, qn, kn, cos, sneg, spos, store_q):
    qs, q_ms, kc, k_ms = proj
    sin_pm = sneg + spos
    lane = lax.broadcasted_iota(jnp.int32, cos.shape, 1)
    first = lax.bitwise_and(lane, HEAD_DIM - 1) < ROT_DIM // 2
    for c, (qc, ms) in enumerate(zip(qs, q_ms)):
        qc = (qc * lax.rsqrt(ms + EPS)) * qn[:, c * 256:(c + 1) * 256]
        for s in range(2):
            r = _rope128(qc[:, s * LANES:(s + 1) * LANES], cos, sin_pm, first)
            store_q(2 * c + s, r.astype(BF16))
    kc = (kc * lax.rsqrt(k_ms + EPS)) * kn
    return _rope128(kc, cos, sin_pm, first)


def _in_body(x_ref, nm_ref, w_ref, cw_ref, qn_ref, kn_ref, cos_ref, sneg_ref, spos_ref,
             s256_ref, s128_ref, cy_ref, q_ref, k_ref, v_ref, ul_ref, us_ref, *, tm, parts):
    i = pl.program_id(1)
    th = tm // parts

    @pl.when(i == 0)
    def _():
        us_ref[0:8, :] = jnp.zeros((8, D_CONV), F32)

    projs = []
    for h in range(parts):
        r0 = h * th
        hb = _rms_rows(x_ref[0, r0:r0 + th, :], nm_ref[...]).astype(BF16)
        projs.append(_qk_project(hb, w_ref, s256_ref[...], s128_ref[...]))
        u = _dot(hb, w_ref[:, C_C:C_C + D_CONV]) * _dot(hb, w_ref[:, C_HC:C_HC + D_CONV])
        row = lax.broadcasted_iota(jnp.int32, (th, 1), 0) + (i * tm + r0)
        u = jnp.where(row >= PAD, u, 0.0)
        us_ref[8 + r0:8 + r0 + th, :] = u
        conv = (us_ref[6 + r0:6 + r0 + th, :] * cw_ref[0:1, :] + us_ref[7 + r0:7 + r0 + th, :] * cw_ref[1:2, :]) \
            + u * cw_ref[2:3, :]
        cy_ref[0, r0:r0 + th, :] = (_dot(hb, w_ref[:, C_B:C_B + D_CONV]) * conv).astype(BF16)
        v_ref[0, r0:r0 + th, :] = _dot(hb, w_ref[:, C_V:C_V + D_KV])

    last = us_ref[tm:tm + 8, :]
    ul_ref[0] = last
    us_ref[0:8, :] = last

    for h in range(parts):
        rows = slice(h * th, (h + 1) * th)

        def store_q(slab, val, rows=rows):
            q_ref[0, rows, slab * LANES:(slab + 1) * LANES] = val

        k_ref[0, rows, :] = _qk_finish(projs[h], qn_ref[...], kn_ref[...], cos_ref[rows, :], sneg_ref[rows, :],
                                       spos_ref[rows, :], store_q)


def _in_first_body(head_ref, *refs, tm, parts):
    nb = tm // BLOCK
    blocks, rest, xpad_ref, us_ref = refs[:nb], refs[nb:-2], refs[-2], refs[-1]
    first = pl.program_id(1) == 0
    xpad_ref[0, 0:BLOCK, :] = jnp.where(first, head_ref[0], blocks[0][0])
    for j in range(1, nb):
        xpad_ref[0, j * BLOCK:(j + 1) * BLOCK, :] = blocks[j][0]
    _in_body(xpad_ref, *rest, us_ref, tm=tm, parts=parts)


def _prompt_in(l, x, nm, w_mix, cw, qn, kn, rope, s256, s128, tm, head=None):
    b = x.shape[0]
    lp = x.shape[1] if head is None else x.shape[1] + BLOCK
    nt = lp // tm
    nb = tm // BLOCK
    cos, sneg, spos = rope
    tok = lambda w: pl.BlockSpec((1, tm, w), lambda bi, i: (bi, i, 0))
    tab = pl.BlockSpec((tm, LANES), lambda bi, i: (i, 0))
    params = [_layer_spec(l, 1, D_MODEL), _layer_spec(l, D_MODEL, D_MIX),
              _layer_spec(l, CONV_W, D_CONV), _layer_spec(l, 1, D_Q), _layer_spec(l, 1, D_KV),
              tab, tab, tab, _const_spec((256, 256)), _const_spec((LANES, LANES))]
    out_specs = [tok(D_CONV), tok(D_Q), tok(D_KV), tok(D_KV), pl.BlockSpec((1, 8, D_CONV), lambda bi, i: (bi, 0, 0))]
    out_shape = [jax.ShapeDtypeStruct((b, lp, D_CONV), BF16), jax.ShapeDtypeStruct((b, lp, D_Q), BF16),
                 jax.ShapeDtypeStruct((b, lp, D_KV), F32), jax.ShapeDtypeStruct((b, lp, D_KV), F32),
                 jax.ShapeDtypeStruct((b, 8, D_CONV), F32)]
    if head is None:
        body, x_specs, x_args = _in_body, [tok(D_MODEL)], (x,)
    else:
        blk = lambda j: pl.BlockSpec((1, BLOCK, D_MODEL), lambda bi, i: (bi, jnp.maximum(nb * i - 1 + j, 0), 0))
        body = _in_first_body
        x_specs = [pl.BlockSpec((1, BLOCK, D_MODEL), lambda bi, i: (bi, 0, 0))] + [blk(j) for j in range(nb)]
        x_args = (head,) + (x,) * nb
        out_specs.append(tok(D_MODEL))
        out_shape.append(jax.ShapeDtypeStruct((b, lp, D_MODEL), F32))
    return pl.pallas_call(
        functools.partial(body, tm=tm, parts=2),
        grid=(b, nt),
        in_specs=x_specs + params,
        out_specs=out_specs,
        out_shape=out_shape,
        scratch_shapes=[pltpu.VMEM((tm + 8, D_CONV), F32)],
        compiler_params=_cp(("arbitrary", "arbitrary")),
        name="prompt_in",
    )(*x_args, nm, w_mix, cw, qn, kn, cos, sneg, spos, s256, s128)


def _sin_body(x_ref, c0_ref, c1_ref, nm_ref, w_ref, cw_ref, qn_ref, kn_ref, cos_ref, sneg_ref,
              spos_ref, s256_ref, s128_ref, cy_ref, qx_ref, k_ref, v_ref, u_ref):
    hb = _rms_rows(x_ref[...], nm_ref[...]).astype(BF16)
    u = _dot(hb, w_ref[:, C_C:C_C + D_CONV]) * _dot(hb, w_ref[:, C_HC:C_HC + D_CONV])
    u_ref[...] = u
    conv = (c0_ref[...] * cw_ref[0:1, :] + c1_ref[...] * cw_ref[1:2, :]) + u * cw_ref[2:3, :]
    cy_ref[...] = (_dot(hb, w_ref[:, C_B:C_B + D_CONV]) * conv).astype(BF16)

    lane = lax.broadcasted_iota(jnp.int32, (x_ref.shape[0], LANES), 1)
    low = lane < HEAD_DIM

    def store_q(slab, val):
        valf = val.astype(F32)
        swapped = pltpu.roll(valf, HEAD_DIM, 1)
        zero = jnp.zeros_like(valf)
        for h in (2 * slab, 2 * slab + 1):
            src = valf if (h % 2) == (h // GQA) else swapped
            keep = low if (h // GQA) == 0 else jnp.logical_not(low)
            qx_ref[h] = jnp.where(keep, src, zero).astype(BF16)

    cos = jnp.broadcast_to(cos_ref[...], (x_ref.shape[0], LANES))
    sneg = jnp.broadcast_to(sneg_ref[...], (x_ref.shape[0], LANES))
    spos = jnp.broadcast_to(spos_ref[...], (x_ref.shape[0], LANES))
    proj = _qk_project(hb, w_ref, s256_ref[...], s128_ref[...])
    k_ref[...] = _qk_finish(proj, qn_ref[...], kn_ref[...], cos, sneg, spos, store_q)
    v_ref[...] = _dot(hb, w_ref[:, C_V:C_V + D_KV])


def _sample_in(l, x, c0, c1, nm, w_mix, cw, qn, kn, rope, s256, s128):
    n = x.shape[0]
    cos, sneg, spos = rope
    full = lambda *s: pl.BlockSpec(s, lambda i: (0,) * len(s))
    return pl.pallas_call(
        _sin_body,
        grid=(1,),
        in_specs=[full(n, D_MODEL), full(n, D_CONV), full(n, D_CONV), _layer_spec(l, 1, D_MODEL),
                  _layer_spec(l, D_MODEL, D_MIX), _layer_spec(l, CONV_W, D_CONV), _layer_spec(l, 1, D_Q),
                  _layer_spec(l, 1, D_KV), full(1, LANES), full(1, LANES), full(1, LANES), full(256, 256),
                  full(LANES, LANES)],
        out_specs=[full(n, D_CONV), full(N_HEADS, n, LANES), full(n, D_KV), full(n, D_KV), full(n, D_CONV)],
        out_shape=[jax.ShapeDtypeStruct((n, D_CONV), BF16), jax.ShapeDtypeStruct((N_HEADS, n, LANES), BF16),
                   jax.ShapeDtypeStruct((n, D_KV), F32), jax.ShapeDtypeStruct((n, D_KV), F32),
                   jax.ShapeDtypeStruct((n, D_CONV), F32)],
        compiler_params=_cp(("arbitrary",)),
        name="sample_in",
    )(x, c0, c1, nm, w_mix, cw, qn, kn, cos, sneg, spos, s256, s128)


def _attn_body(sink_ref, q_ref, kp_ref, kc_ref, vp_ref, vc_ref, o_ref, ke_ref, ko_ref, vt_ref, *, qb, l):
    i = pl.program_id(1)
    lane = lax.broadcasted_iota(jnp.int32, (BLOCK, LANES), 1)
    low = lane < HEAD_DIM

    def prep_k(src, blk0, nblk):
        for t in range(nblk):
            blk = src[0, t * BLOCK:(t + 1) * BLOCK, :]
            swp = pltpu.roll(blk, HEAD_DIM, 1)
            zero = jnp.zeros_like(blk)
            rows = slice((blk0 + t) * BLOCK, (blk0 + t + 1) * BLOCK)
            ke_ref[0, rows, :] = jnp.where(low, blk, zero).astype(BF16)
            ko_ref[0, rows, :] = jnp.where(low, zero, swp).astype(BF16)
            ke_ref[1, rows, :] = jnp.where(low, swp, zero).astype(BF16)
            ko_ref[1, rows, :] = jnp.where(low, zero, blk).astype(BF16)

    def prep_v(src, blk0, nblk):
        for t in range(nblk):
            vt = jnp.transpose(src[0, t * BLOCK:(t + 1) * BLOCK, :]).astype(BF16)
            for j in range(N_KV_HEADS):
                vt_ref[blk0 + t, j] = vt[j * HEAD_DIM:(j + 1) * HEAD_DIM, :]

    prep_k(kp_ref, 0, 1)
    prep_k(kc_ref, 1, qb)
    prep_v(vp_ref, 0, 1)
    prep_v(vc_ref, 1, qb)

    c = lax.broadcasted_iota(jnp.int32, (2 * BLOCK, BLOCK), 0)
    r = lax.broadcasted_iota(jnp.int32, (2 * BLOCK, BLOCK), 1)
    diff = r - (c - BLOCK)
    band = (diff >= 0) & (diff < WINDOW)
    nt = (((1,), (1,)), ((), ()))

    def one_block(b, carry):
        r0 = pl.multiple_of(b * BLOCK, BLOCK)
        kpos = (i * qb + b) * BLOCK + c - BLOCK - PAD
        bias = jnp.where(band & (kpos >= 0), 0.0, NEG)
        for m in range(N_HEADS // 2):
            j = (2 * m) // GQA
            q2 = q_ref[0, pl.ds(r0, BLOCK), m * LANES:(m + 1) * LANES]
            halves = []
            for par, k_ref in ((0, ke_ref), (1, ko_ref)):
                st = lax.dot_general(k_ref[j, pl.ds(r0, 2 * BLOCK), :], q2, nt, preferred_element_type=F32)
                st = st + bias
                sink = sink_ref[l, 2 * m + par]
                mx = jnp.maximum(jnp.max(st, axis=0, keepdims=True), sink)
                p = jnp.exp(st - mx)
                den = jnp.sum(p, axis=0, keepdims=True) + jnp.exp(sink - mx)
                pb = p.astype(BF16)
                ot = _dot(vt_ref[b, j], pb[0:BLOCK]) + _dot(vt_ref[b + 1, j], pb[BLOCK:])
                halves.append(ot * (1.0 / den))
            o2 = jnp.transpose(jnp.concatenate(halves, axis=0))
            o_ref[0, pl.ds(r0, BLOCK), m * LANES:(m + 1) * LANES] = o2.astype(BF16)
        return carry

    lax.fori_loop(0, qb, one_block, 0, unroll=True)


def _prompt_attn(l, q, k, v, sinks, qb):
    b, lp, _ = q.shape
    nsteps = lp // (qb * BLOCK)
    cur = lambda w: pl.BlockSpec((1, qb * BLOCK, w), lambda bi, i: (bi, i, 0))
    prev = pl.BlockSpec((1, BLOCK, D_KV), lambda bi, i: (bi, jnp.maximum(i * qb - 1, 0), 0))
    ext = ((qb + 1) * BLOCK, LANES)
    return pl.pallas_call(
        functools.partial(_attn_body, qb=qb, l=l),
        grid=(b, nsteps),
        in_specs=[pl.BlockSpec(memory_space=pltpu.SMEM), cur(D_Q), prev, cur(D_KV), prev, cur(D_KV)],
        out_specs=cur(D_Q),
        out_shape=jax.ShapeDtypeStruct((b, lp, D_Q), BF16),
        scratch_shapes=[pltpu.VMEM((N_KV_HEADS,) + ext, BF16), pltpu.VMEM((N_KV_HEADS,) + ext, BF16),
                        pltpu.VMEM((qb + 1, N_KV_HEADS, HEAD_DIM, BLOCK), BF16)],
        compiler_params=_cp(("arbitrary", "arbitrary")),
        name="prompt_attn",
    )(sinks, q, k, k, v, v)


def _sattn_body(qx_ref, sink_ref, ck_ref, cv_ref, kn_ref, vn_ref, ox_ref, nk_ref, nv_ref, *, tb):
    def window(c_ref, n_ref, t):
        return jnp.concatenate([c_ref[t, 1:WINDOW, :], n_ref[t:t + 1, :]], axis=0)

    for t in range(tb):
        nk_ref[t] = window(ck_ref, kn_ref, t)
        nv_ref[t] = window(cv_ref, vn_ref, t)
    nt = (((1,), (1,)), ((), ()))
    s = jnp.concatenate([lax.dot_general(qx_ref[t], window(ck_ref, kn_ref, t).astype(BF16), nt,
                                         preferred_element_type=F32) for t in range(tb)], axis=0)
    sink = jnp.concatenate([sink_ref[...][:, 0:1]] * tb, axis=0)
    m = jnp.maximum(jnp.max(s, axis=-1, keepdims=True), sink)
    p = jnp.exp(s - m)
    rden = 1.0 / (jnp.sum(p, axis=-1, keepdims=True) + jnp.exp(sink - m))
    pb = p.astype(BF16)
    for t in range(tb):
        rows = slice(t * N_HEADS, (t + 1) * N_HEADS)
        ox_ref[t] = _dot(pb[rows], window(cv_ref, vn_ref, t).astype(BF16)) * rden[rows]


def _sample_attn(l, qx, sinkb, ck, cv, kn, vn, tb=16):
    n = qx.shape[0]
    blk3 = lambda a, c: pl.BlockSpec((tb, a, c), lambda i: (i, 0, 0))
    cache = pl.BlockSpec((None, tb, WINDOW, D_KV), lambda i: (l, i, 0, 0))
    row = pl.BlockSpec((tb, D_KV), lambda i: (i, 0))
    return pl.pallas_call(
        functools.partial(_sattn_body, tb=tb),
        grid=(n // tb,),
        in_specs=[blk3(N_HEADS, LANES), _layer_spec(l, N_HEADS, LANES), cache, cache, row, row],
        out_specs=[blk3(N_HEADS, LANES), blk3(WINDOW, D_KV), blk3(WINDOW, D_KV)],
        out_shape=[jax.ShapeDtypeStruct((n, N_HEADS, LANES), F32),
                   jax.ShapeDtypeStruct((n, WINDOW, D_KV), F32), jax.ShapeDtypeStruct((n, WINDOW, D_KV), F32)],
        compiler_params=_cp(("arbitrary",)),
        name="sample_attn",
    )(qx, sinkb, ck, cv, kn, vn)


ROUTE_ROWS = 24
RINV_LANE = LANES - 1


def _route(lt):
    top = lt[0:ROUTE_ROWS, :]
    rows = top.shape[1]
    rowf = lax.broadcasted_iota(jnp.int32, top.shape, 0).astype(F32)
    big = jnp.float32(3e38)
    far = jnp.float32(LANES)
    cmax = lambda a: jnp.max(a, axis=0, keepdims=True)
    cmin = lambda a: jnp.min(a, axis=0, keepdims=True)

    gmask = rowf < N_GROUPS
    gl = jnp.where(gmask, top, -big)
    gmax = cmax(gl)
    grp = cmin(jnp.where(gmask & (gl == gmax), rowf, far))
    p_grp = 1.0 / jnp.sum(jnp.where(gmask, jnp.exp(gl - gmax), 0.0), axis=0, keepdims=True)

    e_lo = N_GROUPS + EXPERTS_PER_GROUP * grp
    emask = (rowf >= e_lo) & (rowf < e_lo + EXPERTS_PER_GROUP)
    el = jnp.where(emask, top, -big)
    v1 = cmax(el)
    i1 = cmin(jnp.where(emask & (el == v1), rowf, far))
    rest = emask & (rowf != i1)
    el2 = jnp.where(rest, top, -big)
    v2 = cmax(el2)
    i2 = cmin(jnp.where(rest & (el2 == v2), rowf, far))
    e = jnp.exp(v2 - v1)
    w1 = (1.0 / (1.0 + e)) * p_grp
    w2 = (e / (1.0 + e)) * p_grp
    first_low = i1 < i2
    ea = jnp.where(first_low, i1, i2) - e_lo
    eb = jnp.where(first_low, i2, i1) - e_lo
    w_a = jnp.where(first_low, w1, w2)
    w_b = jnp.where(first_low, w2, w1)
    pair = jnp.where(ea == 0.0, 0.0, jnp.where(ea == 1.0, 3.0, 5.0)) + (eb - ea - 1.0)
    bucket = grp * N_PAIRS + pair

    r8 = lax.broadcasted_iota(jnp.int32, (SUBLANES, rows), 0)
    head = jnp.where(r8 == 0, w_a, jnp.where(r8 == 1, w_b, jnp.where(r8 == 2, lt[RINV_LANE:RINV_LANE + 1, :], 0.0)))
    return bucket, jnp.concatenate([head, jnp.zeros((LANES - SUBLANES, rows), F32)], axis=0)


def _out_body(x_ref, cy_ref, ao_ref, nm_ref, wg_ref, wco_ref, wao_ref, wo_ref, nf_ref, wr_ref, br_ref,
              tri_ref, cin_ref, xe_ref, rank_ref, bkt_ref, cnt_ref, run_ref, *, tm):
    i = pl.program_id(0)
    x = x_ref[...]
    hb = _rms_rows(x, nm_ref[...]).astype(BF16)
    ya = _dot(cy_ref[...], wco_ref[...])
    yb = _dot(ao_ref[...], wao_ref[...])
    mix = jax.nn.sigmoid(_dot(hb, wg_ref[:, 0:D_MODEL])) * ya + jax.nn.sigmoid(_dot(hb, wg_ref[:, D_MODEL:])) * yb
    x1 = x + _dot(mix.astype(BF16), wo_ref[...])
    xe_ref[:, 0:D_MODEL] = x1

    rinv = _rms_scale(x1)
    xnb = ((x1 * rinv) * nf_ref[...]).astype(BF16)
    logits = _dot(xnb, wr_ref[...]) + br_ref[...]
    lane = lax.broadcasted_iota(jnp.int32, (tm, LANES), 1)
    bucket, meta_t = _route(jnp.transpose(jnp.where(lane == RINV_LANE, rinv, logits)))
    xe_ref[:, D_MODEL:] = jnp.transpose(meta_t)

    @pl.when(i == 0)
    def _():
        run_ref[...] = cin_ref[...]

    sub = lax.broadcasted_iota(jnp.int32, (LANES, tm), 0).astype(F32)
    oht = (sub == bucket).astype(F32)
    before = _dot(oht.astype(BF16), tri_ref[...]) + run_ref[:, 0:1]
    rank_ref[0] = jnp.sum(oht * before, axis=0, keepdims=True).astype(jnp.int32)
    bkt_ref[0] = bucket.astype(jnp.int32)
    run_ref[...] = run_ref[...] + jnp.sum(oht, axis=-1, keepdims=True)
    cnt_ref[...] = run_ref[...]


def _mix_out(l, x, cy, ao, nm, w_gate, wco, wao, wo, nf, wr, br, tri, cnt_in, tm):
    t = x.shape[0]
    nt = t // tm
    tok = lambda w: pl.BlockSpec((tm, w), lambda i: (i, 0))
    rowi = pl.BlockSpec((1, 1, tm), lambda i: (i, 0, 0))
    sq = (D_MODEL, D_MODEL)
    return pl.pallas_call(
        functools.partial(_out_body, tm=tm),
        grid=(nt,),
        in_specs=[tok(D_MODEL), tok(D_CONV), tok(D_Q), _layer_spec(l, 1, D_MODEL),
                  _layer_spec(l, D_MODEL, 2 * D_MODEL), _layer_spec(l, *sq), _layer_spec(l, *sq), _layer_spec(l, *sq),
                  _layer_spec(l, 1, D_MODEL), _layer_spec(l, D_MODEL, LANES), _layer_spec(l, 1, LANES),
                  _const_spec((tm, tm)), _const_spec((LANES, LANES))],
        out_specs=[tok(ROW_W), rowi, rowi, pl.BlockSpec((LANES, LANES), lambda i: (0, 0))],
        out_shape=[jax.ShapeDtypeStruct((t, ROW_W), F32),
                   jax.ShapeDtypeStruct((nt, 1, tm), jnp.int32), jax.ShapeDtypeStruct((nt, 1, tm), jnp.int32),
                   jax.ShapeDtypeStruct((LANES, LANES), F32)],
        scratch_shapes=[pltpu.VMEM((LANES, LANES), F32)],
        compiler_params=_cp(("arbitrary",)),
        name="mix_out",
    )(x, cy, ao, nm, w_gate, wco, wao, wo, nf, wr, br, tri, cnt_in)


SUB = SUBLANES


def _wait_rows(block_ref, sem):
    pltpu.make_async_copy(block_ref, block_ref, sem).wait()


def _scatter_tile_rows(pos_ref, src_ref, dst_ref, sem, tm):
    def group(g, c):
        for u in range(SUB):
            row = dst_ref.at[pl.ds(pos_ref[0, 0, g * SUB + u], 1), :]
            pltpu.make_async_copy(src_ref.at[g, pl.ds(u, 1), :], row, sem).start()
        return c

    lax.fori_loop(0, tm // SUB, group, 0)
    _wait_rows(src_ref, sem)


def _scatter_body(fill_ref, na_ref, pos_ref, src_ref, dst_ref, zero_ref, sem, *, tm, n_tiles):
    tile_rows = lambda t: dst_ref.at[pl.ds(pl.multiple_of(t * MOE_TM, MOE_TM), MOE_TM), :]

    @pl.when(pl.program_id(0) == 0)
    def _():
        zero_ref[...] = jnp.zeros(zero_ref.shape, F32)
        for b in range(N_BUCKETS):
            pltpu.make_async_copy(zero_ref, tile_rows(fill_ref[b]), sem).start()

        def tail_start(t, c):
            pltpu.make_async_copy(zero_ref, tile_rows(t), sem).start()
            return c

        lax.fori_loop(na_ref[0], n_tiles, tail_start, 0)
        for b in range(N_BUCKETS):
            pltpu.make_async_copy(zero_ref, tile_rows(fill_ref[b]), sem).wait()

        def tail_wait(t, c):
            pltpu.make_async_copy(zero_ref, tile_rows(t), sem).wait()
            return c

        lax.fori_loop(na_ref[0], n_tiles, tail_wait, 0)

    _scatter_tile_rows(pos_ref, src_ref, dst_ref, sem, tm)


def _scatter_rows(fill_tile, n_act, pos, src, n_tiles, tm):
    nt = src.shape[0] * SUB // tm
    grid_spec = pltpu.PrefetchScalarGridSpec(
        num_scalar_prefetch=2,
        grid=(nt,),
        in_specs=[pl.BlockSpec((1, 1, tm), lambda i, *_: (i, 0, 0), memory_space=pltpu.SMEM),
                  pl.BlockSpec((tm // SUB, SUB, ROW_W), lambda i, *_: (i, 0, 0))],
        out_specs=pl.BlockSpec(memory_space=pl.ANY),
        scratch_shapes=[pltpu.VMEM((MOE_TM, ROW_W), F32), pltpu.SemaphoreType.DMA(())],
    )
    return pl.pallas_call(
        functools.partial(_scatter_body, tm=tm, n_tiles=n_tiles),
        grid_spec=grid_spec,
        out_shape=jax.ShapeDtypeStruct((n_tiles * MOE_TM, ROW_W), F32),
        compiler_params=_cp(("arbitrary",)),
        name="dispatch_scatter",
    )(fill_tile, n_act, pos, src)


def _scatter_more_body(pos_ref, src_ref, dst_in_ref, dst_ref, sem, *, tm):
    del dst_in_ref
    _scatter_tile_rows(pos_ref, src_ref, dst_ref, sem, tm)


def _scatter_more_rows(pos, src, dst, tm):
    nt = src.shape[0] * SUB // tm
    return pl.pallas_call(
        functools.partial(_scatter_more_body, tm=tm),
        grid=(nt,),
        in_specs=[pl.BlockSpec((1, 1, tm), lambda i: (i, 0, 0), memory_space=pltpu.SMEM),
                  pl.BlockSpec((tm // SUB, SUB, ROW_W), lambda i: (i, 0, 0)),
                  pl.BlockSpec(memory_space=pl.ANY)],
        out_specs=pl.BlockSpec(memory_space=pl.ANY),
        out_shape=jax.ShapeDtypeStruct(dst.shape, dst.dtype),
        scratch_shapes=[pltpu.SemaphoreType.DMA(())],
        input_output_aliases={2: 0},
        compiler_params=_cp(("arbitrary",)),
        name="dispatch_scatter_more",
    )(pos, src, dst)


MOE_STEP_TILES = 2


def _moe_body(ta_ref, tb_ref, na_ref, xs_ref, nf_ref, *refs):
    del ta_ref, tb_ref
    w_refs, y_ref = refs[:-1], refs[-1]
    first_tile = pl.program_id(0) * MOE_STEP_TILES

    @pl.when(first_tile < na_ref[0])
    def _():
        staged = []
        for t in range(MOE_STEP_TILES):
            rows = slice(t * MOE_TM, (t + 1) * MOE_TM)
            x1 = xs_ref[rows, 0:D_MODEL]
            xb = ((x1 * xs_ref[rows, D_MODEL + 2:D_MODEL + 3]) * nf_ref[...]).astype(BF16)
            w = w_refs[6 * t:6 * t + 6]
            staged.append((x1, [(_dot(xb, w[3 * k][0]), _dot(xb, w[3 * k + 1][0])) for k in range(2)]))
        for t in range(MOE_STEP_TILES):
            rows = slice(t * MOE_TM, (t + 1) * MOE_TM)
            y, ups = staged[t]
            for k, (a, u) in enumerate(ups):
                hdn = (jax.nn.silu(a) * u) * xs_ref[rows, D_MODEL + k:D_MODEL + k + 1]
                y = y + _dot(hdn.astype(BF16), w_refs[6 * t + 3 * k + 2][0])
            y_ref[rows, :] = y

    @pl.when(first_tile >= na_ref[0])
    def _():
        y_ref[...] = jnp.zeros(y_ref.shape, F32)


def _moe_experts(l, tile_a, tile_b, n_act, xs, nf, wg, wu, wd):
    step_rows = MOE_STEP_TILES * MOE_TM
    assert xs.shape[0] % step_rows == 0
    last = lambda tile, na: jnp.minimum(tile, na[0] - 1)
    expert = lambda sel, t, i, ta, tb, na: (l * N_EXPERTS + sel(ta, tb)[last(MOE_STEP_TILES * i + t, na)], 0, 0)
    w_up = lambda sel, t: pl.BlockSpec((1, D_MODEL, D_EXPERT), functools.partial(expert, sel, t))
    w_dn = lambda sel, t: pl.BlockSpec((1, D_EXPERT, D_MODEL), functools.partial(expert, sel, t))
    sa = lambda ta, tb: ta
    sb = lambda ta, tb: tb
    w_specs = []
    for t in range(MOE_STEP_TILES):
        w_specs += [w_up(sa, t), w_up(sa, t), w_dn(sa, t), w_up(sb, t), w_up(sb, t), w_dn(sb, t)]
    grid_spec = pltpu.PrefetchScalarGridSpec(
        num_scalar_prefetch=3,
        grid=(xs.shape[0] // step_rows,),
        in_specs=[pl.BlockSpec((step_rows, ROW_W), lambda i, ta, tb, na: (jnp.minimum(i, (na[0] - 1) // MOE_STEP_TILES), 0)),
                  _layer_spec(l, 1, D_MODEL)] + w_specs,
        out_specs=pl.BlockSpec((step_rows, D_MODEL), lambda i, ta, tb, na: (i, 0)),
    )
    return pl.pallas_call(
        _moe_body,
        grid_spec=grid_spec,
        out_shape=jax.ShapeDtypeStruct((xs.shape[0], D_MODEL), F32),
        compiler_params=_cp(("arbitrary",)),
        name="moe_experts",
    )(tile_a, tile_b, n_act, xs, nf, *((wg, wu, wd) * (2 * MOE_STEP_TILES)))


def _unpermute_body(pos_ref, ys_ref, o_ref, sem, *, tm):
    def group(g, c):
        for u in range(SUB):
            row = ys_ref.at[pl.ds(pos_ref[0, 0, g * SUB + u], 1), :]
            pltpu.make_async_copy(row, o_ref.at[g, pl.ds(u, 1), :], sem).start()
        return c

    lax.fori_loop(0, tm // SUB, group, 0)
    _wait_rows(o_ref, sem)


def _unpermute(pos, ys, tm):
    t = pos.shape[0] * tm
    return pl.pallas_call(
        functools.partial(_unpermute_body, tm=tm),
        grid=(t // tm,),
        in_specs=[pl.BlockSpec((1, 1, tm), lambda i: (i, 0, 0), memory_space=pltpu.SMEM),
                  pl.BlockSpec(memory_space=pl.ANY)],
        out_specs=pl.BlockSpec((tm // SUB, SUB, D_MODEL), lambda i: (i, 0, 0)),
        out_shape=jax.ShapeDtypeStruct((t // SUB, SUB, D_MODEL), F32),
        scratch_shapes=[pltpu.SemaphoreType.DMA(())],
        compiler_params=_cp(("arbitrary",)),
        name="moe_unpermute",
    )(pos, ys)


def _rope_tables(pos):
    half = ROT_DIM // 2
    inv_freq = jnp.float32(ROPE_THETA) ** (-jnp.arange(half, dtype=jnp.float32) * (2.0 / ROT_DIM))
    dim = np.arange(LANES) % HEAD_DIM
    ang = pos.astype(jnp.float32)[:, None] * inv_freq[dim % half][None, :]
    cos, sin = jnp.cos(ang), jnp.sin(ang)
    first, second = dim < half, (dim >= half) & (dim < ROT_DIM)
    return (jnp.where(first | second, cos, 1.0), jnp.where(first, -sin, 0.0), jnp.where(second, sin, 0.0))


def _seg_ones(n):
    idx = np.arange(n) // HEAD_DIM
    return jnp.asarray(idx[:, None] == idx[None, :], BF16)


def _bucket_experts():
    ea, eb = [], []
    for g in range(N_GROUPS):
        for a in range(EXPERTS_PER_GROUP):
            for b in range(a + 1, EXPERTS_PER_GROUP):
                ea.append(g * EXPERTS_PER_GROUP + a)
                eb.append(g * EXPERTS_PER_GROUP + b)
    return np.asarray(ea, np.int32), np.asarray(eb, np.int32)


def _dispatch_plan(counts, n_tiles):
    padded = ((counts + MOE_TM - 1) // MOE_TM) * MOE_TM
    ends = jnp.cumsum(padded)
    offs = ends - padded
    n_act = jnp.maximum(ends[-1] // MOE_TM, 1)
    starts = jnp.arange(n_tiles, dtype=jnp.int32) * MOE_TM
    tile_bucket = jnp.minimum(jnp.sum(starts[:, None] >= ends[None, :], axis=1), N_BUCKETS - 1)
    ea, eb = _bucket_experts()
    onehot = tile_bucket[:, None] == jnp.arange(N_BUCKETS)[None, :]
    tile_a = jnp.sum(jnp.where(onehot, ea[None, :], 0), axis=1).astype(jnp.int32)
    tile_b = jnp.sum(jnp.where(onehot, eb[None, :], 0), axis=1).astype(jnp.int32)
    fill_tile = jnp.maximum(ends // MOE_TM - 1, 0).astype(jnp.int32)
    return offs, tile_a, tile_b, n_act.astype(jnp.int32).reshape(1), fill_tile


def _positions(offs, bucket, rank):
    onehot = bucket[..., None] == jnp.arange(N_BUCKETS, dtype=jnp.int32)
    return (jnp.sum(jnp.where(onehot, offs.astype(jnp.int32), 0), axis=-1) + rank).astype(jnp.int32)


def kernel(x_prompt, x_sample, cache_k, cache_v, state_conv, meta_tokens, norm_mix, w_in, conv_w, q_norm, k_norm,
           attn_sinks, w_conv_out, w_attn_out, w_o, norm_ffn, w_router_group, b_router_group, w_router_expert,
           b_router_expert, w_exp_gate, w_exp_up, w_exp_down):
    batch, seq, _ = x_prompt.shape
    depth = w_in.shape[0]
    n_dec = x_sample.shape[0]
    past_len = PAST_LEN
    lp = PAD + N_META + seq
    tm_in, tm_out, qb = 640, 640, 5
    tm_move, tm_last = 3328, 4096
    assert PAD + N_META == BLOCK and seq % BLOCK == 0 and tm_in % BLOCK == 0
    assert lp % tm_in == 0 and (batch * lp) % tm_out == 0 and lp % (qb * BLOCK) == 0
    assert (batch * lp) % tm_move == 0 and (batch * seq) % tm_last == 0
    assert x_sample.shape[1] == 1 and cache_k.shape[2] == WINDOW and past_len >= WINDOW

    t_prompt = batch * lp
    t_all = t_prompt + n_dec
    n_tiles = -(-(t_all + N_BUCKETS * (MOE_TM - 1)) // MOE_TM)
    n_tiles = -(-n_tiles // MOE_STEP_TILES) * MOE_STEP_TILES

    meta = jnp.broadcast_to(meta_tokens[None].astype(F32), (batch, N_META, D_MODEL))
    head = jnp.concatenate([jnp.zeros((batch, PAD, D_MODEL), F32), meta], axis=1)
    xs = x_sample.reshape(n_dec, D_MODEL)

    rope_p = _rope_tables(jnp.arange(lp, dtype=jnp.int32) - PAD)
    rope_s = _rope_tables(jnp.full((1,), past_len, jnp.int32))
    s256, s128 = _seg_ones(256), _seg_ones(LANES)
    tri_p = jnp.asarray(np.triu(np.ones((tm_out, tm_out)), 1), BF16)
    tri_s = jnp.asarray(np.triu(np.ones((n_dec, n_dec)), 1), BF16)
    zero_cnt = jnp.zeros((LANES, LANES), F32)

    w_mix = w_in.astype(BF16)
    w_gate = w_mix[:, :, D_MIX:]
    wco, wao, wo = w_conv_out.astype(BF16), w_attn_out.astype(BF16), w_o.astype(BF16)
    wg = w_exp_gate.astype(BF16).reshape(depth * N_EXPERTS, D_MODEL, D_EXPERT)
    wu = w_exp_up.astype(BF16).reshape(depth * N_EXPERTS, D_MODEL, D_EXPERT)
    wd = w_exp_down.astype(BF16).reshape(depth * N_EXPERTS, D_EXPERT, D_MODEL)
    nm, nf = norm_mix.reshape(depth, 1, D_MODEL), norm_ffn.reshape(depth, 1, D_MODEL)
    qn = (jnp.tile(q_norm, (1, N_HEADS)) * Q_SCALE).reshape(depth, 1, D_Q)
    kn = jnp.tile(k_norm, (1, N_KV_HEADS)).reshape(depth, 1, D_KV)
    r_pad = LANES - N_GROUPS - N_EXPERTS
    wr = jnp.concatenate([w_router_group, w_router_expert, jnp.zeros((depth, D_MODEL, r_pad), F32)], axis=-1).astype(BF16)
    br = jnp.concatenate([b_router_group, b_router_expert, jnp.zeros((depth, r_pad), F32)], axis=-1).reshape(depth, 1, LANES)
    sinks = attn_sinks.astype(F32)
    sinkb = jnp.broadcast_to(sinks[:, :, None], (depth, N_HEADS, LANES))
    ck = cache_k.reshape(depth, n_dec, WINDOW, D_KV)
    cv = cache_v.reshape(depth, n_dec, WINDOW, D_KV)

    outs = {k: [] for k in ("kp", "vp", "cp", "ks", "vs", "cs")}
    for l in range(depth):
        if l == 0:
            cy, q, k, v, ulast, xp = _prompt_in(l, x_prompt, nm, w_mix, conv_w, qn, kn, rope_p, s256, s128, tm_in, head)
        else:
            cy, q, k, v, ulast = _prompt_in(l, xp, nm, w_mix, conv_w, qn, kn, rope_p, s256, s128, tm_in)
        ao = _prompt_attn(l, q, k, v, sinks, qb)
        xep, rankp, bktp, cnt = _mix_out(
            l, xp.reshape(t_prompt, D_MODEL), cy.reshape(t_prompt, D_CONV), ao.reshape(t_prompt, D_Q),
            nm, w_gate, wco, wao, wo, nf, wr, br, tri_p, zero_cnt, tm_out)
        outs["kp"].append(k[:, lp - WINDOW:].reshape(batch, WINDOW, N_KV_HEADS, HEAD_DIM))
        outs["vp"].append(v[:, lp - WINDOW:].reshape(batch, WINDOW, N_KV_HEADS, HEAD_DIM))
        outs["cp"].append(ulast[:, 8 - (CONV_W - 1):])

        c0, c1 = state_conv[l, :, 0, :], state_conv[l, :, 1, :]
        cys, qx, ksn, vsn, us = _sample_in(l, xs, c0, c1, nm, w_mix, conv_w, qn, kn, rope_s, s256, s128)
        ox, nk, nv = _sample_attn(l, jnp.transpose(qx, (1, 0, 2)), sinkb, ck, cv, ksn, vsn)
        ox = ox.reshape(n_dec, N_KV_HEADS, GQA, N_KV_HEADS, HEAD_DIM)
        aos = jnp.stack([ox[:, j, :, j, :] for j in range(N_KV_HEADS)], axis=1).reshape(n_dec, D_Q).astype(BF16)
        xes, ranks, bkts, cnt = _mix_out(l, xs, cys, aos, nm, w_gate, wco, wao, wo, nf, wr, br, tri_s, cnt, n_dec)
        outs["ks"].append(nk.reshape(n_dec, WINDOW, N_KV_HEADS, HEAD_DIM))
        outs["vs"].append(nv.reshape(n_dec, WINDOW, N_KV_HEADS, HEAD_DIM))
        outs["cs"].append(jnp.stack([c1, us], axis=1))

        counts = cnt[:N_BUCKETS, 0].astype(jnp.int32)
        offs, tile_a, tile_b, n_act, fill_tile = _dispatch_plan(counts, n_tiles)
        posp = _positions(offs, bktp, rankp).reshape(t_prompt // tm_move, 1, tm_move)
        poss = _positions(offs, bkts, ranks)
        by8 = lambda a: a.reshape(a.shape[0] // SUB, SUB, a.shape[1])
        sorted_rows = _scatter_rows(fill_tile, n_act, posp, by8(xep), n_tiles, tm_move)
        sorted_rows = _scatter_more_rows(poss, by8(xes), sorted_rows, n_dec)
        ys = _moe_experts(l, tile_a, tile_b, n_act, sorted_rows, nf, wg, wu, wd)
        xs = _unpermute(poss, ys, n_dec).reshape(n_dec, D_MODEL)
        if l + 1 < depth:
            xp = _unpermute(posp, ys, tm_move).reshape(batch, lp, D_MODEL)
        else:
            pos_tok = posp.reshape(batch, lp)[:, PAD + N_META:].reshape(batch * seq // tm_last, 1, tm_last)
            y_prompt = _unpermute(pos_tok, ys, tm_last).reshape(batch, seq, D_MODEL)

    y_sample = xs.reshape(n_dec, 1, D_MODEL)
    st = lambda k: jnp.stack(outs[k])
    return (y_prompt, y_sample, st("kp"), st("vp"), st("cp"), st("ks"), st("vs"), st("cs"))
```

```python
import functools

import jax
import jax.numpy as jnp
import numpy as np
from jax import lax
from jax.experimental import pallas as pl
from jax.experimental.pallas import tpu as pltpu

D_MODEL = 1024
N_META = 16
D_CONV = D_MODEL
CONV_W = 3
N_HEADS = 16
N_KV_HEADS = 2
HEAD_DIM = 64
GQA = N_HEADS // N_KV_HEADS
ROT_DIM = HEAD_DIM // 4
ROPE_THETA = 500000.0
WINDOW = 128
PAST_LEN = 8192
BLOCK = 128
N_GROUPS = 4
EXPERTS_PER_GROUP = 4
N_EXPERTS = N_GROUPS * EXPERTS_PER_GROUP
D_EXPERT = 512
EPS = 1e-6
NEG = -1e30
D_Q = N_HEADS * HEAD_DIM
D_KV = N_KV_HEADS * HEAD_DIM
C_B, C_C, C_HC = 0, D_CONV, 2 * D_CONV
C_Q = 3 * D_CONV
C_K = C_Q + D_Q
C_V = C_K + D_KV
C_G = C_V + D_KV
D_MIX = C_G
D_IN = C_G + 2 * D_MODEL

LANES = 128
SUBLANES = 8
PAD = (-N_META) % BLOCK
N_PAIRS = 6
N_BUCKETS = N_GROUPS * N_PAIRS
MOE_TM = 256
ROW_W = D_MODEL + LANES
Q_SCALE = HEAD_DIM ** -0.5
assert Q_SCALE == 0.125

F32 = jnp.float32
BF16 = jnp.bfloat16
VMEM_LIMIT = 56 * 1024 * 1024


def _cp(sem, vmem=VMEM_LIMIT):
    return pltpu.CompilerParams(dimension_semantics=sem, vmem_limit_bytes=vmem)


def _const_spec(shape):
    nd = len(shape)
    return pl.BlockSpec(shape, lambda *_: (0,) * nd, pipeline_mode=pl.Buffered(1))


def _layer_spec(l, *shape):
    n = len(shape)
    return pl.BlockSpec((None,) + shape, lambda *_: (l,) + (0,) * n, pipeline_mode=pl.Buffered(1))


def _w_mix_spec(l):
    return pl.BlockSpec((None, D_MODEL, D_MIX), lambda *_: (l, 0, 1), pipeline_mode=pl.Buffered(1))


def _dot(a, b):
    return jnp.dot(a, b, preferred_element_type=F32)


def _seg_mean_sq(x, seg_ones):
    return _dot((x * x).astype(BF16), seg_ones) * (1.0 / HEAD_DIM)


def _rope128(t, cos, sin_pm, first):
    partner = jnp.where(first, pltpu.roll(t, LANES - ROT_DIM // 2, 1), pltpu.roll(t, ROT_DIM // 2, 1))
    return t * cos + partner * sin_pm


def _rms_scale(x):
    return lax.rsqrt(jnp.mean(x * x, axis=-1, keepdims=True) + EPS)


def _rms_rows(x, g):
    return (x * _rms_scale(x)) * g


def _qk_project(hb, w_ref, s256, s128):
    qs = [_dot(hb, w_ref[:, C_Q + c * 256:C_Q + (c + 1) * 256]) for c in range(D_Q // 256)]
    kc = _dot(hb, w_ref[:, C_K:C_K + D_KV])
    return qs, [_seg_mean_sq(qc, s256) for qc in qs], kc, _seg_mean_sq(kc, s128)


def _qk_finish(proj, qn, kn, cos, sneg, spos, store_q):
    qs, q_ms, kc, k_ms = proj
    sin_pm = sneg + spos
    lane = lax.broadcasted_iota(jnp.int32, cos.shape, 1)
    first = lax.bitwise_and(lane, HEAD_DIM - 1) < ROT_DIM // 2
    for c, (qc, ms) in enumerate(zip(qs, q_ms)):
        qc = (qc * lax.rsqrt(ms + EPS)) * qn[:, c * 256:(c + 1) * 256]
        for s in range(2):
            r = _rope128(qc[:, s * LANES:(s + 1) * LANES], cos, sin_pm, first)
            store_q(2 * c + s, r.astype(BF16))
    kc = (kc * lax.rsqrt(k_ms + EPS)) * kn
    return _rope128(kc, cos, sin_pm, first)


def _in_body(x_ref, nm_ref, w_ref, cw_ref, qn_ref, kn_ref, cos_ref, sneg_ref, spos_ref,
             s256_ref, s128_ref, cy_ref, q_ref, k_ref, v_ref, ul_ref, us_ref, *, tm, parts):
    i = pl.program_id(1)
    th = tm // parts

    @pl.when(i == 0)
    def _():
        us_ref[0:8, :] = jnp.zeros((8, D_CONV), F32)

    projs = []
    for h in range(parts):
        r0 = h * th
        hb = _rms_rows(x_ref[0, r0:r0 + th, :], nm_ref[...]).astype(BF16)
        projs.append(_qk_project(hb, w_ref, s256_ref[...], s128_ref[...]))
        u = _dot(hb, w_ref[:, C_C:C_C + D_CONV]) * _dot(hb, w_ref[:, C_HC:C_HC + D_CONV])
        row = lax.broadcasted_iota(jnp.int32, (th, 1), 0) + (i * tm + r0)
        u = jnp.where(row >= PAD, u, 0.0)
        us_ref[8 + r0:8 + r0 + th, :] = u
        conv = (us_ref[6 + r0:6 + r0 + th, :] * cw_ref[0:1, :] + us_ref[7 + r0:7 + r0 + th, :] * cw_ref[1:2, :]) \
            + u * cw_ref[2:3, :]
        cy_ref[0, r0:r0 + th, :] = (_dot(hb, w_ref[:, C_B:C_B + D_CONV]) * conv).astype(BF16)
        v_ref[0, r0:r0 + th, :] = _dot(hb, w_ref[:, C_V:C_V + D_KV])

    last = us_ref[tm:tm + 8, :]
    ul_ref[0] = last
    us_ref[0:8, :] = last

    for h in range(parts):
        rows = slice(h * th, (h + 1) * th)

        def store_q(slab, val, rows=rows):
            q_ref[0, rows, slab * LANES:(slab + 1) * LANES] = val

        k_ref[0, rows, :] = _qk_finish(projs[h], qn_ref[...], kn_ref[...], cos_ref[rows, :], sneg_ref[rows, :],
                                       spos_ref[rows, :], store_q)


def _in_first_body(head_ref, *refs, tm, parts):
    nb = tm // BLOCK
    blocks, rest, xpad_ref, us_ref = refs[:nb], refs[nb:-2], refs[-2], refs[-1]
    first = pl.program_id(1) == 0
    xpad_ref[0, 0:BLOCK, :] = jnp.where(first, head_ref[0], blocks[0][0])
    for j in range(1, nb):
        xpad_ref[0, j * BLOCK:(j + 1) * BLOCK, :] = blocks[j][0]
    _in_body(xpad_ref, *rest, us_ref, tm=tm, parts=parts)


def _prompt_in(l, x, nm, w_mix, cw, qn, kn, rope, s256, s128, tm, head=None):
    b = x.shape[0]
    lp = x.shape[1] if head is None else x.shape[1] + BLOCK
    nt = lp // tm
    nb = tm // BLOCK
    cos, sneg, spos = rope
    tok = lambda w: pl.BlockSpec((1, tm, w), lambda bi, i: (bi, i, 0))
    tab = pl.BlockSpec((tm, LANES), lambda bi, i: (i, 0))
    params = [_layer_spec(l, 1, D_MODEL), _w_mix_spec(l),
              _layer_spec(l, CONV_W, D_CONV), _layer_spec(l, 1, D_Q), _layer_spec(l, 1, D_KV),
              tab, tab, tab, _const_spec((256, 256)), _const_spec((LANES, LANES))]
    out_specs = [tok(D_CONV), tok(D_Q), tok(D_KV), tok(D_KV), pl.BlockSpec((1, 8, D_CONV), lambda bi, i: (bi, 0, 0))]
    out_shape = [jax.ShapeDtypeStruct((b, lp, D_CONV), BF16), jax.ShapeDtypeStruct((b, lp, D_Q), BF16),
                 jax.ShapeDtypeStruct((b, lp, D_KV), F32), jax.ShapeDtypeStruct((b, lp, D_KV), F32),
                 jax.ShapeDtypeStruct((b, 8, D_CONV), F32)]
    if head is None:
        body, x_specs, x_args = _in_body, [tok(D_MODEL)], (x,)
    else:
        blk = lambda j: pl.BlockSpec((1, BLOCK, D_MODEL), lambda bi, i: (bi, jnp.maximum(nb * i - 1 + j, 0), 0))
        body = _in_first_body
        x_specs = [pl.BlockSpec((1, BLOCK, D_MODEL), lambda bi, i: (bi, 0, 0))] + [blk(j) for j in range(nb)]
        x_args = (head,) + (x,) * nb
        out_specs.append(tok(D_MODEL))
        out_shape.append(jax.ShapeDtypeStruct((b, lp, D_MODEL), F32))
    return pl.pallas_call(
        functools.partial(body, tm=tm, parts=2),
        grid=(b, nt),
        in_specs=x_specs + params,
        out_specs=out_specs,
        out_shape=out_shape,
        scratch_shapes=[pltpu.VMEM((tm + 8, D_CONV), F32)],
        compiler_params=_cp(("arbitrary", "arbitrary")),
        name="prompt_in",
    )(*x_args, nm, w_mix, cw, qn, kn, cos, sneg, spos, s256, s128)


def _sin_body(x_ref, c0_ref, c1_ref, nm_ref, w_ref, cw_ref, qn_ref, kn_ref, cos_ref, sneg_ref,
              spos_ref, s256_ref, s128_ref, cy_ref, qx_ref, k_ref, v_ref, u_ref):
    hb = _rms_rows(x_ref[...], nm_ref[...]).astype(BF16)
    u = _dot(hb, w_ref[:, C_C:C_C + D_CONV]) * _dot(hb, w_ref[:, C_HC:C_HC + D_CONV])
    u_ref[...] = u
    conv = (c0_ref[...] * cw_ref[0:1, :] + c1_ref[...] * cw_ref[1:2, :]) + u * cw_ref[2:3, :]
    cy_ref[...] = (_dot(hb, w_ref[:, C_B:C_B + D_CONV]) * conv).astype(BF16)

    lane = lax.broadcasted_iota(jnp.int32, (x_ref.shape[0], LANES), 1)
    low = lane < HEAD_DIM

    def store_q(slab, val):
        valf = val.astype(F32)
        swapped = pltpu.roll(valf, HEAD_DIM, 1)
        zero = jnp.zeros_like(valf)
        for h in (2 * slab, 2 * slab + 1):
            src = valf if (h % 2) == (h // GQA) else swapped
            keep = low if (h // GQA) == 0 else jnp.logical_not(low)
            qx_ref[h] = jnp.where(keep, src, zero).astype(BF16)

    cos = jnp.broadcast_to(cos_ref[...], (x_ref.shape[0], LANES))
    sneg = jnp.broadcast_to(sneg_ref[...], (x_ref.shape[0], LANES))
    spos = jnp.broadcast_to(spos_ref[...], (x_ref.shape[0], LANES))
    proj = _qk_project(hb, w_ref, s256_ref[...], s128_ref[...])
    k_ref[...] = _qk_finish(proj, qn_ref[...], kn_ref[...], cos, sneg, spos, store_q)
    v_ref[...] = _dot(hb, w_ref[:, C_V:C_V + D_KV])


def _sample_in(l, x, c0, c1, nm, w_mix, cw, qn, kn, rope, s256, s128):
    n = x.shape[0]
    cos, sneg, spos = rope
    full = lambda *s: pl.BlockSpec(s, lambda i: (0,) * len(s))
    return pl.pallas_call(
        _sin_body,
        grid=(1,),
        in_specs=[full(n, D_MODEL), full(n, D_CONV), full(n, D_CONV), _layer_spec(l, 1, D_MODEL),
                  _w_mix_spec(l), _layer_spec(l, CONV_W, D_CONV), _layer_spec(l, 1, D_Q),
                  _layer_spec(l, 1, D_KV), full(1, LANES), full(1, LANES), full(1, LANES), full(256, 256),
                  full(LANES, LANES)],
        out_specs=[full(n, D_CONV), full(N_HEADS, n, LANES), full(n, D_KV), full(n, D_KV), full(n, D_CONV)],
        out_shape=[jax.ShapeDtypeStruct((n, D_CONV), BF16), jax.ShapeDtypeStruct((N_HEADS, n, LANES), BF16),
                   jax.ShapeDtypeStruct((n, D_KV), F32), jax.ShapeDtypeStruct((n, D_KV), F32),
                   jax.ShapeDtypeStruct((n, D_CONV), F32)],
        compiler_params=_cp(("arbitrary",)),
        name="sample_in",
    )(x, c0, c1, nm, w_mix, cw, qn, kn, cos, sneg, spos, s256, s128)


def _attn_body(sink_ref, q_ref, kp_ref, kc_ref, vp_ref, vc_ref, o_ref, ke_ref, ko_ref, vt_ref, *, qb, l):
    i = pl.program_id(1)
    lane = lax.broadcasted_iota(jnp.int32, (BLOCK, LANES), 1)
    low = lane < HEAD_DIM

    def prep_k(src, blk0, nblk):
        for t in range(nblk):
            blk = src[0, t * BLOCK:(t + 1) * BLOCK, :]
            swp = pltpu.roll(blk, HEAD_DIM, 1)
            zero = jnp.zeros_like(blk)
            rows = slice((blk0 + t) * BLOCK, (blk0 + t + 1) * BLOCK)
            ke_ref[0, rows, :] = jnp.where(low, blk, zero).astype(BF16)
            ko_ref[0, rows, :] = jnp.where(low, zero, swp).astype(BF16)
            ke_ref[1, rows, :] = jnp.where(low, swp, zero).astype(BF16)
            ko_ref[1, rows, :] = jnp.where(low, zero, blk).astype(BF16)

    def prep_v(src, blk0, nblk):
        for t in range(nblk):
            vt = jnp.transpose(src[0, t * BLOCK:(t + 1) * BLOCK, :]).astype(BF16)
            for j in range(N_KV_HEADS):
                vt_ref[blk0 + t, j] = vt[j * HEAD_DIM:(j + 1) * HEAD_DIM, :]

    prep_k(kp_ref, 0, 1)
    prep_k(kc_ref, 1, qb)
    prep_v(vp_ref, 0, 1)
    prep_v(vc_ref, 1, qb)

    c = lax.broadcasted_iota(jnp.int32, (2 * BLOCK, BLOCK), 0)
    r = lax.broadcasted_iota(jnp.int32, (2 * BLOCK, BLOCK), 1)
    diff = r - (c - BLOCK)
    band = (diff >= 0) & (diff < WINDOW)
    nt = (((1,), (1,)), ((), ()))

    def one_block(b, carry):
        r0 = pl.multiple_of(b * BLOCK, BLOCK)
        kpos = (i * qb + b) * BLOCK + c - BLOCK - PAD
        bias = jnp.where(band & (kpos >= 0), 0.0, NEG)
        for m in range(N_HEADS // 2):
            j = (2 * m) // GQA
            q2 = q_ref[0, pl.ds(r0, BLOCK), m * LANES:(m + 1) * LANES]
            halves = []
            for par, k_ref in ((0, ke_ref), (1, ko_ref)):
                st = lax.dot_general(k_ref[j, pl.ds(r0, 2 * BLOCK), :], q2, nt, preferred_element_type=F32)
                st = st + bias
                sink = sink_ref[l, 2 * m + par]
                mx = jnp.maximum(jnp.max(st, axis=0, keepdims=True), sink)
                p = jnp.exp(st - mx)
                den = jnp.sum(p, axis=0, keepdims=True) + jnp.exp(sink - mx)
                pb = p.astype(BF16)
                ot = _dot(vt_ref[b, j], pb[0:BLOCK]) + _dot(vt_ref[b + 1, j], pb[BLOCK:])
                halves.append(ot * (1.0 / den))
            o2 = jnp.transpose(jnp.concatenate(halves, axis=0))
            o_ref[0, pl.ds(r0, BLOCK), m * LANES:(m + 1) * LANES] = o2.astype(BF16)
        return carry

    lax.fori_loop(0, qb, one_block, 0, unroll=True)


def _prompt_attn(l, q, k, v, sinks, qb):
    b, lp, _ = q.shape
    nsteps = lp // (qb * BLOCK)
    cur = lambda w: pl.BlockSpec((1, qb * BLOCK, w), lambda bi, i: (bi, i, 0))
    prev = pl.BlockSpec((1, BLOCK, D_KV), lambda bi, i: (bi, jnp.maximum(i * qb - 1, 0), 0))
    ext = ((qb + 1) * BLOCK, LANES)
    return pl.pallas_call(
        functools.partial(_attn_body, qb=qb, l=l),
        grid=(b, nsteps),
        in_specs=[pl.BlockSpec(memory_space=pltpu.SMEM), cur(D_Q), prev, cur(D_KV), prev, cur(D_KV)],
        out_specs=cur(D_Q),
        out_shape=jax.ShapeDtypeStruct((b, lp, D_Q), BF16),
        scratch_shapes=[pltpu.VMEM((N_KV_HEADS,) + ext, BF16), pltpu.VMEM((N_KV_HEADS,) + ext, BF16),
                        pltpu.VMEM((qb + 1, N_KV_HEADS, HEAD_DIM, BLOCK), BF16)],
        compiler_params=_cp(("arbitrary", "arbitrary")),
        name="prompt_attn",
    )(sinks, q, k, k, v, v)


def _sattn_body(qx_ref, sink_ref, ck_ref, cv_ref, kn_ref, vn_ref, ox_ref, nk_ref, nv_ref, *, tb):
    def window(c_ref, n_ref, t):
        return jnp.concatenate([c_ref[t, 1:WINDOW, :], n_ref[t:t + 1, :]], axis=0)

    for t in range(tb):
        nk_ref[t] = window(ck_ref, kn_ref, t)
        nv_ref[t] = window(cv_ref, vn_ref, t)
    nt = (((1,), (1,)), ((), ()))
    s = jnp.concatenate([lax.dot_general(qx_ref[t], window(ck_ref, kn_ref, t).astype(BF16), nt,
                                         preferred_element_type=F32) for t in range(tb)], axis=0)
    sink = jnp.concatenate([sink_ref[...][:, 0:1]] * tb, axis=0)
    m = jnp.maximum(jnp.max(s, axis=-1, keepdims=True), sink)
    p = jnp.exp(s - m)
    rden = 1.0 / (jnp.sum(p, axis=-1, keepdims=True) + jnp.exp(sink - m))
    pb = p.astype(BF16)
    for t in range(tb):
        rows = slice(t * N_HEADS, (t + 1) * N_HEADS)
        ox_ref[t] = _dot(pb[rows], window(cv_ref, vn_ref, t).astype(BF16)) * rden[rows]


def _sample_attn(l, qx, sinkb, ck, cv, kn, vn, tb=16):
    n = qx.shape[0]
    blk3 = lambda a, c: pl.BlockSpec((tb, a, c), lambda i: (i, 0, 0))
    cache = pl.BlockSpec((None, tb, WINDOW, D_KV), lambda i: (l, i, 0, 0))
    row = pl.BlockSpec((tb, D_KV), lambda i: (i, 0))
    return pl.pallas_call(
        functools.partial(_sattn_body, tb=tb),
        grid=(n // tb,),
        in_specs=[blk3(N_HEADS, LANES), _layer_spec(l, N_HEADS, LANES), cache, cache, row, row],
        out_specs=[blk3(N_HEADS, LANES), blk3(WINDOW, D_KV), blk3(WINDOW, D_KV)],
        out_shape=[jax.ShapeDtypeStruct((n, N_HEADS, LANES), F32),
                   jax.ShapeDtypeStruct((n, WINDOW, D_KV), F32), jax.ShapeDtypeStruct((n, WINDOW, D_KV), F32)],
        compiler_params=_cp(("arbitrary",)),
        name="sample_attn",
    )(qx, sinkb, ck, cv, kn, vn)


ROUTE_ROWS = 24
RINV_LANE = LANES - 1


def _route(lt):
    top = lt[0:ROUTE_ROWS, :]
    rows = top.shape[1]
    rowf = lax.broadcasted_iota(jnp.int32, top.shape, 0).astype(F32)
    big = jnp.float32(3e38)
    far = jnp.float32(LANES)
    cmax = lambda a: jnp.max(a, axis=0, keepdims=True)
    cmin = lambda a: jnp.min(a, axis=0, keepdims=True)

    gmask = rowf < N_GROUPS
    gl = jnp.where(gmask, top, -big)
    gmax = cmax(gl)
    grp = cmin(jnp.where(gmask & (gl == gmax), rowf, far))
    p_grp = 1.0 / jnp.sum(jnp.where(gmask, jnp.exp(gl - gmax), 0.0), axis=0, keepdims=True)

    e_lo = N_GROUPS + EXPERTS_PER_GROUP * grp
    emask = (rowf >= e_lo) & (rowf < e_lo + EXPERTS_PER_GROUP)
    el = jnp.where(emask, top, -big)
    v1 = cmax(el)
    i1 = cmin(jnp.where(emask & (el == v1), rowf, far))
    rest = emask & (rowf != i1)
    el2 = jnp.where(rest, top, -big)
    v2 = cmax(el2)
    i2 = cmin(jnp.where(rest & (el2 == v2), rowf, far))
    e = jnp.exp(v2 - v1)
    w1 = (1.0 / (1.0 + e)) * p_grp
    w2 = (e / (1.0 + e)) * p_grp
    first_low = i1 < i2
    ea = jnp.where(first_low, i1, i2) - e_lo
    eb = jnp.where(first_low, i2, i1) - e_lo
    w_a = jnp.where(first_low, w1, w2)
    w_b = jnp.where(first_low, w2, w1)
    pair = jnp.where(ea == 0.0, 0.0, jnp.where(ea == 1.0, 3.0, 5.0)) + (eb - ea - 1.0)
    bucket = grp * N_PAIRS + pair

    r8 = lax.broadcasted_iota(jnp.int32, (SUBLANES, rows), 0)
    head = jnp.where(r8 == 0, w_a, jnp.where(r8 == 1, w_b, jnp.where(r8 == 2, lt[RINV_LANE:RINV_LANE + 1, :], 0.0)))
    return bucket, jnp.concatenate([head, jnp.zeros((LANES - SUBLANES, rows), F32)], axis=0)


def _out_body(x_ref, cy_ref, ao_ref, nm_ref, wg_ref, wco_ref, wao_ref, wo_ref, nf_ref, wr_ref, br_ref,
              tri_ref, cin_ref, xe_ref, rank_ref, bkt_ref, cnt_ref, run_ref, *, tm, parts):
    i = pl.program_id(0)
    th = tm // parts
    stage = []
    for h in range(parts):
        rows = slice(h * th, (h + 1) * th)
        hb = _rms_rows(x_ref[rows, :], nm_ref[...]).astype(BF16)
        stage.append((_dot(cy_ref[rows, :], wco_ref[...]), _dot(ao_ref[rows, :], wao_ref[...]),
                      _dot(hb, wg_ref[:, 0:D_MODEL]), _dot(hb, wg_ref[:, D_MODEL:])))
    for h, (ya, yb, gc, ga) in enumerate(stage):
        rows = slice(h * th, (h + 1) * th)
        mix = jax.nn.sigmoid(gc) * ya + jax.nn.sigmoid(ga) * yb
        xe_ref[rows, 0:D_MODEL] = x_ref[rows, :] + _dot(mix.astype(BF16), wo_ref[...])
    x1 = xe_ref[:, 0:D_MODEL]

    rinv = _rms_scale(x1)
    xnb = ((x1 * rinv) * nf_ref[...]).astype(BF16)
    logits = _dot(xnb, wr_ref[...]) + br_ref[...]
    lane = lax.broadcasted_iota(jnp.int32, (tm, LANES), 1)
    bucket, meta_t = _route(jnp.transpose(jnp.where(lane == RINV_LANE, rinv, logits)))
    xe_ref[:, D_MODEL:] = jnp.transpose(meta_t)

    @pl.when(i == 0)
    def _():
        run_ref[...] = cin_ref[...]

    sub = lax.broadcasted_iota(jnp.int32, (LANES, tm), 0).astype(F32)
    oht = (sub == bucket).astype(F32)
    before = _dot(oht.astype(BF16), tri_ref[...]) + run_ref[:, 0:1]
    rank_ref[0] = jnp.sum(oht * before, axis=0, keepdims=True).astype(jnp.int32)
    bkt_ref[0] = bucket.astype(jnp.int32)
    run_ref[...] = run_ref[...] + jnp.sum(oht, axis=-1, keepdims=True)
    cnt_ref[...] = run_ref[...]


def _mix_out(l, x, cy, ao, nm, w_gate, wco, wao, wo, nf, wr, br, tri, cnt_in, tm):
    t = x.shape[0]
    nt = t // tm
    tok = lambda w: pl.BlockSpec((tm, w), lambda i: (i, 0))
    rowi = pl.BlockSpec((1, 1, tm), lambda i: (i, 0, 0))
    sq = (D_MODEL, D_MODEL)
    return pl.pallas_call(
        functools.partial(_out_body, tm=tm, parts=2 if tm >= 4 * LANES else 1),
        grid=(nt,),
        in_specs=[tok(D_MODEL), tok(D_CONV), tok(D_Q), _layer_spec(l, 1, D_MODEL),
                  _layer_spec(l, D_MODEL, 2 * D_MODEL), _layer_spec(l, *sq), _layer_spec(l, *sq), _layer_spec(l, *sq),
                  _layer_spec(l, 1, D_MODEL), _layer_spec(l, D_MODEL, LANES), _layer_spec(l, 1, LANES),
                  _const_spec((tm, tm)), _const_spec((LANES, LANES))],
        out_specs=[tok(ROW_W), rowi, rowi, pl.BlockSpec((LANES, LANES), lambda i: (0, 0))],
        out_shape=[jax.ShapeDtypeStruct((t, ROW_W), F32),
                   jax.ShapeDtypeStruct((nt, 1, tm), jnp.int32), jax.ShapeDtypeStruct((nt, 1, tm), jnp.int32),
                   jax.ShapeDtypeStruct((LANES, LANES), F32)],
        scratch_shapes=[pltpu.VMEM((LANES, LANES), F32)],
        compiler_params=_cp(("arbitrary",)),
        name="mix_out",
    )(x, cy, ao, nm, w_gate, wco, wao, wo, nf, wr, br, tri, cnt_in)


SUB = SUBLANES


def _wait_rows(block_ref, sem):
    pltpu.make_async_copy(block_ref, block_ref, sem).wait()


def _scatter_tile_rows(pos_ref, src_ref, dst_ref, sem, tm):
    def group(g, c):
        for u in range(SUB):
            row = dst_ref.at[pl.ds(pos_ref[0, 0, g * SUB + u], 1), :]
            pltpu.make_async_copy(src_ref.at[g, pl.ds(u, 1), :], row, sem).start()
        return c

    lax.fori_loop(0, tm // SUB, group, 0)
    _wait_rows(src_ref, sem)


def _scatter_body(fill_ref, na_ref, pos_ref, src_ref, dst_ref, zero_ref, sem, *, tm, n_tiles):
    tile_rows = lambda t: dst_ref.at[pl.ds(pl.multiple_of(t * MOE_TM, MOE_TM), MOE_TM), :]

    @pl.when(pl.program_id(0) == 0)
    def _():
        zero_ref[...] = jnp.zeros(zero_ref.shape, F32)
        for b in range(N_BUCKETS):
            pltpu.make_async_copy(zero_ref, tile_rows(fill_ref[b]), sem).start()

        def tail_start(t, c):
            pltpu.make_async_copy(zero_ref, tile_rows(t), sem).start()
            return c

        lax.fori_loop(na_ref[0], n_tiles, tail_start, 0)
        for b in range(N_BUCKETS):
            pltpu.make_async_copy(zero_ref, tile_rows(fill_ref[b]), sem).wait()

        def tail_wait(t, c):
            pltpu.make_async_copy(zero_ref, tile_rows(t), sem).wait()
            return c

        lax.fori_loop(na_ref[0], n_tiles, tail_wait, 0)

    _scatter_tile_rows(pos_ref, src_ref, dst_ref, sem, tm)


def _scatter_rows(fill_tile, n_act, pos, src, n_tiles, tm):
    nt = src.shape[0] * SUB // tm
    grid_spec = pltpu.PrefetchScalarGridSpec(
        num_scalar_prefetch=2,
        grid=(nt,),
        in_specs=[pl.BlockSpec((1, 1, tm), lambda i, *_: (i, 0, 0), memory_space=pltpu.SMEM),
                  pl.BlockSpec((tm // SUB, SUB, ROW_W), lambda i, *_: (i, 0, 0))],
        out_specs=pl.BlockSpec(memory_space=pl.ANY),
        scratch_shapes=[pltpu.VMEM((MOE_TM, ROW_W), F32), pltpu.SemaphoreType.DMA(())],
    )
    return pl.pallas_call(
        functools.partial(_scatter_body, tm=tm, n_tiles=n_tiles),
        grid_spec=grid_spec,
        out_shape=jax.ShapeDtypeStruct((n_tiles * MOE_TM, ROW_W), F32),
        compiler_params=_cp(("arbitrary",)),
        name="dispatch_scatter",
    )(fill_tile, n_act, pos, src)


def _scatter_more_body(pos_ref, src_ref, dst_in_ref, dst_ref, sem, *, tm):
    del dst_in_ref
    _scatter_tile_rows(pos_ref, src_ref, dst_ref, sem, tm)


def _scatter_more_rows(pos, src, dst, tm):
    nt = src.shape[0] * SUB // tm
    return pl.pallas_call(
        functools.partial(_scatter_more_body, tm=tm),
        grid=(nt,),
        in_specs=[pl.BlockSpec((1, 1, tm), lambda i: (i, 0, 0), memory_space=pltpu.SMEM),
                  pl.BlockSpec((tm // SUB, SUB, ROW_W), lambda i: (i, 0, 0)),
                  pl.BlockSpec(memory_space=pl.ANY)],
        out_specs=pl.BlockSpec(memory_space=pl.ANY),
        out_shape=jax.ShapeDtypeStruct(dst.shape, dst.dtype),
        scratch_shapes=[pltpu.SemaphoreType.DMA(())],
        input_output_aliases={2: 0},
        compiler_params=_cp(("arbitrary",)),
        name="dispatch_scatter_more",
    )(pos, src, dst)


MOE_STEP_TILES = 2


def _moe_body(ta_ref, tb_ref, na_ref, xs_ref, nf_ref, *refs):
    del ta_ref, tb_ref
    w_refs, y_ref = refs[:-1], refs[-1]
    first_tile = pl.program_id(0) * MOE_STEP_TILES

    @pl.when(first_tile < na_ref[0])
    def _():
        staged = []
        for t in range(MOE_STEP_TILES):
            rows = slice(t * MOE_TM, (t + 1) * MOE_TM)
            x1 = xs_ref[rows, 0:D_MODEL]
            xb = ((x1 * xs_ref[rows, D_MODEL + 2:D_MODEL + 3]) * nf_ref[...]).astype(BF16)
            w = w_refs[6 * t:6 * t + 6]
            staged.append((x1, [(_dot(xb, w[3 * k][0]), _dot(xb, w[3 * k + 1][0])) for k in range(2)]))
        for t in range(MOE_STEP_TILES):
            rows = slice(t * MOE_TM, (t + 1) * MOE_TM)
            y, ups = staged[t]
            for k, (a, u) in enumerate(ups):
                hdn = (jax.nn.silu(a) * u) * xs_ref[rows, D_MODEL + k:D_MODEL + k + 1]
                y = y + _dot(hdn.astype(BF16), w_refs[6 * t + 3 * k + 2][0])
            y_ref[rows, :] = y

    @pl.when(first_tile >= na_ref[0])
    def _():
        y_ref[...] = jnp.zeros(y_ref.shape, F32)


def _moe_experts(l, tile_a, tile_b, n_act, xs, nf, wg, wu, wd):
    step_rows = MOE_STEP_TILES * MOE_TM
    assert xs.shape[0] % step_rows == 0
    last = lambda tile, na: jnp.minimum(tile, na[0] - 1)
    expert = lambda sel, t, i, ta, tb, na: (l * N_EXPERTS + sel(ta, tb)[last(MOE_STEP_TILES * i + t, na)], 0, 0)
    w_up = lambda sel, t: pl.BlockSpec((1, D_MODEL, D_EXPERT), functools.partial(expert, sel, t))
    w_dn = lambda sel, t: pl.BlockSpec((1, D_EXPERT, D_MODEL), functools.partial(expert, sel, t))
    sa = lambda ta, tb: ta
    sb = lambda ta, tb: tb
    w_specs = []
    for t in range(MOE_STEP_TILES):
        w_specs += [w_up(sa, t), w_up(sa, t), w_dn(sa, t), w_up(sb, t), w_up(sb, t), w_dn(sb, t)]
    grid_spec = pltpu.PrefetchScalarGridSpec(
        num_scalar_prefetch=3,
        grid=(xs.shape[0] // step_rows,),
        in_specs=[pl.BlockSpec((step_rows, ROW_W), lambda i, ta, tb, na: (jnp.minimum(i, (na[0] - 1) // MOE_STEP_TILES), 0)),
                  _layer_spec(l, 1, D_MODEL)] + w_specs,
        out_specs=pl.BlockSpec((step_rows, D_MODEL), lambda i, ta, tb, na: (i, 0)),
    )
    return pl.pallas_call(
        _moe_body,
        grid_spec=grid_spec,
        out_shape=jax.ShapeDtypeStruct((xs.shape[0], D_MODEL), F32),
        compiler_params=_cp(("arbitrary",)),
        name="moe_experts",
    )(tile_a, tile_b, n_act, xs, nf, *((wg, wu, wd) * (2 * MOE_STEP_TILES)))


def _unpermute_body(pos_ref, ys_ref, o_ref, sem, *, tm):
    def group(g, c):
        for u in range(SUB):
            row = ys_ref.at[pl.ds(pos_ref[0, 0, g * SUB + u], 1), :]
            pltpu.make_async_copy(row, o_ref.at[g, pl.ds(u, 1), :], sem).start()
        return c

    lax.fori_loop(0, tm // SUB, group, 0)
    _wait_rows(o_ref, sem)


def _unpermute(pos, ys, tm):
    t = pos.shape[0] * tm
    return pl.pallas_call(
        functools.partial(_unpermute_body, tm=tm),
        grid=(t // tm,),
        in_specs=[pl.BlockSpec((1, 1, tm), lambda i: (i, 0, 0), memory_space=pltpu.SMEM),
                  pl.BlockSpec(memory_space=pl.ANY)],
        out_specs=pl.BlockSpec((tm // SUB, SUB, D_MODEL), lambda i: (i, 0, 0)),
        out_shape=jax.ShapeDtypeStruct((t // SUB, SUB, D_MODEL), F32),
        scratch_shapes=[pltpu.SemaphoreType.DMA(())],
        compiler_params=_cp(("arbitrary",)),
        name="moe_unpermute",
    )(pos, ys)


def _rope_lane_freq():
    half = ROT_DIM // 2
    inv_freq = jnp.float32(ROPE_THETA) ** (-jnp.arange(half, dtype=jnp.float32) * (2.0 / ROT_DIM))
    dim = np.arange(LANES) % HEAD_DIM
    return inv_freq[dim % half][None, :], dim < half, (dim >= half) & (dim < ROT_DIM)


def _rope_patterns(cos, sin, first, second):
    return (jnp.where(first | second, cos, 1.0), jnp.where(first, -sin, 0.0), jnp.where(second, sin, 0.0))


def _rope_tables(pos):
    freq, first, second = _rope_lane_freq()
    ang = pos.astype(jnp.float32)[:, None] * freq
    return _rope_patterns(jnp.cos(ang), jnp.sin(ang), first, second)


def _rope_tables_padded(n_blocks):
    freq, first, second = _rope_lane_freq()
    ang_a = (jnp.arange(n_blocks, dtype=jnp.int32) * BLOCK).astype(jnp.float32)[:, None] * freq
    ang_b = (jnp.arange(BLOCK, dtype=jnp.int32) - PAD).astype(jnp.float32)[:, None] * freq
    ca, sa = jnp.cos(ang_a)[:, None, :], jnp.sin(ang_a)[:, None, :]
    cb, sb = jnp.cos(ang_b)[None], jnp.sin(ang_b)[None]
    flat = lambda t: t.reshape(n_blocks * BLOCK, LANES)
    return _rope_patterns(flat(ca * cb - sa * sb), flat(sa * cb + ca * sb), first, second)


def _seg_ones(n):
    idx = np.arange(n) // HEAD_DIM
    return jnp.asarray(idx[:, None] == idx[None, :], BF16)


def _bucket_experts():
    ea, eb = [], []
    for g in range(N_GROUPS):
        for a in range(EXPERTS_PER_GROUP):
            for b in range(a + 1, EXPERTS_PER_GROUP):
                ea.append(g * EXPERTS_PER_GROUP + a)
                eb.append(g * EXPERTS_PER_GROUP + b)
    return np.asarray(ea, np.int32), np.asarray(eb, np.int32)


def _dispatch_plan(counts, n_tiles):
    padded = ((counts + MOE_TM - 1) // MOE_TM) * MOE_TM
    ends = jnp.cumsum(padded)
    offs = ends - padded
    n_act = jnp.maximum(ends[-1] // MOE_TM, 1)
    starts = jnp.arange(n_tiles, dtype=jnp.int32) * MOE_TM
    tile_bucket = jnp.minimum(jnp.sum(starts[:, None] >= ends[None, :], axis=1), N_BUCKETS - 1)
    ea, eb = _bucket_experts()
    onehot = tile_bucket[:, None] == jnp.arange(N_BUCKETS)[None, :]
    tile_a = jnp.sum(jnp.where(onehot, ea[None, :], 0), axis=1).astype(jnp.int32)
    tile_b = jnp.sum(jnp.where(onehot, eb[None, :], 0), axis=1).astype(jnp.int32)
    fill_tile = jnp.maximum(ends // MOE_TM - 1, 0).astype(jnp.int32)
    return offs, tile_a, tile_b, n_act.astype(jnp.int32).reshape(1), fill_tile


def _positions(offs, bucket, rank):
    onehot = bucket[..., None] == jnp.arange(N_BUCKETS, dtype=jnp.int32)
    return (jnp.sum(jnp.where(onehot, offs.astype(jnp.int32), 0), axis=-1) + rank).astype(jnp.int32)


def kernel(x_prompt, x_sample, cache_k, cache_v, state_conv, meta_tokens, norm_mix, w_in, conv_w, q_norm, k_norm,
           attn_sinks, w_conv_out, w_attn_out, w_o, norm_ffn, w_router_group, b_router_group, w_router_expert,
           b_router_expert, w_exp_gate, w_exp_up, w_exp_down):
    batch, seq, _ = x_prompt.shape
    depth = w_in.shape[0]
    n_dec = x_sample.shape[0]
    past_len = PAST_LEN
    lp = PAD + N_META + seq
    tm_in, tm_out, qb = 640, 640, 5
    tm_move, tm_last = 3328, 4096
    assert PAD + N_META == BLOCK and seq % BLOCK == 0 and tm_in % BLOCK == 0
    assert lp % tm_in == 0 and (batch * lp) % tm_out == 0 and lp % (qb * BLOCK) == 0
    assert (batch * lp) % tm_move == 0 and (batch * seq) % tm_last == 0
    assert x_sample.shape[1] == 1 and cache_k.shape[2] == WINDOW and past_len >= WINDOW

    t_prompt = batch * lp
    t_all = t_prompt + n_dec
    n_tiles = -(-(t_all + N_BUCKETS * (MOE_TM - 1)) // MOE_TM)
    n_tiles = -(-n_tiles // MOE_STEP_TILES) * MOE_STEP_TILES

    meta = jnp.broadcast_to(meta_tokens[None].astype(F32), (batch, N_META, D_MODEL))
    head = jnp.concatenate([jnp.zeros((batch, PAD, D_MODEL), F32), meta], axis=1)
    xs = x_sample.reshape(n_dec, D_MODEL)

    rope_p = _rope_tables_padded(lp // BLOCK)
    rope_s = _rope_tables(jnp.full((1,), past_len, jnp.int32))
    s256, s128 = _seg_ones(256), _seg_ones(LANES)
    tri_p = jnp.asarray(np.triu(np.ones((tm_out, tm_out)), 1), BF16)
    tri_s = jnp.asarray(np.triu(np.ones((n_dec, n_dec)), 1), BF16)
    zero_cnt = jnp.zeros((LANES, LANES), F32)

    gap = jnp.zeros((depth, D_MODEL, D_MIX - 2 * D_MODEL), F32)
    w_mix = w_gate = jnp.concatenate([w_in[:, :, D_MIX:], gap, w_in[:, :, :D_MIX]], axis=-1).astype(BF16)
    wco, wao, wo = w_conv_out.astype(BF16), w_attn_out.astype(BF16), w_o.astype(BF16)
    wg = w_exp_gate.astype(BF16).reshape(depth * N_EXPERTS, D_MODEL, D_EXPERT)
    wu = w_exp_up.astype(BF16).reshape(depth * N_EXPERTS, D_MODEL, D_EXPERT)
    wd = w_exp_down.astype(BF16).reshape(depth * N_EXPERTS, D_EXPERT, D_MODEL)
    nm, nf = norm_mix.reshape(depth, 1, D_MODEL), norm_ffn.reshape(depth, 1, D_MODEL)
    qn = (jnp.tile(q_norm, (1, N_HEADS)) * Q_SCALE).reshape(depth, 1, D_Q)
    kn = jnp.tile(k_norm, (1, N_KV_HEADS)).reshape(depth, 1, D_KV)
    r_pad = LANES - N_GROUPS - N_EXPERTS
    wr = jnp.concatenate([w_router_group, w_router_expert, jnp.zeros((depth, D_MODEL, r_pad), F32)], axis=-1).astype(BF16)
    br = jnp.concatenate([b_router_group, b_router_expert, jnp.zeros((depth, r_pad), F32)], axis=-1).reshape(depth, 1, LANES)
    sinks = attn_sinks.astype(F32)
    sinkb = jnp.broadcast_to(sinks[:, :, None], (depth, N_HEADS, LANES))
    ck = cache_k.reshape(depth, n_dec, WINDOW, D_KV)
    cv = cache_v.reshape(depth, n_dec, WINDOW, D_KV)

    outs = {k: [] for k in ("kp", "vp", "cp", "ks", "vs", "cs")}
    for l in range(depth):
        if l == 0:
            cy, q, k, v, ulast, xp = _prompt_in(l, x_prompt, nm, w_mix, conv_w, qn, kn, rope_p, s256, s128, tm_in, head)
        else:
            cy, q, k, v, ulast = _prompt_in(l, xp, nm, w_mix, conv_w, qn, kn, rope_p, s256, s128, tm_in)
        ao = _prompt_attn(l, q, k, v, sinks, qb)
        xep, rankp, bktp, cnt = _mix_out(
            l, xp.reshape(t_prompt, D_MODEL), cy.reshape(t_prompt, D_CONV), ao.reshape(t_prompt, D_Q),
            nm, w_gate, wco, wao, wo, nf, wr, br, tri_p, zero_cnt, tm_out)
        outs["kp"].append(k[:, lp - WINDOW:].reshape(batch, WINDOW, N_KV_HEADS, HEAD_DIM))
        outs["vp"].append(v[:, lp - WINDOW:].reshape(batch, WINDOW, N_KV_HEADS, HEAD_DIM))
        outs["cp"].append(ulast[:, 8 - (CONV_W - 1):])

        c0, c1 = state_conv[l, :, 0, :], state_conv[l, :, 1, :]
        cys, qx, ksn, vsn, us = _sample_in(l, xs, c0, c1, nm, w_mix, conv_w, qn, kn, rope_s, s256, s128)
        ox, nk, nv = _sample_attn(l, jnp.transpose(qx, (1, 0, 2)), sinkb, ck, cv, ksn, vsn)
        ox = ox.reshape(n_dec, N_KV_HEADS, GQA, N_KV_HEADS, HEAD_DIM)
        aos = jnp.stack([ox[:, j, :, j, :] for j in range(N_KV_HEADS)], axis=1).reshape(n_dec, D_Q).astype(BF16)
        xes, ranks, bkts, cnt = _mix_out(l, xs, cys, aos, nm, w_gate, wco, wao, wo, nf, wr, br, tri_s, cnt, n_dec)
        outs["ks"].append(nk.reshape(n_dec, WINDOW, N_KV_HEADS, HEAD_DIM))
        outs["vs"].append(nv.reshape(n_dec, WINDOW, N_KV_HEADS, HEAD_DIM))
        outs["cs"].append(jnp.stack([c1, us], axis=1))

        counts = cnt[:N_BUCKETS, 0].astype(jnp.int32)
        offs, tile_a, tile_b, n_act, fill_tile = _dispatch_plan(counts, n_tiles)
        posp = _positions(offs, bktp, rankp).reshape(t_prompt // tm_move, 1, tm_move)
        poss = _positions(offs, bkts, ranks)
        by8 = lambda a: a.reshape(a.shape[0] // SUB, SUB, a.shape[1])
        sorted_rows = _scatter_rows(fill_tile, n_act, posp, by8(xep), n_tiles, tm_move)
        sorted_rows = _scatter_more_rows(poss, by8(xes), sorted_rows, n_dec)
        ys = _moe_experts(l, tile_a, tile_b, n_act, sorted_rows, nf, wg, wu, wd)
        xs = _unpermute(poss, ys, n_dec).reshape(n_dec, D_MODEL)
        if l + 1 < depth:
            xp = _unpermute(posp, ys, tm_move).reshape(batch, lp, D_MODEL)
        else:
            pos_tok = posp.reshape(batch, lp)[:, PAD + N_META:].reshape(batch * seq // tm_last, 1, tm_last)
            y_prompt = _unpermute(pos_tok, ys, tm_last).reshape(batch, seq, D_MODEL)

    y_sample = xs.reshape(n_dec, 1, D_MODEL)
    st = lambda k: jnp.stack(outs[k])
    return (y_prompt, y_sample, st("kp"), st("vp"), st("cp"), st("ks"), st("vs"), st("cs"))
```

```python
import functools
import math

import jax
import jax.numpy as jnp
import numpy as np
from jax import lax
from jax.experimental import pallas as pl
from jax.experimental.pallas import tpu as pltpu

D_MODEL = 1024
N_META = 16
D_CONV = D_MODEL
CONV_W = 3
N_HEADS = 16
N_KV_HEADS = 2
HEAD_DIM = 64
GQA = N_HEADS // N_KV_HEADS
ROT_DIM = HEAD_DIM // 4
ROPE_THETA = 500000.0
WINDOW = 128
PAST_LEN = 8192
BLOCK = 128
N_GROUPS = 4
EXPERTS_PER_GROUP = 4
N_EXPERTS = N_GROUPS * EXPERTS_PER_GROUP
D_EXPERT = 512
EPS = 1e-6
NEG = -1e30
D_Q = N_HEADS * HEAD_DIM
D_KV = N_KV_HEADS * HEAD_DIM
C_B, C_C, C_HC = 0, D_CONV, 2 * D_CONV
C_Q = 3 * D_CONV
C_K = C_Q + D_Q
C_V = C_K + D_KV
C_G = C_V + D_KV
D_MIX = C_G
D_IN = C_G + 2 * D_MODEL

LANES = 128
SUBLANES = 8
PAD = (-N_META) % BLOCK
N_PAIRS = 6
N_BUCKETS = N_GROUPS * N_PAIRS
MOE_TM = 256
ROW_W = D_MODEL + LANES
LOG2E = math.log2(math.e)
Q_SCALE = HEAD_DIM ** -0.5 * LOG2E

F32 = jnp.float32
BF16 = jnp.bfloat16
VMEM_LIMIT = 56 * 1024 * 1024


def _cp(sem, vmem=VMEM_LIMIT):
    return pltpu.CompilerParams(dimension_semantics=sem, vmem_limit_bytes=vmem)


def _const_spec(shape):
    nd = len(shape)
    return pl.BlockSpec(shape, lambda *_: (0,) * nd, pipeline_mode=pl.Buffered(1))


def _layer_spec(l, *shape):
    n = len(shape)
    return pl.BlockSpec((None,) + shape, lambda *_: (l,) + (0,) * n, pipeline_mode=pl.Buffered(1))


def _w_mix_spec(l):
    return pl.BlockSpec((None, D_MODEL, D_MIX), lambda *_: (l, 0, 0), pipeline_mode=pl.Buffered(1))


def _dot(a, b):
    return jnp.dot(a, b, preferred_element_type=F32)


def _seg_mean_sq(x, seg_ones):
    return _dot((x * x).astype(BF16), seg_ones) * (1.0 / HEAD_DIM)


def _rope128(t, cos, sin_pm, first):
    partner = jnp.where(first, pltpu.roll(t, LANES - ROT_DIM // 2, 1), pltpu.roll(t, ROT_DIM // 2, 1))
    return t * cos + partner * sin_pm


def _rms_scale(x):
    return lax.rsqrt(jnp.mean(x * x, axis=-1, keepdims=True) + EPS)


def _rms_rows(x, g):
    return (x * _rms_scale(x)) * g


def _qk_project(hb, w_ref, s256, s128):
    qs = [_dot(hb, w_ref[:, C_Q + c * 256:C_Q + (c + 1) * 256]) for c in range(D_Q // 256)]
    kc = _dot(hb, w_ref[:, C_K:C_K + D_KV])
    return qs, [_seg_mean_sq(qc, s256) for qc in qs], kc, _seg_mean_sq(kc, s128)


def _qk_finish(proj, qn, kn, cos, sneg, spos, store_q):
    qs, q_ms, kc, k_ms = proj
    sin_pm = sneg + spos
    lane = lax.broadcasted_iota(jnp.int32, cos.shape, 1)
    first = lax.bitwise_and(lane, HEAD_DIM - 1) < ROT_DIM // 2
    for c, (qc, ms) in enumerate(zip(qs, q_ms)):
        qc = (qc * lax.rsqrt(ms + EPS)) * qn[:, c * 256:(c + 1) * 256]
        for s in range(2):
            r = _rope128(qc[:, s * LANES:(s + 1) * LANES], cos, sin_pm, first)
            store_q(2 * c + s, r.astype(BF16))
    kc = (kc * lax.rsqrt(k_ms + EPS)) * kn
    return _rope128(kc, cos, sin_pm, first)


def _in_body(x_ref, nm_ref, w_ref, cw_ref, qn_ref, kn_ref, cos_ref, sneg_ref, spos_ref,
             s256_ref, s128_ref, cy_ref, q_ref, k_ref, v_ref, ul_ref, us_ref, *, tm, parts):
    i = pl.program_id(1)
    th = tm // parts

    @pl.when(i == 0)
    def _():
        us_ref[0:8, :] = jnp.zeros((8, D_CONV), F32)

    projs = []
    for h in range(parts):
        r0 = h * th
        hb = _rms_rows(x_ref[0, r0:r0 + th, :], nm_ref[...]).astype(BF16)
        projs.append(_qk_project(hb, w_ref, s256_ref[...], s128_ref[...]))
        u = _dot(hb, w_ref[:, C_C:C_C + D_CONV]) * _dot(hb, w_ref[:, C_HC:C_HC + D_CONV])
        row = lax.broadcasted_iota(jnp.int32, (th, 1), 0) + (i * tm + r0)
        u = jnp.where(row >= PAD, u, 0.0)
        us_ref[8 + r0:8 + r0 + th, :] = u
        conv = (us_ref[6 + r0:6 + r0 + th, :] * cw_ref[0:1, :] + us_ref[7 + r0:7 + r0 + th, :] * cw_ref[1:2, :]) \
            + u * cw_ref[2:3, :]
        cy_ref[0, r0:r0 + th, :] = (_dot(hb, w_ref[:, C_B:C_B + D_CONV]) * conv).astype(BF16)
        v_ref[0, r0:r0 + th, :] = _dot(hb, w_ref[:, C_V:C_V + D_KV])

    last = us_ref[tm:tm + 8, :]
    ul_ref[0] = last
    us_ref[0:8, :] = last

    for h in range(parts):
        rows = slice(h * th, (h + 1) * th)

        def store_q(slab, val, rows=rows):
            q_ref[0, rows, slab * LANES:(slab + 1) * LANES] = val

        k_ref[0, rows, :] = _qk_finish(projs[h], qn_ref[...], kn_ref[...], cos_ref[rows, :], sneg_ref[rows, :],
                                       spos_ref[rows, :], store_q)


def _in_first_body(head_ref, *refs, tm, parts):
    nb = tm // BLOCK
    blocks, rest, xpad_ref, us_ref = refs[:nb], refs[nb:-2], refs[-2], refs[-1]
    first = pl.program_id(1) == 0
    xpad_ref[0, 0:BLOCK, :] = jnp.where(first, head_ref[0], blocks[0][0])
    for j in range(1, nb):
        xpad_ref[0, j * BLOCK:(j + 1) * BLOCK, :] = blocks[j][0]
    _in_body(xpad_ref, *rest, us_ref, tm=tm, parts=parts)


def _prompt_in(l, x, nm, w_mix, cw, qn, kn, rope, s256, s128, tm, head=None):
    b = x.shape[0]
    lp = x.shape[1] if head is None else x.shape[1] + BLOCK
    nt = lp // tm
    nb = tm // BLOCK
    cos, sneg, spos = rope
    tok = lambda w: pl.BlockSpec((1, tm, w), lambda bi, i: (bi, i, 0))
    tab = pl.BlockSpec((tm, LANES), lambda bi, i: (i, 0))
    params = [_layer_spec(l, 1, D_MODEL), _w_mix_spec(l),
              _layer_spec(l, CONV_W, D_CONV), _layer_spec(l, 1, D_Q), _layer_spec(l, 1, D_KV),
              tab, tab, tab, _const_spec((256, 256)), _const_spec((LANES, LANES))]
    out_specs = [tok(D_CONV), tok(D_Q), tok(D_KV), tok(D_KV), pl.BlockSpec((1, 8, D_CONV), lambda bi, i: (bi, 0, 0))]
    out_shape = [jax.ShapeDtypeStruct((b, lp, D_CONV), BF16), jax.ShapeDtypeStruct((b, lp, D_Q), BF16),
                 jax.ShapeDtypeStruct((b, lp, D_KV), F32), jax.ShapeDtypeStruct((b, lp, D_KV), F32),
                 jax.ShapeDtypeStruct((b, 8, D_CONV), F32)]
    if head is None:
        body, x_specs, x_args = _in_body, [tok(D_MODEL)], (x,)
    else:
        blk = lambda j: pl.BlockSpec((1, BLOCK, D_MODEL), lambda bi, i: (bi, jnp.maximum(nb * i - 1 + j, 0), 0))
        body = _in_first_body
        x_specs = [pl.BlockSpec((1, BLOCK, D_MODEL), lambda bi, i: (bi, 0, 0))] + [blk(j) for j in range(nb)]
        x_args = (head,) + (x,) * nb
        out_specs.append(tok(D_MODEL))
        out_shape.append(jax.ShapeDtypeStruct((b, lp, D_MODEL), F32))
    return pl.pallas_call(
        functools.partial(body, tm=tm, parts=2),
        grid=(b, nt),
        in_specs=x_specs + params,
        out_specs=out_specs,
        out_shape=out_shape,
        scratch_shapes=[pltpu.VMEM((tm + 8, D_CONV), F32)],
        compiler_params=_cp(("arbitrary", "arbitrary")),
        name="prompt_in",
    )(*x_args, nm, w_mix, cw, qn, kn, cos, sneg, spos, s256, s128)


def _sin_body(x_ref, c0_ref, c1_ref, nm_ref, w_ref, cw_ref, qn_ref, kn_ref, cos_ref, sneg_ref,
              spos_ref, s256_ref, s128_ref, cy_ref, qx_ref, k_ref, v_ref, u_ref):
    hb = _rms_rows(x_ref[...], nm_ref[...]).astype(BF16)
    u = _dot(hb, w_ref[:, C_C:C_C + D_CONV]) * _dot(hb, w_ref[:, C_HC:C_HC + D_CONV])
    u_ref[...] = u
    conv = (c0_ref[...] * cw_ref[0:1, :] + c1_ref[...] * cw_ref[1:2, :]) + u * cw_ref[2:3, :]
    cy_ref[...] = (_dot(hb, w_ref[:, C_B:C_B + D_CONV]) * conv).astype(BF16)

    lane = lax.broadcasted_iota(jnp.int32, (x_ref.shape[0], LANES), 1)
    low = lane < HEAD_DIM

    def store_q(slab, val):
        valf = val.astype(F32)
        swapped = pltpu.roll(valf, HEAD_DIM, 1)
        zero = jnp.zeros_like(valf)
        for h in (2 * slab, 2 * slab + 1):
            src = valf if (h % 2) == (h // GQA) else swapped
            keep = low if (h // GQA) == 0 else jnp.logical_not(low)
            qx_ref[h] = jnp.where(keep, src, zero).astype(BF16)

    cos = jnp.broadcast_to(cos_ref[...], (x_ref.shape[0], LANES))
    sneg = jnp.broadcast_to(sneg_ref[...], (x_ref.shape[0], LANES))
    spos = jnp.broadcast_to(spos_ref[...], (x_ref.shape[0], LANES))
    proj = _qk_project(hb, w_ref, s256_ref[...], s128_ref[...])
    k_ref[...] = _qk_finish(proj, qn_ref[...], kn_ref[...], cos, sneg, spos, store_q)
    v_ref[...] = _dot(hb, w_ref[:, C_V:C_V + D_KV])


def _sample_in(l, x, c0, c1, nm, w_mix, cw, qn, kn, rope, s256, s128):
    n = x.shape[0]
    cos, sneg, spos = rope
    full = lambda *s: pl.BlockSpec(s, lambda i: (0,) * len(s))
    return pl.pallas_call(
        _sin_body,
        grid=(1,),
        in_specs=[full(n, D_MODEL), full(n, D_CONV), full(n, D_CONV), _layer_spec(l, 1, D_MODEL),
                  _w_mix_spec(l), _layer_spec(l, CONV_W, D_CONV), _layer_spec(l, 1, D_Q),
                  _layer_spec(l, 1, D_KV), full(1, LANES), full(1, LANES), full(1, LANES), full(256, 256),
                  full(LANES, LANES)],
        out_specs=[full(n, D_CONV), full(N_HEADS, n, LANES), full(n, D_KV), full(n, D_KV), full(n, D_CONV)],
        out_shape=[jax.ShapeDtypeStruct((n, D_CONV), BF16), jax.ShapeDtypeStruct((N_HEADS, n, LANES), BF16),
                   jax.ShapeDtypeStruct((n, D_KV), F32), jax.ShapeDtypeStruct((n, D_KV), F32),
                   jax.ShapeDtypeStruct((n, D_CONV), F32)],
        compiler_params=_cp(("arbitrary",)),
        name="sample_in",
    )(x, c0, c1, nm, w_mix, cw, qn, kn, cos, sneg, spos, s256, s128)


def _attn_body(sink_ref, q_ref, kp_ref, kc_ref, vp_ref, vc_ref, o_ref, ke_ref, ko_ref, vt_ref, *, qb, l):
    i = pl.program_id(1)
    lane = lax.broadcasted_iota(jnp.int32, (BLOCK, LANES), 1)
    low = lane < HEAD_DIM

    def prep_k(src, blk0, nblk):
        for t in range(nblk):
            blk = src[0, t * BLOCK:(t + 1) * BLOCK, :]
            swp = pltpu.roll(blk, HEAD_DIM, 1)
            zero = jnp.zeros_like(blk)
            rows = slice((blk0 + t) * BLOCK, (blk0 + t + 1) * BLOCK)
            ke_ref[0, rows, :] = jnp.where(low, blk, zero).astype(BF16)
            ko_ref[0, rows, :] = jnp.where(low, zero, swp).astype(BF16)
            ke_ref[1, rows, :] = jnp.where(low, swp, zero).astype(BF16)
            ko_ref[1, rows, :] = jnp.where(low, zero, blk).astype(BF16)

    def prep_v(src, blk0, nblk):
        for t in range(nblk):
            vt = jnp.transpose(src[0, t * BLOCK:(t + 1) * BLOCK, :]).astype(BF16)
            for j in range(N_KV_HEADS):
                vt_ref[blk0 + t, j] = vt[j * HEAD_DIM:(j + 1) * HEAD_DIM, :]

    prep_k(kp_ref, 0, 1)
    prep_k(kc_ref, 1, qb)
    prep_v(vp_ref, 0, 1)
    prep_v(vc_ref, 1, qb)

    c = lax.broadcasted_iota(jnp.int32, (2 * BLOCK, BLOCK), 0)
    r = lax.broadcasted_iota(jnp.int32, (2 * BLOCK, BLOCK), 1)
    diff = r - (c - BLOCK)
    band = (diff >= 0) & (diff < WINDOW)
    nt = (((1,), (1,)), ((), ()))

    def one_block(b, carry):
        r0 = pl.multiple_of(b * BLOCK, BLOCK)
        kpos = (i * qb + b) * BLOCK + c - BLOCK - PAD
        bias = jnp.where(band & (kpos >= 0), 0.0, NEG)
        for m in range(N_HEADS // 2):
            j = (2 * m) // GQA
            q2 = q_ref[0, pl.ds(r0, BLOCK), m * LANES:(m + 1) * LANES]
            halves = []
            for par, k_ref in ((0, ke_ref), (1, ko_ref)):
                st = lax.dot_general(k_ref[j, pl.ds(r0, 2 * BLOCK), :], q2, nt, preferred_element_type=F32)
                st = st + bias
                sink = sink_ref[l, 2 * m + par]
                mx = jnp.maximum(jnp.max(st, axis=0, keepdims=True), sink)
                p = jnp.exp2(st - mx)
                den = jnp.sum(p, axis=0, keepdims=True) + jnp.exp2(sink - mx)
                pb = p.astype(BF16)
                ot = _dot(vt_ref[b, j], pb[0:BLOCK]) + _dot(vt_ref[b + 1, j], pb[BLOCK:])
                halves.append(ot * (1.0 / den))
            o2 = jnp.transpose(jnp.concatenate(halves, axis=0))
            o_ref[0, pl.ds(r0, BLOCK), m * LANES:(m + 1) * LANES] = o2.astype(BF16)
        return carry

    lax.fori_loop(0, qb, one_block, 0, unroll=True)


def _prompt_attn(l, q, k, v, sinks, qb):
    b, lp, _ = q.shape
    nsteps = lp // (qb * BLOCK)
    cur = lambda w: pl.BlockSpec((1, qb * BLOCK, w), lambda bi, i: (bi, i, 0))
    prev = pl.BlockSpec((1, BLOCK, D_KV), lambda bi, i: (bi, jnp.maximum(i * qb - 1, 0), 0))
    ext = ((qb + 1) * BLOCK, LANES)
    return pl.pallas_call(
        functools.partial(_attn_body, qb=qb, l=l),
        grid=(b, nsteps),
        in_specs=[pl.BlockSpec(memory_space=pltpu.SMEM), cur(D_Q), prev, cur(D_KV), prev, cur(D_KV)],
        out_specs=cur(D_Q),
        out_shape=jax.ShapeDtypeStruct((b, lp, D_Q), BF16),
        scratch_shapes=[pltpu.VMEM((N_KV_HEADS,) + ext, BF16), pltpu.VMEM((N_KV_HEADS,) + ext, BF16),
                        pltpu.VMEM((qb + 1, N_KV_HEADS, HEAD_DIM, BLOCK), BF16)],
        compiler_params=_cp(("arbitrary", "arbitrary")),
        name="prompt_attn",
    )(sinks, q, k, k, v, v)


def _sattn_body(qx_ref, sink_ref, ck_ref, cv_ref, kn_ref, vn_ref, ox_ref, nk_ref, nv_ref, *, tb):
    def window(c_ref, n_ref, t):
        return jnp.concatenate([c_ref[t, 1:WINDOW, :], n_ref[t:t + 1, :]], axis=0)

    for t in range(tb):
        nk_ref[t] = window(ck_ref, kn_ref, t)
        nv_ref[t] = window(cv_ref, vn_ref, t)
    nt = (((1,), (1,)), ((), ()))
    s = jnp.concatenate([lax.dot_general(qx_ref[t], window(ck_ref, kn_ref, t).astype(BF16), nt,
                                         preferred_element_type=F32) for t in range(tb)], axis=0)
    sink = jnp.concatenate([sink_ref[...][:, 0:1]] * tb, axis=0)
    m = jnp.maximum(jnp.max(s, axis=-1, keepdims=True), sink)
    p = jnp.exp2(s - m)
    rden = 1.0 / (jnp.sum(p, axis=-1, keepdims=True) + jnp.exp2(sink - m))
    pb = p.astype(BF16)
    for t in range(tb):
        rows = slice(t * N_HEADS, (t + 1) * N_HEADS)
        ox_ref[t] = _dot(pb[rows], window(cv_ref, vn_ref, t).astype(BF16)) * rden[rows]


def _sample_attn(l, qx, sinkb, ck, cv, kn, vn, tb=16):
    n = qx.shape[0]
    blk3 = lambda a, c: pl.BlockSpec((tb, a, c), lambda i: (i, 0, 0))
    cache = pl.BlockSpec((None, tb, WINDOW, D_KV), lambda i: (l, i, 0, 0))
    row = pl.BlockSpec((tb, D_KV), lambda i: (i, 0))
    return pl.pallas_call(
        functools.partial(_sattn_body, tb=tb),
        grid=(n // tb,),
        in_specs=[blk3(N_HEADS, LANES), _layer_spec(l, N_HEADS, LANES), cache, cache, row, row],
        out_specs=[blk3(N_HEADS, LANES), blk3(WINDOW, D_KV), blk3(WINDOW, D_KV)],
        out_shape=[jax.ShapeDtypeStruct((n, N_HEADS, LANES), F32),
                   jax.ShapeDtypeStruct((n, WINDOW, D_KV), F32), jax.ShapeDtypeStruct((n, WINDOW, D_KV), F32)],
        compiler_params=_cp(("arbitrary",)),
        name="sample_attn",
    )(qx, sinkb, ck, cv, kn, vn)


ROUTE_ROWS = 24
RINV_LANE = LANES - 1


def _route(lt):
    top = lt[0:ROUTE_ROWS, :]
    rows = top.shape[1]
    rowf = lax.broadcasted_iota(jnp.int32, top.shape, 0).astype(F32)
    big = jnp.float32(3e38)
    far = jnp.float32(LANES)
    cmax = lambda a: jnp.max(a, axis=0, keepdims=True)
    cmin = lambda a: jnp.min(a, axis=0, keepdims=True)

    gmask = rowf < N_GROUPS
    gl = jnp.where(gmask, top, -big)
    gmax = cmax(gl)
    grp = cmin(jnp.where(gmask & (gl == gmax), rowf, far))
    p_grp = 1.0 / jnp.sum(jnp.where(gmask, jnp.exp(gl - gmax), 0.0), axis=0, keepdims=True)

    e_lo = N_GROUPS + EXPERTS_PER_GROUP * grp
    emask = (rowf >= e_lo) & (rowf < e_lo + EXPERTS_PER_GROUP)
    el = jnp.where(emask, top, -big)
    v1 = cmax(el)
    i1 = cmin(jnp.where(emask & (el == v1), rowf, far))
    rest = emask & (rowf != i1)
    el2 = jnp.where(rest, top, -big)
    v2 = cmax(el2)
    i2 = cmin(jnp.where(rest & (el2 == v2), rowf, far))
    e = jnp.exp(v2 - v1)
    w1 = (1.0 / (1.0 + e)) * p_grp
    w2 = (e / (1.0 + e)) * p_grp
    first_low = i1 < i2
    ea = jnp.where(first_low, i1, i2) - e_lo
    eb = jnp.where(first_low, i2, i1) - e_lo
    w_a = jnp.where(first_low, w1, w2)
    w_b = jnp.where(first_low, w2, w1)
    pair = jnp.where(ea == 0.0, 0.0, jnp.where(ea == 1.0, 3.0, 5.0)) + (eb - ea - 1.0)
    bucket = grp * N_PAIRS + pair

    r8 = lax.broadcasted_iota(jnp.int32, (SUBLANES, rows), 0)
    head = jnp.where(r8 == 0, w_a, jnp.where(r8 == 1, w_b, jnp.where(r8 == 2, lt[RINV_LANE:RINV_LANE + 1, :], 0.0)))
    return bucket, jnp.concatenate([head, jnp.zeros((LANES - SUBLANES, rows), F32)], axis=0)


def _out_body(x_ref, cy_ref, ao_ref, nm_ref, wg_ref, wco_ref, wao_ref, wo_ref, nf_ref, wr_ref, br_ref,
              tri_ref, cin_ref, xe_ref, rank_ref, bkt_ref, cnt_ref, run_ref, *, tm, parts):
    i = pl.program_id(0)
    th = tm // parts
    stage = []
    for h in range(parts):
        rows = slice(h * th, (h + 1) * th)
        hb = _rms_rows(x_ref[rows, :], nm_ref[...]).astype(BF16)
        stage.append((_dot(cy_ref[rows, :], wco_ref[...]), _dot(ao_ref[rows, :], wao_ref[...]),
                      _dot(hb, wg_ref[:, 0:D_MODEL]), _dot(hb, wg_ref[:, D_MODEL:])))
    for h, (ya, yb, gc, ga) in enumerate(stage):
        rows = slice(h * th, (h + 1) * th)
        mix = jax.nn.sigmoid(gc) * ya + jax.nn.sigmoid(ga) * yb
        xe_ref[rows, 0:D_MODEL] = x_ref[rows, :] + _dot(mix.astype(BF16), wo_ref[...])
    x1 = xe_ref[:, 0:D_MODEL]

    rinv = _rms_scale(x1)
    xnb = ((x1 * rinv) * nf_ref[...]).astype(BF16)
    logits = _dot(xnb, wr_ref[...]) + br_ref[...]
    lane = lax.broadcasted_iota(jnp.int32, (tm, LANES), 1)
    bucket, meta_t = _route(jnp.transpose(jnp.where(lane == RINV_LANE, rinv, logits)))
    xe_ref[:, D_MODEL:] = jnp.transpose(meta_t)

    @pl.when(i == 0)
    def _():
        run_ref[...] = cin_ref[...]

    sub = lax.broadcasted_iota(jnp.int32, (LANES, tm), 0).astype(F32)
    oht = (sub == bucket).astype(F32)
    before = _dot(oht.astype(BF16), tri_ref[...]) + run_ref[:, 0:1]
    rank_ref[0] = jnp.sum(oht * before, axis=0, keepdims=True).astype(jnp.int32)
    bkt_ref[0] = bucket.astype(jnp.int32)
    run_ref[...] = run_ref[...] + jnp.sum(oht, axis=-1, keepdims=True)
    cnt_ref[...] = run_ref[...]


def _mix_out(l, x, cy, ao, nm, w_gate, wco, wao, wo, nf, wr, br, tri, cnt_in, tm):
    t = x.shape[0]
    nt = t // tm
    tok = lambda w: pl.BlockSpec((tm, w), lambda i: (i, 0))
    rowi = pl.BlockSpec((1, 1, tm), lambda i: (i, 0, 0))
    sq = (D_MODEL, D_MODEL)
    return pl.pallas_call(
        functools.partial(_out_body, tm=tm, parts=2 if tm >= 4 * LANES else 1),
        grid=(nt,),
        in_specs=[tok(D_MODEL), tok(D_CONV), tok(D_Q), _layer_spec(l, 1, D_MODEL),
                  _layer_spec(l, D_MODEL, 2 * D_MODEL), _layer_spec(l, *sq), _layer_spec(l, *sq), _layer_spec(l, *sq),
                  _layer_spec(l, 1, D_MODEL), _layer_spec(l, D_MODEL, LANES), _layer_spec(l, 1, LANES),
                  _const_spec((tm, tm)), _const_spec((LANES, LANES))],
        out_specs=[tok(ROW_W), rowi, rowi, pl.BlockSpec((LANES, LANES), lambda i: (0, 0))],
        out_shape=[jax.ShapeDtypeStruct((t, ROW_W), F32),
                   jax.ShapeDtypeStruct((nt, 1, tm), jnp.int32), jax.ShapeDtypeStruct((nt, 1, tm), jnp.int32),
                   jax.ShapeDtypeStruct((LANES, LANES), F32)],
        scratch_shapes=[pltpu.VMEM((LANES, LANES), F32)],
        compiler_params=_cp(("arbitrary",)),
        name="mix_out",
    )(x, cy, ao, nm, w_gate, wco, wao, wo, nf, wr, br, tri, cnt_in)


SUB = SUBLANES


def _wait_rows(block_ref, sem):
    pltpu.make_async_copy(block_ref, block_ref, sem).wait()


def _scatter_tile_rows(pos_ref, src_ref, dst_ref, sem, tm):
    def group(g, c):
        for u in range(SUB):
            row = dst_ref.at[pl.ds(pos_ref[0, 0, g * SUB + u], 1), :]
            pltpu.make_async_copy(src_ref.at[g, pl.ds(u, 1), :], row, sem).start()
        return c

    lax.fori_loop(0, tm // SUB, group, 0)
    _wait_rows(src_ref, sem)


def _scatter_body(fill_ref, na_ref, pos_ref, src_ref, dst_ref, zero_ref, sem, *, tm, n_tiles):
    tile_rows = lambda t: dst_ref.at[pl.ds(pl.multiple_of(t * MOE_TM, MOE_TM), MOE_TM), :]

    @pl.when(pl.program_id(0) == 0)
    def _():
        zero_ref[...] = jnp.zeros(zero_ref.shape, F32)
        for b in range(N_BUCKETS):
            pltpu.make_async_copy(zero_ref, tile_rows(fill_ref[b]), sem).start()

        def tail_start(t, c):
            pltpu.make_async_copy(zero_ref, tile_rows(t), sem).start()
            return c

        lax.fori_loop(na_ref[0], n_tiles, tail_start, 0)
        for b in range(N_BUCKETS):
            pltpu.make_async_copy(zero_ref, tile_rows(fill_ref[b]), sem).wait()

        def tail_wait(t, c):
            pltpu.make_async_copy(zero_ref, tile_rows(t), sem).wait()
            return c

        lax.fori_loop(na_ref[0], n_tiles, tail_wait, 0)

    _scatter_tile_rows(pos_ref, src_ref, dst_ref, sem, tm)


def _scatter_rows(fill_tile, n_act, pos, src, n_tiles, tm):
    nt = src.shape[0] * SUB // tm
    grid_spec = pltpu.PrefetchScalarGridSpec(
        num_scalar_prefetch=2,
        grid=(nt,),
        in_specs=[pl.BlockSpec((1, 1, tm), lambda i, *_: (i, 0, 0), memory_space=pltpu.SMEM),
                  pl.BlockSpec((tm // SUB, SUB, ROW_W), lambda i, *_: (i, 0, 0))],
        out_specs=pl.BlockSpec(memory_space=pl.ANY),
        scratch_shapes=[pltpu.VMEM((MOE_TM, ROW_W), F32), pltpu.SemaphoreType.DMA(())],
    )
    return pl.pallas_call(
        functools.partial(_scatter_body, tm=tm, n_tiles=n_tiles),
        grid_spec=grid_spec,
        out_shape=jax.ShapeDtypeStruct((n_tiles * MOE_TM, ROW_W), F32),
        compiler_params=_cp(("arbitrary",)),
        name="dispatch_scatter",
    )(fill_tile, n_act, pos, src)


def _scatter_more_body(pos_ref, src_ref, dst_in_ref, dst_ref, sem, *, tm):
    del dst_in_ref
    _scatter_tile_rows(pos_ref, src_ref, dst_ref, sem, tm)


def _scatter_more_rows(pos, src, dst, tm):
    nt = src.shape[0] * SUB // tm
    return pl.pallas_call(
        functools.partial(_scatter_more_body, tm=tm),
        grid=(nt,),
        in_specs=[pl.BlockSpec((1, 1, tm), lambda i: (i, 0, 0), memory_space=pltpu.SMEM),
                  pl.BlockSpec((tm // SUB, SUB, ROW_W), lambda i: (i, 0, 0)),
                  pl.BlockSpec(memory_space=pl.ANY)],
        out_specs=pl.BlockSpec(memory_space=pl.ANY),
        out_shape=jax.ShapeDtypeStruct(dst.shape, dst.dtype),
        scratch_shapes=[pltpu.SemaphoreType.DMA(())],
        input_output_aliases={2: 0},
        compiler_params=_cp(("arbitrary",)),
        name="dispatch_scatter_more",
    )(pos, src, dst)


MOE_STEP_TILES = 2


def _moe_body(ta_ref, tb_ref, na_ref, xs_ref, nf_ref, *refs):
    del ta_ref, tb_ref
    w_refs, y_ref = refs[:-1], refs[-1]
    first_tile = pl.program_id(0) * MOE_STEP_TILES

    @pl.when(first_tile < na_ref[0])
    def _():
        staged = []
        for t in range(MOE_STEP_TILES):
            rows = slice(t * MOE_TM, (t + 1) * MOE_TM)
            x1 = xs_ref[rows, 0:D_MODEL]
            xb = ((x1 * xs_ref[rows, D_MODEL + 2:D_MODEL + 3]) * nf_ref[...]).astype(BF16)
            w = w_refs[6 * t:6 * t + 6]
            staged.append((x1, [(_dot(xb, w[3 * k][0]), _dot(xb, w[3 * k + 1][0])) for k in range(2)]))
        for t in range(MOE_STEP_TILES):
            rows = slice(t * MOE_TM, (t + 1) * MOE_TM)
            y, ups = staged[t]
            for k, (a, u) in enumerate(ups):
                hdn = (jax.nn.silu(a) * u) * xs_ref[rows, D_MODEL + k:D_MODEL + k + 1]
                y = y + _dot(hdn.astype(BF16), w_refs[6 * t + 3 * k + 2][0])
            y_ref[rows, :] = y

    @pl.when(first_tile >= na_ref[0])
    def _():
        y_ref[...] = jnp.zeros(y_ref.shape, F32)


def _moe_experts(l, tile_a, tile_b, n_act, xs, nf, wg, wu, wd):
    step_rows = MOE_STEP_TILES * MOE_TM
    assert xs.shape[0] % step_rows == 0
    last = lambda tile, na: jnp.minimum(tile, na[0] - 1)
    expert = lambda sel, t, i, ta, tb, na: (l * N_EXPERTS + sel(ta, tb)[last(MOE_STEP_TILES * i + t, na)], 0, 0)
    w_up = lambda sel, t: pl.BlockSpec((1, D_MODEL, D_EXPERT), functools.partial(expert, sel, t))
    w_dn = lambda sel, t: pl.BlockSpec((1, D_EXPERT, D_MODEL), functools.partial(expert, sel, t))
    sa = lambda ta, tb: ta
    sb = lambda ta, tb: tb
    w_specs = []
    for t in range(MOE_STEP_TILES):
        w_specs += [w_up(sa, t), w_up(sa, t), w_dn(sa, t), w_up(sb, t), w_up(sb, t), w_dn(sb, t)]
    grid_spec = pltpu.PrefetchScalarGridSpec(
        num_scalar_prefetch=3,
        grid=(xs.shape[0] // step_rows,),
        in_specs=[pl.BlockSpec((step_rows, ROW_W), lambda i, ta, tb, na: (jnp.minimum(i, (na[0] - 1) // MOE_STEP_TILES), 0)),
                  _layer_spec(l, 1, D_MODEL)] + w_specs,
        out_specs=pl.BlockSpec((step_rows, D_MODEL), lambda i, ta, tb, na: (i, 0)),
    )
    return pl.pallas_call(
        _moe_body,
        grid_spec=grid_spec,
        out_shape=jax.ShapeDtypeStruct((xs.shape[0], D_MODEL), F32),
        compiler_params=_cp(("arbitrary",)),
        name="moe_experts",
    )(tile_a, tile_b, n_act, xs, nf, *((wg, wu, wd) * (2 * MOE_STEP_TILES)))


def _unpermute_body(pos_ref, ys_ref, o_ref, sem, *, tm):
    def group(g, c):
        for u in range(SUB):
            row = ys_ref.at[pl.ds(pos_ref[0, 0, g * SUB + u], 1), :]
            pltpu.make_async_copy(row, o_ref.at[g, pl.ds(u, 1), :], sem).start()
        return c

    lax.fori_loop(0, tm // SUB, group, 0)
    _wait_rows(o_ref, sem)


def _unpermute(pos, ys, tm):
    t = pos.shape[0] * tm
    return pl.pallas_call(
        functools.partial(_unpermute_body, tm=tm),
        grid=(t // tm,),
        in_specs=[pl.BlockSpec((1, 1, tm), lambda i: (i, 0, 0), memory_space=pltpu.SMEM),
                  pl.BlockSpec(memory_space=pl.ANY)],
        out_specs=pl.BlockSpec((tm // SUB, SUB, D_MODEL), lambda i: (i, 0, 0)),
        out_shape=jax.ShapeDtypeStruct((t // SUB, SUB, D_MODEL), F32),
        scratch_shapes=[pltpu.SemaphoreType.DMA(())],
        compiler_params=_cp(("arbitrary",)),
        name="moe_unpermute",
    )(pos, ys)


def _rope_lane_freq():
    half = ROT_DIM // 2
    inv_freq = jnp.float32(ROPE_THETA) ** (-jnp.arange(half, dtype=jnp.float32) * (2.0 / ROT_DIM))
    dim = np.arange(LANES) % HEAD_DIM
    return inv_freq[dim % half][None, :], dim < half, (dim >= half) & (dim < ROT_DIM)


def _rope_patterns(cos, sin, first, second):
    return (jnp.where(first | second, cos, 1.0), jnp.where(first, -sin, 0.0), jnp.where(second, sin, 0.0))


def _rope_tables(pos):
    freq, first, second = _rope_lane_freq()
    ang = pos.astype(jnp.float32)[:, None] * freq
    return _rope_patterns(jnp.cos(ang), jnp.sin(ang), first, second)


def _rope_tables_padded(n_blocks):
    freq, first, second = _rope_lane_freq()
    ang_a = (jnp.arange(n_blocks, dtype=jnp.int32) * BLOCK).astype(jnp.float32)[:, None] * freq
    ang_b = (jnp.arange(BLOCK, dtype=jnp.int32) - PAD).astype(jnp.float32)[:, None] * freq
    ca, sa = jnp.cos(ang_a)[:, None, :], jnp.sin(ang_a)[:, None, :]
    cb, sb = jnp.cos(ang_b)[None], jnp.sin(ang_b)[None]
    flat = lambda t: t.reshape(n_blocks * BLOCK, LANES)
    return _rope_patterns(flat(ca * cb - sa * sb), flat(sa * cb + ca * sb), first, second)


def _seg_ones(n):
    idx = np.arange(n) // HEAD_DIM
    return jnp.asarray(idx[:, None] == idx[None, :], BF16)


def _bucket_experts():
    ea, eb = [], []
    for g in range(N_GROUPS):
        for a in range(EXPERTS_PER_GROUP):
            for b in range(a + 1, EXPERTS_PER_GROUP):
                ea.append(g * EXPERTS_PER_GROUP + a)
                eb.append(g * EXPERTS_PER_GROUP + b)
    return np.asarray(ea, np.int32), np.asarray(eb, np.int32)


def _dispatch_plan(counts, n_tiles):
    padded = ((counts + MOE_TM - 1) // MOE_TM) * MOE_TM
    ends = jnp.cumsum(padded)
    offs = ends - padded
    n_act = jnp.maximum(ends[-1] // MOE_TM, 1)
    starts = jnp.arange(n_tiles, dtype=jnp.int32) * MOE_TM
    tile_bucket = jnp.minimum(jnp.sum(starts[:, None] >= ends[None, :], axis=1), N_BUCKETS - 1)
    ea, eb = _bucket_experts()
    onehot = tile_bucket[:, None] == jnp.arange(N_BUCKETS)[None, :]
    tile_a = jnp.sum(jnp.where(onehot, ea[None, :], 0), axis=1).astype(jnp.int32)
    tile_b = jnp.sum(jnp.where(onehot, eb[None, :], 0), axis=1).astype(jnp.int32)
    fill_tile = jnp.maximum(ends // MOE_TM - 1, 0).astype(jnp.int32)
    return offs, tile_a, tile_b, n_act.astype(jnp.int32).reshape(1), fill_tile


def _positions(offs, bucket, rank):
    onehot = bucket[..., None] == jnp.arange(N_BUCKETS, dtype=jnp.int32)
    return (jnp.sum(jnp.where(onehot, offs.astype(jnp.int32), 0), axis=-1) + rank).astype(jnp.int32)


def kernel(x_prompt, x_sample, cache_k, cache_v, state_conv, meta_tokens, norm_mix, w_in, conv_w, q_norm, k_norm,
           attn_sinks, w_conv_out, w_attn_out, w_o, norm_ffn, w_router_group, b_router_group, w_router_expert,
           b_router_expert, w_exp_gate, w_exp_up, w_exp_down):
    batch, seq, _ = x_prompt.shape
    depth = w_in.shape[0]
    n_dec = x_sample.shape[0]
    past_len = PAST_LEN
    lp = PAD + N_META + seq
    tm_in, tm_out, qb = 640, 640, 5
    tm_move, tm_last = 3328, 4096
    assert PAD + N_META == BLOCK and seq % BLOCK == 0 and tm_in % BLOCK == 0
    assert lp % tm_in == 0 and (batch * lp) % tm_out == 0 and lp % (qb * BLOCK) == 0
    assert (batch * lp) % tm_move == 0 and (batch * seq) % tm_last == 0
    assert x_sample.shape[1] == 1 and cache_k.shape[2] == WINDOW and past_len >= WINDOW

    t_prompt = batch * lp
    t_all = t_prompt + n_dec
    n_tiles = -(-(t_all + N_BUCKETS * (MOE_TM - 1)) // MOE_TM)
    n_tiles = -(-n_tiles // MOE_STEP_TILES) * MOE_STEP_TILES

    meta = jnp.broadcast_to(meta_tokens[None].astype(F32), (batch, N_META, D_MODEL))
    head = jnp.concatenate([jnp.zeros((batch, PAD, D_MODEL), F32), meta], axis=1)
    xs = x_sample.reshape(n_dec, D_MODEL)

    rope_p = _rope_tables_padded(lp // BLOCK)
    rope_s = _rope_tables(jnp.full((1,), past_len, jnp.int32))
    s256, s128 = _seg_ones(256), _seg_ones(LANES)
    tri_p = jnp.asarray(np.triu(np.ones((tm_out, tm_out)), 1), BF16)
    tri_s = jnp.asarray(np.triu(np.ones((n_dec, n_dec)), 1), BF16)
    zero_cnt = jnp.zeros((LANES, LANES), F32)

    w_mix = w_in.astype(BF16)
    w_gate = w_in[:, :, D_MIX:].astype(BF16)
    wco, wao, wo = w_conv_out.astype(BF16), w_attn_out.astype(BF16), w_o.astype(BF16)
    wg = w_exp_gate.astype(BF16).reshape(depth * N_EXPERTS, D_MODEL, D_EXPERT)
    wu = w_exp_up.astype(BF16).reshape(depth * N_EXPERTS, D_MODEL, D_EXPERT)
    wd = w_exp_down.astype(BF16).reshape(depth * N_EXPERTS, D_EXPERT, D_MODEL)
    nm, nf = norm_mix.reshape(depth, 1, D_MODEL), norm_ffn.reshape(depth, 1, D_MODEL)
    qn = (jnp.tile(q_norm, (1, N_HEADS)) * Q_SCALE).reshape(depth, 1, D_Q)
    kn = jnp.tile(k_norm, (1, N_KV_HEADS)).reshape(depth, 1, D_KV)
    r_pad = LANES - N_GROUPS - N_EXPERTS
    wr = jnp.concatenate([w_router_group, w_router_expert, jnp.zeros((depth, D_MODEL, r_pad), F32)], axis=-1).astype(BF16)
    br = jnp.concatenate([b_router_group, b_router_expert, jnp.zeros((depth, r_pad), F32)], axis=-1).reshape(depth, 1, LANES)
    sinks = attn_sinks.astype(F32) * LOG2E
    sinkb = jnp.broadcast_to(sinks[:, :, None], (depth, N_HEADS, LANES))
    ck = cache_k.reshape(depth, n_dec, WINDOW, D_KV)
    cv = cache_v.reshape(depth, n_dec, WINDOW, D_KV)

    outs = {k: [] for k in ("kp", "vp", "cp", "ks", "vs", "cs")}
    for l in range(depth):
        if l == 0:
            cy, q, k, v, ulast, xp = _prompt_in(l, x_prompt, nm, w_mix, conv_w, qn, kn, rope_p, s256, s128, tm_in, head)
        else:
            cy, q, k, v, ulast = _prompt_in(l, xp, nm, w_mix, conv_w, qn, kn, rope_p, s256, s128, tm_in)
        ao = _prompt_attn(l, q, k, v, sinks, qb)
        xep, rankp, bktp, cnt = _mix_out(
            l, xp.reshape(t_prompt, D_MODEL), cy.reshape(t_prompt, D_CONV), ao.reshape(t_prompt, D_Q),
            nm, w_gate, wco, wao, wo, nf, wr, br, tri_p, zero_cnt, tm_out)
        outs["kp"].append(k[:, lp - WINDOW:].reshape(batch, WINDOW, N_KV_HEADS, HEAD_DIM))
        outs["vp"].append(v[:, lp - WINDOW:].reshape(batch, WINDOW, N_KV_HEADS, HEAD_DIM))
        outs["cp"].append(ulast[:, 8 - (CONV_W - 1):])

        c0, c1 = state_conv[l, :, 0, :], state_conv[l, :, 1, :]
        cys, qx, ksn, vsn, us = _sample_in(l, xs, c0, c1, nm, w_mix, conv_w, qn, kn, rope_s, s256, s128)
        ox, nk, nv = _sample_attn(l, jnp.transpose(qx, (1, 0, 2)), sinkb, ck, cv, ksn, vsn)
        ox = ox.reshape(n_dec, N_KV_HEADS, GQA, N_KV_HEADS, HEAD_DIM)
        aos = jnp.stack([ox[:, j, :, j, :] for j in range(N_KV_HEADS)], axis=1).reshape(n_dec, D_Q).astype(BF16)
        xes, ranks, bkts, cnt = _mix_out(l, xs, cys, aos, nm, w_gate, wco, wao, wo, nf, wr, br, tri_s, cnt, n_dec)
        outs["ks"].append(nk.reshape(n_dec, WINDOW, N_KV_HEADS, HEAD_DIM))
        outs["vs"].append(nv.reshape(n_dec, WINDOW, N_KV_HEADS, HEAD_DIM))
        outs["cs"].append(jnp.stack([c1, us], axis=1))

        counts = cnt[:N_BUCKETS, 0].astype(jnp.int32)
        offs, tile_a, tile_b, n_act, fill_tile = _dispatch_plan(counts, n_tiles)
        posp = _positions(offs, bktp, rankp).reshape(t_prompt // tm_move, 1, tm_move)
        poss = _positions(offs, bkts, ranks)
        by8 = lambda a: a.reshape(a.shape[0] // SUB, SUB, a.shape[1])
        sorted_rows = _scatter_rows(fill_tile, n_act, posp, by8(xep), n_tiles, tm_move)
        sorted_rows = _scatter_more_rows(poss, by8(xes), sorted_rows, n_dec)
        ys = _moe_experts(l, tile_a, tile_b, n_act, sorted_rows, nf, wg, wu, wd)
        xs = _unpermute(poss, ys, n_dec).reshape(n_dec, D_MODEL)
        if l + 1 < depth:
            xp = _unpermute(posp, ys, tm_move).reshape(batch, lp, D_MODEL)
        else:
            pos_tok = posp.reshape(batch, lp)[:, PAD + N_META:].reshape(batch * seq // tm_last, 1, tm_last)
            y_prompt = _unpermute(pos_tok, ys, tm_last).reshape(batch, seq, D_MODEL)

    y_sample = xs.reshape(n_dec, 1, D_MODEL)
    st = lambda k: jnp.stack(outs[k])
    return (y_prompt, y_sample, st("kp"), st("vp"), st("cp"), st("ks"), st("vs"), st("cs"))
```

```python
import functools
import math

import jax
import jax.numpy as jnp
import numpy as np
from jax import lax
from jax.experimental import pallas as pl
from jax.experimental.pallas import tpu as pltpu

D_MODEL = 1024
N_META = 16
D_CONV = D_MODEL
CONV_W = 3
N_HEADS = 16
N_KV_HEADS = 2
HEAD_DIM = 64
GQA = N_HEADS // N_KV_HEADS
ROT_DIM = HEAD_DIM // 4
ROPE_THETA = 500000.0
WINDOW = 128
PAST_LEN = 8192
BLOCK = 128
N_GROUPS = 4
EXPERTS_PER_GROUP = 4
N_EXPERTS = N_GROUPS * EXPERTS_PER_GROUP
D_EXPERT = 512
EPS = 1e-6
NEG = -1e30
D_Q = N_HEADS * HEAD_DIM
D_KV = N_KV_HEADS * HEAD_DIM
C_B, C_C, C_HC = 0, D_CONV, 2 * D_CONV
C_Q = 3 * D_CONV
C_K = C_Q + D_Q
C_V = C_K + D_KV
C_G = C_V + D_KV
D_MIX = C_G
D_IN = C_G + 2 * D_MODEL

LANES = 128
SUBLANES = 8
PAD = (-N_META) % BLOCK
N_PAIRS = 6
N_BUCKETS = N_GROUPS * N_PAIRS
MOE_TM = 256
ROW_W = D_MODEL + LANES
LOG2E = math.log2(math.e)
Q_SCALE = HEAD_DIM ** -0.5 * LOG2E

F32 = jnp.float32
BF16 = jnp.bfloat16
VMEM_LIMIT = 56 * 1024 * 1024


def _cp(sem, vmem=VMEM_LIMIT):
    return pltpu.CompilerParams(dimension_semantics=sem, vmem_limit_bytes=vmem)


def _const_spec(shape):
    nd = len(shape)
    return pl.BlockSpec(shape, lambda *_: (0,) * nd, pipeline_mode=pl.Buffered(1))


def _layer_spec(l, *shape):
    n = len(shape)
    return pl.BlockSpec((None,) + shape, lambda *_: (l,) + (0,) * n, pipeline_mode=pl.Buffered(1))


def _w_mix_spec(l):
    return pl.BlockSpec((None, D_MODEL, D_MIX), lambda *_: (l, 0, 0), pipeline_mode=pl.Buffered(1))


def _dot(a, b):
    return jnp.dot(a, b, preferred_element_type=F32)


def _seg_mean_sq(x, seg_ones):
    return _dot((x * x).astype(BF16), seg_ones) * (1.0 / HEAD_DIM)


def _rope128(t, cos, sin_pm, first):
    partner = jnp.where(first, pltpu.roll(t, LANES - ROT_DIM // 2, 1), pltpu.roll(t, ROT_DIM // 2, 1))
    return t * cos + partner * sin_pm


def _rms_scale(x):
    return lax.rsqrt(jnp.mean(x * x, axis=-1, keepdims=True) + EPS)


def _rms_rows(x, g):
    return (x * _rms_scale(x)) * g


def _qk_project(hb, w_ref, s256, s128):
    qs = [_dot(hb, w_ref[:, C_Q + c * 256:C_Q + (c + 1) * 256]) for c in range(D_Q // 256)]
    kv = _dot(hb, w_ref[:, C_K:C_K + 2 * D_KV])
    kc, v = kv[:, 0:D_KV], kv[:, D_KV:]
    return qs, [_seg_mean_sq(qc, s256) for qc in qs], kc, _seg_mean_sq(kc, s128), v


def _qk_finish(proj, qn, kn, cos, sneg, spos, store_q):
    qs, q_ms, kc, k_ms, _ = proj
    sin_pm = sneg + spos
    lane = lax.broadcasted_iota(jnp.int32, cos.shape, 1)
    first = lax.bitwise_and(lane, HEAD_DIM - 1) < ROT_DIM // 2
    for c, (qc, ms) in enumerate(zip(qs, q_ms)):
        qc = (qc * lax.rsqrt(ms + EPS)) * qn[:, c * 256:(c + 1) * 256]
        for s in range(2):
            r = _rope128(qc[:, s * LANES:(s + 1) * LANES], cos, sin_pm, first)
            store_q(2 * c + s, r.astype(BF16))
    kc = (kc * lax.rsqrt(k_ms + EPS)) * kn
    return _rope128(kc, cos, sin_pm, first)


def _in_body(x_ref, nm_ref, w_ref, cw_ref, qn_ref, kn_ref, cos_ref, sneg_ref, spos_ref,
             s256_ref, s128_ref, cy_ref, q_ref, k_ref, v_ref, ul_ref, us_ref, *, tm, parts):
    i = pl.program_id(1)
    th = tm // parts

    @pl.when(i == 0)
    def _():
        us_ref[0:8, :] = jnp.zeros((8, D_CONV), F32)

    projs = []
    for h in range(parts):
        r0 = h * th
        hb = _rms_rows(x_ref[0, r0:r0 + th, :], nm_ref[...]).astype(BF16)
        projs.append(_qk_project(hb, w_ref, s256_ref[...], s128_ref[...]))
        u = _dot(hb, w_ref[:, C_C:C_C + D_CONV]) * _dot(hb, w_ref[:, C_HC:C_HC + D_CONV])
        row = lax.broadcasted_iota(jnp.int32, (th, 1), 0) + (i * tm + r0)
        u = jnp.where(row >= PAD, u, 0.0)
        us_ref[8 + r0:8 + r0 + th, :] = u
        conv = (us_ref[6 + r0:6 + r0 + th, :] * cw_ref[0:1, :] + us_ref[7 + r0:7 + r0 + th, :] * cw_ref[1:2, :]) \
            + u * cw_ref[2:3, :]
        cy_ref[0, r0:r0 + th, :] = (_dot(hb, w_ref[:, C_B:C_B + D_CONV]) * conv).astype(BF16)
        v_ref[0, r0:r0 + th, :] = projs[h][4]

    last = us_ref[tm:tm + 8, :]
    ul_ref[0] = last
    us_ref[0:8, :] = last

    for h in range(parts):
        rows = slice(h * th, (h + 1) * th)

        def store_q(slab, val, rows=rows):
            q_ref[0, rows, slab * LANES:(slab + 1) * LANES] = val

        k_ref[0, rows, :] = _qk_finish(projs[h], qn_ref[...], kn_ref[...], cos_ref[rows, :], sneg_ref[rows, :],
                                       spos_ref[rows, :], store_q)


def _in_first_body(head_ref, *refs, tm, parts):
    nb = tm // BLOCK
    blocks, rest, xpad_ref, us_ref = refs[:nb], refs[nb:-2], refs[-2], refs[-1]
    first = pl.program_id(1) == 0
    xpad_ref[0, 0:BLOCK, :] = jnp.where(first, head_ref[0], blocks[0][0])
    for j in range(1, nb):
        xpad_ref[0, j * BLOCK:(j + 1) * BLOCK, :] = blocks[j][0]
    _in_body(xpad_ref, *rest, us_ref, tm=tm, parts=parts)


def _prompt_in(l, x, nm, w_mix, cw, qn, kn, rope, s256, s128, tm, head=None):
    b = x.shape[0]
    lp = x.shape[1] if head is None else x.shape[1] + BLOCK
    nt = lp // tm
    nb = tm // BLOCK
    cos, sneg, spos = rope
    tok = lambda w: pl.BlockSpec((1, tm, w), lambda bi, i: (bi, i, 0))
    tab = pl.BlockSpec((tm, LANES), lambda bi, i: (i, 0))
    params = [_layer_spec(l, 1, D_MODEL), _w_mix_spec(l),
              _layer_spec(l, CONV_W, D_CONV), _layer_spec(l, 1, D_Q), _layer_spec(l, 1, D_KV),
              tab, tab, tab, _const_spec((256, 256)), _const_spec((LANES, LANES))]
    out_specs = [tok(D_CONV), tok(D_Q), tok(D_KV), tok(D_KV), pl.BlockSpec((1, 8, D_CONV), lambda bi, i: (bi, 0, 0))]
    out_shape = [jax.ShapeDtypeStruct((b, lp, D_CONV), BF16), jax.ShapeDtypeStruct((b, lp, D_Q), BF16),
                 jax.ShapeDtypeStruct((b, lp, D_KV), F32), jax.ShapeDtypeStruct((b, lp, D_KV), F32),
                 jax.ShapeDtypeStruct((b, 8, D_CONV), F32)]
    if head is None:
        body, x_specs, x_args = _in_body, [tok(D_MODEL)], (x,)
    else:
        blk = lambda j: pl.BlockSpec((1, BLOCK, D_MODEL), lambda bi, i: (bi, jnp.maximum(nb * i - 1 + j, 0), 0))
        body = _in_first_body
        x_specs = [pl.BlockSpec((1, BLOCK, D_MODEL), lambda bi, i: (bi, 0, 0))] + [blk(j) for j in range(nb)]
        x_args = (head,) + (x,) * nb
        out_specs.append(tok(D_MODEL))
        out_shape.append(jax.ShapeDtypeStruct((b, lp, D_MODEL), F32))
    return pl.pallas_call(
        functools.partial(body, tm=tm, parts=2),
        grid=(b, nt),
        in_specs=x_specs + params,
        out_specs=out_specs,
        out_shape=out_shape,
        scratch_shapes=[pltpu.VMEM((tm + 8, D_CONV), F32)],
        compiler_params=_cp(("arbitrary", "arbitrary")),
        name="prompt_in",
    )(*x_args, nm, w_mix, cw, qn, kn, cos, sneg, spos, s256, s128)


def _sin_body(x_ref, c0_ref, c1_ref, nm_ref, w_ref, cw_ref, qn_ref, kn_ref, cos_ref, sneg_ref,
              spos_ref, s256_ref, s128_ref, cy_ref, qx_ref, k_ref, v_ref, u_ref):
    hb = _rms_rows(x_ref[...], nm_ref[...]).astype(BF16)
    u = _dot(hb, w_ref[:, C_C:C_C + D_CONV]) * _dot(hb, w_ref[:, C_HC:C_HC + D_CONV])
    u_ref[...] = u
    conv = (c0_ref[...] * cw_ref[0:1, :] + c1_ref[...] * cw_ref[1:2, :]) + u * cw_ref[2:3, :]
    cy_ref[...] = (_dot(hb, w_ref[:, C_B:C_B + D_CONV]) * conv).astype(BF16)

    lane = lax.broadcasted_iota(jnp.int32, (x_ref.shape[0], LANES), 1)
    low = lane < HEAD_DIM

    def store_q(slab, val):
        valf = val.astype(F32)
        swapped = pltpu.roll(valf, HEAD_DIM, 1)
        zero = jnp.zeros_like(valf)
        for h in (2 * slab, 2 * slab + 1):
            src = valf if (h % 2) == (h // GQA) else swapped
            keep = low if (h // GQA) == 0 else jnp.logical_not(low)
            qx_ref[h] = jnp.where(keep, src, zero).astype(BF16)

    cos = jnp.broadcast_to(cos_ref[...], (x_ref.shape[0], LANES))
    sneg = jnp.broadcast_to(sneg_ref[...], (x_ref.shape[0], LANES))
    spos = jnp.broadcast_to(spos_ref[...], (x_ref.shape[0], LANES))
    proj = _qk_project(hb, w_ref, s256_ref[...], s128_ref[...])
    k_ref[...] = _qk_finish(proj, qn_ref[...], kn_ref[...], cos, sneg, spos, store_q)
    v_ref[...] = proj[4]


def _sample_in(l, x, c0, c1, nm, w_mix, cw, qn, kn, rope, s256, s128):
    n = x.shape[0]
    cos, sneg, spos = rope
    full = lambda *s: pl.BlockSpec(s, lambda i: (0,) * len(s))
    return pl.pallas_call(
        _sin_body,
        grid=(1,),
        in_specs=[full(n, D_MODEL), full(n, D_CONV), full(n, D_CONV), _layer_spec(l, 1, D_MODEL),
                  _w_mix_spec(l), _layer_spec(l, CONV_W, D_CONV), _layer_spec(l, 1, D_Q),
                  _layer_spec(l, 1, D_KV), full(1, LANES), full(1, LANES), full(1, LANES), full(256, 256),
                  full(LANES, LANES)],
        out_specs=[full(n, D_CONV), full(N_HEADS, n, LANES), full(n, D_KV), full(n, D_KV), full(n, D_CONV)],
        out_shape=[jax.ShapeDtypeStruct((n, D_CONV), BF16), jax.ShapeDtypeStruct((N_HEADS, n, LANES), BF16),
                   jax.ShapeDtypeStruct((n, D_KV), F32), jax.ShapeDtypeStruct((n, D_KV), F32),
                   jax.ShapeDtypeStruct((n, D_CONV), F32)],
        compiler_params=_cp(("arbitrary",)),
        name="sample_in",
    )(x, c0, c1, nm, w_mix, cw, qn, kn, cos, sneg, spos, s256, s128)


def _attn_body(sink_ref, q_ref, kp_ref, kc_ref, vp_ref, vc_ref, o_ref, ke_ref, ko_ref, vt_ref, *, qb, l):
    i = pl.program_id(1)
    lane = lax.broadcasted_iota(jnp.int32, (BLOCK, LANES), 1)
    low = lane < HEAD_DIM

    def prep_k(src, blk0, nblk):
        for t in range(nblk):
            blk = src[0, t * BLOCK:(t + 1) * BLOCK, :]
            swp = pltpu.roll(blk, HEAD_DIM, 1)
            zero = jnp.zeros_like(blk)
            rows = slice((blk0 + t) * BLOCK, (blk0 + t + 1) * BLOCK)
            ke_ref[0, rows, :] = jnp.where(low, blk, zero).astype(BF16)
            ko_ref[0, rows, :] = jnp.where(low, zero, swp).astype(BF16)
            ke_ref[1, rows, :] = jnp.where(low, swp, zero).astype(BF16)
            ko_ref[1, rows, :] = jnp.where(low, zero, blk).astype(BF16)

    def prep_v(src, blk0, nblk):
        for t in range(nblk):
            vt = jnp.transpose(src[0, t * BLOCK:(t + 1) * BLOCK, :]).astype(BF16)
            for j in range(N_KV_HEADS):
                vt_ref[blk0 + t, j] = vt[j * HEAD_DIM:(j + 1) * HEAD_DIM, :]

    prep_k(kp_ref, 0, 1)
    prep_k(kc_ref, 1, qb)
    prep_v(vp_ref, 0, 1)
    prep_v(vc_ref, 1, qb)

    c = lax.broadcasted_iota(jnp.int32, (2 * BLOCK, BLOCK), 0)
    r = lax.broadcasted_iota(jnp.int32, (2 * BLOCK, BLOCK), 1)
    diff = r - (c - BLOCK)
    band = (diff >= 0) & (diff < WINDOW)
    nt = (((1,), (1,)), ((), ()))

    def one_block(b, carry):
        r0 = pl.multiple_of(b * BLOCK, BLOCK)
        kpos = (i * qb + b) * BLOCK + c - BLOCK - PAD
        bias = jnp.where(band & (kpos >= 0), 0.0, NEG)
        for m in range(N_HEADS // 2):
            j = (2 * m) // GQA
            q2 = q_ref[0, pl.ds(r0, BLOCK), m * LANES:(m + 1) * LANES]
            halves = []
            for par, k_ref in ((0, ke_ref), (1, ko_ref)):
                st = lax.dot_general(k_ref[j, pl.ds(r0, 2 * BLOCK), :], q2, nt, preferred_element_type=F32)
                st = st + bias
                sink = sink_ref[l, 2 * m + par]
                mx = jnp.maximum(jnp.max(st, axis=0, keepdims=True), sink)
                p = jnp.exp2(st - mx)
                den = jnp.sum(p, axis=0, keepdims=True) + jnp.exp2(sink - mx)
                pb = p.astype(BF16)
                ot = _dot(vt_ref[b, j], pb[0:BLOCK]) + _dot(vt_ref[b + 1, j], pb[BLOCK:])
                halves.append(ot * (1.0 / den))
            o2 = jnp.transpose(jnp.concatenate(halves, axis=0))
            o_ref[0, pl.ds(r0, BLOCK), m * LANES:(m + 1) * LANES] = o2.astype(BF16)
        return carry

    lax.fori_loop(0, qb, one_block, 0, unroll=True)


def _prompt_attn(l, q, k, v, sinks, qb):
    b, lp, _ = q.shape
    nsteps = lp // (qb * BLOCK)
    cur = lambda w: pl.BlockSpec((1, qb * BLOCK, w), lambda bi, i: (bi, i, 0))
    prev = pl.BlockSpec((1, BLOCK, D_KV), lambda bi, i: (bi, jnp.maximum(i * qb - 1, 0), 0))
    ext = ((qb + 1) * BLOCK, LANES)
    return pl.pallas_call(
        functools.partial(_attn_body, qb=qb, l=l),
        grid=(b, nsteps),
        in_specs=[pl.BlockSpec(memory_space=pltpu.SMEM), cur(D_Q), prev, cur(D_KV), prev, cur(D_KV)],
        out_specs=cur(D_Q),
        out_shape=jax.ShapeDtypeStruct((b, lp, D_Q), BF16),
        scratch_shapes=[pltpu.VMEM((N_KV_HEADS,) + ext, BF16), pltpu.VMEM((N_KV_HEADS,) + ext, BF16),
                        pltpu.VMEM((qb + 1, N_KV_HEADS, HEAD_DIM, BLOCK), BF16)],
        compiler_params=_cp(("arbitrary", "arbitrary")),
        name="prompt_attn",
    )(sinks, q, k, k, v, v)


def _sattn_body(qx_ref, sink_ref, ck_ref, cv_ref, kn_ref, vn_ref, ox_ref, nk_ref, nv_ref, *, tb):
    def window(c_ref, n_ref, t):
        return jnp.concatenate([c_ref[t, 1:WINDOW, :], n_ref[t:t + 1, :]], axis=0)

    for t in range(tb):
        nk_ref[t] = window(ck_ref, kn_ref, t)
        nv_ref[t] = window(cv_ref, vn_ref, t)
    nt = (((1,), (1,)), ((), ()))
    s = jnp.concatenate([lax.dot_general(qx_ref[t], window(ck_ref, kn_ref, t).astype(BF16), nt,
                                         preferred_element_type=F32) for t in range(tb)], axis=0)
    sink = jnp.concatenate([sink_ref[...][:, 0:1]] * tb, axis=0)
    m = jnp.maximum(jnp.max(s, axis=-1, keepdims=True), sink)
    p = jnp.exp2(s - m)
    rden = 1.0 / (jnp.sum(p, axis=-1, keepdims=True) + jnp.exp2(sink - m))
    pb = p.astype(BF16)
    for t in range(tb):
        rows = slice(t * N_HEADS, (t + 1) * N_HEADS)
        ox_ref[t] = _dot(pb[rows], window(cv_ref, vn_ref, t).astype(BF16)) * rden[rows]


def _sample_attn(l, qx, sinkb, ck, cv, kn, vn, tb=16):
    n = qx.shape[0]
    blk3 = lambda a, c: pl.BlockSpec((tb, a, c), lambda i: (i, 0, 0))
    cache = pl.BlockSpec((None, tb, WINDOW, D_KV), lambda i: (l, i, 0, 0))
    row = pl.BlockSpec((tb, D_KV), lambda i: (i, 0))
    return pl.pallas_call(
        functools.partial(_sattn_body, tb=tb),
        grid=(n // tb,),
        in_specs=[blk3(N_HEADS, LANES), _layer_spec(l, N_HEADS, LANES), cache, cache, row, row],
        out_specs=[blk3(N_HEADS, LANES), blk3(WINDOW, D_KV), blk3(WINDOW, D_KV)],
        out_shape=[jax.ShapeDtypeStruct((n, N_HEADS, LANES), F32),
                   jax.ShapeDtypeStruct((n, WINDOW, D_KV), F32), jax.ShapeDtypeStruct((n, WINDOW, D_KV), F32)],
        compiler_params=_cp(("arbitrary",)),
        name="sample_attn",
    )(qx, sinkb, ck, cv, kn, vn)


ROUTE_ROWS = 24
RINV_LANE = LANES - 1


def _route(lt):
    top = lt[0:ROUTE_ROWS, :]
    rows = top.shape[1]
    rowf = lax.broadcasted_iota(jnp.int32, top.shape, 0).astype(F32)
    big = jnp.float32(3e38)
    far = jnp.float32(LANES)
    cmax = lambda a: jnp.max(a, axis=0, keepdims=True)
    cmin = lambda a: jnp.min(a, axis=0, keepdims=True)

    gmask = rowf < N_GROUPS
    gl = jnp.where(gmask, top, -big)
    gmax = cmax(gl)
    grp = cmin(jnp.where(gmask & (gl == gmax), rowf, far))
    p_grp = 1.0 / jnp.sum(jnp.where(gmask, jnp.exp(gl - gmax), 0.0), axis=0, keepdims=True)

    e_lo = N_GROUPS + EXPERTS_PER_GROUP * grp
    emask = (rowf >= e_lo) & (rowf < e_lo + EXPERTS_PER_GROUP)
    el = jnp.where(emask, top, -big)
    v1 = cmax(el)
    i1 = cmin(jnp.where(emask & (el == v1), rowf, far))
    rest = emask & (rowf != i1)
    el2 = jnp.where(rest, top, -big)
    v2 = cmax(el2)
    i2 = cmin(jnp.where(rest & (el2 == v2), rowf, far))
    e = jnp.exp(v2 - v1)
    w1 = (1.0 / (1.0 + e)) * p_grp
    w2 = (e / (1.0 + e)) * p_grp
    first_low = i1 < i2
    ea = jnp.where(first_low, i1, i2) - e_lo
    eb = jnp.where(first_low, i2, i1) - e_lo
    w_a = jnp.where(first_low, w1, w2)
    w_b = jnp.where(first_low, w2, w1)
    pair = jnp.where(ea == 0.0, 0.0, jnp.where(ea == 1.0, 3.0, 5.0)) + (eb - ea - 1.0)
    bucket = grp * N_PAIRS + pair

    r8 = lax.broadcasted_iota(jnp.int32, (SUBLANES, rows), 0)
    head = jnp.where(r8 == 0, w_a, jnp.where(r8 == 1, w_b, jnp.where(r8 == 2, lt[RINV_LANE:RINV_LANE + 1, :], 0.0)))
    return bucket, jnp.concatenate([head, jnp.zeros((LANES - SUBLANES, rows), F32)], axis=0)


def _out_body(x_ref, cy_ref, ao_ref, nm_ref, wg_ref, wco_ref, wao_ref, wo_ref, nf_ref, wr_ref, br_ref,
              tri_ref, cin_ref, xe_ref, rank_ref, bkt_ref, cnt_ref, run_ref, *, tm, parts):
    i = pl.program_id(0)
    th = tm // parts
    stage = []
    for h in range(parts):
        rows = slice(h * th, (h + 1) * th)
        hb = _rms_rows(x_ref[rows, :], nm_ref[...]).astype(BF16)
        stage.append((_dot(cy_ref[rows, :], wco_ref[...]), _dot(ao_ref[rows, :], wao_ref[...]),
                      _dot(hb, wg_ref[:, 0:D_MODEL]), _dot(hb, wg_ref[:, D_MODEL:])))
    for h, (ya, yb, gc, ga) in enumerate(stage):
        rows = slice(h * th, (h + 1) * th)
        mix = jax.nn.sigmoid(gc) * ya + jax.nn.sigmoid(ga) * yb
        xe_ref[rows, 0:D_MODEL] = x_ref[rows, :] + _dot(mix.astype(BF16), wo_ref[...])
    x1 = xe_ref[:, 0:D_MODEL]

    rinv = _rms_scale(x1)
    xnb = ((x1 * rinv) * nf_ref[...]).astype(BF16)
    logits = _dot(xnb, wr_ref[...]) + br_ref[...]
    lane = lax.broadcasted_iota(jnp.int32, (tm, LANES), 1)
    bucket, meta_t = _route(jnp.transpose(jnp.where(lane == RINV_LANE, rinv, logits)))
    xe_ref[:, D_MODEL:] = jnp.transpose(meta_t)

    @pl.when(i == 0)
    def _():
        run_ref[...] = cin_ref[...]

    sub = lax.broadcasted_iota(jnp.int32, (LANES, tm), 0).astype(F32)
    oht = (sub == bucket).astype(F32)
    before = _dot(oht.astype(BF16), tri_ref[...]) + run_ref[:, 0:1]
    rank_ref[0] = jnp.sum(oht * before, axis=0, keepdims=True).astype(jnp.int32)
    bkt_ref[0] = bucket.astype(jnp.int32)
    run_ref[...] = run_ref[...] + jnp.sum(oht, axis=-1, keepdims=True)
    cnt_ref[...] = run_ref[...]


def _mix_out(l, x, cy, ao, nm, w_gate, wco, wao, wo, nf, wr, br, tri, cnt_in, tm):
    t = x.shape[0]
    nt = t // tm
    tok = lambda w: pl.BlockSpec((tm, w), lambda i: (i, 0))
    rowi = pl.BlockSpec((1, 1, tm), lambda i: (i, 0, 0))
    sq = (D_MODEL, D_MODEL)
    return pl.pallas_call(
        functools.partial(_out_body, tm=tm, parts=2 if tm >= 4 * LANES else 1),
        grid=(nt,),
        in_specs=[tok(D_MODEL), tok(D_CONV), tok(D_Q), _layer_spec(l, 1, D_MODEL),
                  _layer_spec(l, D_MODEL, 2 * D_MODEL), _layer_spec(l, *sq), _layer_spec(l, *sq), _layer_spec(l, *sq),
                  _layer_spec(l, 1, D_MODEL), _layer_spec(l, D_MODEL, LANES), _layer_spec(l, 1, LANES),
                  _const_spec((tm, tm)), _const_spec((LANES, LANES))],
        out_specs=[tok(ROW_W), rowi, rowi, pl.BlockSpec((LANES, LANES), lambda i: (0, 0))],
        out_shape=[jax.ShapeDtypeStruct((t, ROW_W), F32),
                   jax.ShapeDtypeStruct((nt, 1, tm), jnp.int32), jax.ShapeDtypeStruct((nt, 1, tm), jnp.int32),
                   jax.ShapeDtypeStruct((LANES, LANES), F32)],
        scratch_shapes=[pltpu.VMEM((LANES, LANES), F32)],
        compiler_params=_cp(("arbitrary",)),
        name="mix_out",
    )(x, cy, ao, nm, w_gate, wco, wao, wo, nf, wr, br, tri, cnt_in)


SUB = SUBLANES


def _wait_rows(block_ref, sem):
    pltpu.make_async_copy(block_ref, block_ref, sem).wait()


def _scatter_tile_rows(pos_ref, src_ref, dst_ref, sem, tm):
    def group(g, c):
        for u in range(SUB):
            row = dst_ref.at[pl.ds(pos_ref[0, 0, g * SUB + u], 1), :]
            pltpu.make_async_copy(src_ref.at[g, pl.ds(u, 1), :], row, sem).start()
        return c

    lax.fori_loop(0, tm // SUB, group, 0)
    _wait_rows(src_ref, sem)


def _scatter_body(fill_ref, na_ref, pos_ref, src_ref, dst_ref, zero_ref, sem, *, tm, n_tiles):
    tile_rows = lambda t: dst_ref.at[pl.ds(pl.multiple_of(t * MOE_TM, MOE_TM), MOE_TM), :]

    @pl.when(pl.program_id(0) == 0)
    def _():
        zero_ref[...] = jnp.zeros(zero_ref.shape, F32)
        for b in range(N_BUCKETS):
            pltpu.make_async_copy(zero_ref, tile_rows(fill_ref[b]), sem).start()

        def tail_start(t, c):
            pltpu.make_async_copy(zero_ref, tile_rows(t), sem).start()
            return c

        lax.fori_loop(na_ref[0], n_tiles, tail_start, 0)
        for b in range(N_BUCKETS):
            pltpu.make_async_copy(zero_ref, tile_rows(fill_ref[b]), sem).wait()

        def tail_wait(t, c):
            pltpu.make_async_copy(zero_ref, tile_rows(t), sem).wait()
            return c

        lax.fori_loop(na_ref[0], n_tiles, tail_wait, 0)

    _scatter_tile_rows(pos_ref, src_ref, dst_ref, sem, tm)


def _scatter_rows(fill_tile, n_act, pos, src, n_tiles, tm):
    nt = src.shape[0] * SUB // tm
    grid_spec = pltpu.PrefetchScalarGridSpec(
        num_scalar_prefetch=2,
        grid=(nt,),
        in_specs=[pl.BlockSpec((1, 1, tm), lambda i, *_: (i, 0, 0), memory_space=pltpu.SMEM),
                  pl.BlockSpec((tm // SUB, SUB, ROW_W), lambda i, *_: (i, 0, 0))],
        out_specs=pl.BlockSpec(memory_space=pl.ANY),
        scratch_shapes=[pltpu.VMEM((MOE_TM, ROW_W), F32), pltpu.SemaphoreType.DMA(())],
    )
    return pl.pallas_call(
        functools.partial(_scatter_body, tm=tm, n_tiles=n_tiles),
        grid_spec=grid_spec,
        out_shape=jax.ShapeDtypeStruct((n_tiles * MOE_TM, ROW_W), F32),
        compiler_params=_cp(("arbitrary",)),
        name="dispatch_scatter",
    )(fill_tile, n_act, pos, src)


def _scatter_more_body(pos_ref, src_ref, dst_in_ref, dst_ref, sem, *, tm):
    del dst_in_ref
    _scatter_tile_rows(pos_ref, src_ref, dst_ref, sem, tm)


def _scatter_more_rows(pos, src, dst, tm):
    nt = src.shape[0] * SUB // tm
    return pl.pallas_call(
        functools.partial(_scatter_more_body, tm=tm),
        grid=(nt,),
        in_specs=[pl.BlockSpec((1, 1, tm), lambda i: (i, 0, 0), memory_space=pltpu.SMEM),
                  pl.BlockSpec((tm // SUB, SUB, ROW_W), lambda i: (i, 0, 0)),
                  pl.BlockSpec(memory_space=pl.ANY)],
        out_specs=pl.BlockSpec(memory_space=pl.ANY),
        out_shape=jax.ShapeDtypeStruct(dst.shape, dst.dtype),
        scratch_shapes=[pltpu.SemaphoreType.DMA(())],
        input_output_aliases={2: 0},
        compiler_params=_cp(("arbitrary",)),
        name="dispatch_scatter_more",
    )(pos, src, dst)


MOE_STEP_TILES = 2


def _moe_body(ta_ref, tb_ref, na_ref, xs_ref, nf_ref, *refs):
    del ta_ref, tb_ref
    w_refs, y_ref = refs[:-1], refs[-1]
    first_tile = pl.program_id(0) * MOE_STEP_TILES

    @pl.when(first_tile < na_ref[0])
    def _():
        staged = []
        for t in range(MOE_STEP_TILES):
            rows = slice(t * MOE_TM, (t + 1) * MOE_TM)
            x1 = xs_ref[rows, 0:D_MODEL]
            xb = ((x1 * xs_ref[rows, D_MODEL + 2:D_MODEL + 3]) * nf_ref[...]).astype(BF16)
            w = w_refs[6 * t:6 * t + 6]
            staged.append((x1, [(_dot(xb, w[3 * k][0]), _dot(xb, w[3 * k + 1][0])) for k in range(2)]))
        for t in range(MOE_STEP_TILES):
            rows = slice(t * MOE_TM, (t + 1) * MOE_TM)
            y, ups = staged[t]
            for k, (a, u) in enumerate(ups):
                hdn = (jax.nn.silu(a) * u) * xs_ref[rows, D_MODEL + k:D_MODEL + k + 1]
                y = y + _dot(hdn.astype(BF16), w_refs[6 * t + 3 * k + 2][0])
            y_ref[rows, :] = y

    @pl.when(first_tile >= na_ref[0])
    def _():
        y_ref[...] = jnp.zeros(y_ref.shape, F32)


def _moe_experts(l, tile_a, tile_b, n_act, xs, nf, wg, wu, wd):
    step_rows = MOE_STEP_TILES * MOE_TM
    assert xs.shape[0] % step_rows == 0
    last = lambda tile, na: jnp.minimum(tile, na[0] - 1)
    expert = lambda sel, t, i, ta, tb, na: (l * N_EXPERTS + sel(ta, tb)[last(MOE_STEP_TILES * i + t, na)], 0, 0)
    w_up = lambda sel, t: pl.BlockSpec((1, D_MODEL, D_EXPERT), functools.partial(expert, sel, t))
    w_dn = lambda sel, t: pl.BlockSpec((1, D_EXPERT, D_MODEL), functools.partial(expert, sel, t))
    sa = lambda ta, tb: ta
    sb = lambda ta, tb: tb
    w_specs = []
    for t in range(MOE_STEP_TILES):
        w_specs += [w_up(sa, t), w_up(sa, t), w_dn(sa, t), w_up(sb, t), w_up(sb, t), w_dn(sb, t)]
    grid_spec = pltpu.PrefetchScalarGridSpec(
        num_scalar_prefetch=3,
        grid=(xs.shape[0] // step_rows,),
        in_specs=[pl.BlockSpec((step_rows, ROW_W), lambda i, ta, tb, na: (jnp.minimum(i, (na[0] - 1) // MOE_STEP_TILES), 0)),
                  _layer_spec(l, 1, D_MODEL)] + w_specs,
        out_specs=pl.BlockSpec((step_rows, D_MODEL), lambda i, ta, tb, na: (i, 0)),
    )
    return pl.pallas_call(
        _moe_body,
        grid_spec=grid_spec,
        out_shape=jax.ShapeDtypeStruct((xs.shape[0], D_MODEL), F32),
        compiler_params=_cp(("arbitrary",)),
        name="moe_experts",
    )(tile_a, tile_b, n_act, xs, nf, *((wg, wu, wd) * (2 * MOE_STEP_TILES)))


def _unpermute_body(pos_ref, ys_ref, o_ref, sem, *, tm):
    def group(g, c):
        for u in range(SUB):
            row = ys_ref.at[pl.ds(pos_ref[0, 0, g * SUB + u], 1), :]
            pltpu.make_async_copy(row, o_ref.at[g, pl.ds(u, 1), :], sem).start()
        return c

    lax.fori_loop(0, tm // SUB, group, 0)
    _wait_rows(o_ref, sem)


def _unpermute(pos, ys, tm):
    t = pos.shape[0] * tm
    return pl.pallas_call(
        functools.partial(_unpermute_body, tm=tm),
        grid=(t // tm,),
        in_specs=[pl.BlockSpec((1, 1, tm), lambda i: (i, 0, 0), memory_space=pltpu.SMEM),
                  pl.BlockSpec(memory_space=pl.ANY)],
        out_specs=pl.BlockSpec((tm // SUB, SUB, D_MODEL), lambda i: (i, 0, 0)),
        out_shape=jax.ShapeDtypeStruct((t // SUB, SUB, D_MODEL), F32),
        scratch_shapes=[pltpu.SemaphoreType.DMA(())],
        compiler_params=_cp(("arbitrary",)),
        name="moe_unpermute",
    )(pos, ys)


def _rope_lane_freq():
    half = ROT_DIM // 2
    inv_freq = jnp.float32(ROPE_THETA) ** (-jnp.arange(half, dtype=jnp.float32) * (2.0 / ROT_DIM))
    dim = np.arange(LANES) % HEAD_DIM
    return inv_freq[dim % half][None, :], dim < half, (dim >= half) & (dim < ROT_DIM)


def _rope_patterns(cos, sin, first, second):
    return (jnp.where(first | second, cos, 1.0), jnp.where(first, -sin, 0.0), jnp.where(second, sin, 0.0))


def _rope_tables(pos):
    freq, first, second = _rope_lane_freq()
    ang = pos.astype(jnp.float32)[:, None] * freq
    return _rope_patterns(jnp.cos(ang), jnp.sin(ang), first, second)


def _rope_tables_padded(n_blocks):
    freq, first, second = _rope_lane_freq()
    ang_a = (jnp.arange(n_blocks, dtype=jnp.int32) * BLOCK).astype(jnp.float32)[:, None] * freq
    ang_b = (jnp.arange(BLOCK, dtype=jnp.int32) - PAD).astype(jnp.float32)[:, None] * freq
    ca, sa = jnp.cos(ang_a)[:, None, :], jnp.sin(ang_a)[:, None, :]
    cb, sb = jnp.cos(ang_b)[None], jnp.sin(ang_b)[None]
    flat = lambda t: t.reshape(n_blocks * BLOCK, LANES)
    return _rope_patterns(flat(ca * cb - sa * sb), flat(sa * cb + ca * sb), first, second)


def _seg_ones(n):
    idx = np.arange(n) // HEAD_DIM
    return jnp.asarray(idx[:, None] == idx[None, :], BF16)


def _bucket_experts():
    ea, eb = [], []
    for g in range(N_GROUPS):
        for a in range(EXPERTS_PER_GROUP):
            for b in range(a + 1, EXPERTS_PER_GROUP):
                ea.append(g * EXPERTS_PER_GROUP + a)
                eb.append(g * EXPERTS_PER_GROUP + b)
    return np.asarray(ea, np.int32), np.asarray(eb, np.int32)


def _dispatch_plan(counts, n_tiles):
    padded = ((counts + MOE_TM - 1) // MOE_TM) * MOE_TM
    ends = jnp.cumsum(padded)
    offs = ends - padded
    n_act = jnp.maximum(ends[-1] // MOE_TM, 1)
    starts = jnp.arange(n_tiles, dtype=jnp.int32) * MOE_TM
    tile_bucket = jnp.minimum(jnp.sum(starts[:, None] >= ends[None, :], axis=1), N_BUCKETS - 1)
    ea, eb = _bucket_experts()
    onehot = tile_bucket[:, None] == jnp.arange(N_BUCKETS)[None, :]
    tile_a = jnp.sum(jnp.where(onehot, ea[None, :], 0), axis=1).astype(jnp.int32)
    tile_b = jnp.sum(jnp.where(onehot, eb[None, :], 0), axis=1).astype(jnp.int32)
    fill_tile = jnp.maximum(ends // MOE_TM - 1, 0).astype(jnp.int32)
    return offs, tile_a, tile_b, n_act.astype(jnp.int32).reshape(1), fill_tile


def _positions(offs, bucket, rank):
    onehot = bucket[..., None] == jnp.arange(N_BUCKETS, dtype=jnp.int32)
    return (jnp.sum(jnp.where(onehot, offs.astype(jnp.int32), 0), axis=-1) + rank).astype(jnp.int32)


def kernel(x_prompt, x_sample, cache_k, cache_v, state_conv, meta_tokens, norm_mix, w_in, conv_w, q_norm, k_norm,
           attn_sinks, w_conv_out, w_attn_out, w_o, norm_ffn, w_router_group, b_router_group, w_router_expert,
           b_router_expert, w_exp_gate, w_exp_up, w_exp_down):
    batch, seq, _ = x_prompt.shape
    depth = w_in.shape[0]
    n_dec = x_sample.shape[0]
    past_len = PAST_LEN
    lp = PAD + N_META + seq
    tm_in, tm_out, qb = 640, 640, 5
    tm_move, tm_last = 3328, 4096
    assert PAD + N_META == BLOCK and seq % BLOCK == 0 and tm_in % BLOCK == 0
    assert lp % tm_in == 0 and (batch * lp) % tm_out == 0 and lp % (qb * BLOCK) == 0
    assert (batch * lp) % tm_move == 0 and (batch * seq) % tm_last == 0
    assert x_sample.shape[1] == 1 and cache_k.shape[2] == WINDOW and past_len >= WINDOW

    t_prompt = batch * lp
    t_all = t_prompt + n_dec
    n_tiles = -(-(t_all + N_BUCKETS * (MOE_TM - 1)) // MOE_TM)
    n_tiles = -(-n_tiles // MOE_STEP_TILES) * MOE_STEP_TILES

    meta = jnp.broadcast_to(meta_tokens[None].astype(F32), (batch, N_META, D_MODEL))
    head = jnp.concatenate([jnp.zeros((batch, PAD, D_MODEL), F32), meta], axis=1)
    xs = x_sample.reshape(n_dec, D_MODEL)

    rope_p = _rope_tables_padded(lp // BLOCK)
    rope_s = _rope_tables(jnp.full((1,), past_len, jnp.int32))
    s256, s128 = _seg_ones(256), _seg_ones(LANES)
    tri_p = jnp.asarray(np.triu(np.ones((tm_out, tm_out)), 1), BF16)
    tri_s = jnp.asarray(np.triu(np.ones((n_dec, n_dec)), 1), BF16)
    zero_cnt = jnp.zeros((LANES, LANES), F32)

    w_mix = w_in.astype(BF16)
    w_gate = w_in[:, :, D_MIX:].astype(BF16)
    wco, wao, wo = w_conv_out.astype(BF16), w_attn_out.astype(BF16), w_o.astype(BF16)
    wg = w_exp_gate.astype(BF16).reshape(depth * N_EXPERTS, D_MODEL, D_EXPERT)
    wu = w_exp_up.astype(BF16).reshape(depth * N_EXPERTS, D_MODEL, D_EXPERT)
    wd = w_exp_down.astype(BF16).reshape(depth * N_EXPERTS, D_EXPERT, D_MODEL)
    nm, nf = norm_mix.reshape(depth, 1, D_MODEL), norm_ffn.reshape(depth, 1, D_MODEL)
    qn = (jnp.tile(q_norm, (1, N_HEADS)) * Q_SCALE).reshape(depth, 1, D_Q)
    kn = jnp.tile(k_norm, (1, N_KV_HEADS)).reshape(depth, 1, D_KV)
    r_pad = LANES - N_GROUPS - N_EXPERTS
    wr = jnp.concatenate([w_router_group, w_router_expert, jnp.zeros((depth, D_MODEL, r_pad), F32)], axis=-1).astype(BF16)
    br = jnp.concatenate([b_router_group, b_router_expert, jnp.zeros((depth, r_pad), F32)], axis=-1).reshape(depth, 1, LANES)
    sinks = attn_sinks.astype(F32) * LOG2E
    sinkb = jnp.broadcast_to(sinks[:, :, None], (depth, N_HEADS, LANES))
    ck = cache_k.reshape(depth, n_dec, WINDOW, D_KV)
    cv = cache_v.reshape(depth, n_dec, WINDOW, D_KV)

    outs = {k: [] for k in ("kp", "vp", "cp", "ks", "vs", "cs")}
    for l in range(depth):
        if l == 0:
            cy, q, k, v, ulast, xp = _prompt_in(l, x_prompt, nm, w_mix, conv_w, qn, kn, rope_p, s256, s128, tm_in, head)
        else:
            cy, q, k, v, ulast = _prompt_in(l, xp, nm, w_mix, conv_w, qn, kn, rope_p, s256, s128, tm_in)
        ao = _prompt_attn(l, q, k, v, sinks, qb)
        xep, rankp, bktp, cnt = _mix_out(
            l, xp.reshape(t_prompt, D_MODEL), cy.reshape(t_prompt, D_CONV), ao.reshape(t_prompt, D_Q),
            nm, w_gate, wco, wao, wo, nf, wr, br, tri_p, zero_cnt, tm_out)
        outs["kp"].append(k[:, lp - WINDOW:].reshape(batch, WINDOW, N_KV_HEADS, HEAD_DIM))
        outs["vp"].append(v[:, lp - WINDOW:].reshape(batch, WINDOW, N_KV_HEADS, HEAD_DIM))
        outs["cp"].append(ulast[:, 8 - (CONV_W - 1):])

        c0, c1 = state_conv[l, :, 0, :], state_conv[l, :, 1, :]
        cys, qx, ksn, vsn, us = _sample_in(l, xs, c0, c1, nm, w_mix, conv_w, qn, kn, rope_s, s256, s128)
        ox, nk, nv = _sample_attn(l, jnp.transpose(qx, (1, 0, 2)), sinkb, ck, cv, ksn, vsn)
        ox = ox.reshape(n_dec, N_KV_HEADS, GQA, N_KV_HEADS, HEAD_DIM)
        aos = jnp.stack([ox[:, j, :, j, :] for j in range(N_KV_HEADS)], axis=1).reshape(n_dec, D_Q).astype(BF16)
        xes, ranks, bkts, cnt = _mix_out(l, xs, cys, aos, nm, w_gate, wco, wao, wo, nf, wr, br, tri_s, cnt, n_dec)
        outs["ks"].append(nk.reshape(n_dec, WINDOW, N_KV_HEADS, HEAD_DIM))
        outs["vs"].append(nv.reshape(n_dec, WINDOW, N_KV_HEADS, HEAD_DIM))
        outs["cs"].append(jnp.stack([c1, us], axis=1))

        counts = cnt[:N_BUCKETS, 0].astype(jnp.int32)
        offs, tile_a, tile_b, n_act, fill_tile = _dispatch_plan(counts, n_tiles)
        posp = _positions(offs, bktp, rankp).reshape(t_prompt // tm_move, 1, tm_move)
        poss = _positions(offs, bkts, ranks)
        by8 = lambda a: a.reshape(a.shape[0] // SUB, SUB, a.shape[1])
        sorted_rows = _scatter_rows(fill_tile, n_act, posp, by8(xep), n_tiles, tm_move)
        sorted_rows = _scatter_more_rows(poss, by8(xes), sorted_rows, n_dec)
        ys = _moe_experts(l, tile_a, tile_b, n_act, sorted_rows, nf, wg, wu, wd)
        xs = _unpermute(poss, ys, n_dec).reshape(n_dec, D_MODEL)
        if l + 1 < depth:
            xp = _unpermute(posp, ys, tm_move).reshape(batch, lp, D_MODEL)
        else:
            pos_tok = posp.reshape(batch, lp)[:, PAD + N_META:].reshape(batch * seq // tm_last, 1, tm_last)
            y_prompt = _unpermute(pos_tok, ys, tm_last).reshape(batch, seq, D_MODEL)

    y_sample = xs.reshape(n_dec, 1, D_MODEL)
    st = lambda k: jnp.stack(outs[k])
    return (y_prompt, y_sample, st("kp"), st("vp"), st("cp"), st("ks"), st("vs"), st("cs"))
```

```python
import functools
import math

import jax
import jax.numpy as jnp
import numpy as np
from jax import lax
from jax.experimental import pallas as pl
from jax.experimental.pallas import tpu as pltpu

D_MODEL = 1024
N_META = 16
D_CONV = D_MODEL
CONV_W = 3
N_HEADS = 16
N_KV_HEADS = 2
HEAD_DIM = 64
GQA = N_HEADS // N_KV_HEADS
ROT_DIM = HEAD_DIM // 4
ROPE_THETA = 500000.0
WINDOW = 128
PAST_LEN = 8192
BLOCK = 128
N_GROUPS = 4
EXPERTS_PER_GROUP = 4
N_EXPERTS = N_GROUPS * EXPERTS_PER_GROUP
D_EXPERT = 512
EPS = 1e-6
NEG = -1e30
D_Q = N_HEADS * HEAD_DIM
D_KV = N_KV_HEADS * HEAD_DIM
C_B, C_C, C_HC = 0, D_CONV, 2 * D_CONV
C_Q = 3 * D_CONV
C_K = C_Q + D_Q
C_V = C_K + D_KV
C_G = C_V + D_KV
D_MIX = C_G
D_IN = C_G + 2 * D_MODEL

LANES = 128
SUBLANES = 8
PAD = (-N_META) % BLOCK
N_PAIRS = 6
N_BUCKETS = N_GROUPS * N_PAIRS
MOE_TM = 256
ROW_W = D_MODEL + LANES
LOG2E = math.log2(math.e)
Q_SCALE = HEAD_DIM ** -0.5 * LOG2E

F32 = jnp.float32
BF16 = jnp.bfloat16
VMEM_LIMIT = 56 * 1024 * 1024


def _cp(sem, vmem=VMEM_LIMIT):
    return pltpu.CompilerParams(dimension_semantics=sem, vmem_limit_bytes=vmem)


def _const_spec(shape):
    nd = len(shape)
    return pl.BlockSpec(shape, lambda *_: (0,) * nd, pipeline_mode=pl.Buffered(1))


def _layer_spec(l, *shape):
    n = len(shape)
    return pl.BlockSpec((None,) + shape, lambda *_: (l,) + (0,) * n, pipeline_mode=pl.Buffered(1))


def _w_mix_spec(l):
    return pl.BlockSpec((None, D_MODEL, D_MIX), lambda *_: (l, 0, 0), pipeline_mode=pl.Buffered(1))


def _dot(a, b):
    return jnp.dot(a, b, preferred_element_type=F32)


def _seg_mean_sq(x, seg_ones):
    return _dot((x * x).astype(BF16), seg_ones) * (1.0 / HEAD_DIM)


def _rope128(t, cos, sin_pm, first):
    partner = jnp.where(first, pltpu.roll(t, LANES - ROT_DIM // 2, 1), pltpu.roll(t, ROT_DIM // 2, 1))
    return t * cos + partner * sin_pm


def _rms_scale(x):
    return lax.rsqrt(jnp.mean(x * x, axis=-1, keepdims=True) + EPS)


def _rms_rows(x, g):
    return (x * _rms_scale(x)) * g


def _qk_project(hb, w_ref, s256, s128):
    qs = [_dot(hb, w_ref[:, C_Q + c * 256:C_Q + (c + 1) * 256]) for c in range(D_Q // 256)]
    kv = _dot(hb, w_ref[:, C_K:C_K + 2 * D_KV])
    kc, v = kv[:, 0:D_KV], kv[:, D_KV:]
    return qs, [_seg_mean_sq(qc, s256) for qc in qs], kc, _seg_mean_sq(kc, s128), v


def _qk_finish(proj, qn, kn, cos, sneg, spos, store_q):
    qs, q_ms, kc, k_ms, _ = proj
    sin_pm = sneg + spos
    lane = lax.broadcasted_iota(jnp.int32, cos.shape, 1)
    first = lax.bitwise_and(lane, HEAD_DIM - 1) < ROT_DIM // 2
    for c, (qc, ms) in enumerate(zip(qs, q_ms)):
        qc = (qc * lax.rsqrt(ms + EPS)) * qn[:, c * 256:(c + 1) * 256]
        for s in range(2):
            r = _rope128(qc[:, s * LANES:(s + 1) * LANES], cos, sin_pm, first)
            store_q(2 * c + s, r.astype(BF16))
    kc = (kc * lax.rsqrt(k_ms + EPS)) * kn
    return _rope128(kc, cos, sin_pm, first)


def _in_body(x_ref, nm_ref, w_ref, cw_ref, qn_ref, kn_ref, cos_ref, sneg_ref, spos_ref,
             s256_ref, s128_ref, cy_ref, q_ref, k_ref, v_ref, ul_ref, us_ref, *, tm, parts):
    i = pl.program_id(1)
    th = tm // parts

    @pl.when(i == 0)
    def _():
        us_ref[0:8, :] = jnp.zeros((8, D_CONV), F32)

    projs = []
    for h in range(parts):
        r0 = h * th
        hb = _rms_rows(x_ref[0, r0:r0 + th, :], nm_ref[...]).astype(BF16)
        projs.append(_qk_project(hb, w_ref, s256_ref[...], s128_ref[...]))
        u = _dot(hb, w_ref[:, C_C:C_C + D_CONV]) * _dot(hb, w_ref[:, C_HC:C_HC + D_CONV])
        row = lax.broadcasted_iota(jnp.int32, (th, 1), 0) + (i * tm + r0)
        u = jnp.where(row >= PAD, u, 0.0)
        us_ref[8 + r0:8 + r0 + th, :] = u
        conv = (us_ref[6 + r0:6 + r0 + th, :] * cw_ref[0:1, :] + us_ref[7 + r0:7 + r0 + th, :] * cw_ref[1:2, :]) \
            + u * cw_ref[2:3, :]
        cy_ref[0, r0:r0 + th, :] = (_dot(hb, w_ref[:, C_B:C_B + D_CONV]) * conv).astype(BF16)
        v_ref[0, r0:r0 + th, :] = projs[h][4]

    last = us_ref[tm:tm + 8, :]
    ul_ref[0] = last
    us_ref[0:8, :] = last

    for h in range(parts):
        rows = slice(h * th, (h + 1) * th)

        def store_q(slab, val, rows=rows):
            q_ref[0, rows, slab * LANES:(slab + 1) * LANES] = val

        k_ref[0, rows, :] = _qk_finish(projs[h], qn_ref[...], kn_ref[...], cos_ref[rows, :], sneg_ref[rows, :],
                                       spos_ref[rows, :], store_q)


def _in_first_body(head_ref, *refs, tm, parts):
    nb = tm // BLOCK
    blocks, rest, xpad_ref, us_ref = refs[:nb], refs[nb:-2], refs[-2], refs[-1]
    first = pl.program_id(1) == 0
    xpad_ref[0, 0:BLOCK, :] = jnp.where(first, head_ref[0], blocks[0][0])
    for j in range(1, nb):
        xpad_ref[0, j * BLOCK:(j + 1) * BLOCK, :] = blocks[j][0]
    _in_body(xpad_ref, *rest, us_ref, tm=tm, parts=parts)


def _prompt_in(l, x, nm, w_mix, cw, qn, kn, rope, s256, s128, tm, head=None):
    b = x.shape[0]
    lp = x.shape[1] if head is None else x.shape[1] + BLOCK
    nt = lp // tm
    nb = tm // BLOCK
    cos, sneg, spos = rope
    tok = lambda w: pl.BlockSpec((1, tm, w), lambda bi, i: (bi, i, 0))
    tab = pl.BlockSpec((tm, LANES), lambda bi, i: (i, 0))
    params = [_layer_spec(l, 1, D_MODEL), _w_mix_spec(l),
              _layer_spec(l, CONV_W, D_CONV), _layer_spec(l, 1, D_Q), _layer_spec(l, 1, D_KV),
              tab, tab, tab, _const_spec((256, 256)), _const_spec((LANES, LANES))]
    out_specs = [tok(D_CONV), tok(D_Q), tok(D_KV), tok(D_KV), pl.BlockSpec((1, 8, D_CONV), lambda bi, i: (bi, 0, 0))]
    out_shape = [jax.ShapeDtypeStruct((b, lp, D_CONV), BF16), jax.ShapeDtypeStruct((b, lp, D_Q), BF16),
                 jax.ShapeDtypeStruct((b, lp, D_KV), F32), jax.ShapeDtypeStruct((b, lp, D_KV), F32),
                 jax.ShapeDtypeStruct((b, 8, D_CONV), F32)]
    if head is None:
        body, x_specs, x_args = _in_body, [tok(D_MODEL)], (x,)
    else:
        blk = lambda j: pl.BlockSpec((1, BLOCK, D_MODEL), lambda bi, i: (bi, jnp.maximum(nb * i - 1 + j, 0), 0))
        body = _in_first_body
        x_specs = [pl.BlockSpec((1, BLOCK, D_MODEL), lambda bi, i: (bi, 0, 0))] + [blk(j) for j in range(nb)]
        x_args = (head,) + (x,) * nb
        out_specs.append(tok(D_MODEL))
        out_shape.append(jax.ShapeDtypeStruct((b, lp, D_MODEL), F32))
    return pl.pallas_call(
        functools.partial(body, tm=tm, parts=2),
        grid=(b, nt),
        in_specs=x_specs + params,
        out_specs=out_specs,
        out_shape=out_shape,
        scratch_shapes=[pltpu.VMEM((tm + 8, D_CONV), F32)],
        compiler_params=_cp(("arbitrary", "arbitrary")),
        name="prompt_in",
    )(*x_args, nm, w_mix, cw, qn, kn, cos, sneg, spos, s256, s128)


def _sin_body(x_ref, c0_ref, c1_ref, nm_ref, w_ref, cw_ref, qn_ref, kn_ref, cos_ref, sneg_ref,
              spos_ref, s256_ref, s128_ref, cy_ref, qx_ref, k_ref, v_ref, u_ref):
    hb = _rms_rows(x_ref[...], nm_ref[...]).astype(BF16)
    u = _dot(hb, w_ref[:, C_C:C_C + D_CONV]) * _dot(hb, w_ref[:, C_HC:C_HC + D_CONV])
    u_ref[...] = u
    conv = (c0_ref[...] * cw_ref[0:1, :] + c1_ref[...] * cw_ref[1:2, :]) + u * cw_ref[2:3, :]
    cy_ref[...] = (_dot(hb, w_ref[:, C_B:C_B + D_CONV]) * conv).astype(BF16)

    lane = lax.broadcasted_iota(jnp.int32, (x_ref.shape[0], LANES), 1)
    low = lane < HEAD_DIM

    def store_q(slab, val):
        valf = val.astype(F32)
        swapped = pltpu.roll(valf, HEAD_DIM, 1)
        zero = jnp.zeros_like(valf)
        for h in (2 * slab, 2 * slab + 1):
            src = valf if (h % 2) == (h // GQA) else swapped
            keep = low if (h // GQA) == 0 else jnp.logical_not(low)
            qx_ref[h] = jnp.where(keep, src, zero).astype(BF16)

    cos = jnp.broadcast_to(cos_ref[...], (x_ref.shape[0], LANES))
    sneg = jnp.broadcast_to(sneg_ref[...], (x_ref.shape[0], LANES))
    spos = jnp.broadcast_to(spos_ref[...], (x_ref.shape[0], LANES))
    proj = _qk_project(hb, w_ref, s256_ref[...], s128_ref[...])
    k_ref[...] = _qk_finish(proj, qn_ref[...], kn_ref[...], cos, sneg, spos, store_q)
    v_ref[...] = proj[4]


def _sample_in(l, x, c0, c1, nm, w_mix, cw, qn, kn, rope, s256, s128):
    n = x.shape[0]
    cos, sneg, spos = rope
    full = lambda *s: pl.BlockSpec(s, lambda i: (0,) * len(s))
    return pl.pallas_call(
        _sin_body,
        grid=(1,),
        in_specs=[full(n, D_MODEL), full(n, D_CONV), full(n, D_CONV), _layer_spec(l, 1, D_MODEL),
                  _w_mix_spec(l), _layer_spec(l, CONV_W, D_CONV), _layer_spec(l, 1, D_Q),
                  _layer_spec(l, 1, D_KV), full(1, LANES), full(1, LANES), full(1, LANES), full(256, 256),
                  full(LANES, LANES)],
        out_specs=[full(n, D_CONV), full(N_HEADS, n, LANES), full(n, D_KV), full(n, D_KV), full(n, D_CONV)],
        out_shape=[jax.ShapeDtypeStruct((n, D_CONV), BF16), jax.ShapeDtypeStruct((N_HEADS, n, LANES), BF16),
                   jax.ShapeDtypeStruct((n, D_KV), F32), jax.ShapeDtypeStruct((n, D_KV), F32),
                   jax.ShapeDtypeStruct((n, D_CONV), F32)],
        compiler_params=_cp(("arbitrary",)),
        name="sample_in",
    )(x, c0, c1, nm, w_mix, cw, qn, kn, cos, sneg, spos, s256, s128)


def _attn_body(sink_ref, q_ref, kp_ref, kc_ref, vp_ref, vc_ref, o_ref, ke_ref, ko_ref, vt_ref, *, qb, l):
    i = pl.program_id(1)
    lane = lax.broadcasted_iota(jnp.int32, (BLOCK, LANES), 1)
    low = lane < HEAD_DIM

    def prep_k(src, blk0, nblk):
        for t in range(nblk):
            blk = src[0, t * BLOCK:(t + 1) * BLOCK, :]
            swp = pltpu.roll(blk, HEAD_DIM, 1)
            zero = jnp.zeros_like(blk)
            rows = slice((blk0 + t) * BLOCK, (blk0 + t + 1) * BLOCK)
            ke_ref[0, rows, :] = jnp.where(low, blk, zero).astype(BF16)
            ko_ref[0, rows, :] = jnp.where(low, zero, swp).astype(BF16)
            ke_ref[1, rows, :] = jnp.where(low, swp, zero).astype(BF16)
            ko_ref[1, rows, :] = jnp.where(low, zero, blk).astype(BF16)

    def prep_v(src, blk0, nblk):
        for t in range(nblk):
            vt = jnp.transpose(src[0, t * BLOCK:(t + 1) * BLOCK, :]).astype(BF16)
            for j in range(N_KV_HEADS):
                vt_ref[blk0 + t, j] = vt[j * HEAD_DIM:(j + 1) * HEAD_DIM, :]

    prep_k(kp_ref, 0, 1)
    prep_k(kc_ref, 1, qb)
    prep_v(vp_ref, 0, 1)
    prep_v(vc_ref, 1, qb)

    c = lax.broadcasted_iota(jnp.int32, (2 * BLOCK, BLOCK), 0)
    r = lax.broadcasted_iota(jnp.int32, (2 * BLOCK, BLOCK), 1)
    diff = r - (c - BLOCK)
    band = (diff >= 0) & (diff < WINDOW)
    nt = (((1,), (1,)), ((), ()))

    def one_block(b, carry):
        r0 = pl.multiple_of(b * BLOCK, BLOCK)
        kpos = (i * qb + b) * BLOCK + c - BLOCK - PAD
        bias = jnp.where(band & (kpos >= 0), 0.0, NEG)
        for m in range(N_HEADS // 2):
            j = (2 * m) // GQA
            q2 = q_ref[0, pl.ds(r0, BLOCK), m * LANES:(m + 1) * LANES]
            halves = []
            for par, k_ref in ((0, ke_ref), (1, ko_ref)):
                st = lax.dot_general(k_ref[j, pl.ds(r0, 2 * BLOCK), :], q2, nt, preferred_element_type=F32)
                st = st + bias
                sink = sink_ref[l, 2 * m + par]
                mx = jnp.maximum(jnp.max(st, axis=0, keepdims=True), sink)
                p = jnp.exp2(st - mx)
                den = jnp.sum(p, axis=0, keepdims=True) + jnp.exp2(sink - mx)
                pb = p.astype(BF16)
                ot = _dot(vt_ref[b, j], pb[0:BLOCK]) + _dot(vt_ref[b + 1, j], pb[BLOCK:])
                halves.append(ot * (1.0 / den))
            o2 = jnp.transpose(jnp.concatenate(halves, axis=0))
            o_ref[0, pl.ds(r0, BLOCK), m * LANES:(m + 1) * LANES] = o2.astype(BF16)
        return carry

    lax.fori_loop(0, qb, one_block, 0, unroll=True)


def _prompt_attn(l, q, k, v, sinks, qb):
    b, lp, _ = q.shape
    nsteps = lp // (qb * BLOCK)
    cur = lambda w: pl.BlockSpec((1, qb * BLOCK, w), lambda bi, i: (bi, i, 0))
    prev = pl.BlockSpec((1, BLOCK, D_KV), lambda bi, i: (bi, jnp.maximum(i * qb - 1, 0), 0))
    ext = ((qb + 1) * BLOCK, LANES)
    return pl.pallas_call(
        functools.partial(_attn_body, qb=qb, l=l),
        grid=(b, nsteps),
        in_specs=[pl.BlockSpec(memory_space=pltpu.SMEM), cur(D_Q), prev, cur(D_KV), prev, cur(D_KV)],
        out_specs=cur(D_Q),
        out_shape=jax.ShapeDtypeStruct((b, lp, D_Q), BF16),
        scratch_shapes=[pltpu.VMEM((N_KV_HEADS,) + ext, BF16), pltpu.VMEM((N_KV_HEADS,) + ext, BF16),
                        pltpu.VMEM((qb + 1, N_KV_HEADS, HEAD_DIM, BLOCK), BF16)],
        compiler_params=_cp(("arbitrary", "arbitrary")),
        name="prompt_attn",
    )(sinks, q, k, k, v, v)


def _sattn_body(qx_ref, sink_ref, ck_ref, cv_ref, kn_ref, vn_ref, ox_ref, nk_ref, nv_ref, *, tb):
    def window(c_ref, n_ref, t):
        return jnp.concatenate([c_ref[t, 1:WINDOW, :], n_ref[t:t + 1, :]], axis=0)

    for t in range(tb):
        nk_ref[t] = window(ck_ref, kn_ref, t)
        nv_ref[t] = window(cv_ref, vn_ref, t)
    nt = (((1,), (1,)), ((), ()))
    s = jnp.concatenate([lax.dot_general(qx_ref[t], window(ck_ref, kn_ref, t).astype(BF16), nt,
                                         preferred_element_type=F32) for t in range(tb)], axis=0)
    sink = jnp.concatenate([sink_ref[...][:, 0:1]] * tb, axis=0)
    m = jnp.maximum(jnp.max(s, axis=-1, keepdims=True), sink)
    p = jnp.exp2(s - m)
    rden = 1.0 / (jnp.sum(p, axis=-1, keepdims=True) + jnp.exp2(sink - m))
    pb = p.astype(BF16)
    for t in range(tb):
        rows = slice(t * N_HEADS, (t + 1) * N_HEADS)
        ox_ref[t] = _dot(pb[rows], window(cv_ref, vn_ref, t).astype(BF16)) * rden[rows]


def _sample_attn(l, qx, sinkb, ck, cv, kn, vn, tb=16):
    n = qx.shape[0]
    blk3 = lambda a, c: pl.BlockSpec((tb, a, c), lambda i: (i, 0, 0))
    cache = pl.BlockSpec((None, tb, WINDOW, D_KV), lambda i: (l, i, 0, 0))
    row = pl.BlockSpec((tb, D_KV), lambda i: (i, 0))
    return pl.pallas_call(
        functools.partial(_sattn_body, tb=tb),
        grid=(n // tb,),
        in_specs=[blk3(N_HEADS, LANES), _layer_spec(l, N_HEADS, LANES), cache, cache, row, row],
        out_specs=[blk3(N_HEADS, LANES), blk3(WINDOW, D_KV), blk3(WINDOW, D_KV)],
        out_shape=[jax.ShapeDtypeStruct((n, N_HEADS, LANES), F32),
                   jax.ShapeDtypeStruct((n, WINDOW, D_KV), F32), jax.ShapeDtypeStruct((n, WINDOW, D_KV), F32)],
        compiler_params=_cp(("arbitrary",)),
        name="sample_attn",
    )(qx, sinkb, ck, cv, kn, vn)


ROUTE_ROWS = 24
RINV_LANE = LANES - 1


def _route(lt):
    top = lt[0:ROUTE_ROWS, :]
    rows = top.shape[1]
    rowf = lax.broadcasted_iota(jnp.int32, top.shape, 0).astype(F32)
    big = jnp.float32(3e38)
    far = jnp.float32(LANES)
    cmax = lambda a: jnp.max(a, axis=0, keepdims=True)
    cmin = lambda a: jnp.min(a, axis=0, keepdims=True)

    gmask = rowf < N_GROUPS
    gl = jnp.where(gmask, top, -big)
    gmax = cmax(gl)
    grp = cmin(jnp.where(gmask & (gl == gmax), rowf, far))
    p_grp = 1.0 / jnp.sum(jnp.where(gmask, jnp.exp(gl - gmax), 0.0), axis=0, keepdims=True)

    e_lo = N_GROUPS + EXPERTS_PER_GROUP * grp
    emask = (rowf >= e_lo) & (rowf < e_lo + EXPERTS_PER_GROUP)
    el = jnp.where(emask, top, -big)
    v1 = cmax(el)
    i1 = cmin(jnp.where(emask & (el == v1), rowf, far))
    rest = emask & (rowf != i1)
    el2 = jnp.where(rest, top, -big)
    v2 = cmax(el2)
    i2 = cmin(jnp.where(rest & (el2 == v2), rowf, far))
    e = jnp.exp(v2 - v1)
    w1 = (1.0 / (1.0 + e)) * p_grp
    w2 = (e / (1.0 + e)) * p_grp
    first_low = i1 < i2
    ea = jnp.where(first_low, i1, i2) - e_lo
    eb = jnp.where(first_low, i2, i1) - e_lo
    w_a = jnp.where(first_low, w1, w2)
    w_b = jnp.where(first_low, w2, w1)
    pair = jnp.where(ea == 0.0, 0.0, jnp.where(ea == 1.0, 3.0, 5.0)) + (eb - ea - 1.0)
    bucket = grp * N_PAIRS + pair

    r8 = lax.broadcasted_iota(jnp.int32, (SUBLANES, rows), 0)
    head = jnp.where(r8 == 0, w_a, jnp.where(r8 == 1, w_b, jnp.where(r8 == 2, lt[RINV_LANE:RINV_LANE + 1, :], 0.0)))
    return bucket, jnp.concatenate([head, jnp.zeros((LANES - SUBLANES, rows), F32)], axis=0)


def _out_body(x_ref, cy_ref, ao_ref, nm_ref, wg_ref, wco_ref, wao_ref, wo_ref, nf_ref, wr_ref, br_ref,
              tri_ref, cin_ref, xe_ref, rank_ref, bkt_ref, cnt_ref, run_ref, *, tm, parts):
    i = pl.program_id(0)
    th = tm // parts
    stage = []
    for h in range(parts):
        rows = slice(h * th, (h + 1) * th)
        hb = _rms_rows(x_ref[rows, :], nm_ref[...]).astype(BF16)
        stage.append((_dot(cy_ref[rows, :], wco_ref[...]), _dot(ao_ref[rows, :], wao_ref[...]),
                      _dot(hb, wg_ref[:, 0:D_MODEL]), _dot(hb, wg_ref[:, D_MODEL:])))
    for h, (ya, yb, gc, ga) in enumerate(stage):
        rows = slice(h * th, (h + 1) * th)
        mix = jax.nn.sigmoid(gc) * ya + jax.nn.sigmoid(ga) * yb
        xe_ref[rows, 0:D_MODEL] = x_ref[rows, :] + _dot(mix.astype(BF16), wo_ref[...])
    x1 = xe_ref[:, 0:D_MODEL]

    rinv = _rms_scale(x1)
    xnb = ((x1 * rinv) * nf_ref[...]).astype(BF16)
    logits = _dot(xnb, wr_ref[...]) + br_ref[...]
    lane = lax.broadcasted_iota(jnp.int32, (tm, LANES), 1)
    bucket, meta_t = _route(jnp.transpose(jnp.where(lane == RINV_LANE, rinv, logits)))
    xe_ref[:, D_MODEL:] = jnp.transpose(meta_t)

    @pl.when(i == 0)
    def _():
        run_ref[...] = cin_ref[...]

    sub = lax.broadcasted_iota(jnp.int32, (LANES, tm), 0).astype(F32)
    oht = (sub == bucket).astype(F32)
    before = _dot(oht.astype(BF16), tri_ref[...]) + run_ref[:, 0:1]
    rank_ref[0] = jnp.sum(oht * before, axis=0, keepdims=True).astype(jnp.int32)
    bkt_ref[0] = bucket.astype(jnp.int32)
    run_ref[...] = run_ref[...] + jnp.sum(oht, axis=-1, keepdims=True)
    cnt_ref[...] = run_ref[...]


def _mix_out(l, x, cy, ao, nm, w_gate, wco, wao, wo, nf, wr, br, tri, cnt_in, tm):
    t = x.shape[0]
    nt = t // tm
    tok = lambda w: pl.BlockSpec((tm, w), lambda i: (i, 0))
    rowi = pl.BlockSpec((1, 1, tm), lambda i: (i, 0, 0))
    sq = (D_MODEL, D_MODEL)
    return pl.pallas_call(
        functools.partial(_out_body, tm=tm, parts=2 if tm >= 4 * LANES else 1),
        grid=(nt,),
        in_specs=[tok(D_MODEL), tok(D_CONV), tok(D_Q), _layer_spec(l, 1, D_MODEL),
                  _layer_spec(l, D_MODEL, 2 * D_MODEL), _layer_spec(l, *sq), _layer_spec(l, *sq), _layer_spec(l, *sq),
                  _layer_spec(l, 1, D_MODEL), _layer_spec(l, D_MODEL, LANES), _layer_spec(l, 1, LANES),
                  _const_spec((tm, tm)), _const_spec((LANES, LANES))],
        out_specs=[tok(ROW_W), rowi, rowi, pl.BlockSpec((LANES, LANES), lambda i: (0, 0))],
        out_shape=[jax.ShapeDtypeStruct((t, ROW_W), F32),
                   jax.ShapeDtypeStruct((nt, 1, tm), jnp.int32), jax.ShapeDtypeStruct((nt, 1, tm), jnp.int32),
                   jax.ShapeDtypeStruct((LANES, LANES), F32)],
        scratch_shapes=[pltpu.VMEM((LANES, LANES), F32)],
        compiler_params=_cp(("arbitrary",)),
        name="mix_out",
    )(x, cy, ao, nm, w_gate, wco, wao, wo, nf, wr, br, tri, cnt_in)


SUB = SUBLANES


def _wait_rows(block_ref, sem):
    pltpu.make_async_copy(block_ref, block_ref, sem).wait()


def _scatter_tile_rows(pos_ref, src_ref, dst_ref, sem, tm):
    def group(g, c):
        for u in range(SUB):
            row = dst_ref.at[pl.ds(pos_ref[0, 0, g * SUB + u], 1), :]
            pltpu.make_async_copy(src_ref.at[g, pl.ds(u, 1), :], row, sem).start()
        return c

    lax.fori_loop(0, tm // SUB, group, 0)
    _wait_rows(src_ref, sem)


CAST_ROWS = 16


def _scatter_body(fill_ref, na_ref, pos_ref, src_ref, wg_ref, wu_ref, wd_ref, dst_ref, wgb_ref, wub_ref, wdb_ref,
                  zero_ref, sem, *, tm, n_tiles):
    tile_rows = lambda t: dst_ref.at[pl.ds(pl.multiple_of(t * MOE_TM, MOE_TM), MOE_TM), :]

    @pl.when(pl.program_id(0) == 0)
    def _():
        zero_ref[...] = jnp.zeros(zero_ref.shape, F32)
        for b in range(N_BUCKETS):
            pltpu.make_async_copy(zero_ref, tile_rows(fill_ref[b]), sem).start()

        def tail_start(t, c):
            pltpu.make_async_copy(zero_ref, tile_rows(t), sem).start()
            return c

        lax.fori_loop(na_ref[0], n_tiles, tail_start, 0)
        for b in range(N_BUCKETS):
            pltpu.make_async_copy(zero_ref, tile_rows(fill_ref[b]), sem).wait()

        def tail_wait(t, c):
            pltpu.make_async_copy(zero_ref, tile_rows(t), sem).wait()
            return c

        lax.fori_loop(na_ref[0], n_tiles, tail_wait, 0)

    n_up, n_down = D_MODEL // CAST_ROWS, D_EXPERT // CAST_ROWS

    def cast(src, dst, chunk):
        rows = pl.ds(pl.multiple_of(chunk * CAST_ROWS, CAST_ROWS), CAST_ROWS)
        dst[0, rows, :] = src[0, rows, :].astype(BF16)

    def step(it, c):
        for g in (2 * it, 2 * it + 1):
            for u in range(SUB):
                row = dst_ref.at[pl.ds(pos_ref[0, 0, g * SUB + u], 1), :]
                pltpu.make_async_copy(src_ref.at[g, pl.ds(u, 1), :], row, sem).start()
        cast(wg_ref, wgb_ref, jnp.minimum(it, n_up - 1))
        cast(wu_ref, wub_ref, jnp.minimum(it, n_up - 1))
        cast(wd_ref, wdb_ref, jnp.minimum(it, n_down - 1))
        return c

    assert tm % (2 * SUB) == 0 and tm // (2 * SUB) >= n_up
    lax.fori_loop(0, tm // (2 * SUB), step, 0)
    _wait_rows(src_ref, sem)


def _scatter_rows(l, fill_tile, n_act, pos, src, n_tiles, wg, wu, wd):
    nt = N_EXPERTS
    tm = src.shape[0] * SUB // nt
    w_in_spec = lambda r, c: pl.BlockSpec((1, r, c), lambda i, *_: (l * N_EXPERTS + i, 0, 0))
    w_out_spec = lambda r, c: pl.BlockSpec((1, r, c), lambda i, *_: (i, 0, 0))
    grid_spec = pltpu.PrefetchScalarGridSpec(
        num_scalar_prefetch=2,
        grid=(nt,),
        in_specs=[pl.BlockSpec((1, 1, tm), lambda i, *_: (i, 0, 0), memory_space=pltpu.SMEM),
                  pl.BlockSpec((tm // SUB, SUB, ROW_W), lambda i, *_: (i, 0, 0)),
                  w_in_spec(D_MODEL, D_EXPERT), w_in_spec(D_MODEL, D_EXPERT), w_in_spec(D_EXPERT, D_MODEL)],
        out_specs=[pl.BlockSpec(memory_space=pl.ANY),
                   w_out_spec(D_MODEL, D_EXPERT), w_out_spec(D_MODEL, D_EXPERT), w_out_spec(D_EXPERT, D_MODEL)],
        scratch_shapes=[pltpu.VMEM((MOE_TM, ROW_W), F32), pltpu.SemaphoreType.DMA(())],
    )
    return pl.pallas_call(
        functools.partial(_scatter_body, tm=tm, n_tiles=n_tiles),
        grid_spec=grid_spec,
        out_shape=[jax.ShapeDtypeStruct((n_tiles * MOE_TM, ROW_W), F32),
                   jax.ShapeDtypeStruct((N_EXPERTS, D_MODEL, D_EXPERT), BF16),
                   jax.ShapeDtypeStruct((N_EXPERTS, D_MODEL, D_EXPERT), BF16),
                   jax.ShapeDtypeStruct((N_EXPERTS, D_EXPERT, D_MODEL), BF16)],
        compiler_params=_cp(("arbitrary",)),
        name="dispatch_scatter",
    )(fill_tile, n_act, pos, src, wg, wu, wd)


def _scatter_more_body(pos_ref, src_ref, dst_in_ref, dst_ref, sem, *, tm):
    del dst_in_ref
    _scatter_tile_rows(pos_ref, src_ref, dst_ref, sem, tm)


def _scatter_more_rows(pos, src, dst, tm):
    nt = src.shape[0] * SUB // tm
    return pl.pallas_call(
        functools.partial(_scatter_more_body, tm=tm),
        grid=(nt,),
        in_specs=[pl.BlockSpec((1, 1, tm), lambda i: (i, 0, 0), memory_space=pltpu.SMEM),
                  pl.BlockSpec((tm // SUB, SUB, ROW_W), lambda i: (i, 0, 0)),
                  pl.BlockSpec(memory_space=pl.ANY)],
        out_specs=pl.BlockSpec(memory_space=pl.ANY),
        out_shape=jax.ShapeDtypeStruct(dst.shape, dst.dtype),
        scratch_shapes=[pltpu.SemaphoreType.DMA(())],
        input_output_aliases={2: 0},
        compiler_params=_cp(("arbitrary",)),
        name="dispatch_scatter_more",
    )(pos, src, dst)


MOE_STEP_TILES = 2


def _moe_body(ta_ref, tb_ref, na_ref, xs_ref, nf_ref, *refs):
    del ta_ref, tb_ref
    w_refs, y_ref = refs[:-1], refs[-1]
    first_tile = pl.program_id(0) * MOE_STEP_TILES

    @pl.when(first_tile < na_ref[0])
    def _():
        staged = []
        for t in range(MOE_STEP_TILES):
            rows = slice(t * MOE_TM, (t + 1) * MOE_TM)
            x1 = xs_ref[rows, 0:D_MODEL]
            xb = ((x1 * xs_ref[rows, D_MODEL + 2:D_MODEL + 3]) * nf_ref[...]).astype(BF16)
            w = w_refs[6 * t:6 * t + 6]
            staged.append((x1, [(_dot(xb, w[3 * k][0]), _dot(xb, w[3 * k + 1][0])) for k in range(2)]))
        for t in range(MOE_STEP_TILES):
            rows = slice(t * MOE_TM, (t + 1) * MOE_TM)
            y, ups = staged[t]
            for k, (a, u) in enumerate(ups):
                hdn = (jax.nn.silu(a) * u) * xs_ref[rows, D_MODEL + k:D_MODEL + k + 1]
                y = y + _dot(hdn.astype(BF16), w_refs[6 * t + 3 * k + 2][0])
            y_ref[rows, :] = y

    @pl.when(first_tile >= na_ref[0])
    def _():
        y_ref[...] = jnp.zeros(y_ref.shape, F32)


def _moe_experts(l, tile_a, tile_b, n_act, xs, nf, wg, wu, wd):
    step_rows = MOE_STEP_TILES * MOE_TM
    assert xs.shape[0] % step_rows == 0
    last = lambda tile, na: jnp.minimum(tile, na[0] - 1)
    expert = lambda sel, t, i, ta, tb, na: (sel(ta, tb)[last(MOE_STEP_TILES * i + t, na)], 0, 0)
    w_up = lambda sel, t: pl.BlockSpec((1, D_MODEL, D_EXPERT), functools.partial(expert, sel, t))
    w_dn = lambda sel, t: pl.BlockSpec((1, D_EXPERT, D_MODEL), functools.partial(expert, sel, t))
    sa = lambda ta, tb: ta
    sb = lambda ta, tb: tb
    w_specs = []
    for t in range(MOE_STEP_TILES):
        w_specs += [w_up(sa, t), w_up(sa, t), w_dn(sa, t), w_up(sb, t), w_up(sb, t), w_dn(sb, t)]
    grid_spec = pltpu.PrefetchScalarGridSpec(
        num_scalar_prefetch=3,
        grid=(xs.shape[0] // step_rows,),
        in_specs=[pl.BlockSpec((step_rows, ROW_W), lambda i, ta, tb, na: (jnp.minimum(i, (na[0] - 1) // MOE_STEP_TILES), 0)),
                  _layer_spec(l, 1, D_MODEL)] + w_specs,
        out_specs=pl.BlockSpec((step_rows, D_MODEL), lambda i, ta, tb, na: (i, 0)),
    )
    return pl.pallas_call(
        _moe_body,
        grid_spec=grid_spec,
        out_shape=jax.ShapeDtypeStruct((xs.shape[0], D_MODEL), F32),
        compiler_params=_cp(("arbitrary",)),
        name="moe_experts",
    )(tile_a, tile_b, n_act, xs, nf, *((wg, wu, wd) * (2 * MOE_STEP_TILES)))


def _unpermute_body(pos_ref, ys_ref, o_ref, sem, *, tm):
    def group(g, c):
        for u in range(SUB):
            row = ys_ref.at[pl.ds(pos_ref[0, 0, g * SUB + u], 1), :]
            pltpu.make_async_copy(row, o_ref.at[g, pl.ds(u, 1), :], sem).start()
        return c

    lax.fori_loop(0, tm // SUB, group, 0)
    _wait_rows(o_ref, sem)


def _unpermute(pos, ys, tm):
    t = pos.shape[0] * tm
    return pl.pallas_call(
        functools.partial(_unpermute_body, tm=tm),
        grid=(t // tm,),
        in_specs=[pl.BlockSpec((1, 1, tm), lambda i: (i, 0, 0), memory_space=pltpu.SMEM),
                  pl.BlockSpec(memory_space=pl.ANY)],
        out_specs=pl.BlockSpec((tm // SUB, SUB, D_MODEL), lambda i: (i, 0, 0)),
        out_shape=jax.ShapeDtypeStruct((t // SUB, SUB, D_MODEL), F32),
        scratch_shapes=[pltpu.SemaphoreType.DMA(())],
        compiler_params=_cp(("arbitrary",)),
        name="moe_unpermute",
    )(pos, ys)


def _rope_lane_freq():
    half = ROT_DIM // 2
    inv_freq = jnp.float32(ROPE_THETA) ** (-jnp.arange(half, dtype=jnp.float32) * (2.0 / ROT_DIM))
    dim = np.arange(LANES) % HEAD_DIM
    return inv_freq[dim % half][None, :], dim < half, (dim >= half) & (dim < ROT_DIM)


def _rope_patterns(cos, sin, first, second):
    return (jnp.where(first | second, cos, 1.0), jnp.where(first, -sin, 0.0), jnp.where(second, sin, 0.0))


def _rope_tables(pos):
    freq, first, second = _rope_lane_freq()
    ang = pos.astype(jnp.float32)[:, None] * freq
    return _rope_patterns(jnp.cos(ang), jnp.sin(ang), first, second)


def _rope_tables_padded(n_blocks):
    freq, first, second = _rope_lane_freq()
    ang_a = (jnp.arange(n_blocks, dtype=jnp.int32) * BLOCK).astype(jnp.float32)[:, None] * freq
    ang_b = (jnp.arange(BLOCK, dtype=jnp.int32) - PAD).astype(jnp.float32)[:, None] * freq
    ca, sa = jnp.cos(ang_a)[:, None, :], jnp.sin(ang_a)[:, None, :]
    cb, sb = jnp.cos(ang_b)[None], jnp.sin(ang_b)[None]
    flat = lambda t: t.reshape(n_blocks * BLOCK, LANES)
    return _rope_patterns(flat(ca * cb - sa * sb), flat(sa * cb + ca * sb), first, second)


def _seg_ones(n):
    idx = np.arange(n) // HEAD_DIM
    return jnp.asarray(idx[:, None] == idx[None, :], BF16)


def _bucket_experts():
    ea, eb = [], []
    for g in range(N_GROUPS):
        for a in range(EXPERTS_PER_GROUP):
            for b in range(a + 1, EXPERTS_PER_GROUP):
                ea.append(g * EXPERTS_PER_GROUP + a)
                eb.append(g * EXPERTS_PER_GROUP + b)
    return np.asarray(ea, np.int32), np.asarray(eb, np.int32)


def _dispatch_plan(counts, n_tiles):
    padded = ((counts + MOE_TM - 1) // MOE_TM) * MOE_TM
    ends = jnp.cumsum(padded)
    offs = ends - padded
    n_act = jnp.maximum(ends[-1] // MOE_TM, 1)
    starts = jnp.arange(n_tiles, dtype=jnp.int32) * MOE_TM
    tile_bucket = jnp.minimum(jnp.sum(starts[:, None] >= ends[None, :], axis=1), N_BUCKETS - 1)
    ea, eb = _bucket_experts()
    onehot = tile_bucket[:, None] == jnp.arange(N_BUCKETS)[None, :]
    tile_a = jnp.sum(jnp.where(onehot, ea[None, :], 0), axis=1).astype(jnp.int32)
    tile_b = jnp.sum(jnp.where(onehot, eb[None, :], 0), axis=1).astype(jnp.int32)
    fill_tile = jnp.maximum(ends // MOE_TM - 1, 0).astype(jnp.int32)
    return offs, tile_a, tile_b, n_act.astype(jnp.int32).reshape(1), fill_tile


def _positions(offs, bucket, rank):
    onehot = bucket[..., None] == jnp.arange(N_BUCKETS, dtype=jnp.int32)
    return (jnp.sum(jnp.where(onehot, offs.astype(jnp.int32), 0), axis=-1) + rank).astype(jnp.int32)


def kernel(x_prompt, x_sample, cache_k, cache_v, state_conv, meta_tokens, norm_mix, w_in, conv_w, q_norm, k_norm,
           attn_sinks, w_conv_out, w_attn_out, w_o, norm_ffn, w_router_group, b_router_group, w_router_expert,
           b_router_expert, w_exp_gate, w_exp_up, w_exp_down):
    batch, seq, _ = x_prompt.shape
    depth = w_in.shape[0]
    n_dec = x_sample.shape[0]
    past_len = PAST_LEN
    lp = PAD + N_META + seq
    tm_in, tm_out, qb = 640, 640, 5
    tm_move, tm_last = 3328, 4096
    assert PAD + N_META == BLOCK and seq % BLOCK == 0 and tm_in % BLOCK == 0
    assert lp % tm_in == 0 and (batch * lp) % tm_out == 0 and lp % (qb * BLOCK) == 0
    assert (batch * lp) % tm_move == 0 and (batch * seq) % tm_last == 0
    assert x_sample.shape[1] == 1 and cache_k.shape[2] == WINDOW and past_len >= WINDOW

    t_prompt = batch * lp
    t_all = t_prompt + n_dec
    n_tiles = -(-(t_all + N_BUCKETS * (MOE_TM - 1)) // MOE_TM)
    n_tiles = -(-n_tiles // MOE_STEP_TILES) * MOE_STEP_TILES

    meta = jnp.broadcast_to(meta_tokens[None].astype(F32), (batch, N_META, D_MODEL))
    head = jnp.concatenate([jnp.zeros((batch, PAD, D_MODEL), F32), meta], axis=1)
    xs = x_sample.reshape(n_dec, D_MODEL)

    rope_p = _rope_tables_padded(lp // BLOCK)
    rope_s = _rope_tables(jnp.full((1,), past_len, jnp.int32))
    s256, s128 = _seg_ones(256), _seg_ones(LANES)
    tri_p = jnp.asarray(np.triu(np.ones((tm_out, tm_out)), 1), BF16)
    tri_s = jnp.asarray(np.triu(np.ones((n_dec, n_dec)), 1), BF16)
    zero_cnt = jnp.zeros((LANES, LANES), F32)

    w_mix = w_in.astype(BF16)
    w_gate = w_in[:, :, D_MIX:].astype(BF16)
    wco, wao, wo = w_conv_out.astype(BF16), w_attn_out.astype(BF16), w_o.astype(BF16)
    wg32 = w_exp_gate.reshape(depth * N_EXPERTS, D_MODEL, D_EXPERT)
    wu32 = w_exp_up.reshape(depth * N_EXPERTS, D_MODEL, D_EXPERT)
    wd32 = w_exp_down.reshape(depth * N_EXPERTS, D_EXPERT, D_MODEL)
    nm, nf = norm_mix.reshape(depth, 1, D_MODEL), norm_ffn.reshape(depth, 1, D_MODEL)
    qn = (jnp.tile(q_norm, (1, N_HEADS)) * Q_SCALE).reshape(depth, 1, D_Q)
    kn = jnp.tile(k_norm, (1, N_KV_HEADS)).reshape(depth, 1, D_KV)
    r_pad = LANES - N_GROUPS - N_EXPERTS
    wr = jnp.concatenate([w_router_group, w_router_expert, jnp.zeros((depth, D_MODEL, r_pad), F32)], axis=-1).astype(BF16)
    br = jnp.concatenate([b_router_group, b_router_expert, jnp.zeros((depth, r_pad), F32)], axis=-1).reshape(depth, 1, LANES)
    sinks = attn_sinks.astype(F32) * LOG2E
    sinkb = jnp.broadcast_to(sinks[:, :, None], (depth, N_HEADS, LANES))
    ck = cache_k.reshape(depth, n_dec, WINDOW, D_KV)
    cv = cache_v.reshape(depth, n_dec, WINDOW, D_KV)

    outs = {k: [] for k in ("kp", "vp", "cp", "ks", "vs", "cs")}
    for l in range(depth):
        if l == 0:
            cy, q, k, v, ulast, xp = _prompt_in(l, x_prompt, nm, w_mix, conv_w, qn, kn, rope_p, s256, s128, tm_in, head)
        else:
            cy, q, k, v, ulast = _prompt_in(l, xp, nm, w_mix, conv_w, qn, kn, rope_p, s256, s128, tm_in)
        ao = _prompt_attn(l, q, k, v, sinks, qb)
        xep, rankp, bktp, cnt = _mix_out(
            l, xp.reshape(t_prompt, D_MODEL), cy.reshape(t_prompt, D_CONV), ao.reshape(t_prompt, D_Q),
            nm, w_gate, wco, wao, wo, nf, wr, br, tri_p, zero_cnt, tm_out)
        outs["kp"].append(k[:, lp - WINDOW:].reshape(batch, WINDOW, N_KV_HEADS, HEAD_DIM))
        outs["vp"].append(v[:, lp - WINDOW:].reshape(batch, WINDOW, N_KV_HEADS, HEAD_DIM))
        outs["cp"].append(ulast[:, 8 - (CONV_W - 1):])

        c0, c1 = state_conv[l, :, 0, :], state_conv[l, :, 1, :]
        cys, qx, ksn, vsn, us = _sample_in(l, xs, c0, c1, nm, w_mix, conv_w, qn, kn, rope_s, s256, s128)
        ox, nk, nv = _sample_attn(l, jnp.transpose(qx, (1, 0, 2)), sinkb, ck, cv, ksn, vsn)
        ox = ox.reshape(n_dec, N_KV_HEADS, GQA, N_KV_HEADS, HEAD_DIM)
        aos = jnp.stack([ox[:, j, :, j, :] for j in range(N_KV_HEADS)], axis=1).reshape(n_dec, D_Q).astype(BF16)
        xes, ranks, bkts, cnt = _mix_out(l, xs, cys, aos, nm, w_gate, wco, wao, wo, nf, wr, br, tri_s, cnt, n_dec)
        outs["ks"].append(nk.reshape(n_dec, WINDOW, N_KV_HEADS, HEAD_DIM))
        outs["vs"].append(nv.reshape(n_dec, WINDOW, N_KV_HEADS, HEAD_DIM))
        outs["cs"].append(jnp.stack([c1, us], axis=1))

        counts = cnt[:N_BUCKETS, 0].astype(jnp.int32)
        offs, tile_a, tile_b, n_act, fill_tile = _dispatch_plan(counts, n_tiles)
        posp = _positions(offs, bktp, rankp)
        poss = _positions(offs, bkts, ranks)
        by8 = lambda a: a.reshape(a.shape[0] // SUB, SUB, a.shape[1])
        sorted_rows, wg, wu, wd = _scatter_rows(l, fill_tile, n_act, posp.reshape(N_EXPERTS, 1, t_prompt // N_EXPERTS),
                                                by8(xep), n_tiles, wg32, wu32, wd32)
        posp = posp.reshape(t_prompt // tm_move, 1, tm_move)
        sorted_rows = _scatter_more_rows(poss, by8(xes), sorted_rows, n_dec)
        ys = _moe_experts(l, tile_a, tile_b, n_act, sorted_rows, nf, wg, wu, wd)
        xs = _unpermute(poss, ys, n_dec).reshape(n_dec, D_MODEL)
        if l + 1 < depth:
            xp = _unpermute(posp, ys, tm_move).reshape(batch, lp, D_MODEL)
        else:
            pos_tok = posp.reshape(batch, lp)[:, PAD + N_META:].reshape(batch * seq // tm_last, 1, tm_last)
            y_prompt = _unpermute(pos_tok, ys, tm_last).reshape(batch, seq, D_MODEL)

    y_sample = xs.reshape(n_dec, 1, D_MODEL)
    st = lambda k: jnp.stack(outs[k])
    return (y_prompt, y_sample, st("kp"), st("vp"), st("cp"), st("ks"), st("vs"), st("cs"))
```

```python
import functools
import math

import jax
import jax.numpy as jnp
import numpy as np
from jax import lax
from jax.experimental import pallas as pl
from jax.experimental.pallas import tpu as pltpu

D_MODEL = 1024
N_META = 16
D_CONV = D_MODEL
CONV_W = 3
N_HEADS = 16
N_KV_HEADS = 2
HEAD_DIM = 64
GQA = N_HEADS // N_KV_HEADS
ROT_DIM = HEAD_DIM // 4
ROPE_THETA = 500000.0
WINDOW = 128
PAST_LEN = 8192
BLOCK = 128
N_GROUPS = 4
EXPERTS_PER_GROUP = 4
N_EXPERTS = N_GROUPS * EXPERTS_PER_GROUP
D_EXPERT = 512
EPS = 1e-6
NEG = -1e30
D_Q = N_HEADS * HEAD_DIM
D_KV = N_KV_HEADS * HEAD_DIM
C_B, C_C, C_HC = 0, D_CONV, 2 * D_CONV
C_Q = 3 * D_CONV
C_K = C_Q + D_Q
C_V = C_K + D_KV
C_G = C_V + D_KV
D_MIX = C_G
D_IN = C_G + 2 * D_MODEL

LANES = 128
SUBLANES = 8
PAD = (-N_META) % BLOCK
N_PAIRS = 6
N_BUCKETS = N_GROUPS * N_PAIRS
MOE_TM = 256
ROW_W = D_MODEL + LANES
LOG2E = math.log2(math.e)
Q_SCALE = HEAD_DIM ** -0.5 * LOG2E

F32 = jnp.float32
BF16 = jnp.bfloat16
VMEM_LIMIT = 56 * 1024 * 1024


def _cp(sem, vmem=VMEM_LIMIT):
    return pltpu.CompilerParams(dimension_semantics=sem, vmem_limit_bytes=vmem)


def _const_spec(shape):
    nd = len(shape)
    return pl.BlockSpec(shape, lambda *_: (0,) * nd, pipeline_mode=pl.Buffered(1))


def _layer_spec(l, *shape):
    n = len(shape)
    return pl.BlockSpec((None,) + shape, lambda *_: (l,) + (0,) * n, pipeline_mode=pl.Buffered(1))


def _w_mix_spec(l):
    return pl.BlockSpec((None, D_MODEL, D_MIX), lambda *_: (l, 0, 0), pipeline_mode=pl.Buffered(1))


def _dot(a, b):
    return jnp.dot(a, b, preferred_element_type=F32)


def _seg_mean_sq(x, seg_ones):
    return _dot((x * x).astype(BF16), seg_ones) * (1.0 / HEAD_DIM)


def _rope128(t, cos, sin_pm, first):
    partner = jnp.where(first, pltpu.roll(t, LANES - ROT_DIM // 2, 1), pltpu.roll(t, ROT_DIM // 2, 1))
    return t * cos + partner * sin_pm


def _rms_scale(x):
    return lax.rsqrt(jnp.mean(x * x, axis=-1, keepdims=True) + EPS)


def _rms_rows(x, g):
    return (x * _rms_scale(x)) * g


def _qk_project(hb, w_ref, s256, s128):
    qs = [_dot(hb, w_ref[:, C_Q + c * 256:C_Q + (c + 1) * 256]) for c in range(D_Q // 256)]
    kv = _dot(hb, w_ref[:, C_K:C_K + 2 * D_KV])
    kc, v = kv[:, 0:D_KV], kv[:, D_KV:]
    return qs, [_seg_mean_sq(qc, s256) for qc in qs], kc, _seg_mean_sq(kc, s128), v


def _qk_finish(proj, qn, kn, cos, sneg, spos, store_q):
    qs, q_ms, kc, k_ms, _ = proj
    sin_pm = sneg + spos
    lane = lax.broadcasted_iota(jnp.int32, cos.shape, 1)
    first = lax.bitwise_and(lane, HEAD_DIM - 1) < ROT_DIM // 2
    for c, (qc, ms) in enumerate(zip(qs, q_ms)):
        qc = (qc * lax.rsqrt(ms + EPS)) * qn[:, c * 256:(c + 1) * 256]
        for s in range(2):
            r = _rope128(qc[:, s * LANES:(s + 1) * LANES], cos, sin_pm, first)
            store_q(2 * c + s, r.astype(BF16))
    kc = (kc * lax.rsqrt(k_ms + EPS)) * kn
    return _rope128(kc, cos, sin_pm, first)


def _in_body(x_ref, nm_ref, w_ref, cw_ref, qn_ref, kn_ref, cos_ref, sneg_ref, spos_ref,
             s256_ref, s128_ref, cy_ref, q_ref, k_ref, v_ref, ul_ref, us_ref, *, tm, parts):
    i = pl.program_id(1)
    th = tm // parts

    @pl.when(i == 0)
    def _():
        us_ref[0:8, :] = jnp.zeros((8, D_CONV), F32)

    projs = []
    for h in range(parts):
        r0 = h * th
        hb = _rms_rows(x_ref[0, r0:r0 + th, :], nm_ref[...]).astype(BF16)
        projs.append(_qk_project(hb, w_ref, s256_ref[...], s128_ref[...]))
        u = _dot(hb, w_ref[:, C_C:C_C + D_CONV]) * _dot(hb, w_ref[:, C_HC:C_HC + D_CONV])
        row = lax.broadcasted_iota(jnp.int32, (th, 1), 0) + (i * tm + r0)
        u = jnp.where(row >= PAD, u, 0.0)
        us_ref[8 + r0:8 + r0 + th, :] = u
        conv = (us_ref[6 + r0:6 + r0 + th, :] * cw_ref[0:1, :] + us_ref[7 + r0:7 + r0 + th, :] * cw_ref[1:2, :]) \
            + u * cw_ref[2:3, :]
        cy_ref[0, r0:r0 + th, :] = (_dot(hb, w_ref[:, C_B:C_B + D_CONV]) * conv).astype(BF16)
        v_ref[0, r0:r0 + th, :] = projs[h][4]

    last = us_ref[tm:tm + 8, :]
    ul_ref[0] = last
    us_ref[0:8, :] = last

    for h in range(parts):
        rows = slice(h * th, (h + 1) * th)

        def store_q(slab, val, rows=rows):
            q_ref[0, rows, slab * LANES:(slab + 1) * LANES] = val

        k_ref[0, rows, :] = _qk_finish(projs[h], qn_ref[...], kn_ref[...], cos_ref[rows, :], sneg_ref[rows, :],
                                       spos_ref[rows, :], store_q)


def _in_first_body(head_ref, *refs, tm, parts):
    nb = tm // BLOCK
    blocks, rest, xpad_ref, us_ref = refs[:nb], refs[nb:-2], refs[-2], refs[-1]
    first = pl.program_id(1) == 0
    xpad_ref[0, 0:BLOCK, :] = jnp.where(first, head_ref[0], blocks[0][0])
    for j in range(1, nb):
        xpad_ref[0, j * BLOCK:(j + 1) * BLOCK, :] = blocks[j][0]
    _in_body(xpad_ref, *rest, us_ref, tm=tm, parts=parts)


def _prompt_in(l, x, nm, w_mix, cw, qn, kn, rope, s256, s128, tm, head=None):
    b = x.shape[0]
    lp = x.shape[1] if head is None else x.shape[1] + BLOCK
    nt = lp // tm
    nb = tm // BLOCK
    cos, sneg, spos = rope
    tok = lambda w: pl.BlockSpec((1, tm, w), lambda bi, i: (bi, i, 0))
    tab = pl.BlockSpec((tm, LANES), lambda bi, i: (i, 0))
    params = [_layer_spec(l, 1, D_MODEL), _w_mix_spec(l),
              _layer_spec(l, CONV_W, D_CONV), _layer_spec(l, 1, D_Q), _layer_spec(l, 1, D_KV),
              tab, tab, tab, _const_spec((256, 256)), _const_spec((LANES, LANES))]
    out_specs = [tok(D_CONV), tok(D_Q), tok(D_KV), tok(D_KV), pl.BlockSpec((1, 8, D_CONV), lambda bi, i: (bi, 0, 0))]
    out_shape = [jax.ShapeDtypeStruct((b, lp, D_CONV), BF16), jax.ShapeDtypeStruct((b, lp, D_Q), BF16),
                 jax.ShapeDtypeStruct((b, lp, D_KV), F32), jax.ShapeDtypeStruct((b, lp, D_KV), F32),
                 jax.ShapeDtypeStruct((b, 8, D_CONV), F32)]
    if head is None:
        body, x_specs, x_args = _in_body, [tok(D_MODEL)], (x,)
    else:
        blk = lambda j: pl.BlockSpec((1, BLOCK, D_MODEL), lambda bi, i: (bi, jnp.maximum(nb * i - 1 + j, 0), 0))
        body = _in_first_body
        x_specs = [pl.BlockSpec((1, BLOCK, D_MODEL), lambda bi, i: (bi, 0, 0))] + [blk(j) for j in range(nb)]
        x_args = (head,) + (x,) * nb
        out_specs.append(tok(D_MODEL))
        out_shape.append(jax.ShapeDtypeStruct((b, lp, D_MODEL), F32))
    return pl.pallas_call(
        functools.partial(body, tm=tm, parts=2),
        grid=(b, nt),
        in_specs=x_specs + params,
        out_specs=out_specs,
        out_shape=out_shape,
        scratch_shapes=[pltpu.VMEM((tm + 8, D_CONV), F32)],
        compiler_params=_cp(("arbitrary", "arbitrary")),
        name="prompt_in",
    )(*x_args, nm, w_mix, cw, qn, kn, cos, sneg, spos, s256, s128)


def _sin_body(x_ref, c0_ref, c1_ref, nm_ref, w_ref, cw_ref, qn_ref, kn_ref, cos_ref, sneg_ref,
              spos_ref, s256_ref, s128_ref, cy_ref, qx_ref, k_ref, v_ref, u_ref):
    hb = _rms_rows(x_ref[...], nm_ref[...]).astype(BF16)
    u = _dot(hb, w_ref[:, C_C:C_C + D_CONV]) * _dot(hb, w_ref[:, C_HC:C_HC + D_CONV])
    u_ref[...] = u
    conv = (c0_ref[...] * cw_ref[0:1, :] + c1_ref[...] * cw_ref[1:2, :]) + u * cw_ref[2:3, :]
    cy_ref[...] = (_dot(hb, w_ref[:, C_B:C_B + D_CONV]) * conv).astype(BF16)

    lane = lax.broadcasted_iota(jnp.int32, (x_ref.shape[0], LANES), 1)
    low = lane < HEAD_DIM

    def store_q(slab, val):
        valf = val.astype(F32)
        swapped = pltpu.roll(valf, HEAD_DIM, 1)
        zero = jnp.zeros_like(valf)
        for h in (2 * slab, 2 * slab + 1):
            src = valf if (h % 2) == (h // GQA) else swapped
            keep = low if (h // GQA) == 0 else jnp.logical_not(low)
            qx_ref[h] = jnp.where(keep, src, zero).astype(BF16)

    cos = jnp.broadcast_to(cos_ref[...], (x_ref.shape[0], LANES))
    sneg = jnp.broadcast_to(sneg_ref[...], (x_ref.shape[0], LANES))
    spos = jnp.broadcast_to(spos_ref[...], (x_ref.shape[0], LANES))
    proj = _qk_project(hb, w_ref, s256_ref[...], s128_ref[...])
    k_ref[...] = _qk_finish(proj, qn_ref[...], kn_ref[...], cos, sneg, spos, store_q)
    v_ref[...] = proj[4]


def _sample_in(l, x, c0, c1, nm, w_mix, cw, qn, kn, rope, s256, s128):
    n = x.shape[0]
    cos, sneg, spos = rope
    full = lambda *s: pl.BlockSpec(s, lambda i: (0,) * len(s))
    return pl.pallas_call(
        _sin_body,
        grid=(1,),
        in_specs=[full(n, D_MODEL), full(n, D_CONV), full(n, D_CONV), _layer_spec(l, 1, D_MODEL),
                  _w_mix_spec(l), _layer_spec(l, CONV_W, D_CONV), _layer_spec(l, 1, D_Q),
                  _layer_spec(l, 1, D_KV), full(1, LANES), full(1, LANES), full(1, LANES), full(256, 256),
                  full(LANES, LANES)],
        out_specs=[full(n, D_CONV), full(N_HEADS, n, LANES), full(n, D_KV), full(n, D_KV), full(n, D_CONV)],
        out_shape=[jax.ShapeDtypeStruct((n, D_CONV), BF16), jax.ShapeDtypeStruct((N_HEADS, n, LANES), BF16),
                   jax.ShapeDtypeStruct((n, D_KV), F32), jax.ShapeDtypeStruct((n, D_KV), F32),
                   jax.ShapeDtypeStruct((n, D_CONV), F32)],
        compiler_params=_cp(("arbitrary",)),
        name="sample_in",
    )(x, c0, c1, nm, w_mix, cw, qn, kn, cos, sneg, spos, s256, s128)


def _attn_body(sink_ref, q_ref, kp_ref, kc_ref, vp_ref, vc_ref, o_ref, ke_ref, ko_ref, vt_ref, *, qb, l):
    i = pl.program_id(1)
    lane = lax.broadcasted_iota(jnp.int32, (BLOCK, LANES), 1)
    low = lane < HEAD_DIM

    def prep_k(src, blk0, nblk):
        for t in range(nblk):
            blk = src[0, t * BLOCK:(t + 1) * BLOCK, :]
            swp = pltpu.roll(blk, HEAD_DIM, 1)
            zero = jnp.zeros_like(blk)
            rows = slice((blk0 + t) * BLOCK, (blk0 + t + 1) * BLOCK)
            ke_ref[0, rows, :] = jnp.where(low, blk, zero).astype(BF16)
            ko_ref[0, rows, :] = jnp.where(low, zero, swp).astype(BF16)
            ke_ref[1, rows, :] = jnp.where(low, swp, zero).astype(BF16)
            ko_ref[1, rows, :] = jnp.where(low, zero, blk).astype(BF16)

    def prep_v(src, blk0, nblk):
        for t in range(nblk):
            vt = jnp.transpose(src[0, t * BLOCK:(t + 1) * BLOCK, :]).astype(BF16)
            for j in range(N_KV_HEADS):
                vt_ref[blk0 + t, j] = vt[j * HEAD_DIM:(j + 1) * HEAD_DIM, :]

    prep_k(kp_ref, 0, 1)
    prep_k(kc_ref, 1, qb)
    prep_v(vp_ref, 0, 1)
    prep_v(vc_ref, 1, qb)

    c = lax.broadcasted_iota(jnp.int32, (2 * BLOCK, BLOCK), 0)
    r = lax.broadcasted_iota(jnp.int32, (2 * BLOCK, BLOCK), 1)
    diff = r - (c - BLOCK)
    band = (diff >= 0) & (diff < WINDOW)
    nt = (((1,), (1,)), ((), ()))

    def one_block(b, carry):
        r0 = pl.multiple_of(b * BLOCK, BLOCK)
        kpos = (i * qb + b) * BLOCK + c - BLOCK - PAD
        bias = jnp.where(band & (kpos >= 0), 0.0, NEG)
        for m in range(N_HEADS // 2):
            j = (2 * m) // GQA
            q2 = q_ref[0, pl.ds(r0, BLOCK), m * LANES:(m + 1) * LANES]
            halves = []
            for par, k_ref in ((0, ke_ref), (1, ko_ref)):
                st = lax.dot_general(k_ref[j, pl.ds(r0, 2 * BLOCK), :], q2, nt, preferred_element_type=F32)
                st = st + bias
                sink = sink_ref[l, 2 * m + par]
                mx = jnp.maximum(jnp.max(st, axis=0, keepdims=True), sink)
                p = jnp.exp2(st - mx)
                den = jnp.sum(p, axis=0, keepdims=True) + jnp.exp2(sink - mx)
                pb = p.astype(BF16)
                ot = _dot(vt_ref[b, j], pb[0:BLOCK]) + _dot(vt_ref[b + 1, j], pb[BLOCK:])
                halves.append(ot * (1.0 / den))
            o2 = jnp.transpose(jnp.concatenate(halves, axis=0))
            o_ref[0, pl.ds(r0, BLOCK), m * LANES:(m + 1) * LANES] = o2.astype(BF16)
        return carry

    lax.fori_loop(0, qb, one_block, 0, unroll=True)


def _prompt_attn(l, q, k, v, sinks, qb):
    b, lp, _ = q.shape
    nsteps = lp // (qb * BLOCK)
    cur = lambda w: pl.BlockSpec((1, qb * BLOCK, w), lambda bi, i: (bi, i, 0))
    prev = pl.BlockSpec((1, BLOCK, D_KV), lambda bi, i: (bi, jnp.maximum(i * qb - 1, 0), 0))
    ext = ((qb + 1) * BLOCK, LANES)
    return pl.pallas_call(
        functools.partial(_attn_body, qb=qb, l=l),
        grid=(b, nsteps),
        in_specs=[pl.BlockSpec(memory_space=pltpu.SMEM), cur(D_Q), prev, cur(D_KV), prev, cur(D_KV)],
        out_specs=cur(D_Q),
        out_shape=jax.ShapeDtypeStruct((b, lp, D_Q), BF16),
        scratch_shapes=[pltpu.VMEM((N_KV_HEADS,) + ext, BF16), pltpu.VMEM((N_KV_HEADS,) + ext, BF16),
                        pltpu.VMEM((qb + 1, N_KV_HEADS, HEAD_DIM, BLOCK), BF16)],
        compiler_params=_cp(("arbitrary", "arbitrary")),
        name="prompt_attn",
    )(sinks, q, k, k, v, v)


def _sattn_body(qx_ref, sink_ref, ck_ref, cv_ref, kn_ref, vn_ref, ox_ref, nk_ref, nv_ref, *, tb):
    def window(c_ref, n_ref, t):
        return jnp.concatenate([c_ref[t, 1:WINDOW, :], n_ref[t:t + 1, :]], axis=0)

    for t in range(tb):
        nk_ref[t] = window(ck_ref, kn_ref, t)
        nv_ref[t] = window(cv_ref, vn_ref, t)
    nt = (((1,), (1,)), ((), ()))
    s = jnp.concatenate([lax.dot_general(qx_ref[t], window(ck_ref, kn_ref, t).astype(BF16), nt,
                                         preferred_element_type=F32) for t in range(tb)], axis=0)
    sink = jnp.concatenate([sink_ref[...][:, 0:1]] * tb, axis=0)
    m = jnp.maximum(jnp.max(s, axis=-1, keepdims=True), sink)
    p = jnp.exp2(s - m)
    rden = 1.0 / (jnp.sum(p, axis=-1, keepdims=True) + jnp.exp2(sink - m))
    pb = p.astype(BF16)
    for t in range(tb):
        rows = slice(t * N_HEADS, (t + 1) * N_HEADS)
        ox_ref[t] = _dot(pb[rows], window(cv_ref, vn_ref, t).astype(BF16)) * rden[rows]


def _sample_attn(l, qx, sinkb, ck, cv, kn, vn, tb=16):
    n = qx.shape[0]
    blk3 = lambda a, c: pl.BlockSpec((tb, a, c), lambda i: (i, 0, 0))
    cache = pl.BlockSpec((None, tb, WINDOW, D_KV), lambda i: (l, i, 0, 0))
    row = pl.BlockSpec((tb, D_KV), lambda i: (i, 0))
    return pl.pallas_call(
        functools.partial(_sattn_body, tb=tb),
        grid=(n // tb,),
        in_specs=[blk3(N_HEADS, LANES), _layer_spec(l, N_HEADS, LANES), cache, cache, row, row],
        out_specs=[blk3(N_HEADS, LANES), blk3(WINDOW, D_KV), blk3(WINDOW, D_KV)],
        out_shape=[jax.ShapeDtypeStruct((n, N_HEADS, LANES), F32),
                   jax.ShapeDtypeStruct((n, WINDOW, D_KV), F32), jax.ShapeDtypeStruct((n, WINDOW, D_KV), F32)],
        compiler_params=_cp(("arbitrary",)),
        name="sample_attn",
    )(qx, sinkb, ck, cv, kn, vn)


ROUTE_ROWS = 24
RINV_LANE = LANES - 1


def _route(lt):
    top = lt[0:ROUTE_ROWS, :]
    rows = top.shape[1]
    rowf = lax.broadcasted_iota(jnp.int32, top.shape, 0).astype(F32)
    big = jnp.float32(3e38)
    far = jnp.float32(LANES)
    cmax = lambda a: jnp.max(a, axis=0, keepdims=True)
    cmin = lambda a: jnp.min(a, axis=0, keepdims=True)

    gmask = rowf < N_GROUPS
    gl = jnp.where(gmask, top, -big)
    gmax = cmax(gl)
    grp = cmin(jnp.where(gmask & (gl == gmax), rowf, far))
    p_grp = 1.0 / jnp.sum(jnp.where(gmask, jnp.exp(gl - gmax), 0.0), axis=0, keepdims=True)

    e_lo = N_GROUPS + EXPERTS_PER_GROUP * grp
    emask = (rowf >= e_lo) & (rowf < e_lo + EXPERTS_PER_GROUP)
    el = jnp.where(emask, top, -big)
    v1 = cmax(el)
    i1 = cmin(jnp.where(emask & (el == v1), rowf, far))
    rest = emask & (rowf != i1)
    el2 = jnp.where(rest, top, -big)
    v2 = cmax(el2)
    i2 = cmin(jnp.where(rest & (el2 == v2), rowf, far))
    e = jnp.exp(v2 - v1)
    w1 = (1.0 / (1.0 + e)) * p_grp
    w2 = (e / (1.0 + e)) * p_grp
    first_low = i1 < i2
    ea = jnp.where(first_low, i1, i2) - e_lo
    eb = jnp.where(first_low, i2, i1) - e_lo
    w_a = jnp.where(first_low, w1, w2)
    w_b = jnp.where(first_low, w2, w1)
    pair = jnp.where(ea == 0.0, 0.0, jnp.where(ea == 1.0, 3.0, 5.0)) + (eb - ea - 1.0)
    bucket = grp * N_PAIRS + pair

    r8 = lax.broadcasted_iota(jnp.int32, (SUBLANES, rows), 0)
    head = jnp.where(r8 == 0, w_a, jnp.where(r8 == 1, w_b, jnp.where(r8 == 2, lt[RINV_LANE:RINV_LANE + 1, :], 0.0)))
    return bucket, jnp.concatenate([head, jnp.zeros((LANES - SUBLANES, rows), F32)], axis=0)


def _out_body(x_ref, cy_ref, ao_ref, nm_ref, wg_ref, wco_ref, wao_ref, wo_ref, nf_ref, wr_ref, br_ref,
              tri_ref, cin_ref, xe_ref, rank_ref, bkt_ref, cnt_ref, run_ref, *, tm, parts):
    i = pl.program_id(0)
    th = tm // parts
    stage = []
    for h in range(parts):
        rows = slice(h * th, (h + 1) * th)
        hb = _rms_rows(x_ref[rows, :], nm_ref[...]).astype(BF16)
        stage.append((_dot(cy_ref[rows, :], wco_ref[...]), _dot(ao_ref[rows, :], wao_ref[...]),
                      _dot(hb, wg_ref[:, 0:D_MODEL]), _dot(hb, wg_ref[:, D_MODEL:])))
    for h, (ya, yb, gc, ga) in enumerate(stage):
        rows = slice(h * th, (h + 1) * th)
        mix = jax.nn.sigmoid(gc) * ya + jax.nn.sigmoid(ga) * yb
        xe_ref[rows, 0:D_MODEL] = x_ref[rows, :] + _dot(mix.astype(BF16), wo_ref[...])
    x1 = xe_ref[:, 0:D_MODEL]

    rinv = _rms_scale(x1)
    xnb = ((x1 * rinv) * nf_ref[...]).astype(BF16)
    logits = _dot(xnb, wr_ref[...]) + br_ref[...]
    lane = lax.broadcasted_iota(jnp.int32, (tm, LANES), 1)
    bucket, meta_t = _route(jnp.transpose(jnp.where(lane == RINV_LANE, rinv, logits)))
    xe_ref[:, D_MODEL:] = jnp.transpose(meta_t)

    @pl.when(i == 0)
    def _():
        run_ref[...] = cin_ref[...]

    sub = lax.broadcasted_iota(jnp.int32, (LANES, tm), 0).astype(F32)
    oht = (sub == bucket).astype(F32)
    before = _dot(oht.astype(BF16), tri_ref[...]) + run_ref[:, 0:1]
    rank_ref[0] = jnp.sum(oht * before, axis=0, keepdims=True).astype(jnp.int32)
    bkt_ref[0] = bucket.astype(jnp.int32)
    run_ref[...] = run_ref[...] + jnp.sum(oht, axis=-1, keepdims=True)
    cnt_ref[...] = run_ref[...]


def _mix_out(l, x, cy, ao, nm, w_gate, wco, wao, wo, nf, wr, br, tri, cnt_in, tm):
    t = x.shape[0]
    nt = t // tm
    tok = lambda w: pl.BlockSpec((tm, w), lambda i: (i, 0))
    rowi = pl.BlockSpec((1, 1, tm), lambda i: (i, 0, 0))
    sq = (D_MODEL, D_MODEL)
    return pl.pallas_call(
        functools.partial(_out_body, tm=tm, parts=2 if tm >= 4 * LANES else 1),
        grid=(nt,),
        in_specs=[tok(D_MODEL), tok(D_CONV), tok(D_Q), _layer_spec(l, 1, D_MODEL),
                  _layer_spec(l, D_MODEL, 2 * D_MODEL), _layer_spec(l, *sq), _layer_spec(l, *sq), _layer_spec(l, *sq),
                  _layer_spec(l, 1, D_MODEL), _layer_spec(l, D_MODEL, LANES), _layer_spec(l, 1, LANES),
                  _const_spec((tm, tm)), _const_spec((LANES, LANES))],
        out_specs=[tok(ROW_W), rowi, rowi, pl.BlockSpec((LANES, LANES), lambda i: (0, 0))],
        out_shape=[jax.ShapeDtypeStruct((t, ROW_W), F32),
                   jax.ShapeDtypeStruct((nt, 1, tm), jnp.int32), jax.ShapeDtypeStruct((nt, 1, tm), jnp.int32),
                   jax.ShapeDtypeStruct((LANES, LANES), F32)],
        scratch_shapes=[pltpu.VMEM((LANES, LANES), F32)],
        compiler_params=_cp(("arbitrary",)),
        name="mix_out",
    )(x, cy, ao, nm, w_gate, wco, wao, wo, nf, wr, br, tri, cnt_in)


SUB = SUBLANES


def _wait_rows(block_ref, sem):
    pltpu.make_async_copy(block_ref, block_ref, sem).wait()


def _scatter_tile_rows(pos_ref, src_ref, dst_ref, sem, tm):
    def group(g, c):
        for u in range(SUB):
            row = dst_ref.at[pl.ds(pos_ref[0, 0, g * SUB + u], 1), :]
            pltpu.make_async_copy(src_ref.at[g, pl.ds(u, 1), :], row, sem).start()
        return c

    lax.fori_loop(0, tm // SUB, group, 0)
    _wait_rows(src_ref, sem)


CAST_ROWS = 16


def _scatter_body(fill_ref, na_ref, pos_ref, src_ref, wg_ref, wu_ref, wd_ref, dst_ref, wgb_ref, wub_ref, wdb_ref,
                  zero_ref, stage_ref, sems, *, tm, n_tiles):
    step_id = pl.program_id(0)
    slot = lax.rem(step_id, 2)
    sem = sems.at[2]
    tile_rows = lambda t: dst_ref.at[pl.ds(pl.multiple_of(t * MOE_TM, MOE_TM), MOE_TM), :]

    @pl.when(pl.program_id(0) == 0)
    def _():
        zero_ref[...] = jnp.zeros(zero_ref.shape, F32)
        for b in range(N_BUCKETS):
            pltpu.make_async_copy(zero_ref, tile_rows(fill_ref[b]), sem).start()

        def tail_start(t, c):
            pltpu.make_async_copy(zero_ref, tile_rows(t), sem).start()
            return c

        lax.fori_loop(na_ref[0], n_tiles, tail_start, 0)
        for b in range(N_BUCKETS):
            pltpu.make_async_copy(zero_ref, tile_rows(fill_ref[b]), sem).wait()

        def tail_wait(t, c):
            pltpu.make_async_copy(zero_ref, tile_rows(t), sem).wait()
            return c

        lax.fori_loop(na_ref[0], n_tiles, tail_wait, 0)

    n_up, n_down = D_MODEL // CAST_ROWS, D_EXPERT // CAST_ROWS

    def cast(src, dst, chunk):
        rows = pl.ds(pl.multiple_of(chunk * CAST_ROWS, CAST_ROWS), CAST_ROWS)
        dst[0, rows, :] = src[0, rows, :].astype(BF16)

    def step(it, c):
        for g in (2 * it, 2 * it + 1):
            stage_ref[slot, g] = src_ref[g]
            for u in range(SUB):
                row = dst_ref.at[pl.ds(pos_ref[0, 0, g * SUB + u], 1), :]
                pltpu.make_async_copy(stage_ref.at[slot, g, pl.ds(u, 1), :], row, sems.at[slot]).start()
        cast(wg_ref, wgb_ref, jnp.minimum(it, n_up - 1))
        cast(wu_ref, wub_ref, jnp.minimum(it, n_up - 1))
        cast(wd_ref, wdb_ref, jnp.minimum(it, n_down - 1))
        return c

    assert tm % (2 * SUB) == 0 and tm // (2 * SUB) >= n_up
    lax.fori_loop(0, tm // (2 * SUB), step, 0)

    @pl.when(step_id > 0)
    def _():
        _wait_rows(stage_ref.at[1 - slot], sems.at[1 - slot])

    @pl.when(step_id == pl.num_programs(0) - 1)
    def _():
        _wait_rows(stage_ref.at[slot], sems.at[slot])


def _scatter_rows(l, fill_tile, n_act, pos, src, n_tiles, wg, wu, wd):
    nt = N_EXPERTS
    tm = src.shape[0] * SUB // nt
    w_in_spec = lambda r, c: pl.BlockSpec((1, r, c), lambda i, *_: (l * N_EXPERTS + i, 0, 0))
    w_out_spec = lambda r, c: pl.BlockSpec((1, r, c), lambda i, *_: (i, 0, 0))
    grid_spec = pltpu.PrefetchScalarGridSpec(
        num_scalar_prefetch=2,
        grid=(nt,),
        in_specs=[pl.BlockSpec((1, 1, tm), lambda i, *_: (i, 0, 0), memory_space=pltpu.SMEM),
                  pl.BlockSpec((tm // SUB, SUB, ROW_W), lambda i, *_: (i, 0, 0)),
                  w_in_spec(D_MODEL, D_EXPERT), w_in_spec(D_MODEL, D_EXPERT), w_in_spec(D_EXPERT, D_MODEL)],
        out_specs=[pl.BlockSpec(memory_space=pl.ANY),
                   w_out_spec(D_MODEL, D_EXPERT), w_out_spec(D_MODEL, D_EXPERT), w_out_spec(D_EXPERT, D_MODEL)],
        scratch_shapes=[pltpu.VMEM((MOE_TM, ROW_W), F32), pltpu.VMEM((2, tm // SUB, SUB, ROW_W), F32),
                        pltpu.SemaphoreType.DMA((3,))],
    )
    return pl.pallas_call(
        functools.partial(_scatter_body, tm=tm, n_tiles=n_tiles),
        grid_spec=grid_spec,
        out_shape=[jax.ShapeDtypeStruct((n_tiles * MOE_TM, ROW_W), F32),
                   jax.ShapeDtypeStruct((N_EXPERTS, D_MODEL, D_EXPERT), BF16),
                   jax.ShapeDtypeStruct((N_EXPERTS, D_MODEL, D_EXPERT), BF16),
                   jax.ShapeDtypeStruct((N_EXPERTS, D_EXPERT, D_MODEL), BF16)],
        compiler_params=_cp(("arbitrary",)),
        name="dispatch_scatter",
    )(fill_tile, n_act, pos, src, wg, wu, wd)


def _scatter_more_body(pos_ref, src_ref, dst_in_ref, dst_ref, sem, *, tm):
    del dst_in_ref
    _scatter_tile_rows(pos_ref, src_ref, dst_ref, sem, tm)


def _scatter_more_rows(pos, src, dst, tm):
    nt = src.shape[0] * SUB // tm
    return pl.pallas_call(
        functools.partial(_scatter_more_body, tm=tm),
        grid=(nt,),
        in_specs=[pl.BlockSpec((1, 1, tm), lambda i: (i, 0, 0), memory_space=pltpu.SMEM),
                  pl.BlockSpec((tm // SUB, SUB, ROW_W), lambda i: (i, 0, 0)),
                  pl.BlockSpec(memory_space=pl.ANY)],
        out_specs=pl.BlockSpec(memory_space=pl.ANY),
        out_shape=jax.ShapeDtypeStruct(dst.shape, dst.dtype),
        scratch_shapes=[pltpu.SemaphoreType.DMA(())],
        input_output_aliases={2: 0},
        compiler_params=_cp(("arbitrary",)),
        name="dispatch_scatter_more",
    )(pos, src, dst)


MOE_STEP_TILES = 2


def _moe_body(ta_ref, tb_ref, na_ref, xs_ref, nf_ref, *refs):
    del ta_ref, tb_ref
    w_refs, y_ref = refs[:-1], refs[-1]
    first_tile = pl.program_id(0) * MOE_STEP_TILES

    @pl.when(first_tile < na_ref[0])
    def _():
        staged = []
        for t in range(MOE_STEP_TILES):
            rows = slice(t * MOE_TM, (t + 1) * MOE_TM)
            x1 = xs_ref[rows, 0:D_MODEL]
            xb = ((x1 * xs_ref[rows, D_MODEL + 2:D_MODEL + 3]) * nf_ref[...]).astype(BF16)
            w = w_refs[6 * t:6 * t + 6]
            staged.append((x1, [(_dot(xb, w[3 * k][0]), _dot(xb, w[3 * k + 1][0])) for k in range(2)]))
        for t in range(MOE_STEP_TILES):
            rows = slice(t * MOE_TM, (t + 1) * MOE_TM)
            y, ups = staged[t]
            for k, (a, u) in enumerate(ups):
                hdn = (jax.nn.silu(a) * u) * xs_ref[rows, D_MODEL + k:D_MODEL + k + 1]
                y = y + _dot(hdn.astype(BF16), w_refs[6 * t + 3 * k + 2][0])
            y_ref[rows, :] = y

    @pl.when(first_tile >= na_ref[0])
    def _():
        y_ref[...] = jnp.zeros(y_ref.shape, F32)


def _moe_experts(l, tile_a, tile_b, n_act, xs, nf, wg, wu, wd):
    step_rows = MOE_STEP_TILES * MOE_TM
    assert xs.shape[0] % step_rows == 0
    last = lambda tile, na: jnp.minimum(tile, na[0] - 1)
    expert = lambda sel, t, i, ta, tb, na: (sel(ta, tb)[last(MOE_STEP_TILES * i + t, na)], 0, 0)
    w_up = lambda sel, t: pl.BlockSpec((1, D_MODEL, D_EXPERT), functools.partial(expert, sel, t))
    w_dn = lambda sel, t: pl.BlockSpec((1, D_EXPERT, D_MODEL), functools.partial(expert, sel, t))
    sa = lambda ta, tb: ta
    sb = lambda ta, tb: tb
    w_specs = []
    for t in range(MOE_STEP_TILES):
        w_specs += [w_up(sa, t), w_up(sa, t), w_dn(sa, t), w_up(sb, t), w_up(sb, t), w_dn(sb, t)]
    grid_spec = pltpu.PrefetchScalarGridSpec(
        num_scalar_prefetch=3,
        grid=(xs.shape[0] // step_rows,),
        in_specs=[pl.BlockSpec((step_rows, ROW_W), lambda i, ta, tb, na: (jnp.minimum(i, (na[0] - 1) // MOE_STEP_TILES), 0)),
                  _layer_spec(l, 1, D_MODEL)] + w_specs,
        out_specs=pl.BlockSpec((step_rows, D_MODEL), lambda i, ta, tb, na: (i, 0)),
    )
    return pl.pallas_call(
        _moe_body,
        grid_spec=grid_spec,
        out_shape=jax.ShapeDtypeStruct((xs.shape[0], D_MODEL), F32),
        compiler_params=_cp(("arbitrary",)),
        name="moe_experts",
    )(tile_a, tile_b, n_act, xs, nf, *((wg, wu, wd) * (2 * MOE_STEP_TILES)))


def _unpermute_body(pos_ref, ys_ref, o_ref, sem, *, tm):
    def group(g, c):
        for u in range(SUB):
            row = ys_ref.at[pl.ds(pos_ref[0, 0, g * SUB + u], 1), :]
            pltpu.make_async_copy(row, o_ref.at[g, pl.ds(u, 1), :], sem).start()
        return c

    lax.fori_loop(0, tm // SUB, group, 0)
    _wait_rows(o_ref, sem)


def _unpermute(pos, ys, tm):
    t = pos.shape[0] * tm
    return pl.pallas_call(
        functools.partial(_unpermute_body, tm=tm),
        grid=(t // tm,),
        in_specs=[pl.BlockSpec((1, 1, tm), lambda i: (i, 0, 0), memory_space=pltpu.SMEM),
                  pl.BlockSpec(memory_space=pl.ANY)],
        out_specs=pl.BlockSpec((tm // SUB, SUB, D_MODEL), lambda i: (i, 0, 0)),
        out_shape=jax.ShapeDtypeStruct((t // SUB, SUB, D_MODEL), F32),
        scratch_shapes=[pltpu.SemaphoreType.DMA(())],
        compiler_params=_cp(("arbitrary",)),
        name="moe_unpermute",
    )(pos, ys)


def _rope_lane_freq():
    half = ROT_DIM // 2
    inv_freq = jnp.float32(ROPE_THETA) ** (-jnp.arange(half, dtype=jnp.float32) * (2.0 / ROT_DIM))
    dim = np.arange(LANES) % HEAD_DIM
    return inv_freq[dim % half][None, :], dim < half, (dim >= half) & (dim < ROT_DIM)


def _rope_patterns(cos, sin, first, second):
    return (jnp.where(first | second, cos, 1.0), jnp.where(first, -sin, 0.0), jnp.where(second, sin, 0.0))


def _rope_tables(pos):
    freq, first, second = _rope_lane_freq()
    ang = pos.astype(jnp.float32)[:, None] * freq
    return _rope_patterns(jnp.cos(ang), jnp.sin(ang), first, second)


def _rope_tables_padded(n_blocks):
    freq, first, second = _rope_lane_freq()
    ang_a = (jnp.arange(n_blocks, dtype=jnp.int32) * BLOCK).astype(jnp.float32)[:, None] * freq
    ang_b = (jnp.arange(BLOCK, dtype=jnp.int32) - PAD).astype(jnp.float32)[:, None] * freq
    ca, sa = jnp.cos(ang_a)[:, None, :], jnp.sin(ang_a)[:, None, :]
    cb, sb = jnp.cos(ang_b)[None], jnp.sin(ang_b)[None]
    flat = lambda t: t.reshape(n_blocks * BLOCK, LANES)
    return _rope_patterns(flat(ca * cb - sa * sb), flat(sa * cb + ca * sb), first, second)


def _seg_ones(n):
    idx = np.arange(n) // HEAD_DIM
    return jnp.asarray(idx[:, None] == idx[None, :], BF16)


def _bucket_experts():
    ea, eb = [], []
    for g in range(N_GROUPS):
        for a in range(EXPERTS_PER_GROUP):
            for b in range(a + 1, EXPERTS_PER_GROUP):
                ea.append(g * EXPERTS_PER_GROUP + a)
                eb.append(g * EXPERTS_PER_GROUP + b)
    return np.asarray(ea, np.int32), np.asarray(eb, np.int32)


def _dispatch_plan(counts, n_tiles):
    padded = ((counts + MOE_TM - 1) // MOE_TM) * MOE_TM
    ends = jnp.cumsum(padded)
    offs = ends - padded
    n_act = jnp.maximum(ends[-1] // MOE_TM, 1)
    starts = jnp.arange(n_tiles, dtype=jnp.int32) * MOE_TM
    tile_bucket = jnp.minimum(jnp.sum(starts[:, None] >= ends[None, :], axis=1), N_BUCKETS - 1)
    ea, eb = _bucket_experts()
    onehot = tile_bucket[:, None] == jnp.arange(N_BUCKETS)[None, :]
    tile_a = jnp.sum(jnp.where(onehot, ea[None, :], 0), axis=1).astype(jnp.int32)
    tile_b = jnp.sum(jnp.where(onehot, eb[None, :], 0), axis=1).astype(jnp.int32)
    fill_tile = jnp.maximum(ends // MOE_TM - 1, 0).astype(jnp.int32)
    return offs, tile_a, tile_b, n_act.astype(jnp.int32).reshape(1), fill_tile


def _positions(offs, bucket, rank):
    onehot = bucket[..., None] == jnp.arange(N_BUCKETS, dtype=jnp.int32)
    return (jnp.sum(jnp.where(onehot, offs.astype(jnp.int32), 0), axis=-1) + rank).astype(jnp.int32)


def kernel(x_prompt, x_sample, cache_k, cache_v, state_conv, meta_tokens, norm_mix, w_in, conv_w, q_norm, k_norm,
           attn_sinks, w_conv_out, w_attn_out, w_o, norm_ffn, w_router_group, b_router_group, w_router_expert,
           b_router_expert, w_exp_gate, w_exp_up, w_exp_down):
    batch, seq, _ = x_prompt.shape
    depth = w_in.shape[0]
    n_dec = x_sample.shape[0]
    past_len = PAST_LEN
    lp = PAD + N_META + seq
    tm_in, tm_out, qb = 640, 640, 5
    tm_move, tm_last = 3328, 4096
    assert PAD + N_META == BLOCK and seq % BLOCK == 0 and tm_in % BLOCK == 0
    assert lp % tm_in == 0 and (batch * lp) % tm_out == 0 and lp % (qb * BLOCK) == 0
    assert (batch * lp) % tm_move == 0 and (batch * seq) % tm_last == 0
    assert x_sample.shape[1] == 1 and cache_k.shape[2] == WINDOW and past_len >= WINDOW

    t_prompt = batch * lp
    t_all = t_prompt + n_dec
    n_tiles = -(-(t_all + N_BUCKETS * (MOE_TM - 1)) // MOE_TM)
    n_tiles = -(-n_tiles // MOE_STEP_TILES) * MOE_STEP_TILES

    meta = jnp.broadcast_to(meta_tokens[None].astype(F32), (batch, N_META, D_MODEL))
    head = jnp.concatenate([jnp.zeros((batch, PAD, D_MODEL), F32), meta], axis=1)
    xs = x_sample.reshape(n_dec, D_MODEL)

    rope_p = _rope_tables_padded(lp // BLOCK)
    rope_s = _rope_tables(jnp.full((1,), past_len, jnp.int32))
    s256, s128 = _seg_ones(256), _seg_ones(LANES)
    tri_p = jnp.asarray(np.triu(np.ones((tm_out, tm_out)), 1), BF16)
    tri_s = jnp.asarray(np.triu(np.ones((n_dec, n_dec)), 1), BF16)
    zero_cnt = jnp.zeros((LANES, LANES), F32)

    w_mix = w_in.astype(BF16)
    w_gate = w_in[:, :, D_MIX:].astype(BF16)
    wco, wao, wo = w_conv_out.astype(BF16), w_attn_out.astype(BF16), w_o.astype(BF16)
    wg32 = w_exp_gate.reshape(depth * N_EXPERTS, D_MODEL, D_EXPERT)
    wu32 = w_exp_up.reshape(depth * N_EXPERTS, D_MODEL, D_EXPERT)
    wd32 = w_exp_down.reshape(depth * N_EXPERTS, D_EXPERT, D_MODEL)
    nm, nf = norm_mix.reshape(depth, 1, D_MODEL), norm_ffn.reshape(depth, 1, D_MODEL)
    qn = (jnp.tile(q_norm, (1, N_HEADS)) * Q_SCALE).reshape(depth, 1, D_Q)
    kn = jnp.tile(k_norm, (1, N_KV_HEADS)).reshape(depth, 1, D_KV)
    r_pad = LANES - N_GROUPS - N_EXPERTS
    wr = jnp.concatenate([w_router_group, w_router_expert, jnp.zeros((depth, D_MODEL, r_pad), F32)], axis=-1).astype(BF16)
    br = jnp.concatenate([b_router_group, b_router_expert, jnp.zeros((depth, r_pad), F32)], axis=-1).reshape(depth, 1, LANES)
    sinks = attn_sinks.astype(F32) * LOG2E
    sinkb = jnp.broadcast_to(sinks[:, :, None], (depth, N_HEADS, LANES))
    ck = cache_k.reshape(depth, n_dec, WINDOW, D_KV)
    cv = cache_v.reshape(depth, n_dec, WINDOW, D_KV)

    outs = {k: [] for k in ("kp", "vp", "cp", "ks", "vs", "cs")}
    for l in range(depth):
        if l == 0:
            cy, q, k, v, ulast, xp = _prompt_in(l, x_prompt, nm, w_mix, conv_w, qn, kn, rope_p, s256, s128, tm_in, head)
        else:
            cy, q, k, v, ulast = _prompt_in(l, xp, nm, w_mix, conv_w, qn, kn, rope_p, s256, s128, tm_in)
        ao = _prompt_attn(l, q, k, v, sinks, qb)
        xep, rankp, bktp, cnt = _mix_out(
            l, xp.reshape(t_prompt, D_MODEL), cy.reshape(t_prompt, D_CONV), ao.reshape(t_prompt, D_Q),
            nm, w_gate, wco, wao, wo, nf, wr, br, tri_p, zero_cnt, tm_out)
        outs["kp"].append(k[:, lp - WINDOW:].reshape(batch, WINDOW, N_KV_HEADS, HEAD_DIM))
        outs["vp"].append(v[:, lp - WINDOW:].reshape(batch, WINDOW, N_KV_HEADS, HEAD_DIM))
        outs["cp"].append(ulast[:, 8 - (CONV_W - 1):])

        c0, c1 = state_conv[l, :, 0, :], state_conv[l, :, 1, :]
        cys, qx, ksn, vsn, us = _sample_in(l, xs, c0, c1, nm, w_mix, conv_w, qn, kn, rope_s, s256, s128)
        ox, nk, nv = _sample_attn(l, jnp.transpose(qx, (1, 0, 2)), sinkb, ck, cv, ksn, vsn)
        ox = ox.reshape(n_dec, N_KV_HEADS, GQA, N_KV_HEADS, HEAD_DIM)
        aos = jnp.stack([ox[:, j, :, j, :] for j in range(N_KV_HEADS)], axis=1).reshape(n_dec, D_Q).astype(BF16)
        xes, ranks, bkts, cnt = _mix_out(l, xs, cys, aos, nm, w_gate, wco, wao, wo, nf, wr, br, tri_s, cnt, n_dec)
        outs["ks"].append(nk.reshape(n_dec, WINDOW, N_KV_HEADS, HEAD_DIM))
        outs["vs"].append(nv.reshape(n_dec, WINDOW, N_KV_HEADS, HEAD_DIM))
        outs["cs"].append(jnp.stack([c1, us], axis=1))

        counts = cnt[:N_BUCKETS, 0].astype(jnp.int32)
        offs, tile_a, tile_b, n_act, fill_tile = _dispatch_plan(counts, n_tiles)
        posp = _positions(offs, bktp, rankp)
        poss = _positions(offs, bkts, ranks)
        by8 = lambda a: a.reshape(a.shape[0] // SUB, SUB, a.shape[1])
        sorted_rows, wg, wu, wd = _scatter_rows(l, fill_tile, n_act, posp.reshape(N_EXPERTS, 1, t_prompt // N_EXPERTS),
                                                by8(xep), n_tiles, wg32, wu32, wd32)
        posp = posp.reshape(t_prompt // tm_move, 1, tm_move)
        sorted_rows = _scatter_more_rows(poss, by8(xes), sorted_rows, n_dec)
        ys = _moe_experts(l, tile_a, tile_b, n_act, sorted_rows, nf, wg, wu, wd)
        xs = _unpermute(poss, ys, n_dec).reshape(n_dec, D_MODEL)
        if l + 1 < depth:
            xp = _unpermute(posp, ys, tm_move).reshape(batch, lp, D_MODEL)
        else:
            pos_tok = posp.reshape(batch, lp)[:, PAD + N_META:].reshape(batch * seq // tm_last, 1, tm_last)
            y_prompt = _unpermute(pos_tok, ys, tm_last).reshape(batch, seq, D_MODEL)

    y_sample = xs.reshape(n_dec, 1, D_MODEL)
    st = lambda k: jnp.stack(outs[k])
    return (y_prompt, y_sample, st("kp"), st("vp"), st("cp"), st("ks"), st("vs"), st("cs"))
```

```python
import functools
import math

import jax
import jax.numpy as jnp
import numpy as np
from jax import lax
from jax.experimental import pallas as pl
from jax.experimental.pallas import tpu as pltpu

D_MODEL = 1024
N_META = 16
D_CONV = D_MODEL
CONV_W = 3
N_HEADS = 16
N_KV_HEADS = 2
HEAD_DIM = 64
GQA = N_HEADS // N_KV_HEADS
ROT_DIM = HEAD_DIM // 4
ROPE_THETA = 500000.0
WINDOW = 128
PAST_LEN = 8192
BLOCK = 128
N_GROUPS = 4
EXPERTS_PER_GROUP = 4
N_EXPERTS = N_GROUPS * EXPERTS_PER_GROUP
D_EXPERT = 512
EPS = 1e-6
NEG = -1e30
D_Q = N_HEADS * HEAD_DIM
D_KV = N_KV_HEADS * HEAD_DIM
C_B, C_C, C_HC = 0, D_CONV, 2 * D_CONV
C_Q = 3 * D_CONV
C_K = C_Q + D_Q
C_V = C_K + D_KV
C_G = C_V + D_KV
D_MIX = C_G
D_IN = C_G + 2 * D_MODEL

LANES = 128
SUBLANES = 8
PAD = (-N_META) % BLOCK
N_PAIRS = 6
N_BUCKETS = N_GROUPS * N_PAIRS
MOE_TM = 256
ROW_W = D_MODEL + LANES
LOG2E = math.log2(math.e)
Q_SCALE = HEAD_DIM ** -0.5 * LOG2E

F32 = jnp.float32
BF16 = jnp.bfloat16
VMEM_LIMIT = 56 * 1024 * 1024


def _cp(sem, vmem=VMEM_LIMIT):
    return pltpu.CompilerParams(dimension_semantics=sem, vmem_limit_bytes=vmem)


def _const_spec(shape):
    nd = len(shape)
    return pl.BlockSpec(shape, lambda *_: (0,) * nd, pipeline_mode=pl.Buffered(1))


def _layer_spec(l, *shape):
    n = len(shape)
    return pl.BlockSpec((None,) + shape, lambda *_: (l,) + (0,) * n, pipeline_mode=pl.Buffered(1))


def _w_mix_spec(l):
    return pl.BlockSpec((None, D_MODEL, D_MIX), lambda *_: (l, 0, 0), pipeline_mode=pl.Buffered(1))


def _dot(a, b):
    return jnp.dot(a, b, preferred_element_type=F32)


def _seg_mean_sq(x, seg_ones):
    return _dot((x * x).astype(BF16), seg_ones) * (1.0 / HEAD_DIM)


def _rope128(t, cos, sin_pm, first):
    partner = jnp.where(first, pltpu.roll(t, LANES - ROT_DIM // 2, 1), pltpu.roll(t, ROT_DIM // 2, 1))
    return t * cos + partner * sin_pm


def _rms_scale(x):
    return lax.rsqrt(jnp.mean(x * x, axis=-1, keepdims=True) + EPS)


def _rms_rows(x, g):
    return (x * _rms_scale(x)) * g


def _qk_project(hb, w_ref, s256, s128):
    qs = [_dot(hb, w_ref[:, C_Q + c * 256:C_Q + (c + 1) * 256]) for c in range(D_Q // 256)]
    kv = _dot(hb, w_ref[:, C_K:C_K + 2 * D_KV])
    kc, v = kv[:, 0:D_KV], kv[:, D_KV:]
    return qs, [_seg_mean_sq(qc, s256) for qc in qs], kc, _seg_mean_sq(kc, s128), v


def _qk_finish(proj, qn, kn, cos, sneg, spos, store_q):
    qs, q_ms, kc, k_ms, _ = proj
    sin_pm = sneg + spos
    lane = lax.broadcasted_iota(jnp.int32, cos.shape, 1)
    first = lax.bitwise_and(lane, HEAD_DIM - 1) < ROT_DIM // 2
    for c, (qc, ms) in enumerate(zip(qs, q_ms)):
        qc = (qc * lax.rsqrt(ms + EPS)) * qn[:, c * 256:(c + 1) * 256]
        for s in range(2):
            r = _rope128(qc[:, s * LANES:(s + 1) * LANES], cos, sin_pm, first)
            store_q(2 * c + s, r.astype(BF16))
    kc = (kc * lax.rsqrt(k_ms + EPS)) * kn
    return _rope128(kc, cos, sin_pm, first)


def _in_body(x_ref, nm_ref, w_ref, cw_ref, qn_ref, kn_ref, cos_ref, sneg_ref, spos_ref,
             s256_ref, s128_ref, cy_ref, q_ref, k_ref, v_ref, ul_ref, us_ref, *, tm, parts):
    i = pl.program_id(1)
    th = tm // parts

    @pl.when(i == 0)
    def _():
        us_ref[0:8, :] = jnp.zeros((8, D_CONV), F32)

    projs = []
    for h in range(parts):
        r0 = h * th
        hb = _rms_rows(x_ref[0, r0:r0 + th, :], nm_ref[...]).astype(BF16)
        projs.append(_qk_project(hb, w_ref, s256_ref[...], s128_ref[...]))
        u = _dot(hb, w_ref[:, C_C:C_C + D_CONV]) * _dot(hb, w_ref[:, C_HC:C_HC + D_CONV])
        row = lax.broadcasted_iota(jnp.int32, (th, 1), 0) + (i * tm + r0)
        u = jnp.where(row >= PAD, u, 0.0)
        us_ref[8 + r0:8 + r0 + th, :] = u
        conv = (us_ref[6 + r0:6 + r0 + th, :] * cw_ref[0:1, :] + us_ref[7 + r0:7 + r0 + th, :] * cw_ref[1:2, :]) \
            + u * cw_ref[2:3, :]
        cy_ref[0, r0:r0 + th, :] = (_dot(hb, w_ref[:, C_B:C_B + D_CONV]) * conv).astype(BF16)
        v_ref[0, r0:r0 + th, :] = projs[h][4]

    last = us_ref[tm:tm + 8, :]
    ul_ref[0] = last
    us_ref[0:8, :] = last

    for h in range(parts):
        rows = slice(h * th, (h + 1) * th)

        def store_q(slab, val, rows=rows):
            q_ref[0, rows, slab * LANES:(slab + 1) * LANES] = val

        k_ref[0, rows, :] = _qk_finish(projs[h], qn_ref[...], kn_ref[...], cos_ref[rows, :], sneg_ref[rows, :],
                                       spos_ref[rows, :], store_q)


def _in_first_body(head_ref, *refs, tm, parts):
    nb = tm // BLOCK
    blocks, rest, xpad_ref, us_ref = refs[:nb], refs[nb:-2], refs[-2], refs[-1]
    first = pl.program_id(1) == 0
    xpad_ref[0, 0:BLOCK, :] = jnp.where(first, head_ref[0], blocks[0][0])
    for j in range(1, nb):
        xpad_ref[0, j * BLOCK:(j + 1) * BLOCK, :] = blocks[j][0]
    _in_body(xpad_ref, *rest, us_ref, tm=tm, parts=parts)


def _prompt_in(l, x, nm, w_mix, cw, qn, kn, rope, s256, s128, tm, head=None):
    b = x.shape[0]
    lp = x.shape[1] if head is None else x.shape[1] + BLOCK
    nt = lp // tm
    nb = tm // BLOCK
    cos, sneg, spos = rope
    tok = lambda w: pl.BlockSpec((1, tm, w), lambda bi, i: (bi, i, 0))
    tab = pl.BlockSpec((tm, LANES), lambda bi, i: (i, 0))
    params = [_layer_spec(l, 1, D_MODEL), _w_mix_spec(l),
              _layer_spec(l, CONV_W, D_CONV), _layer_spec(l, 1, D_Q), _layer_spec(l, 1, D_KV),
              tab, tab, tab, _const_spec((256, 256)), _const_spec((LANES, LANES))]
    out_specs = [tok(D_CONV), tok(D_Q), tok(D_KV), tok(D_KV), pl.BlockSpec((1, 8, D_CONV), lambda bi, i: (bi, 0, 0))]
    out_shape = [jax.ShapeDtypeStruct((b, lp, D_CONV), BF16), jax.ShapeDtypeStruct((b, lp, D_Q), BF16),
                 jax.ShapeDtypeStruct((b, lp, D_KV), F32), jax.ShapeDtypeStruct((b, lp, D_KV), F32),
                 jax.ShapeDtypeStruct((b, 8, D_CONV), F32)]
    if head is None:
        body, x_specs, x_args = _in_body, [tok(D_MODEL)], (x,)
    else:
        blk = lambda j: pl.BlockSpec((1, BLOCK, D_MODEL), lambda bi, i: (bi, jnp.maximum(nb * i - 1 + j, 0), 0))
        body = _in_first_body
        x_specs = [pl.BlockSpec((1, BLOCK, D_MODEL), lambda bi, i: (bi, 0, 0))] + [blk(j) for j in range(nb)]
        x_args = (head,) + (x,) * nb
        out_specs.append(tok(D_MODEL))
        out_shape.append(jax.ShapeDtypeStruct((b, lp, D_MODEL), F32))
    return pl.pallas_call(
        functools.partial(body, tm=tm, parts=4),
        grid=(b, nt),
        in_specs=x_specs + params,
        out_specs=out_specs,
        out_shape=out_shape,
        scratch_shapes=[pltpu.VMEM((tm + 8, D_CONV), F32)],
        compiler_params=_cp(("arbitrary", "arbitrary")),
        name="prompt_in",
    )(*x_args, nm, w_mix, cw, qn, kn, cos, sneg, spos, s256, s128)


def _sin_body(x_ref, c0_ref, c1_ref, nm_ref, w_ref, cw_ref, qn_ref, kn_ref, cos_ref, sneg_ref,
              spos_ref, s256_ref, s128_ref, cy_ref, qx_ref, k_ref, v_ref, u_ref):
    hb = _rms_rows(x_ref[...], nm_ref[...]).astype(BF16)
    u = _dot(hb, w_ref[:, C_C:C_C + D_CONV]) * _dot(hb, w_ref[:, C_HC:C_HC + D_CONV])
    u_ref[...] = u
    conv = (c0_ref[...] * cw_ref[0:1, :] + c1_ref[...] * cw_ref[1:2, :]) + u * cw_ref[2:3, :]
    cy_ref[...] = (_dot(hb, w_ref[:, C_B:C_B + D_CONV]) * conv).astype(BF16)

    lane = lax.broadcasted_iota(jnp.int32, (x_ref.shape[0], LANES), 1)
    low = lane < HEAD_DIM

    def store_q(slab, val):
        valf = val.astype(F32)
        swapped = pltpu.roll(valf, HEAD_DIM, 1)
        zero = jnp.zeros_like(valf)
        for h in (2 * slab, 2 * slab + 1):
            src = valf if (h % 2) == (h // GQA) else swapped
            keep = low if (h // GQA) == 0 else jnp.logical_not(low)
            qx_ref[h] = jnp.where(keep, src, zero).astype(BF16)

    cos = jnp.broadcast_to(cos_ref[...], (x_ref.shape[0], LANES))
    sneg = jnp.broadcast_to(sneg_ref[...], (x_ref.shape[0], LANES))
    spos = jnp.broadcast_to(spos_ref[...], (x_ref.shape[0], LANES))
    proj = _qk_project(hb, w_ref, s256_ref[...], s128_ref[...])
    k_ref[...] = _qk_finish(proj, qn_ref[...], kn_ref[...], cos, sneg, spos, store_q)
    v_ref[...] = proj[4]


def _sample_in(l, x, c0, c1, nm, w_mix, cw, qn, kn, rope, s256, s128):
    n = x.shape[0]
    cos, sneg, spos = rope
    full = lambda *s: pl.BlockSpec(s, lambda i: (0,) * len(s))
    return pl.pallas_call(
        _sin_body,
        grid=(1,),
        in_specs=[full(n, D_MODEL), full(n, D_CONV), full(n, D_CONV), _layer_spec(l, 1, D_MODEL),
                  _w_mix_spec(l), _layer_spec(l, CONV_W, D_CONV), _layer_spec(l, 1, D_Q),
                  _layer_spec(l, 1, D_KV), full(1, LANES), full(1, LANES), full(1, LANES), full(256, 256),
                  full(LANES, LANES)],
        out_specs=[full(n, D_CONV), full(N_HEADS, n, LANES), full(n, D_KV), full(n, D_KV), full(n, D_CONV)],
        out_shape=[jax.ShapeDtypeStruct((n, D_CONV), BF16), jax.ShapeDtypeStruct((N_HEADS, n, LANES), BF16),
                   jax.ShapeDtypeStruct((n, D_KV), F32), jax.ShapeDtypeStruct((n, D_KV), F32),
                   jax.ShapeDtypeStruct((n, D_CONV), F32)],
        compiler_params=_cp(("arbitrary",)),
        name="sample_in",
    )(x, c0, c1, nm, w_mix, cw, qn, kn, cos, sneg, spos, s256, s128)


def _attn_body(sink_ref, q_ref, kp_ref, kc_ref, vp_ref, vc_ref, o_ref, ke_ref, ko_ref, vt_ref, *, qb, l):
    i = pl.program_id(1)
    lane = lax.broadcasted_iota(jnp.int32, (BLOCK, LANES), 1)
    low = lane < HEAD_DIM

    def prep_k(src, blk0, nblk):
        for t in range(nblk):
            blk = src[0, t * BLOCK:(t + 1) * BLOCK, :]
            swp = pltpu.roll(blk, HEAD_DIM, 1)
            zero = jnp.zeros_like(blk)
            rows = slice((blk0 + t) * BLOCK, (blk0 + t + 1) * BLOCK)
            ke_ref[0, rows, :] = jnp.where(low, blk, zero).astype(BF16)
            ko_ref[0, rows, :] = jnp.where(low, zero, swp).astype(BF16)
            ke_ref[1, rows, :] = jnp.where(low, swp, zero).astype(BF16)
            ko_ref[1, rows, :] = jnp.where(low, zero, blk).astype(BF16)

    def prep_v(src, blk0, nblk):
        for t in range(nblk):
            vt = jnp.transpose(src[0, t * BLOCK:(t + 1) * BLOCK, :]).astype(BF16)
            for j in range(N_KV_HEADS):
                vt_ref[blk0 + t, j] = vt[j * HEAD_DIM:(j + 1) * HEAD_DIM, :]

    prep_k(kp_ref, 0, 1)
    prep_k(kc_ref, 1, qb)
    prep_v(vp_ref, 0, 1)
    prep_v(vc_ref, 1, qb)

    c = lax.broadcasted_iota(jnp.int32, (2 * BLOCK, BLOCK), 0)
    r = lax.broadcasted_iota(jnp.int32, (2 * BLOCK, BLOCK), 1)
    diff = r - (c - BLOCK)
    band = (diff >= 0) & (diff < WINDOW)
    nt = (((1,), (1,)), ((), ()))

    def one_block(b, carry):
        r0 = pl.multiple_of(b * BLOCK, BLOCK)
        kpos = (i * qb + b) * BLOCK + c - BLOCK - PAD
        bias = jnp.where(band & (kpos >= 0), 0.0, NEG)
        for m in range(N_HEADS // 2):
            j = (2 * m) // GQA
            q2 = q_ref[0, pl.ds(r0, BLOCK), m * LANES:(m + 1) * LANES]
            halves = []
            for par, k_ref in ((0, ke_ref), (1, ko_ref)):
                st = lax.dot_general(k_ref[j, pl.ds(r0, 2 * BLOCK), :], q2, nt, preferred_element_type=F32)
                st = st + bias
                sink = sink_ref[l, 2 * m + par]
                mx = jnp.maximum(jnp.max(st, axis=0, keepdims=True), sink)
                p = jnp.exp2(st - mx)
                den = jnp.sum(p, axis=0, keepdims=True) + jnp.exp2(sink - mx)
                pb = p.astype(BF16)
                ot = _dot(vt_ref[b, j], pb[0:BLOCK]) + _dot(vt_ref[b + 1, j], pb[BLOCK:])
                halves.append(ot * (1.0 / den))
            o2 = jnp.transpose(jnp.concatenate(halves, axis=0))
            o_ref[0, pl.ds(r0, BLOCK), m * LANES:(m + 1) * LANES] = o2.astype(BF16)
        return carry

    lax.fori_loop(0, qb, one_block, 0, unroll=True)


def _prompt_attn(l, q, k, v, sinks, qb):
    b, lp, _ = q.shape
    nsteps = lp // (qb * BLOCK)
    cur = lambda w: pl.BlockSpec((1, qb * BLOCK, w), lambda bi, i: (bi, i, 0))
    prev = pl.BlockSpec((1, BLOCK, D_KV), lambda bi, i: (bi, jnp.maximum(i * qb - 1, 0), 0))
    ext = ((qb + 1) * BLOCK, LANES)
    return pl.pallas_call(
        functools.partial(_attn_body, qb=qb, l=l),
        grid=(b, nsteps),
        in_specs=[pl.BlockSpec(memory_space=pltpu.SMEM), cur(D_Q), prev, cur(D_KV), prev, cur(D_KV)],
        out_specs=cur(D_Q),
        out_shape=jax.ShapeDtypeStruct((b, lp, D_Q), BF16),
        scratch_shapes=[pltpu.VMEM((N_KV_HEADS,) + ext, BF16), pltpu.VMEM((N_KV_HEADS,) + ext, BF16),
                        pltpu.VMEM((qb + 1, N_KV_HEADS, HEAD_DIM, BLOCK), BF16)],
        compiler_params=_cp(("arbitrary", "arbitrary")),
        name="prompt_attn",
    )(sinks, q, k, k, v, v)


def _sattn_body(qx_ref, sink_ref, ck_ref, cv_ref, kn_ref, vn_ref, ox_ref, nk_ref, nv_ref, *, tb):
    def window(c_ref, n_ref, t):
        return jnp.concatenate([c_ref[t, 1:WINDOW, :], n_ref[t:t + 1, :]], axis=0)

    for t in range(tb):
        nk_ref[t] = window(ck_ref, kn_ref, t)
        nv_ref[t] = window(cv_ref, vn_ref, t)
    nt = (((1,), (1,)), ((), ()))
    s = jnp.concatenate([lax.dot_general(qx_ref[t], window(ck_ref, kn_ref, t).astype(BF16), nt,
                                         preferred_element_type=F32) for t in range(tb)], axis=0)
    sink = jnp.concatenate([sink_ref[...][:, 0:1]] * tb, axis=0)
    m = jnp.maximum(jnp.max(s, axis=-1, keepdims=True), sink)
    p = jnp.exp2(s - m)
    rden = 1.0 / (jnp.sum(p, axis=-1, keepdims=True) + jnp.exp2(sink - m))
    pb = p.astype(BF16)
    for t in range(tb):
        rows = slice(t * N_HEADS, (t + 1) * N_HEADS)
        ox_ref[t] = _dot(pb[rows], window(cv_ref, vn_ref, t).astype(BF16)) * rden[rows]


def _sample_attn(l, qx, sinkb, ck, cv, kn, vn, tb=16):
    n = qx.shape[0]
    blk3 = lambda a, c: pl.BlockSpec((tb, a, c), lambda i: (i, 0, 0))
    cache = pl.BlockSpec((None, tb, WINDOW, D_KV), lambda i: (l, i, 0, 0))
    row = pl.BlockSpec((tb, D_KV), lambda i: (i, 0))
    return pl.pallas_call(
        functools.partial(_sattn_body, tb=tb),
        grid=(n // tb,),
        in_specs=[blk3(N_HEADS, LANES), _layer_spec(l, N_HEADS, LANES), cache, cache, row, row],
        out_specs=[blk3(N_HEADS, LANES), blk3(WINDOW, D_KV), blk3(WINDOW, D_KV)],
        out_shape=[jax.ShapeDtypeStruct((n, N_HEADS, LANES), F32),
                   jax.ShapeDtypeStruct((n, WINDOW, D_KV), F32), jax.ShapeDtypeStruct((n, WINDOW, D_KV), F32)],
        compiler_params=_cp(("arbitrary",)),
        name="sample_attn",
    )(qx, sinkb, ck, cv, kn, vn)


ROUTE_ROWS = 24
RINV_LANE = LANES - 1


def _route(lt):
    top = lt[0:ROUTE_ROWS, :]
    rows = top.shape[1]
    rowf = lax.broadcasted_iota(jnp.int32, top.shape, 0).astype(F32)
    big = jnp.float32(3e38)
    far = jnp.float32(LANES)
    cmax = lambda a: jnp.max(a, axis=0, keepdims=True)
    cmin = lambda a: jnp.min(a, axis=0, keepdims=True)

    gmask = rowf < N_GROUPS
    gl = jnp.where(gmask, top, -big)
    gmax = cmax(gl)
    grp = cmin(jnp.where(gmask & (gl == gmax), rowf, far))
    p_grp = 1.0 / jnp.sum(jnp.where(gmask, jnp.exp(gl - gmax), 0.0), axis=0, keepdims=True)

    e_lo = N_GROUPS + EXPERTS_PER_GROUP * grp
    emask = (rowf >= e_lo) & (rowf < e_lo + EXPERTS_PER_GROUP)
    el = jnp.where(emask, top, -big)
    v1 = cmax(el)
    i1 = cmin(jnp.where(emask & (el == v1), rowf, far))
    rest = emask & (rowf != i1)
    el2 = jnp.where(rest, top, -big)
    v2 = cmax(el2)
    i2 = cmin(jnp.where(rest & (el2 == v2), rowf, far))
    e = jnp.exp(v2 - v1)
    w1 = (1.0 / (1.0 + e)) * p_grp
    w2 = (e / (1.0 + e)) * p_grp
    first_low = i1 < i2
    ea = jnp.where(first_low, i1, i2) - e_lo
    eb = jnp.where(first_low, i2, i1) - e_lo
    w_a = jnp.where(first_low, w1, w2)
    w_b = jnp.where(first_low, w2, w1)
    pair = jnp.where(ea == 0.0, 0.0, jnp.where(ea == 1.0, 3.0, 5.0)) + (eb - ea - 1.0)
    bucket = grp * N_PAIRS + pair

    r8 = lax.broadcasted_iota(jnp.int32, (SUBLANES, rows), 0)
    head = jnp.where(r8 == 0, w_a, jnp.where(r8 == 1, w_b, jnp.where(r8 == 2, lt[RINV_LANE:RINV_LANE + 1, :], 0.0)))
    return bucket, jnp.concatenate([head, jnp.zeros((LANES - SUBLANES, rows), F32)], axis=0)


def _out_body(x_ref, cy_ref, ao_ref, nm_ref, wg_ref, wco_ref, wao_ref, wo_ref, nf_ref, wr_ref, br_ref,
              tri_ref, cin_ref, xe_ref, rank_ref, bkt_ref, cnt_ref, run_ref, *, tm, parts):
    i = pl.program_id(0)
    th = tm // parts
    stage = []
    for h in range(parts):
        rows = slice(h * th, (h + 1) * th)
        hb = _rms_rows(x_ref[rows, :], nm_ref[...]).astype(BF16)
        stage.append((_dot(cy_ref[rows, :], wco_ref[...]), _dot(ao_ref[rows, :], wao_ref[...]),
                      _dot(hb, wg_ref[:, 0:D_MODEL]), _dot(hb, wg_ref[:, D_MODEL:])))
    for h, (ya, yb, gc, ga) in enumerate(stage):
        rows = slice(h * th, (h + 1) * th)
        mix = jax.nn.sigmoid(gc) * ya + jax.nn.sigmoid(ga) * yb
        xe_ref[rows, 0:D_MODEL] = x_ref[rows, :] + _dot(mix.astype(BF16), wo_ref[...])
    x1 = xe_ref[:, 0:D_MODEL]

    rinv = _rms_scale(x1)
    xnb = ((x1 * rinv) * nf_ref[...]).astype(BF16)
    logits = _dot(xnb, wr_ref[...]) + br_ref[...]
    lane = lax.broadcasted_iota(jnp.int32, (tm, LANES), 1)
    bucket, meta_t = _route(jnp.transpose(jnp.where(lane == RINV_LANE, rinv, logits)))
    xe_ref[:, D_MODEL:] = jnp.transpose(meta_t)

    @pl.when(i == 0)
    def _():
        run_ref[...] = cin_ref[...]

    sub = lax.broadcasted_iota(jnp.int32, (LANES, tm), 0).astype(F32)
    oht = (sub == bucket).astype(F32)
    before = _dot(oht.astype(BF16), tri_ref[...]) + run_ref[:, 0:1]
    rank_ref[0] = jnp.sum(oht * before, axis=0, keepdims=True).astype(jnp.int32)
    bkt_ref[0] = bucket.astype(jnp.int32)
    run_ref[...] = run_ref[...] + jnp.sum(oht, axis=-1, keepdims=True)
    cnt_ref[...] = run_ref[...]


def _mix_out(l, x, cy, ao, nm, w_gate, wco, wao, wo, nf, wr, br, tri, cnt_in, tm):
    t = x.shape[0]
    nt = t // tm
    tok = lambda w: pl.BlockSpec((tm, w), lambda i: (i, 0))
    rowi = pl.BlockSpec((1, 1, tm), lambda i: (i, 0, 0))
    sq = (D_MODEL, D_MODEL)
    return pl.pallas_call(
        functools.partial(_out_body, tm=tm, parts=4 if tm >= 4 * LANES else 1),
        grid=(nt,),
        in_specs=[tok(D_MODEL), tok(D_CONV), tok(D_Q), _layer_spec(l, 1, D_MODEL),
                  _layer_spec(l, D_MODEL, 2 * D_MODEL), _layer_spec(l, *sq), _layer_spec(l, *sq), _layer_spec(l, *sq),
                  _layer_spec(l, 1, D_MODEL), _layer_spec(l, D_MODEL, LANES), _layer_spec(l, 1, LANES),
                  _const_spec((tm, tm)), _const_spec((LANES, LANES))],
        out_specs=[tok(ROW_W), rowi, rowi, pl.BlockSpec((LANES, LANES), lambda i: (0, 0))],
        out_shape=[jax.ShapeDtypeStruct((t, ROW_W), F32),
                   jax.ShapeDtypeStruct((nt, 1, tm), jnp.int32), jax.ShapeDtypeStruct((nt, 1, tm), jnp.int32),
                   jax.ShapeDtypeStruct((LANES, LANES), F32)],
        scratch_shapes=[pltpu.VMEM((LANES, LANES), F32)],
        compiler_params=_cp(("arbitrary",)),
        name="mix_out",
    )(x, cy, ao, nm, w_gate, wco, wao, wo, nf, wr, br, tri, cnt_in)


SUB = SUBLANES


def _wait_rows(block_ref, sem):
    pltpu.make_async_copy(block_ref, block_ref, sem).wait()


def _scatter_tile_rows(pos_ref, src_ref, dst_ref, sem, tm):
    def group(g, c):
        for u in range(SUB):
            row = dst_ref.at[pl.ds(pos_ref[0, 0, g * SUB + u], 1), :]
            pltpu.make_async_copy(src_ref.at[g, pl.ds(u, 1), :], row, sem).start()
        return c

    lax.fori_loop(0, tm // SUB, group, 0)
    _wait_rows(src_ref, sem)


CAST_ROWS = 16


def _scatter_body(fill_ref, na_ref, pos_ref, src_ref, wg_ref, wu_ref, wd_ref, dst_ref, wgb_ref, wub_ref, wdb_ref,
                  zero_ref, sem, *, tm, n_tiles):
    tile_rows = lambda t: dst_ref.at[pl.ds(pl.multiple_of(t * MOE_TM, MOE_TM), MOE_TM), :]

    @pl.when(pl.program_id(0) == 0)
    def _():
        zero_ref[...] = jnp.zeros(zero_ref.shape, F32)
        for b in range(N_BUCKETS):
            pltpu.make_async_copy(zero_ref, tile_rows(fill_ref[b]), sem).start()

        def tail_start(t, c):
            pltpu.make_async_copy(zero_ref, tile_rows(t), sem).start()
            return c

        lax.fori_loop(na_ref[0], n_tiles, tail_start, 0)
        for b in range(N_BUCKETS):
            pltpu.make_async_copy(zero_ref, tile_rows(fill_ref[b]), sem).wait()

        def tail_wait(t, c):
            pltpu.make_async_copy(zero_ref, tile_rows(t), sem).wait()
            return c

        lax.fori_loop(na_ref[0], n_tiles, tail_wait, 0)

    n_up, n_down = D_MODEL // CAST_ROWS, D_EXPERT // CAST_ROWS

    def cast(src, dst, chunk):
        rows = pl.ds(pl.multiple_of(chunk * CAST_ROWS, CAST_ROWS), CAST_ROWS)
        dst[0, rows, :] = src[0, rows, :].astype(BF16)

    def step(it, c):
        for g in (2 * it, 2 * it + 1):
            for u in range(SUB):
                row = dst_ref.at[pl.ds(pos_ref[0, 0, g * SUB + u], 1), :]
                pltpu.make_async_copy(src_ref.at[g, pl.ds(u, 1), :], row, sem).start()
        cast(wg_ref, wgb_ref, jnp.minimum(it, n_up - 1))
        cast(wu_ref, wub_ref, jnp.minimum(it, n_up - 1))
        cast(wd_ref, wdb_ref, jnp.minimum(it, n_down - 1))
        return c

    assert tm % (2 * SUB) == 0 and tm // (2 * SUB) >= n_up
    lax.fori_loop(0, tm // (2 * SUB), step, 0)
    _wait_rows(src_ref, sem)


def _scatter_rows(l, fill_tile, n_act, pos, src, n_tiles, wg, wu, wd):
    nt = N_EXPERTS
    tm = src.shape[0] * SUB // nt
    w_in_spec = lambda r, c: pl.BlockSpec((1, r, c), lambda i, *_: (l * N_EXPERTS + i, 0, 0))
    w_out_spec = lambda r, c: pl.BlockSpec((1, r, c), lambda i, *_: (i, 0, 0))
    grid_spec = pltpu.PrefetchScalarGridSpec(
        num_scalar_prefetch=2,
        grid=(nt,),
        in_specs=[pl.BlockSpec((1, 1, tm), lambda i, *_: (i, 0, 0), memory_space=pltpu.SMEM),
                  pl.BlockSpec((tm // SUB, SUB, ROW_W), lambda i, *_: (i, 0, 0)),
                  w_in_spec(D_MODEL, D_EXPERT), w_in_spec(D_MODEL, D_EXPERT), w_in_spec(D_EXPERT, D_MODEL)],
        out_specs=[pl.BlockSpec(memory_space=pl.ANY),
                   w_out_spec(D_MODEL, D_EXPERT), w_out_spec(D_MODEL, D_EXPERT), w_out_spec(D_EXPERT, D_MODEL)],
        scratch_shapes=[pltpu.VMEM((MOE_TM, ROW_W), F32), pltpu.SemaphoreType.DMA(())],
    )
    return pl.pallas_call(
        functools.partial(_scatter_body, tm=tm, n_tiles=n_tiles),
        grid_spec=grid_spec,
        out_shape=[jax.ShapeDtypeStruct((n_tiles * MOE_TM, ROW_W), F32),
                   jax.ShapeDtypeStruct((N_EXPERTS, D_MODEL, D_EXPERT), BF16),
                   jax.ShapeDtypeStruct((N_EXPERTS, D_MODEL, D_EXPERT), BF16),
                   jax.ShapeDtypeStruct((N_EXPERTS, D_EXPERT, D_MODEL), BF16)],
        compiler_params=_cp(("arbitrary",)),
        name="dispatch_scatter",
    )(fill_tile, n_act, pos, src, wg, wu, wd)


def _scatter_more_body(pos_ref, src_ref, dst_in_ref, dst_ref, sem, *, tm):
    del dst_in_ref
    _scatter_tile_rows(pos_ref, src_ref, dst_ref, sem, tm)


def _scatter_more_rows(pos, src, dst, tm):
    nt = src.shape[0] * SUB // tm
    return pl.pallas_call(
        functools.partial(_scatter_more_body, tm=tm),
        grid=(nt,),
        in_specs=[pl.BlockSpec((1, 1, tm), lambda i: (i, 0, 0), memory_space=pltpu.SMEM),
                  pl.BlockSpec((tm // SUB, SUB, ROW_W), lambda i: (i, 0, 0)),
                  pl.BlockSpec(memory_space=pl.ANY)],
        out_specs=pl.BlockSpec(memory_space=pl.ANY),
        out_shape=jax.ShapeDtypeStruct(dst.shape, dst.dtype),
        scratch_shapes=[pltpu.SemaphoreType.DMA(())],
        input_output_aliases={2: 0},
        compiler_params=_cp(("arbitrary",)),
        name="dispatch_scatter_more",
    )(pos, src, dst)


MOE_STEP_TILES = 2


def _moe_body(ta_ref, tb_ref, na_ref, xs_ref, nf_ref, *refs):
    del ta_ref, tb_ref
    w_refs, y_ref = refs[:-1], refs[-1]
    first_tile = pl.program_id(0) * MOE_STEP_TILES

    @pl.when(first_tile < na_ref[0])
    def _():
        staged = []
        for t in range(MOE_STEP_TILES):
            rows = slice(t * MOE_TM, (t + 1) * MOE_TM)
            x1 = xs_ref[rows, 0:D_MODEL]
            xb = ((x1 * xs_ref[rows, D_MODEL + 2:D_MODEL + 3]) * nf_ref[...]).astype(BF16)
            w = w_refs[6 * t:6 * t + 6]
            staged.append((x1, [(_dot(xb, w[3 * k][0]), _dot(xb, w[3 * k + 1][0])) for k in range(2)]))
        for t in range(MOE_STEP_TILES):
            rows = slice(t * MOE_TM, (t + 1) * MOE_TM)
            y, ups = staged[t]
            for k, (a, u) in enumerate(ups):
                hdn = (jax.nn.silu(a) * u) * xs_ref[rows, D_MODEL + k:D_MODEL + k + 1]
                y = y + _dot(hdn.astype(BF16), w_refs[6 * t + 3 * k + 2][0])
            y_ref[rows, :] = y

    @pl.when(first_tile >= na_ref[0])
    def _():
        y_ref[...] = jnp.zeros(y_ref.shape, F32)


def _moe_experts(l, tile_a, tile_b, n_act, xs, nf, wg, wu, wd):
    step_rows = MOE_STEP_TILES * MOE_TM
    assert xs.shape[0] % step_rows == 0
    last = lambda tile, na: jnp.minimum(tile, na[0] - 1)
    expert = lambda sel, t, i, ta, tb, na: (sel(ta, tb)[last(MOE_STEP_TILES * i + t, na)], 0, 0)
    w_up = lambda sel, t: pl.BlockSpec((1, D_MODEL, D_EXPERT), functools.partial(expert, sel, t))
    w_dn = lambda sel, t: pl.BlockSpec((1, D_EXPERT, D_MODEL), functools.partial(expert, sel, t))
    sa = lambda ta, tb: ta
    sb = lambda ta, tb: tb
    w_specs = []
    for t in range(MOE_STEP_TILES):
        w_specs += [w_up(sa, t), w_up(sa, t), w_dn(sa, t), w_up(sb, t), w_up(sb, t), w_dn(sb, t)]
    grid_spec = pltpu.PrefetchScalarGridSpec(
        num_scalar_prefetch=3,
        grid=(xs.shape[0] // step_rows,),
        in_specs=[pl.BlockSpec((step_rows, ROW_W), lambda i, ta, tb, na: (jnp.minimum(i, (na[0] - 1) // MOE_STEP_TILES), 0)),
                  _layer_spec(l, 1, D_MODEL)] + w_specs,
        out_specs=pl.BlockSpec((step_rows, D_MODEL), lambda i, ta, tb, na: (i, 0)),
    )
    return pl.pallas_call(
        _moe_body,
        grid_spec=grid_spec,
        out_shape=jax.ShapeDtypeStruct((xs.shape[0], D_MODEL), F32),
        compiler_params=_cp(("arbitrary",)),
        name="moe_experts",
    )(tile_a, tile_b, n_act, xs, nf, *((wg, wu, wd) * (2 * MOE_STEP_TILES)))


def _unpermute_body(pos_ref, ys_ref, o_ref, sem, *, tm):
    def group(g, c):
        for u in range(SUB):
            row = ys_ref.at[pl.ds(pos_ref[0, 0, g * SUB + u], 1), :]
            pltpu.make_async_copy(row, o_ref.at[g, pl.ds(u, 1), :], sem).start()
        return c

    lax.fori_loop(0, tm // SUB, group, 0)
    _wait_rows(o_ref, sem)


def _unpermute(pos, ys, tm):
    t = pos.shape[0] * tm
    return pl.pallas_call(
        functools.partial(_unpermute_body, tm=tm),
        grid=(t // tm,),
        in_specs=[pl.BlockSpec((1, 1, tm), lambda i: (i, 0, 0), memory_space=pltpu.SMEM),
                  pl.BlockSpec(memory_space=pl.ANY)],
        out_specs=pl.BlockSpec((tm // SUB, SUB, D_MODEL), lambda i: (i, 0, 0)),
        out_shape=jax.ShapeDtypeStruct((t // SUB, SUB, D_MODEL), F32),
        scratch_shapes=[pltpu.SemaphoreType.DMA(())],
        compiler_params=_cp(("arbitrary",)),
        name="moe_unpermute",
    )(pos, ys)


def _rope_lane_freq():
    half = ROT_DIM // 2
    inv_freq = jnp.float32(ROPE_THETA) ** (-jnp.arange(half, dtype=jnp.float32) * (2.0 / ROT_DIM))
    dim = np.arange(LANES) % HEAD_DIM
    return inv_freq[dim % half][None, :], dim < half, (dim >= half) & (dim < ROT_DIM)


def _rope_patterns(cos, sin, first, second):
    return (jnp.where(first | second, cos, 1.0), jnp.where(first, -sin, 0.0), jnp.where(second, sin, 0.0))


def _rope_tables(pos):
    freq, first, second = _rope_lane_freq()
    ang = pos.astype(jnp.float32)[:, None] * freq
    return _rope_patterns(jnp.cos(ang), jnp.sin(ang), first, second)


def _rope_tables_padded(n_blocks):
    freq, first, second = _rope_lane_freq()
    ang_a = (jnp.arange(n_blocks, dtype=jnp.int32) * BLOCK).astype(jnp.float32)[:, None] * freq
    ang_b = (jnp.arange(BLOCK, dtype=jnp.int32) - PAD).astype(jnp.float32)[:, None] * freq
    ca, sa = jnp.cos(ang_a)[:, None, :], jnp.sin(ang_a)[:, None, :]
    cb, sb = jnp.cos(ang_b)[None], jnp.sin(ang_b)[None]
    flat = lambda t: t.reshape(n_blocks * BLOCK, LANES)
    return _rope_patterns(flat(ca * cb - sa * sb), flat(sa * cb + ca * sb), first, second)


def _seg_ones(n):
    idx = np.arange(n) // HEAD_DIM
    return jnp.asarray(idx[:, None] == idx[None, :], BF16)


def _bucket_experts():
    ea, eb = [], []
    for g in range(N_GROUPS):
        for a in range(EXPERTS_PER_GROUP):
            for b in range(a + 1, EXPERTS_PER_GROUP):
                ea.append(g * EXPERTS_PER_GROUP + a)
                eb.append(g * EXPERTS_PER_GROUP + b)
    return np.asarray(ea, np.int32), np.asarray(eb, np.int32)


def _dispatch_plan(counts, n_tiles):
    padded = ((counts + MOE_TM - 1) // MOE_TM) * MOE_TM
    ends = jnp.cumsum(padded)
    offs = ends - padded
    n_act = jnp.maximum(ends[-1] // MOE_TM, 1)
    starts = jnp.arange(n_tiles, dtype=jnp.int32) * MOE_TM
    tile_bucket = jnp.minimum(jnp.sum(starts[:, None] >= ends[None, :], axis=1), N_BUCKETS - 1)
    ea, eb = _bucket_experts()
    onehot = tile_bucket[:, None] == jnp.arange(N_BUCKETS)[None, :]
    tile_a = jnp.sum(jnp.where(onehot, ea[None, :], 0), axis=1).astype(jnp.int32)
    tile_b = jnp.sum(jnp.where(onehot, eb[None, :], 0), axis=1).astype(jnp.int32)
    fill_tile = jnp.maximum(ends // MOE_TM - 1, 0).astype(jnp.int32)
    return offs, tile_a, tile_b, n_act.astype(jnp.int32).reshape(1), fill_tile


def _positions(offs, bucket, rank):
    onehot = bucket[..., None] == jnp.arange(N_BUCKETS, dtype=jnp.int32)
    return (jnp.sum(jnp.where(onehot, offs.astype(jnp.int32), 0), axis=-1) + rank).astype(jnp.int32)


def kernel(x_prompt, x_sample, cache_k, cache_v, state_conv, meta_tokens, norm_mix, w_in, conv_w, q_norm, k_norm,
           attn_sinks, w_conv_out, w_attn_out, w_o, norm_ffn, w_router_group, b_router_group, w_router_expert,
           b_router_expert, w_exp_gate, w_exp_up, w_exp_down):
    batch, seq, _ = x_prompt.shape
    depth = w_in.shape[0]
    n_dec = x_sample.shape[0]
    past_len = PAST_LEN
    lp = PAD + N_META + seq
    tm_in, tm_out, qb = 640, 640, 5
    tm_move, tm_last = 3328, 4096
    assert PAD + N_META == BLOCK and seq % BLOCK == 0 and tm_in % BLOCK == 0
    assert lp % tm_in == 0 and (batch * lp) % tm_out == 0 and lp % (qb * BLOCK) == 0
    assert (batch * lp) % tm_move == 0 and (batch * seq) % tm_last == 0
    assert x_sample.shape[1] == 1 and cache_k.shape[2] == WINDOW and past_len >= WINDOW and w_in.shape[-1] == D_IN

    t_prompt = batch * lp
    t_all = t_prompt + n_dec
    n_tiles = -(-(t_all + N_BUCKETS * (MOE_TM - 1)) // MOE_TM)
    n_tiles = -(-n_tiles // MOE_STEP_TILES) * MOE_STEP_TILES

    meta = jnp.broadcast_to(meta_tokens[None].astype(F32), (batch, N_META, D_MODEL))
    head = jnp.concatenate([jnp.zeros((batch, PAD, D_MODEL), F32), meta], axis=1)
    xs = x_sample.reshape(n_dec, D_MODEL)

    rope_p = _rope_tables_padded(lp // BLOCK)
    rope_s = _rope_tables(jnp.full((1,), past_len, jnp.int32))
    s256, s128 = _seg_ones(256), _seg_ones(LANES)
    tri_p = jnp.asarray(np.triu(np.ones((tm_out, tm_out)), 1), BF16)
    tri_s = jnp.asarray(np.triu(np.ones((n_dec, n_dec)), 1), BF16)
    zero_cnt = jnp.zeros((LANES, LANES), F32)

    w_mix = w_in.astype(BF16)
    w_gate = w_in[:, :, D_MIX:].astype(BF16)
    wco, wao, wo = w_conv_out.astype(BF16), w_attn_out.astype(BF16), w_o.astype(BF16)
    wg32 = w_exp_gate.reshape(depth * N_EXPERTS, D_MODEL, D_EXPERT)
    wu32 = w_exp_up.reshape(depth * N_EXPERTS, D_MODEL, D_EXPERT)
    wd32 = w_exp_down.reshape(depth * N_EXPERTS, D_EXPERT, D_MODEL)
    nm, nf = norm_mix.reshape(depth, 1, D_MODEL), norm_ffn.reshape(depth, 1, D_MODEL)
    qn = (jnp.tile(q_norm, (1, N_HEADS)) * Q_SCALE).reshape(depth, 1, D_Q)
    kn = jnp.tile(k_norm, (1, N_KV_HEADS)).reshape(depth, 1, D_KV)
    r_pad = LANES - N_GROUPS - N_EXPERTS
    wr = jnp.concatenate([w_router_group, w_router_expert, jnp.zeros((depth, D_MODEL, r_pad), F32)], axis=-1).astype(BF16)
    br = jnp.concatenate([b_router_group, b_router_expert, jnp.zeros((depth, r_pad), F32)], axis=-1).reshape(depth, 1, LANES)
    sinks = attn_sinks.astype(F32) * LOG2E
    sinkb = jnp.broadcast_to(sinks[:, :, None], (depth, N_HEADS, LANES))
    ck = cache_k.reshape(depth, n_dec, WINDOW, D_KV)
    cv = cache_v.reshape(depth, n_dec, WINDOW, D_KV)

    outs = {k: [] for k in ("kp", "vp", "cp", "ks", "vs", "cs")}
    for l in range(depth):
        if l == 0:
            cy, q, k, v, ulast, xp = _prompt_in(l, x_prompt, nm, w_mix, conv_w, qn, kn, rope_p, s256, s128, tm_in, head)
        else:
            cy, q, k, v, ulast = _prompt_in(l, xp, nm, w_mix, conv_w, qn, kn, rope_p, s256, s128, tm_in)
        ao = _prompt_attn(l, q, k, v, sinks, qb)
        xep, rankp, bktp, cnt = _mix_out(
            l, xp.reshape(t_prompt, D_MODEL), cy.reshape(t_prompt, D_CONV), ao.reshape(t_prompt, D_Q),
            nm, w_gate, wco, wao, wo, nf, wr, br, tri_p, zero_cnt, tm_out)
        outs["kp"].append(k[:, lp - WINDOW:].reshape(batch, WINDOW, N_KV_HEADS, HEAD_DIM))
        outs["vp"].append(v[:, lp - WINDOW:].reshape(batch, WINDOW, N_KV_HEADS, HEAD_DIM))
        outs["cp"].append(ulast[:, 8 - (CONV_W - 1):])

        c0, c1 = state_conv[l, :, 0, :], state_conv[l, :, 1, :]
        cys, qx, ksn, vsn, us = _sample_in(l, xs, c0, c1, nm, w_mix, conv_w, qn, kn, rope_s, s256, s128)
        ox, nk, nv = _sample_attn(l, jnp.transpose(qx, (1, 0, 2)), sinkb, ck, cv, ksn, vsn)
        ox = ox.reshape(n_dec, N_KV_HEADS, GQA, N_KV_HEADS, HEAD_DIM)
        aos = jnp.stack([ox[:, j, :, j, :] for j in range(N_KV_HEADS)], axis=1).reshape(n_dec, D_Q).astype(BF16)
        xes, ranks, bkts, cnt = _mix_out(l, xs, cys, aos, nm, w_gate, wco, wao, wo, nf, wr, br, tri_s, cnt, n_dec)
        outs["ks"].append(nk.reshape(n_dec, WINDOW, N_KV_HEADS, HEAD_DIM))
        outs["vs"].append(nv.reshape(n_dec, WINDOW, N_KV_HEADS, HEAD_DIM))
        outs["cs"].append(jnp.stack([c1, us], axis=1))

        counts = cnt[:N_BUCKETS, 0].astype(jnp.int32)
        offs, tile_a, tile_b, n_act, fill_tile = _dispatch_plan(counts, n_tiles)
        posp = _positions(offs, bktp, rankp)
        poss = _positions(offs, bkts, ranks)
        by8 = lambda a: a.reshape(a.shape[0] // SUB, SUB, a.shape[1])
        sorted_rows, wg, wu, wd = _scatter_rows(l, fill_tile, n_act, posp.reshape(N_EXPERTS, 1, t_prompt // N_EXPERTS),
                                                by8(xep), n_tiles, wg32, wu32, wd32)
        posp = posp.reshape(t_prompt // tm_move, 1, tm_move)
        sorted_rows = _scatter_more_rows(poss, by8(xes), sorted_rows, n_dec)
        ys = _moe_experts(l, tile_a, tile_b, n_act, sorted_rows, nf, wg, wu, wd)
        xs = _unpermute(poss, ys, n_dec).reshape(n_dec, D_MODEL)
        if l + 1 < depth:
            xp = _unpermute(posp, ys, tm_move).reshape(batch, lp, D_MODEL)
        else:
            pos_tok = posp.reshape(batch, lp)[:, PAD + N_META:].reshape(batch * seq // tm_last, 1, tm_last)
            y_prompt = _unpermute(pos_tok, ys, tm_last).reshape(batch, seq, D_MODEL)

    y_sample = xs.reshape(n_dec, 1, D_MODEL)
    st = lambda k: jnp.stack(outs[k])
    return (y_prompt, y_sample, st("kp"), st("vp"), st("cp"), st("ks"), st("vs"), st("cs"))
```

```python
import functools
import math

import jax
import jax.numpy as jnp
import numpy as np
from jax import lax
from jax.experimental import pallas as pl
from jax.experimental.pallas import tpu as pltpu

D_MODEL = 1024
N_META = 16
D_CONV = D_MODEL
CONV_W = 3
N_HEADS = 16
N_KV_HEADS = 2
HEAD_DIM = 64
GQA = N_HEADS // N_KV_HEADS
ROT_DIM = HEAD_DIM // 4
ROPE_THETA = 500000.0
WINDOW = 128
PAST_LEN = 8192
BLOCK = 128
N_GROUPS = 4
EXPERTS_PER_GROUP = 4
N_EXPERTS = N_GROUPS * EXPERTS_PER_GROUP
D_EXPERT = 512
EPS = 1e-6
NEG = -1e30
D_Q = N_HEADS * HEAD_DIM
D_KV = N_KV_HEADS * HEAD_DIM
C_B, C_C, C_HC = 0, D_CONV, 2 * D_CONV
C_Q = 3 * D_CONV
C_K = C_Q + D_Q
C_V = C_K + D_KV
C_G = C_V + D_KV
D_MIX = C_G
D_IN = C_G + 2 * D_MODEL

LANES = 128
SUBLANES = 8
PAD = (-N_META) % BLOCK
N_PAIRS = 6
N_BUCKETS = N_GROUPS * N_PAIRS
MOE_TM = 256
ROW_W = D_MODEL + LANES
LOG2E = math.log2(math.e)
Q_SCALE = HEAD_DIM ** -0.5 * LOG2E

F32 = jnp.float32
BF16 = jnp.bfloat16
VMEM_LIMIT = 56 * 1024 * 1024


def _cp(sem, vmem=VMEM_LIMIT):
    return pltpu.CompilerParams(dimension_semantics=sem, vmem_limit_bytes=vmem)


def _const_spec(shape):
    nd = len(shape)
    return pl.BlockSpec(shape, lambda *_: (0,) * nd, pipeline_mode=pl.Buffered(1))


def _layer_spec(l, *shape):
    n = len(shape)
    return pl.BlockSpec((None,) + shape, lambda *_: (l,) + (0,) * n, pipeline_mode=pl.Buffered(1))


def _w_mix_spec(l):
    return pl.BlockSpec((None, D_MODEL, D_MIX), lambda *_: (l, 0, 0), pipeline_mode=pl.Buffered(1))


def _dot(a, b):
    return jnp.dot(a, b, preferred_element_type=F32)


def _seg_mean_sq(x, seg_ones):
    return _dot((x * x).astype(BF16), seg_ones) * (1.0 / HEAD_DIM)


def _rope128(t, cos, sin_pm, first):
    partner = jnp.where(first, pltpu.roll(t, LANES - ROT_DIM // 2, 1), pltpu.roll(t, ROT_DIM // 2, 1))
    return t * cos + partner * sin_pm


def _rms_scale(x):
    return lax.rsqrt(jnp.mean(x * x, axis=-1, keepdims=True) + EPS)


def _rms_rows(x, g):
    return (x * _rms_scale(x)) * g


def _qk_project(hb, w_ref, s256, s128):
    qs = [_dot(hb, w_ref[:, C_Q + c * 256:C_Q + (c + 1) * 256]) for c in range(D_Q // 256)]
    kv = _dot(hb, w_ref[:, C_K:C_K + 2 * D_KV])
    kc, v = kv[:, 0:D_KV], kv[:, D_KV:]
    return qs, [_seg_mean_sq(qc, s256) for qc in qs], kc, _seg_mean_sq(kc, s128), v


def _qk_finish(proj, qn, kn, cos, sneg, spos, store_q):
    qs, q_ms, kc, k_ms, _ = proj
    sin_pm = sneg + spos
    lane = lax.broadcasted_iota(jnp.int32, cos.shape, 1)
    first = lax.bitwise_and(lane, HEAD_DIM - 1) < ROT_DIM // 2
    for c, (qc, ms) in enumerate(zip(qs, q_ms)):
        qc = (qc * lax.rsqrt(ms + EPS)) * qn[:, c * 256:(c + 1) * 256]
        for s in range(2):
            r = _rope128(qc[:, s * LANES:(s + 1) * LANES], cos, sin_pm, first)
            store_q(2 * c + s, r.astype(BF16))
    kc = (kc * lax.rsqrt(k_ms + EPS)) * kn
    return _rope128(kc, cos, sin_pm, first)


def _in_body(x_ref, nm_ref, w_ref, cw_ref, qn_ref, kn_ref, cos_ref, sneg_ref, spos_ref,
             s256_ref, s128_ref, cy_ref, q_ref, k_ref, v_ref, ul_ref, us_ref, *, tm, parts):
    i = pl.program_id(1)
    th = tm // parts

    @pl.when(i == 0)
    def _():
        us_ref[0:8, :] = jnp.zeros((8, D_CONV), F32)

    projs = []
    for h in range(parts):
        r0 = h * th
        hb = _rms_rows(x_ref[0, r0:r0 + th, :], nm_ref[...]).astype(BF16)
        projs.append(_qk_project(hb, w_ref, s256_ref[...], s128_ref[...]))
        u = _dot(hb, w_ref[:, C_C:C_C + D_CONV]) * _dot(hb, w_ref[:, C_HC:C_HC + D_CONV])
        row = lax.broadcasted_iota(jnp.int32, (th, 1), 0) + (i * tm + r0)
        u = jnp.where(row >= PAD, u, 0.0)
        us_ref[8 + r0:8 + r0 + th, :] = u
        conv = (us_ref[6 + r0:6 + r0 + th, :] * cw_ref[0:1, :] + us_ref[7 + r0:7 + r0 + th, :] * cw_ref[1:2, :]) \
            + u * cw_ref[2:3, :]
        cy_ref[0, r0:r0 + th, :] = (_dot(hb, w_ref[:, C_B:C_B + D_CONV]) * conv).astype(BF16)
        v_ref[0, r0:r0 + th, :] = projs[h][4]

    last = us_ref[tm:tm + 8, :]
    ul_ref[0] = last
    us_ref[0:8, :] = last

    for h in range(parts):
        rows = slice(h * th, (h + 1) * th)

        def store_q(slab, val, rows=rows):
            q_ref[0, rows, slab * LANES:(slab + 1) * LANES] = val

        k_ref[0, rows, :] = _qk_finish(projs[h], qn_ref[...], kn_ref[...], cos_ref[rows, :], sneg_ref[rows, :],
                                       spos_ref[rows, :], store_q)


def _in_first_body(head_ref, *refs, tm, parts):
    nb = tm // BLOCK
    blocks, rest, xpad_ref, us_ref = refs[:nb], refs[nb:-2], refs[-2], refs[-1]
    first = pl.program_id(1) == 0
    xpad_ref[0, 0:BLOCK, :] = jnp.where(first, head_ref[0], blocks[0][0])
    for j in range(1, nb):
        xpad_ref[0, j * BLOCK:(j + 1) * BLOCK, :] = blocks[j][0]
    _in_body(xpad_ref, *rest, us_ref, tm=tm, parts=parts)


def _prompt_in(l, x, nm, w_mix, cw, qn, kn, rope, s256, s128, tm, head=None):
    b = x.shape[0]
    lp = x.shape[1] if head is None else x.shape[1] + BLOCK
    nt = lp // tm
    nb = tm // BLOCK
    cos, sneg, spos = rope
    tok = lambda w: pl.BlockSpec((1, tm, w), lambda bi, i: (bi, i, 0))
    tab = pl.BlockSpec((tm, LANES), lambda bi, i: (i, 0))
    params = [_layer_spec(l, 1, D_MODEL), _w_mix_spec(l),
              _layer_spec(l, CONV_W, D_CONV), _layer_spec(l, 1, D_Q), _layer_spec(l, 1, D_KV),
              tab, tab, tab, _const_spec((256, 256)), _const_spec((LANES, LANES))]
    out_specs = [tok(D_CONV), tok(D_Q), tok(D_KV), tok(D_KV), pl.BlockSpec((1, 8, D_CONV), lambda bi, i: (bi, 0, 0))]
    out_shape = [jax.ShapeDtypeStruct((b, lp, D_CONV), BF16), jax.ShapeDtypeStruct((b, lp, D_Q), BF16),
                 jax.ShapeDtypeStruct((b, lp, D_KV), F32), jax.ShapeDtypeStruct((b, lp, D_KV), F32),
                 jax.ShapeDtypeStruct((b, 8, D_CONV), F32)]
    if head is None:
        body, x_specs, x_args = _in_body, [tok(D_MODEL)], (x,)
    else:
        blk = lambda j: pl.BlockSpec((1, BLOCK, D_MODEL), lambda bi, i: (bi, jnp.maximum(nb * i - 1 + j, 0), 0))
        body = _in_first_body
        x_specs = [pl.BlockSpec((1, BLOCK, D_MODEL), lambda bi, i: (bi, 0, 0))] + [blk(j) for j in range(nb)]
        x_args = (head,) + (x,) * nb
        out_specs.append(tok(D_MODEL))
        out_shape.append(jax.ShapeDtypeStruct((b, lp, D_MODEL), F32))
    return pl.pallas_call(
        functools.partial(body, tm=tm, parts=2),
        grid=(b, nt),
        in_specs=x_specs + params,
        out_specs=out_specs,
        out_shape=out_shape,
        scratch_shapes=[pltpu.VMEM((tm + 8, D_CONV), F32)],
        compiler_params=_cp(("arbitrary", "arbitrary")),
        name="prompt_in",
    )(*x_args, nm, w_mix, cw, qn, kn, cos, sneg, spos, s256, s128)


def _sin_body(x_ref, c0_ref, c1_ref, nm_ref, w_ref, cw_ref, qn_ref, kn_ref, cos_ref, sneg_ref,
              spos_ref, s256_ref, s128_ref, cy_ref, qx_ref, k_ref, v_ref, u_ref):
    hb = _rms_rows(x_ref[...], nm_ref[...]).astype(BF16)
    u = _dot(hb, w_ref[:, C_C:C_C + D_CONV]) * _dot(hb, w_ref[:, C_HC:C_HC + D_CONV])
    u_ref[...] = u
    conv = (c0_ref[...] * cw_ref[0:1, :] + c1_ref[...] * cw_ref[1:2, :]) + u * cw_ref[2:3, :]
    cy_ref[...] = (_dot(hb, w_ref[:, C_B:C_B + D_CONV]) * conv).astype(BF16)

    lane = lax.broadcasted_iota(jnp.int32, (x_ref.shape[0], LANES), 1)
    low = lane < HEAD_DIM

    def store_q(slab, val):
        valf = val.astype(F32)
        swapped = pltpu.roll(valf, HEAD_DIM, 1)
        zero = jnp.zeros_like(valf)
        for h in (2 * slab, 2 * slab + 1):
            src = valf if (h % 2) == (h // GQA) else swapped
            keep = low if (h // GQA) == 0 else jnp.logical_not(low)
            qx_ref[h] = jnp.where(keep, src, zero).astype(BF16)

    cos = jnp.broadcast_to(cos_ref[...], (x_ref.shape[0], LANES))
    sneg = jnp.broadcast_to(sneg_ref[...], (x_ref.shape[0], LANES))
    spos = jnp.broadcast_to(spos_ref[...], (x_ref.shape[0], LANES))
    proj = _qk_project(hb, w_ref, s256_ref[...], s128_ref[...])
    k_ref[...] = _qk_finish(proj, qn_ref[...], kn_ref[...], cos, sneg, spos, store_q)
    v_ref[...] = proj[4]


def _sample_in(l, x, c0, c1, nm, w_mix, cw, qn, kn, rope, s256, s128):
    n = x.shape[0]
    cos, sneg, spos = rope
    full = lambda *s: pl.BlockSpec(s, lambda i: (0,) * len(s))
    return pl.pallas_call(
        _sin_body,
        grid=(1,),
        in_specs=[full(n, D_MODEL), full(n, D_CONV), full(n, D_CONV), _layer_spec(l, 1, D_MODEL),
                  _w_mix_spec(l), _layer_spec(l, CONV_W, D_CONV), _layer_spec(l, 1, D_Q),
                  _layer_spec(l, 1, D_KV), full(1, LANES), full(1, LANES), full(1, LANES), full(256, 256),
                  full(LANES, LANES)],
        out_specs=[full(n, D_CONV), full(N_HEADS, n, LANES), full(n, D_KV), full(n, D_KV), full(n, D_CONV)],
        out_shape=[jax.ShapeDtypeStruct((n, D_CONV), BF16), jax.ShapeDtypeStruct((N_HEADS, n, LANES), BF16),
                   jax.ShapeDtypeStruct((n, D_KV), F32), jax.ShapeDtypeStruct((n, D_KV), F32),
                   jax.ShapeDtypeStruct((n, D_CONV), F32)],
        compiler_params=_cp(("arbitrary",)),
        name="sample_in",
    )(x, c0, c1, nm, w_mix, cw, qn, kn, cos, sneg, spos, s256, s128)


def _attn_body(sink_ref, q_ref, kp_ref, kc_ref, vp_ref, vc_ref, o_ref, ke_ref, ko_ref, vt_ref, *, qb, l):
    i = pl.program_id(1)
    lane = lax.broadcasted_iota(jnp.int32, (BLOCK, LANES), 1)
    low = lane < HEAD_DIM

    def prep_k(src, blk0, nblk):
        for t in range(nblk):
            blk = src[0, t * BLOCK:(t + 1) * BLOCK, :]
            swp = pltpu.roll(blk, HEAD_DIM, 1)
            zero = jnp.zeros_like(blk)
            rows = slice((blk0 + t) * BLOCK, (blk0 + t + 1) * BLOCK)
            ke_ref[0, rows, :] = jnp.where(low, blk, zero).astype(BF16)
            ko_ref[0, rows, :] = jnp.where(low, zero, swp).astype(BF16)
            ke_ref[1, rows, :] = jnp.where(low, swp, zero).astype(BF16)
            ko_ref[1, rows, :] = jnp.where(low, zero, blk).astype(BF16)

    def prep_v(src, blk0, nblk):
        for t in range(nblk):
            vt = jnp.transpose(src[0, t * BLOCK:(t + 1) * BLOCK, :]).astype(BF16)
            for j in range(N_KV_HEADS):
                vt_ref[blk0 + t, j] = vt[j * HEAD_DIM:(j + 1) * HEAD_DIM, :]

    prep_k(kp_ref, 0, 1)
    prep_k(kc_ref, 1, qb)
    prep_v(vp_ref, 0, 1)
    prep_v(vc_ref, 1, qb)

    c = lax.broadcasted_iota(jnp.int32, (2 * BLOCK, BLOCK), 0)
    r = lax.broadcasted_iota(jnp.int32, (2 * BLOCK, BLOCK), 1)
    diff = r - (c - BLOCK)
    band = (diff >= 0) & (diff < WINDOW)
    nt = (((1,), (1,)), ((), ()))

    def one_block(b, carry):
        r0 = pl.multiple_of(b * BLOCK, BLOCK)
        kpos = (i * qb + b) * BLOCK + c - BLOCK - PAD
        bias = jnp.where(band & (kpos >= 0), 0.0, NEG)
        for m in range(N_HEADS // 2):
            j = (2 * m) // GQA
            q2 = q_ref[0, pl.ds(r0, BLOCK), m * LANES:(m + 1) * LANES]
            halves = []
            for par, k_ref in ((0, ke_ref), (1, ko_ref)):
                st = lax.dot_general(k_ref[j, pl.ds(r0, 2 * BLOCK), :], q2, nt, preferred_element_type=F32)
                st = st + bias
                sink = sink_ref[l, 2 * m + par]
                mx = jnp.maximum(jnp.max(st, axis=0, keepdims=True), sink)
                p = jnp.exp2(st - mx)
                den = jnp.sum(p, axis=0, keepdims=True) + jnp.exp2(sink - mx)
                pb = p.astype(BF16)
                ot = _dot(vt_ref[b, j], pb[0:BLOCK]) + _dot(vt_ref[b + 1, j], pb[BLOCK:])
                halves.append(ot * (1.0 / den))
            o2 = jnp.transpose(jnp.concatenate(halves, axis=0))
            o_ref[0, pl.ds(r0, BLOCK), m * LANES:(m + 1) * LANES] = o2.astype(BF16)
        return carry

    lax.fori_loop(0, qb, one_block, 0, unroll=True)


def _prompt_attn(l, q, k, v, sinks, qb):
    b, lp, _ = q.shape
    nsteps = lp // (qb * BLOCK)
    cur = lambda w: pl.BlockSpec((1, qb * BLOCK, w), lambda bi, i: (bi, i, 0))
    prev = pl.BlockSpec((1, BLOCK, D_KV), lambda bi, i: (bi, jnp.maximum(i * qb - 1, 0), 0))
    ext = ((qb + 1) * BLOCK, LANES)
    return pl.pallas_call(
        functools.partial(_attn_body, qb=qb, l=l),
        grid=(b, nsteps),
        in_specs=[pl.BlockSpec(memory_space=pltpu.SMEM), cur(D_Q), prev, cur(D_KV), prev, cur(D_KV)],
        out_specs=cur(D_Q),
        out_shape=jax.ShapeDtypeStruct((b, lp, D_Q), BF16),
        scratch_shapes=[pltpu.VMEM((N_KV_HEADS,) + ext, BF16), pltpu.VMEM((N_KV_HEADS,) + ext, BF16),
                        pltpu.VMEM((qb + 1, N_KV_HEADS, HEAD_DIM, BLOCK), BF16)],
        compiler_params=_cp(("arbitrary", "arbitrary")),
        name="prompt_attn",
    )(sinks, q, k, k, v, v)


def _sattn_body(qx_ref, sink_ref, ck_ref, cv_ref, kn_ref, vn_ref, ox_ref, nk_ref, nv_ref, *, tb):
    def window(c_ref, n_ref, t):
        return jnp.concatenate([c_ref[t, 1:WINDOW, :], n_ref[t:t + 1, :]], axis=0)

    for t in range(tb):
        nk_ref[t] = window(ck_ref, kn_ref, t)
        nv_ref[t] = window(cv_ref, vn_ref, t)
    nt = (((1,), (1,)), ((), ()))
    s = jnp.concatenate([lax.dot_general(qx_ref[t], window(ck_ref, kn_ref, t).astype(BF16), nt,
                                         preferred_element_type=F32) for t in range(tb)], axis=0)
    sink = jnp.concatenate([sink_ref[...][:, 0:1]] * tb, axis=0)
    m = jnp.maximum(jnp.max(s, axis=-1, keepdims=True), sink)
    p = jnp.exp2(s - m)
    rden = 1.0 / (jnp.sum(p, axis=-1, keepdims=True) + jnp.exp2(sink - m))
    pb = p.astype(BF16)
    for t in range(tb):
        rows = slice(t * N_HEADS, (t + 1) * N_HEADS)
        ox_ref[t] = _dot(pb[rows], window(cv_ref, vn_ref, t).astype(BF16)) * rden[rows]


def _sample_attn(l, qx, sinkb, ck, cv, kn, vn, tb=16):
    n = qx.shape[0]
    blk3 = lambda a, c: pl.BlockSpec((tb, a, c), lambda i: (i, 0, 0))
    cache = pl.BlockSpec((None, tb, WINDOW, D_KV), lambda i: (l, i, 0, 0))
    row = pl.BlockSpec((tb, D_KV), lambda i: (i, 0))
    return pl.pallas_call(
        functools.partial(_sattn_body, tb=tb),
        grid=(n // tb,),
        in_specs=[blk3(N_HEADS, LANES), _layer_spec(l, N_HEADS, LANES), cache, cache, row, row],
        out_specs=[blk3(N_HEADS, LANES), blk3(WINDOW, D_KV), blk3(WINDOW, D_KV)],
        out_shape=[jax.ShapeDtypeStruct((n, N_HEADS, LANES), F32),
                   jax.ShapeDtypeStruct((n, WINDOW, D_KV), F32), jax.ShapeDtypeStruct((n, WINDOW, D_KV), F32)],
        compiler_params=_cp(("arbitrary",)),
        name="sample_attn",
    )(qx, sinkb, ck, cv, kn, vn)


ROUTE_ROWS = 24
RINV_LANE = LANES - 1


def _route(lt):
    top = lt[0:ROUTE_ROWS, :]
    rows = top.shape[1]
    rowf = lax.broadcasted_iota(jnp.int32, top.shape, 0).astype(F32)
    big = jnp.float32(3e38)
    far = jnp.float32(LANES)
    cmax = lambda a: jnp.max(a, axis=0, keepdims=True)
    cmin = lambda a: jnp.min(a, axis=0, keepdims=True)

    gmask = rowf < N_GROUPS
    gl = jnp.where(gmask, top, -big)
    gmax = cmax(gl)
    grp = cmin(jnp.where(gmask & (gl == gmax), rowf, far))
    p_grp = 1.0 / jnp.sum(jnp.where(gmask, jnp.exp(gl - gmax), 0.0), axis=0, keepdims=True)

    e_lo = N_GROUPS + EXPERTS_PER_GROUP * grp
    emask = (rowf >= e_lo) & (rowf < e_lo + EXPERTS_PER_GROUP)
    el = jnp.where(emask, top, -big)
    v1 = cmax(el)
    i1 = cmin(jnp.where(emask & (el == v1), rowf, far))
    rest = emask & (rowf != i1)
    el2 = jnp.where(rest, top, -big)
    v2 = cmax(el2)
    i2 = cmin(jnp.where(rest & (el2 == v2), rowf, far))
    e = jnp.exp(v2 - v1)
    w1 = (1.0 / (1.0 + e)) * p_grp
    w2 = (e / (1.0 + e)) * p_grp
    first_low = i1 < i2
    ea = jnp.where(first_low, i1, i2) - e_lo
    eb = jnp.where(first_low, i2, i1) - e_lo
    w_a = jnp.where(first_low, w1, w2)
    w_b = jnp.where(first_low, w2, w1)
    pair = jnp.where(ea == 0.0, 0.0, jnp.where(ea == 1.0, 3.0, 5.0)) + (eb - ea - 1.0)
    bucket = grp * N_PAIRS + pair

    r8 = lax.broadcasted_iota(jnp.int32, (SUBLANES, rows), 0)
    head = jnp.where(r8 == 0, w_a, jnp.where(r8 == 1, w_b, jnp.where(r8 == 2, lt[RINV_LANE:RINV_LANE + 1, :], 0.0)))
    return bucket, jnp.concatenate([head, jnp.zeros((LANES - SUBLANES, rows), F32)], axis=0)


def _out_body(x_ref, cy_ref, ao_ref, nm_ref, wg_ref, wco_ref, wao_ref, wo_ref, nf_ref, wr_ref, br_ref,
              tri_ref, cin_ref, xe_ref, rank_ref, bkt_ref, cnt_ref, run_ref, *, tm, parts):
    i = pl.program_id(0)
    th = tm // parts
    stage = []
    for h in range(parts):
        rows = slice(h * th, (h + 1) * th)
        hb = _rms_rows(x_ref[rows, :], nm_ref[...]).astype(BF16)
        stage.append((_dot(cy_ref[rows, :], wco_ref[...]), _dot(ao_ref[rows, :], wao_ref[...]),
                      _dot(hb, wg_ref[:, 0:D_MODEL]), _dot(hb, wg_ref[:, D_MODEL:])))
    for h, (ya, yb, gc, ga) in enumerate(stage):
        rows = slice(h * th, (h + 1) * th)
        mix = jax.nn.sigmoid(gc) * ya + jax.nn.sigmoid(ga) * yb
        xe_ref[rows, 0:D_MODEL] = x_ref[rows, :] + _dot(mix.astype(BF16), wo_ref[...])
    x1 = xe_ref[:, 0:D_MODEL]

    rinv = _rms_scale(x1)
    xnb = ((x1 * rinv) * nf_ref[...]).astype(BF16)
    logits = _dot(xnb, wr_ref[...]) + br_ref[...]
    lane = lax.broadcasted_iota(jnp.int32, (tm, LANES), 1)
    bucket, meta_t = _route(jnp.transpose(jnp.where(lane == RINV_LANE, rinv, logits)))
    xe_ref[:, D_MODEL:] = jnp.transpose(meta_t)

    @pl.when(i == 0)
    def _():
        run_ref[...] = cin_ref[...]

    sub = lax.broadcasted_iota(jnp.int32, (LANES, tm), 0).astype(F32)
    oht = (sub == bucket).astype(F32)
    before = _dot(oht.astype(BF16), tri_ref[...]) + run_ref[:, 0:1]
    rank_ref[0] = jnp.sum(oht * before, axis=0, keepdims=True).astype(jnp.int32)
    bkt_ref[0] = bucket.astype(jnp.int32)
    run_ref[...] = run_ref[...] + jnp.sum(oht, axis=-1, keepdims=True)
    cnt_ref[...] = run_ref[...]


def _mix_out(l, x, cy, ao, nm, w_gate, wco, wao, wo, nf, wr, br, tri, cnt_in, tm):
    t = x.shape[0]
    nt = t // tm
    tok = lambda w: pl.BlockSpec((tm, w), lambda i: (i, 0))
    rowi = pl.BlockSpec((1, 1, tm), lambda i: (i, 0, 0))
    sq = (D_MODEL, D_MODEL)
    return pl.pallas_call(
        functools.partial(_out_body, tm=tm, parts=2 if tm >= 4 * LANES else 1),
        grid=(nt,),
        in_specs=[tok(D_MODEL), tok(D_CONV), tok(D_Q), _layer_spec(l, 1, D_MODEL),
                  _layer_spec(l, D_MODEL, 2 * D_MODEL), _layer_spec(l, *sq), _layer_spec(l, *sq), _layer_spec(l, *sq),
                  _layer_spec(l, 1, D_MODEL), _layer_spec(l, D_MODEL, LANES), _layer_spec(l, 1, LANES),
                  _const_spec((tm, tm)), _const_spec((LANES, LANES))],
        out_specs=[tok(ROW_W), rowi, rowi, pl.BlockSpec((LANES, LANES), lambda i: (0, 0))],
        out_shape=[jax.ShapeDtypeStruct((t, ROW_W), F32),
                   jax.ShapeDtypeStruct((nt, 1, tm), jnp.int32), jax.ShapeDtypeStruct((nt, 1, tm), jnp.int32),
                   jax.ShapeDtypeStruct((LANES, LANES), F32)],
        scratch_shapes=[pltpu.VMEM((LANES, LANES), F32)],
        compiler_params=_cp(("arbitrary",)),
        name="mix_out",
    )(x, cy, ao, nm, w_gate, wco, wao, wo, nf, wr, br, tri, cnt_in)


SUB = SUBLANES


def _wait_rows(block_ref, sem):
    pltpu.make_async_copy(block_ref, block_ref, sem).wait()


def _scatter_tile_rows(pos_ref, src_ref, dst_ref, sem, tm):
    def group(g, c):
        for u in range(SUB):
            row = dst_ref.at[pl.ds(pos_ref[0, 0, g * SUB + u], 1), :]
            pltpu.make_async_copy(src_ref.at[g, pl.ds(u, 1), :], row, sem).start()
        return c

    lax.fori_loop(0, tm // SUB, group, 0)
    _wait_rows(src_ref, sem)


CAST_ROWS = 16


STAGE_SLOTS = 3


def _scatter_body(fill_ref, na_ref, pos_ref, src_ref, wg_ref, wu_ref, wd_ref, dst_ref, wgb_ref, wub_ref, wdb_ref,
                  zero_ref, stage_ref, sem, in_sems, row_sems, *, tm, n_tiles):
    step_id = pl.program_id(0)
    n_steps = pl.num_programs(0)
    groups = tm // SUB
    slot = lax.rem(step_id, STAGE_SLOTS)
    prev_slot = lax.rem(step_id + STAGE_SLOTS - 1, STAGE_SLOTS)
    tile_rows = lambda t: dst_ref.at[pl.ds(pl.multiple_of(t * MOE_TM, MOE_TM), MOE_TM), :]

    def fetch(step, into):
        return pltpu.make_async_copy(src_ref.at[pl.ds(step * groups, groups)], stage_ref.at[into], in_sems.at[into])

    @pl.when(step_id == 0)
    def _():
        fetch(0, 0).start()
        fetch(1, 1).start()
        zero_ref[...] = jnp.zeros(zero_ref.shape, F32)
        for b in range(N_BUCKETS):
            pltpu.make_async_copy(zero_ref, tile_rows(fill_ref[b]), sem).start()

        def tail_start(t, c):
            pltpu.make_async_copy(zero_ref, tile_rows(t), sem).start()
            return c

        lax.fori_loop(na_ref[0], n_tiles, tail_start, 0)
        for b in range(N_BUCKETS):
            pltpu.make_async_copy(zero_ref, tile_rows(fill_ref[b]), sem).wait()

        def tail_wait(t, c):
            pltpu.make_async_copy(zero_ref, tile_rows(t), sem).wait()
            return c

        lax.fori_loop(na_ref[0], n_tiles, tail_wait, 0)

    n_up, n_down = D_MODEL // CAST_ROWS, D_EXPERT // CAST_ROWS

    def cast(src, dst, chunk):
        rows = pl.ds(pl.multiple_of(chunk * CAST_ROWS, CAST_ROWS), CAST_ROWS)
        dst[0, rows, :] = src[0, rows, :].astype(BF16)

    fetch(step_id, slot).wait()

    def step(it, c):
        for g in (2 * it, 2 * it + 1):
            for u in range(SUB):
                row = dst_ref.at[pl.ds(pos_ref[0, 0, g * SUB + u], 1), :]
                pltpu.make_async_copy(stage_ref.at[slot, g, pl.ds(u, 1), :], row, row_sems.at[slot]).start()
        cast(wg_ref, wgb_ref, jnp.minimum(it, n_up - 1))
        cast(wu_ref, wub_ref, jnp.minimum(it, n_up - 1))
        cast(wd_ref, wdb_ref, jnp.minimum(it, n_down - 1))
        return c

    assert tm % (2 * SUB) == 0 and tm // (2 * SUB) >= n_up
    lax.fori_loop(0, tm // (2 * SUB), step, 0)

    @pl.when(step_id > 0)
    def _():
        _wait_rows(stage_ref.at[prev_slot], row_sems.at[prev_slot])

    @pl.when(step_id + 2 < n_steps)
    def _():
        fetch(step_id + 2, prev_slot).start()

    @pl.when(step_id == n_steps - 1)
    def _():
        _wait_rows(stage_ref.at[slot], row_sems.at[slot])


def _scatter_rows(l, fill_tile, n_act, pos, src, n_tiles, wg, wu, wd):
    nt = N_EXPERTS
    tm = src.shape[0] * SUB // nt
    w_in_spec = lambda r, c: pl.BlockSpec((1, r, c), lambda i, *_: (l * N_EXPERTS + i, 0, 0))
    w_out_spec = lambda r, c: pl.BlockSpec((1, r, c), lambda i, *_: (i, 0, 0))
    grid_spec = pltpu.PrefetchScalarGridSpec(
        num_scalar_prefetch=2,
        grid=(nt,),
        in_specs=[pl.BlockSpec((1, 1, tm), lambda i, *_: (i, 0, 0), memory_space=pltpu.SMEM),
                  pl.BlockSpec(memory_space=pl.ANY),
                  w_in_spec(D_MODEL, D_EXPERT), w_in_spec(D_MODEL, D_EXPERT), w_in_spec(D_EXPERT, D_MODEL)],
        out_specs=[pl.BlockSpec(memory_space=pl.ANY),
                   w_out_spec(D_MODEL, D_EXPERT), w_out_spec(D_MODEL, D_EXPERT), w_out_spec(D_EXPERT, D_MODEL)],
        scratch_shapes=[pltpu.VMEM((MOE_TM, ROW_W), F32), pltpu.VMEM((STAGE_SLOTS, tm // SUB, SUB, ROW_W), F32),
                        pltpu.SemaphoreType.DMA(()), pltpu.SemaphoreType.DMA((STAGE_SLOTS,)),
                        pltpu.SemaphoreType.DMA((STAGE_SLOTS,))],
    )
    assert nt >= 2
    return pl.pallas_call(
        functools.partial(_scatter_body, tm=tm, n_tiles=n_tiles),
        grid_spec=grid_spec,
        out_shape=[jax.ShapeDtypeStruct((n_tiles * MOE_TM, ROW_W), F32),
                   jax.ShapeDtypeStruct((N_EXPERTS, D_MODEL, D_EXPERT), BF16),
                   jax.ShapeDtypeStruct((N_EXPERTS, D_MODEL, D_EXPERT), BF16),
                   jax.ShapeDtypeStruct((N_EXPERTS, D_EXPERT, D_MODEL), BF16)],
        compiler_params=_cp(("arbitrary",)),
        name="dispatch_scatter",
    )(fill_tile, n_act, pos, src, wg, wu, wd)


def _scatter_more_body(pos_ref, src_ref, dst_in_ref, dst_ref, sem, *, tm):
    del dst_in_ref
    _scatter_tile_rows(pos_ref, src_ref, dst_ref, sem, tm)


def _scatter_more_rows(pos, src, dst, tm):
    nt = src.shape[0] * SUB // tm
    return pl.pallas_call(
        functools.partial(_scatter_more_body, tm=tm),
        grid=(nt,),
        in_specs=[pl.BlockSpec((1, 1, tm), lambda i: (i, 0, 0), memory_space=pltpu.SMEM),
                  pl.BlockSpec((tm // SUB, SUB, ROW_W), lambda i: (i, 0, 0)),
                  pl.BlockSpec(memory_space=pl.ANY)],
        out_specs=pl.BlockSpec(memory_space=pl.ANY),
        out_shape=jax.ShapeDtypeStruct(dst.shape, dst.dtype),
        scratch_shapes=[pltpu.SemaphoreType.DMA(())],
        input_output_aliases={2: 0},
        compiler_params=_cp(("arbitrary",)),
        name="dispatch_scatter_more",
    )(pos, src, dst)


MOE_STEP_TILES = 2


def _moe_body(ta_ref, tb_ref, na_ref, xs_ref, nf_ref, *refs):
    del ta_ref, tb_ref
    w_refs, y_ref = refs[:-1], refs[-1]
    first_tile = pl.program_id(0) * MOE_STEP_TILES

    @pl.when(first_tile < na_ref[0])
    def _():
        staged = []
        for t in range(MOE_STEP_TILES):
            rows = slice(t * MOE_TM, (t + 1) * MOE_TM)
            x1 = xs_ref[rows, 0:D_MODEL]
            xb = ((x1 * xs_ref[rows, D_MODEL + 2:D_MODEL + 3]) * nf_ref[...]).astype(BF16)
            w = w_refs[6 * t:6 * t + 6]
            staged.append((x1, [(_dot(xb, w[3 * k][0]), _dot(xb, w[3 * k + 1][0])) for k in range(2)]))
        for t in range(MOE_STEP_TILES):
            rows = slice(t * MOE_TM, (t + 1) * MOE_TM)
            y, ups = staged[t]
            for k, (a, u) in enumerate(ups):
                hdn = (jax.nn.silu(a) * u) * xs_ref[rows, D_MODEL + k:D_MODEL + k + 1]
                y = y + _dot(hdn.astype(BF16), w_refs[6 * t + 3 * k + 2][0])
            y_ref[rows, :] = y

    @pl.when(first_tile >= na_ref[0])
    def _():
        y_ref[...] = jnp.zeros(y_ref.shape, F32)


def _moe_experts(l, tile_a, tile_b, n_act, xs, nf, wg, wu, wd):
    step_rows = MOE_STEP_TILES * MOE_TM
    assert xs.shape[0] % step_rows == 0
    last = lambda tile, na: jnp.minimum(tile, na[0] - 1)
    expert = lambda sel, t, i, ta, tb, na: (sel(ta, tb)[last(MOE_STEP_TILES * i + t, na)], 0, 0)
    w_up = lambda sel, t: pl.BlockSpec((1, D_MODEL, D_EXPERT), functools.partial(expert, sel, t))
    w_dn = lambda sel, t: pl.BlockSpec((1, D_EXPERT, D_MODEL), functools.partial(expert, sel, t))
    sa = lambda ta, tb: ta
    sb = lambda ta, tb: tb
    w_specs = []
    for t in range(MOE_STEP_TILES):
        w_specs += [w_up(sa, t), w_up(sa, t), w_dn(sa, t), w_up(sb, t), w_up(sb, t), w_dn(sb, t)]
    grid_spec = pltpu.PrefetchScalarGridSpec(
        num_scalar_prefetch=3,
        grid=(xs.shape[0] // step_rows,),
        in_specs=[pl.BlockSpec((step_rows, ROW_W), lambda i, ta, tb, na: (jnp.minimum(i, (na[0] - 1) // MOE_STEP_TILES), 0)),
                  _layer_spec(l, 1, D_MODEL)] + w_specs,
        out_specs=pl.BlockSpec((step_rows, D_MODEL), lambda i, ta, tb, na: (i, 0)),
    )
    return pl.pallas_call(
        _moe_body,
        grid_spec=grid_spec,
        out_shape=jax.ShapeDtypeStruct((xs.shape[0], D_MODEL), F32),
        compiler_params=_cp(("arbitrary",)),
        name="moe_experts",
    )(tile_a, tile_b, n_act, xs, nf, *((wg, wu, wd) * (2 * MOE_STEP_TILES)))


def _unpermute_body(pos_ref, ys_ref, o_ref, sem, *, tm):
    def group(g, c):
        for u in range(SUB):
            row = ys_ref.at[pl.ds(pos_ref[0, 0, g * SUB + u], 1), :]
            pltpu.make_async_copy(row, o_ref.at[g, pl.ds(u, 1), :], sem).start()
        return c

    lax.fori_loop(0, tm // SUB, group, 0)
    _wait_rows(o_ref, sem)


def _unpermute(pos, ys, tm):
    t = pos.shape[0] * tm
    return pl.pallas_call(
        functools.partial(_unpermute_body, tm=tm),
        grid=(t // tm,),
        in_specs=[pl.BlockSpec((1, 1, tm), lambda i: (i, 0, 0), memory_space=pltpu.SMEM),
                  pl.BlockSpec(memory_space=pl.ANY)],
        out_specs=pl.BlockSpec((tm // SUB, SUB, D_MODEL), lambda i: (i, 0, 0)),
        out_shape=jax.ShapeDtypeStruct((t // SUB, SUB, D_MODEL), F32),
        scratch_shapes=[pltpu.SemaphoreType.DMA(())],
        compiler_params=_cp(("arbitrary",)),
        name="moe_unpermute",
    )(pos, ys)


def _rope_lane_freq():
    half = ROT_DIM // 2
    inv_freq = jnp.float32(ROPE_THETA) ** (-jnp.arange(half, dtype=jnp.float32) * (2.0 / ROT_DIM))
    dim = np.arange(LANES) % HEAD_DIM
    return inv_freq[dim % half][None, :], dim < half, (dim >= half) & (dim < ROT_DIM)


def _rope_patterns(cos, sin, first, second):
    return (jnp.where(first | second, cos, 1.0), jnp.where(first, -sin, 0.0), jnp.where(second, sin, 0.0))


def _rope_tables(pos):
    freq, first, second = _rope_lane_freq()
    ang = pos.astype(jnp.float32)[:, None] * freq
    return _rope_patterns(jnp.cos(ang), jnp.sin(ang), first, second)


def _rope_tables_padded(n_blocks):
    freq, first, second = _rope_lane_freq()
    ang_a = (jnp.arange(n_blocks, dtype=jnp.int32) * BLOCK).astype(jnp.float32)[:, None] * freq
    ang_b = (jnp.arange(BLOCK, dtype=jnp.int32) - PAD).astype(jnp.float32)[:, None] * freq
    ca, sa = jnp.cos(ang_a)[:, None, :], jnp.sin(ang_a)[:, None, :]
    cb, sb = jnp.cos(ang_b)[None], jnp.sin(ang_b)[None]
    flat = lambda t: t.reshape(n_blocks * BLOCK, LANES)
    return _rope_patterns(flat(ca * cb - sa * sb), flat(sa * cb + ca * sb), first, second)


def _seg_ones(n):
    idx = np.arange(n) // HEAD_DIM
    return jnp.asarray(idx[:, None] == idx[None, :], BF16)


def _bucket_experts():
    ea, eb = [], []
    for g in range(N_GROUPS):
        for a in range(EXPERTS_PER_GROUP):
            for b in range(a + 1, EXPERTS_PER_GROUP):
                ea.append(g * EXPERTS_PER_GROUP + a)
                eb.append(g * EXPERTS_PER_GROUP + b)
    return np.asarray(ea, np.int32), np.asarray(eb, np.int32)


def _dispatch_plan(counts, n_tiles):
    padded = ((counts + MOE_TM - 1) // MOE_TM) * MOE_TM
    ends = jnp.cumsum(padded)
    offs = ends - padded
    n_act = jnp.maximum(ends[-1] // MOE_TM, 1)
    starts = jnp.arange(n_tiles, dtype=jnp.int32) * MOE_TM
    tile_bucket = jnp.minimum(jnp.sum(starts[:, None] >= ends[None, :], axis=1), N_BUCKETS - 1)
    ea, eb = _bucket_experts()
    onehot = tile_bucket[:, None] == jnp.arange(N_BUCKETS)[None, :]
    tile_a = jnp.sum(jnp.where(onehot, ea[None, :], 0), axis=1).astype(jnp.int32)
    tile_b = jnp.sum(jnp.where(onehot, eb[None, :], 0), axis=1).astype(jnp.int32)
    fill_tile = jnp.maximum(ends // MOE_TM - 1, 0).astype(jnp.int32)
    return offs, tile_a, tile_b, n_act.astype(jnp.int32).reshape(1), fill_tile


def _positions(offs, bucket, rank):
    onehot = bucket[..., None] == jnp.arange(N_BUCKETS, dtype=jnp.int32)
    return (jnp.sum(jnp.where(onehot, offs.astype(jnp.int32), 0), axis=-1) + rank).astype(jnp.int32)


def kernel(x_prompt, x_sample, cache_k, cache_v, state_conv, meta_tokens, norm_mix, w_in, conv_w, q_norm, k_norm,
           attn_sinks, w_conv_out, w_attn_out, w_o, norm_ffn, w_router_group, b_router_group, w_router_expert,
           b_router_expert, w_exp_gate, w_exp_up, w_exp_down):
    batch, seq, _ = x_prompt.shape
    depth = w_in.shape[0]
    n_dec = x_sample.shape[0]
    past_len = PAST_LEN
    lp = PAD + N_META + seq
    tm_in, tm_out, qb = 640, 640, 5
    tm_move, tm_last = 3328, 4096
    assert PAD + N_META == BLOCK and seq % BLOCK == 0 and tm_in % BLOCK == 0
    assert lp % tm_in == 0 and (batch * lp) % tm_out == 0 and lp % (qb * BLOCK) == 0
    assert (batch * lp) % tm_move == 0 and (batch * seq) % tm_last == 0
    assert x_sample.shape[1] == 1 and cache_k.shape[2] == WINDOW and past_len >= WINDOW

    t_prompt = batch * lp
    t_all = t_prompt + n_dec
    n_tiles = -(-(t_all + N_BUCKETS * (MOE_TM - 1)) // MOE_TM)
    n_tiles = -(-n_tiles // MOE_STEP_TILES) * MOE_STEP_TILES

    meta = jnp.broadcast_to(meta_tokens[None].astype(F32), (batch, N_META, D_MODEL))
    head = jnp.concatenate([jnp.zeros((batch, PAD, D_MODEL), F32), meta], axis=1)
    xs = x_sample.reshape(n_dec, D_MODEL)

    rope_p = _rope_tables_padded(lp // BLOCK)
    rope_s = _rope_tables(jnp.full((1,), past_len, jnp.int32))
    s256, s128 = _seg_ones(256), _seg_ones(LANES)
    tri_p = jnp.asarray(np.triu(np.ones((tm_out, tm_out)), 1), BF16)
    tri_s = jnp.asarray(np.triu(np.ones((n_dec, n_dec)), 1), BF16)
    zero_cnt = jnp.zeros((LANES, LANES), F32)

    w_mix = w_in.astype(BF16)
    w_gate = w_in[:, :, D_MIX:].astype(BF16)
    wco, wao, wo = w_conv_out.astype(BF16), w_attn_out.astype(BF16), w_o.astype(BF16)
    wg32 = w_exp_gate.reshape(depth * N_EXPERTS, D_MODEL, D_EXPERT)
    wu32 = w_exp_up.reshape(depth * N_EXPERTS, D_MODEL, D_EXPERT)
    wd32 = w_exp_down.reshape(depth * N_EXPERTS, D_EXPERT, D_MODEL)
    nm, nf = norm_mix.reshape(depth, 1, D_MODEL), norm_ffn.reshape(depth, 1, D_MODEL)
    qn = (jnp.tile(q_norm, (1, N_HEADS)) * Q_SCALE).reshape(depth, 1, D_Q)
    kn = jnp.tile(k_norm, (1, N_KV_HEADS)).reshape(depth, 1, D_KV)
    r_pad = LANES - N_GROUPS - N_EXPERTS
    wr = jnp.concatenate([w_router_group, w_router_expert, jnp.zeros((depth, D_MODEL, r_pad), F32)], axis=-1).astype(BF16)
    br = jnp.concatenate([b_router_group, b_router_expert, jnp.zeros((depth, r_pad), F32)], axis=-1).reshape(depth, 1, LANES)
    sinks = attn_sinks.astype(F32) * LOG2E
    sinkb = jnp.broadcast_to(sinks[:, :, None], (depth, N_HEADS, LANES))
    ck = cache_k.reshape(depth, n_dec, WINDOW, D_KV)
    cv = cache_v.reshape(depth, n_dec, WINDOW, D_KV)

    outs = {k: [] for k in ("kp", "vp", "cp", "ks", "vs", "cs")}
    for l in range(depth):
        if l == 0:
            cy, q, k, v, ulast, xp = _prompt_in(l, x_prompt, nm, w_mix, conv_w, qn, kn, rope_p, s256, s128, tm_in, head)
        else:
            cy, q, k, v, ulast = _prompt_in(l, xp, nm, w_mix, conv_w, qn, kn, rope_p, s256, s128, tm_in)
        ao = _prompt_attn(l, q, k, v, sinks, qb)
        xep, rankp, bktp, cnt = _mix_out(
            l, xp.reshape(t_prompt, D_MODEL), cy.reshape(t_prompt, D_CONV), ao.reshape(t_prompt, D_Q),
            nm, w_gate, wco, wao, wo, nf, wr, br, tri_p, zero_cnt, tm_out)
        outs["kp"].append(k[:, lp - WINDOW:].reshape(batch, WINDOW, N_KV_HEADS, HEAD_DIM))
        outs["vp"].append(v[:, lp - WINDOW:].reshape(batch, WINDOW, N_KV_HEADS, HEAD_DIM))
        outs["cp"].append(ulast[:, 8 - (CONV_W - 1):])

        c0, c1 = state_conv[l, :, 0, :], state_conv[l, :, 1, :]
        cys, qx, ksn, vsn, us = _sample_in(l, xs, c0, c1, nm, w_mix, conv_w, qn, kn, rope_s, s256, s128)
        ox, nk, nv = _sample_attn(l, jnp.transpose(qx, (1, 0, 2)), sinkb, ck, cv, ksn, vsn)
        ox = ox.reshape(n_dec, N_KV_HEADS, GQA, N_KV_HEADS, HEAD_DIM)
        aos = jnp.stack([ox[:, j, :, j, :] for j in range(N_KV_HEADS)], axis=1).reshape(n_dec, D_Q).astype(BF16)
        xes, ranks, bkts, cnt = _mix_out(l, xs, cys, aos, nm, w_gate, wco, wao, wo, nf, wr, br, tri_s, cnt, n_dec)
        outs["ks"].append(nk.reshape(n_dec, WINDOW, N_KV_HEADS, HEAD_DIM))
        outs["vs"].append(nv.reshape(n_dec, WINDOW, N_KV_HEADS, HEAD_DIM))
        outs["cs"].append(jnp.stack([c1, us], axis=1))

        counts = cnt[:N_BUCKETS, 0].astype(jnp.int32)
        offs, tile_a, tile_b, n_act, fill_tile = _dispatch_plan(counts, n_tiles)
        posp = _positions(offs, bktp, rankp)
        poss = _positions(offs, bkts, ranks)
        by8 = lambda a: a.reshape(a.shape[0] // SUB, SUB, a.shape[1])
        sorted_rows, wg, wu, wd = _scatter_rows(l, fill_tile, n_act, posp.reshape(N_EXPERTS, 1, t_prompt // N_EXPERTS),
                                                by8(xep), n_tiles, wg32, wu32, wd32)
        posp = posp.reshape(t_prompt // tm_move, 1, tm_move)
        sorted_rows = _scatter_more_rows(poss, by8(xes), sorted_rows, n_dec)
        ys = _moe_experts(l, tile_a, tile_b, n_act, sorted_rows, nf, wg, wu, wd)
        xs = _unpermute(poss, ys, n_dec).reshape(n_dec, D_MODEL)
        if l + 1 < depth:
            xp = _unpermute(posp, ys, tm_move).reshape(batch, lp, D_MODEL)
        else:
            pos_tok = posp.reshape(batch, lp)[:, PAD + N_META:].reshape(batch * seq // tm_last, 1, tm_last)
            y_prompt = _unpermute(pos_tok, ys, tm_last).reshape(batch, seq, D_MODEL)

    y_sample = xs.reshape(n_dec, 1, D_MODEL)
    st = lambda k: jnp.stack(outs[k])
    return (y_prompt, y_sample, st("kp"), st("vp"), st("cp"), st("ks"), st("vs"), st("cs"))
```

```python
import functools
import math

import jax
import jax.numpy as jnp
import numpy as np
from jax import lax
from jax.experimental import pallas as pl
from jax.experimental.pallas import tpu as pltpu

D_MODEL = 1024
N_META = 16
D_CONV = D_MODEL
CONV_W = 3
N_HEADS = 16
N_KV_HEADS = 2
HEAD_DIM = 64
GQA = N_HEADS // N_KV_HEADS
ROT_DIM = HEAD_DIM // 4
ROPE_THETA = 500000.0
WINDOW = 128
PAST_LEN = 8192
BLOCK = 128
N_GROUPS = 4
EXPERTS_PER_GROUP = 4
N_EXPERTS = N_GROUPS * EXPERTS_PER_GROUP
D_EXPERT = 512
EPS = 1e-6
NEG = -1e30
D_Q = N_HEADS * HEAD_DIM
D_KV = N_KV_HEADS * HEAD_DIM
C_B, C_C, C_HC = 0, D_CONV, 2 * D_CONV
C_Q = 3 * D_CONV
C_K = C_Q + D_Q
C_V = C_K + D_KV
C_G = C_V + D_KV
D_MIX = C_G
D_IN = C_G + 2 * D_MODEL

LANES = 128
SUBLANES = 8
PAD = (-N_META) % BLOCK
N_PAIRS = 6
N_BUCKETS = N_GROUPS * N_PAIRS
MOE_TM = 256
ROW_W = D_MODEL + LANES
LOG2E = math.log2(math.e)
Q_SCALE = HEAD_DIM ** -0.5 * LOG2E

F32 = jnp.float32
BF16 = jnp.bfloat16
VMEM_LIMIT = 56 * 1024 * 1024


def _cp(sem, vmem=VMEM_LIMIT):
    return pltpu.CompilerParams(dimension_semantics=sem, vmem_limit_bytes=vmem)


def _const_spec(shape):
    nd = len(shape)
    return pl.BlockSpec(shape, lambda *_: (0,) * nd, pipeline_mode=pl.Buffered(1))


def _layer_spec(l, *shape):
    n = len(shape)
    return pl.BlockSpec((None,) + shape, lambda *_: (l,) + (0,) * n, pipeline_mode=pl.Buffered(1))


def _w_mix_spec(l):
    return pl.BlockSpec((None, D_MODEL, D_MIX), lambda *_: (l, 0, 0), pipeline_mode=pl.Buffered(1))


def _dot(a, b):
    return jnp.dot(a, b, preferred_element_type=F32)


def _seg_mean_sq(x, seg_ones):
    return _dot((x * x).astype(BF16), seg_ones) * (1.0 / HEAD_DIM)


def _rope128(t, cos, sin_pm, first):
    partner = jnp.where(first, pltpu.roll(t, LANES - ROT_DIM // 2, 1), pltpu.roll(t, ROT_DIM // 2, 1))
    return t * cos + partner * sin_pm


def _rms_scale(x):
    return lax.rsqrt(jnp.mean(x * x, axis=-1, keepdims=True) + EPS)


def _rms_rows(x, g):
    return (x * _rms_scale(x)) * g


def _qk_project(hb, w_ref, s256, s128):
    qs = [_dot(hb, w_ref[:, C_Q + c * 256:C_Q + (c + 1) * 256]) for c in range(D_Q // 256)]
    kv = _dot(hb, w_ref[:, C_K:C_K + 2 * D_KV])
    kc, v = kv[:, 0:D_KV], kv[:, D_KV:]
    return qs, [_seg_mean_sq(qc, s256) for qc in qs], kc, _seg_mean_sq(kc, s128), v


def _qk_finish(proj, qn, kn, cos, sneg, spos, store_q):
    qs, q_ms, kc, k_ms, _ = proj
    sin_pm = sneg + spos
    lane = lax.broadcasted_iota(jnp.int32, cos.shape, 1)
    first = lax.bitwise_and(lane, HEAD_DIM - 1) < ROT_DIM // 2
    for c, (qc, ms) in enumerate(zip(qs, q_ms)):
        qc = (qc * lax.rsqrt(ms + EPS)) * qn[:, c * 256:(c + 1) * 256]
        for s in range(2):
            r = _rope128(qc[:, s * LANES:(s + 1) * LANES], cos, sin_pm, first)
            store_q(2 * c + s, r.astype(BF16))
    kc = (kc * lax.rsqrt(k_ms + EPS)) * kn
    return _rope128(kc, cos, sin_pm, first)


def _in_body(x_ref, nm_ref, w_ref, cw_ref, qn_ref, kn_ref, cos_ref, sneg_ref, spos_ref,
             s256_ref, s128_ref, cy_ref, q_ref, k_ref, v_ref, ul_ref, us_ref, *, tm, parts):
    i = pl.program_id(1)
    th = tm // parts

    @pl.when(i == 0)
    def _():
        us_ref[0:8, :] = jnp.zeros((8, D_CONV), F32)

    projs = []
    for h in range(parts):
        r0 = h * th
        hb = _rms_rows(x_ref[0, r0:r0 + th, :], nm_ref[...]).astype(BF16)
        projs.append(_qk_project(hb, w_ref, s256_ref[...], s128_ref[...]))
        u = _dot(hb, w_ref[:, C_C:C_C + D_CONV]) * _dot(hb, w_ref[:, C_HC:C_HC + D_CONV])
        row = lax.broadcasted_iota(jnp.int32, (th, 1), 0) + (i * tm + r0)
        u = jnp.where(row >= PAD, u, 0.0)
        us_ref[8 + r0:8 + r0 + th, :] = u
        conv = (us_ref[6 + r0:6 + r0 + th, :] * cw_ref[0:1, :] + us_ref[7 + r0:7 + r0 + th, :] * cw_ref[1:2, :]) \
            + u * cw_ref[2:3, :]
        cy_ref[0, r0:r0 + th, :] = (_dot(hb, w_ref[:, C_B:C_B + D_CONV]) * conv).astype(BF16)
        v_ref[0, r0:r0 + th, :] = projs[h][4]

    last = us_ref[tm:tm + 8, :]
    ul_ref[0] = last
    us_ref[0:8, :] = last

    for h in range(parts):
        rows = slice(h * th, (h + 1) * th)

        def store_q(slab, val, rows=rows):
            q_ref[0, rows, slab * LANES:(slab + 1) * LANES] = val

        k_ref[0, rows, :] = _qk_finish(projs[h], qn_ref[...], kn_ref[...], cos_ref[rows, :], sneg_ref[rows, :],
                                       spos_ref[rows, :], store_q)


def _in_first_body(head_ref, *refs, tm, parts):
    nb = tm // BLOCK
    blocks, rest, xpad_ref, us_ref = refs[:nb], refs[nb:-2], refs[-2], refs[-1]
    first = pl.program_id(1) == 0
    xpad_ref[0, 0:BLOCK, :] = jnp.where(first, head_ref[0], blocks[0][0])
    for j in range(1, nb):
        xpad_ref[0, j * BLOCK:(j + 1) * BLOCK, :] = blocks[j][0]
    _in_body(xpad_ref, *rest, us_ref, tm=tm, parts=parts)


def _prompt_in(l, x, nm, w_mix, cw, qn, kn, rope, s256, s128, tm, head=None):
    b = x.shape[0]
    lp = x.shape[1] if head is None else x.shape[1] + BLOCK
    nt = lp // tm
    nb = tm // BLOCK
    cos, sneg, spos = rope
    tok = lambda w: pl.BlockSpec((1, tm, w), lambda bi, i: (bi, i, 0))
    tab = pl.BlockSpec((tm, LANES), lambda bi, i: (i, 0))
    params = [_layer_spec(l, 1, D_MODEL), _w_mix_spec(l),
              _layer_spec(l, CONV_W, D_CONV), _layer_spec(l, 1, D_Q), _layer_spec(l, 1, D_KV),
              tab, tab, tab, _const_spec((256, 256)), _const_spec((LANES, LANES))]
    out_specs = [tok(D_CONV), tok(D_Q), tok(D_KV), tok(D_KV), pl.BlockSpec((1, 8, D_CONV), lambda bi, i: (bi, 0, 0))]
    out_shape = [jax.ShapeDtypeStruct((b, lp, D_CONV), BF16), jax.ShapeDtypeStruct((b, lp, D_Q), BF16),
                 jax.ShapeDtypeStruct((b, lp, D_KV), F32), jax.ShapeDtypeStruct((b, lp, D_KV), F32),
                 jax.ShapeDtypeStruct((b, 8, D_CONV), F32)]
    if head is None:
        body, x_specs, x_args = _in_body, [tok(D_MODEL)], (x,)
    else:
        blk = lambda j: pl.BlockSpec((1, BLOCK, D_MODEL), lambda bi, i: (bi, jnp.maximum(nb * i - 1 + j, 0), 0))
        body = _in_first_body
        x_specs = [pl.BlockSpec((1, BLOCK, D_MODEL), lambda bi, i: (bi, 0, 0))] + [blk(j) for j in range(nb)]
        x_args = (head,) + (x,) * nb
        out_specs.append(tok(D_MODEL))
        out_shape.append(jax.ShapeDtypeStruct((b, lp, D_MODEL), F32))
    return pl.pallas_call(
        functools.partial(body, tm=tm, parts=2),
        grid=(b, nt),
        in_specs=x_specs + params,
        out_specs=out_specs,
        out_shape=out_shape,
        scratch_shapes=[pltpu.VMEM((tm + 8, D_CONV), F32)],
        compiler_params=_cp(("arbitrary", "arbitrary")),
        name="prompt_in",
    )(*x_args, nm, w_mix, cw, qn, kn, cos, sneg, spos, s256, s128)


def _sin_body(x_ref, c0_ref, c1_ref, nm_ref, w_ref, cw_ref, qn_ref, kn_ref, cos_ref, sneg_ref,
              spos_ref, s256_ref, s128_ref, cy_ref, qx_ref, k_ref, v_ref, u_ref):
    hb = _rms_rows(x_ref[...], nm_ref[...]).astype(BF16)
    u = _dot(hb, w_ref[:, C_C:C_C + D_CONV]) * _dot(hb, w_ref[:, C_HC:C_HC + D_CONV])
    u_ref[...] = u
    conv = (c0_ref[...] * cw_ref[0:1, :] + c1_ref[...] * cw_ref[1:2, :]) + u * cw_ref[2:3, :]
    cy_ref[...] = (_dot(hb, w_ref[:, C_B:C_B + D_CONV]) * conv).astype(BF16)

    lane = lax.broadcasted_iota(jnp.int32, (x_ref.shape[0], LANES), 1)
    low = lane < HEAD_DIM

    def store_q(slab, val):
        valf = val.astype(F32)
        swapped = pltpu.roll(valf, HEAD_DIM, 1)
        zero = jnp.zeros_like(valf)
        for h in (2 * slab, 2 * slab + 1):
            src = valf if (h % 2) == (h // GQA) else swapped
            keep = low if (h // GQA) == 0 else jnp.logical_not(low)
            qx_ref[h] = jnp.where(keep, src, zero).astype(BF16)

    cos = jnp.broadcast_to(cos_ref[...], (x_ref.shape[0], LANES))
    sneg = jnp.broadcast_to(sneg_ref[...], (x_ref.shape[0], LANES))
    spos = jnp.broadcast_to(spos_ref[...], (x_ref.shape[0], LANES))
    proj = _qk_project(hb, w_ref, s256_ref[...], s128_ref[...])
    k_ref[...] = _qk_finish(proj, qn_ref[...], kn_ref[...], cos, sneg, spos, store_q)
    v_ref[...] = proj[4]


def _sample_in(l, x, c0, c1, nm, w_mix, cw, qn, kn, rope, s256, s128):
    n = x.shape[0]
    cos, sneg, spos = rope
    full = lambda *s: pl.BlockSpec(s, lambda i: (0,) * len(s))
    return pl.pallas_call(
        _sin_body,
        grid=(1,),
        in_specs=[full(n, D_MODEL), full(n, D_CONV), full(n, D_CONV), _layer_spec(l, 1, D_MODEL),
                  _w_mix_spec(l), _layer_spec(l, CONV_W, D_CONV), _layer_spec(l, 1, D_Q),
                  _layer_spec(l, 1, D_KV), full(1, LANES), full(1, LANES), full(1, LANES), full(256, 256),
                  full(LANES, LANES)],
        out_specs=[full(n, D_CONV), full(N_HEADS, n, LANES), full(n, D_KV), full(n, D_KV), full(n, D_CONV)],
        out_shape=[jax.ShapeDtypeStruct((n, D_CONV), BF16), jax.ShapeDtypeStruct((N_HEADS, n, LANES), BF16),
                   jax.ShapeDtypeStruct((n, D_KV), F32), jax.ShapeDtypeStruct((n, D_KV), F32),
                   jax.ShapeDtypeStruct((n, D_CONV), F32)],
        compiler_params=_cp(("arbitrary",)),
        name="sample_in",
    )(x, c0, c1, nm, w_mix, cw, qn, kn, cos, sneg, spos, s256, s128)


def _attn_body(sink_ref, q_ref, kp_ref, kc_ref, vp_ref, vc_ref, o_ref, ke_ref, ko_ref, vt_ref, *, qb, l):
    i = pl.program_id(1)
    lane = lax.broadcasted_iota(jnp.int32, (BLOCK, LANES), 1)
    low = lane < HEAD_DIM

    def prep_k(src, blk0, nblk):
        for t in range(nblk):
            blk = src[0, t * BLOCK:(t + 1) * BLOCK, :]
            swp = pltpu.roll(blk, HEAD_DIM, 1)
            zero = jnp.zeros_like(blk)
            rows = slice((blk0 + t) * BLOCK, (blk0 + t + 1) * BLOCK)
            ke_ref[0, rows, :] = jnp.where(low, blk, zero).astype(BF16)
            ko_ref[0, rows, :] = jnp.where(low, zero, swp).astype(BF16)
            ke_ref[1, rows, :] = jnp.where(low, swp, zero).astype(BF16)
            ko_ref[1, rows, :] = jnp.where(low, zero, blk).astype(BF16)

    def prep_v(src, blk0, nblk):
        for t in range(nblk):
            vt = jnp.transpose(src[0, t * BLOCK:(t + 1) * BLOCK, :]).astype(BF16)
            for j in range(N_KV_HEADS):
                vt_ref[blk0 + t, j] = vt[j * HEAD_DIM:(j + 1) * HEAD_DIM, :]

    prep_k(kp_ref, 0, 1)
    prep_k(kc_ref, 1, qb)
    prep_v(vp_ref, 0, 1)
    prep_v(vc_ref, 1, qb)

    c = lax.broadcasted_iota(jnp.int32, (2 * BLOCK, BLOCK), 0)
    r = lax.broadcasted_iota(jnp.int32, (2 * BLOCK, BLOCK), 1)
    diff = r - (c - BLOCK)
    band = (diff >= 0) & (diff < WINDOW)
    nt = (((1,), (1,)), ((), ()))

    def one_block(b, carry):
        r0 = pl.multiple_of(b * BLOCK, BLOCK)
        kpos = (i * qb + b) * BLOCK + c - BLOCK - PAD
        bias = jnp.where(band & (kpos >= 0), 0.0, NEG)
        for m in range(N_HEADS // 2):
            j = (2 * m) // GQA
            q2 = q_ref[0, pl.ds(r0, BLOCK), m * LANES:(m + 1) * LANES]
            halves = []
            for par, k_ref in ((0, ke_ref), (1, ko_ref)):
                st = lax.dot_general(k_ref[j, pl.ds(r0, 2 * BLOCK), :], q2, nt, preferred_element_type=F32)
                st = st + bias
                sink = sink_ref[l, 2 * m + par]
                mx = jnp.maximum(jnp.max(st, axis=0, keepdims=True), sink)
                p = jnp.exp2(st - mx)
                den = jnp.sum(p, axis=0, keepdims=True) + jnp.exp2(sink - mx)
                pb = p.astype(BF16)
                ot = _dot(vt_ref[b, j], pb[0:BLOCK]) + _dot(vt_ref[b + 1, j], pb[BLOCK:])
                halves.append(ot * (1.0 / den))
            o2 = jnp.transpose(jnp.concatenate(halves, axis=0))
            o_ref[0, pl.ds(r0, BLOCK), m * LANES:(m + 1) * LANES] = o2.astype(BF16)
        return carry

    lax.fori_loop(0, qb, one_block, 0, unroll=True)


def _prompt_attn(l, q, k, v, sinks, qb):
    b, lp, _ = q.shape
    nsteps = lp // (qb * BLOCK)
    cur = lambda w: pl.BlockSpec((1, qb * BLOCK, w), lambda bi, i: (bi, i, 0))
    prev = pl.BlockSpec((1, BLOCK, D_KV), lambda bi, i: (bi, jnp.maximum(i * qb - 1, 0), 0))
    ext = ((qb + 1) * BLOCK, LANES)
    return pl.pallas_call(
        functools.partial(_attn_body, qb=qb, l=l),
        grid=(b, nsteps),
        in_specs=[pl.BlockSpec(memory_space=pltpu.SMEM), cur(D_Q), prev, cur(D_KV), prev, cur(D_KV)],
        out_specs=cur(D_Q),
        out_shape=jax.ShapeDtypeStruct((b, lp, D_Q), BF16),
        scratch_shapes=[pltpu.VMEM((N_KV_HEADS,) + ext, BF16), pltpu.VMEM((N_KV_HEADS,) + ext, BF16),
                        pltpu.VMEM((qb + 1, N_KV_HEADS, HEAD_DIM, BLOCK), BF16)],
        compiler_params=_cp(("arbitrary", "arbitrary")),
        name="prompt_attn",
    )(sinks, q, k, k, v, v)


def _sattn_body(qx_ref, sink_ref, ck_ref, cv_ref, kn_ref, vn_ref, ox_ref, nk_ref, nv_ref, *, tb):
    def window(c_ref, n_ref, t):
        return jnp.concatenate([c_ref[t, 1:WINDOW, :], n_ref[t:t + 1, :]], axis=0)

    for t in range(tb):
        nk_ref[t] = window(ck_ref, kn_ref, t)
        nv_ref[t] = window(cv_ref, vn_ref, t)
    nt = (((1,), (1,)), ((), ()))
    s = jnp.concatenate([lax.dot_general(qx_ref[t], window(ck_ref, kn_ref, t).astype(BF16), nt,
                                         preferred_element_type=F32) for t in range(tb)], axis=0)
    sink = jnp.concatenate([sink_ref[...][:, 0:1]] * tb, axis=0)
    m = jnp.maximum(jnp.max(s, axis=-1, keepdims=True), sink)
    p = jnp.exp2(s - m)
    rden = 1.0 / (jnp.sum(p, axis=-1, keepdims=True) + jnp.exp2(sink - m))
    pb = p.astype(BF16)
    for t in range(tb):
        rows = slice(t * N_HEADS, (t + 1) * N_HEADS)
        ox_ref[t] = _dot(pb[rows], window(cv_ref, vn_ref, t).astype(BF16)) * rden[rows]


def _sample_attn(l, qx, sinkb, ck, cv, kn, vn, tb=16):
    n = qx.shape[0]
    blk3 = lambda a, c: pl.BlockSpec((tb, a, c), lambda i: (i, 0, 0))
    cache = pl.BlockSpec((None, tb, WINDOW, D_KV), lambda i: (l, i, 0, 0))
    row = pl.BlockSpec((tb, D_KV), lambda i: (i, 0))
    return pl.pallas_call(
        functools.partial(_sattn_body, tb=tb),
        grid=(n // tb,),
        in_specs=[blk3(N_HEADS, LANES), _layer_spec(l, N_HEADS, LANES), cache, cache, row, row],
        out_specs=[blk3(N_HEADS, LANES), blk3(WINDOW, D_KV), blk3(WINDOW, D_KV)],
        out_shape=[jax.ShapeDtypeStruct((n, N_HEADS, LANES), F32),
                   jax.ShapeDtypeStruct((n, WINDOW, D_KV), F32), jax.ShapeDtypeStruct((n, WINDOW, D_KV), F32)],
        compiler_params=_cp(("arbitrary",)),
        name="sample_attn",
    )(qx, sinkb, ck, cv, kn, vn)


ROUTE_ROWS = 24
RINV_LANE = LANES - 1


def _route(lt):
    top = lt[0:ROUTE_ROWS, :]
    rows = top.shape[1]
    rowf = lax.broadcasted_iota(jnp.int32, top.shape, 0).astype(F32)
    big = jnp.float32(3e38)
    far = jnp.float32(LANES)
    cmax = lambda a: jnp.max(a, axis=0, keepdims=True)
    cmin = lambda a: jnp.min(a, axis=0, keepdims=True)

    gmask = rowf < N_GROUPS
    gl = jnp.where(gmask, top, -big)
    gmax = cmax(gl)
    grp = cmin(jnp.where(gmask & (gl == gmax), rowf, far))
    p_grp = 1.0 / jnp.sum(jnp.where(gmask, jnp.exp(gl - gmax), 0.0), axis=0, keepdims=True)

    e_lo = N_GROUPS + EXPERTS_PER_GROUP * grp
    emask = (rowf >= e_lo) & (rowf < e_lo + EXPERTS_PER_GROUP)
    el = jnp.where(emask, top, -big)
    v1 = cmax(el)
    i1 = cmin(jnp.where(emask & (el == v1), rowf, far))
    rest = emask & (rowf != i1)
    el2 = jnp.where(rest, top, -big)
    v2 = cmax(el2)
    i2 = cmin(jnp.where(rest & (el2 == v2), rowf, far))
    e = jnp.exp(v2 - v1)
    w1 = (1.0 / (1.0 + e)) * p_grp
    w2 = (e / (1.0 + e)) * p_grp
    first_low = i1 < i2
    ea = jnp.where(first_low, i1, i2) - e_lo
    eb = jnp.where(first_low, i2, i1) - e_lo
    w_a = jnp.where(first_low, w1, w2)
    w_b = jnp.where(first_low, w2, w1)
    pair = jnp.where(ea == 0.0, 0.0, jnp.where(ea == 1.0, 3.0, 5.0)) + (eb - ea - 1.0)
    bucket = grp * N_PAIRS + pair

    r8 = lax.broadcasted_iota(jnp.int32, (SUBLANES, rows), 0)
    head = jnp.where(r8 == 0, w_a, jnp.where(r8 == 1, w_b, jnp.where(r8 == 2, lt[RINV_LANE:RINV_LANE + 1, :], 0.0)))
    return bucket, jnp.concatenate([head, jnp.zeros((LANES - SUBLANES, rows), F32)], axis=0)


def _out_body(x_ref, cy_ref, ao_ref, nm_ref, wg_ref, wco_ref, wao_ref, wo_ref, nf_ref, wr_ref, br_ref,
              tri_ref, cin_ref, xe_ref, rank_ref, bkt_ref, cnt_ref, run_ref, *, tm, parts):
    i = pl.program_id(0)
    th = tm // parts
    stage = []
    for h in range(parts):
        rows = slice(h * th, (h + 1) * th)
        hb = _rms_rows(x_ref[rows, :], nm_ref[...]).astype(BF16)
        stage.append((_dot(cy_ref[rows, :], wco_ref[...]), _dot(ao_ref[rows, :], wao_ref[...]),
                      _dot(hb, wg_ref[:, 0:D_MODEL]), _dot(hb, wg_ref[:, D_MODEL:])))
    for h, (ya, yb, gc, ga) in enumerate(stage):
        rows = slice(h * th, (h + 1) * th)
        mix = jax.nn.sigmoid(gc) * ya + jax.nn.sigmoid(ga) * yb
        xe_ref[rows, 0:D_MODEL] = x_ref[rows, :] + _dot(mix.astype(BF16), wo_ref[...])
    x1 = xe_ref[:, 0:D_MODEL]

    rinv = _rms_scale(x1)
    xnb = ((x1 * rinv) * nf_ref[...]).astype(BF16)
    logits = _dot(xnb, wr_ref[...]) + br_ref[...]
    lane = lax.broadcasted_iota(jnp.int32, (tm, LANES), 1)
    bucket, meta_t = _route(jnp.transpose(jnp.where(lane == RINV_LANE, rinv, logits)))
    xe_ref[:, D_MODEL:] = jnp.transpose(meta_t)

    @pl.when(i == 0)
    def _():
        run_ref[...] = cin_ref[...]

    sub = lax.broadcasted_iota(jnp.int32, (LANES, tm), 0).astype(F32)
    oht = (sub == bucket).astype(F32)
    before = _dot(oht.astype(BF16), tri_ref[...]) + run_ref[:, 0:1]
    rank_ref[0] = jnp.sum(oht * before, axis=0, keepdims=True).astype(jnp.int32)
    bkt_ref[0] = bucket.astype(jnp.int32)
    run_ref[...] = run_ref[...] + jnp.sum(oht, axis=-1, keepdims=True)
    cnt_ref[...] = run_ref[...]


def _mix_out(l, x, cy, ao, nm, w_gate, wco, wao, wo, nf, wr, br, tri, cnt_in, tm):
    t = x.shape[0]
    nt = t // tm
    tok = lambda w: pl.BlockSpec((tm, w), lambda i: (i, 0))
    rowi = pl.BlockSpec((1, 1, tm), lambda i: (i, 0, 0))
    sq = (D_MODEL, D_MODEL)
    return pl.pallas_call(
        functools.partial(_out_body, tm=tm, parts=2 if tm >= 4 * LANES else 1),
        grid=(nt,),
        in_specs=[tok(D_MODEL), tok(D_CONV), tok(D_Q), _layer_spec(l, 1, D_MODEL),
                  _layer_spec(l, D_MODEL, 2 * D_MODEL), _layer_spec(l, *sq), _layer_spec(l, *sq), _layer_spec(l, *sq),
                  _layer_spec(l, 1, D_MODEL), _layer_spec(l, D_MODEL, LANES), _layer_spec(l, 1, LANES),
                  _const_spec((tm, tm)), _const_spec((LANES, LANES))],
        out_specs=[tok(ROW_W), rowi, rowi, pl.BlockSpec((LANES, LANES), lambda i: (0, 0))],
        out_shape=[jax.ShapeDtypeStruct((t, ROW_W), F32),
                   jax.ShapeDtypeStruct((nt, 1, tm), jnp.int32), jax.ShapeDtypeStruct((nt, 1, tm), jnp.int32),
                   jax.ShapeDtypeStruct((LANES, LANES), F32)],
        scratch_shapes=[pltpu.VMEM((LANES, LANES), F32)],
        compiler_params=_cp(("arbitrary",)),
        name="mix_out",
    )(x, cy, ao, nm, w_gate, wco, wao, wo, nf, wr, br, tri, cnt_in)


SUB = SUBLANES


def _wait_rows(block_ref, sem):
    pltpu.make_async_copy(block_ref, block_ref, sem).wait()


def _scatter_tile_rows(pos_ref, src_ref, dst_ref, sem, tm):
    def group(g, c):
        for u in range(SUB):
            row = dst_ref.at[pl.ds(pos_ref[0, 0, g * SUB + u], 1), :]
            pltpu.make_async_copy(src_ref.at[g, pl.ds(u, 1), :], row, sem).start()
        return c

    lax.fori_loop(0, tm // SUB, group, 0)
    _wait_rows(src_ref, sem)


CAST_ROWS = 16


def _scatter_body(fill_ref, na_ref, pos_ref, src_ref, wg_ref, wu_ref, wd_ref, dst_ref, wgb_ref, wub_ref, wdb_ref,
                  zero_ref, sem, *, tm, n_tiles):
    tile_rows = lambda t: dst_ref.at[pl.ds(pl.multiple_of(t * MOE_TM, MOE_TM), MOE_TM), :]

    @pl.when(pl.program_id(0) == 0)
    def _():
        zero_ref[...] = jnp.zeros(zero_ref.shape, F32)
        for b in range(N_BUCKETS):
            pltpu.make_async_copy(zero_ref, tile_rows(fill_ref[b]), sem).start()

        def tail_start(t, c):
            pltpu.make_async_copy(zero_ref, tile_rows(t), sem).start()
            return c

        lax.fori_loop(na_ref[0], n_tiles, tail_start, 0)
        for b in range(N_BUCKETS):
            pltpu.make_async_copy(zero_ref, tile_rows(fill_ref[b]), sem).wait()

        def tail_wait(t, c):
            pltpu.make_async_copy(zero_ref, tile_rows(t), sem).wait()
            return c

        lax.fori_loop(na_ref[0], n_tiles, tail_wait, 0)

    n_up, n_down = D_MODEL // CAST_ROWS, D_EXPERT // CAST_ROWS

    def cast(src, dst, chunk):
        rows = pl.ds(pl.multiple_of(chunk * CAST_ROWS, CAST_ROWS), CAST_ROWS)
        dst[0, rows, :] = src[0, rows, :].astype(BF16)

    def step(it, c):
        for g in (2 * it, 2 * it + 1):
            for u in range(SUB):
                row = dst_ref.at[pl.ds(pos_ref[0, 0, g * SUB + u], 1), :]
                pltpu.make_async_copy(src_ref.at[g, pl.ds(u, 1), :], row, sem).start()
        cast(wg_ref, wgb_ref, jnp.minimum(it, n_up - 1))
        cast(wu_ref, wub_ref, jnp.minimum(it, n_up - 1))
        cast(wd_ref, wdb_ref, jnp.minimum(it, n_down - 1))
        return c

    assert tm % (2 * SUB) == 0 and tm // (2 * SUB) >= n_up
    lax.fori_loop(0, tm // (2 * SUB), step, 0)
    _wait_rows(src_ref, sem)


def _scatter_rows(l, fill_tile, n_act, pos, src, n_tiles, wg, wu, wd):
    nt = N_EXPERTS
    tm = src.shape[0] * SUB // nt
    w_in_spec = lambda r, c: pl.BlockSpec((1, r, c), lambda i, *_: (l * N_EXPERTS + i, 0, 0))
    w_out_spec = lambda r, c: pl.BlockSpec((1, r, c), lambda i, *_: (i, 0, 0))
    grid_spec = pltpu.PrefetchScalarGridSpec(
        num_scalar_prefetch=2,
        grid=(nt,),
        in_specs=[pl.BlockSpec((1, 1, tm), lambda i, *_: (i, 0, 0), memory_space=pltpu.SMEM),
                  pl.BlockSpec((tm // SUB, SUB, ROW_W), lambda i, *_: (i, 0, 0)),
                  w_in_spec(D_MODEL, D_EXPERT), w_in_spec(D_MODEL, D_EXPERT), w_in_spec(D_EXPERT, D_MODEL)],
        out_specs=[pl.BlockSpec(memory_space=pl.ANY),
                   w_out_spec(D_MODEL, D_EXPERT), w_out_spec(D_MODEL, D_EXPERT), w_out_spec(D_EXPERT, D_MODEL)],
        scratch_shapes=[pltpu.VMEM((MOE_TM, ROW_W), F32), pltpu.SemaphoreType.DMA(())],
    )
    return pl.pallas_call(
        functools.partial(_scatter_body, tm=tm, n_tiles=n_tiles),
        grid_spec=grid_spec,
        out_shape=[jax.ShapeDtypeStruct((n_tiles * MOE_TM, ROW_W), F32),
                   jax.ShapeDtypeStruct((N_EXPERTS, D_MODEL, D_EXPERT), BF16),
                   jax.ShapeDtypeStruct((N_EXPERTS, D_MODEL, D_EXPERT), BF16),
                   jax.ShapeDtypeStruct((N_EXPERTS, D_EXPERT, D_MODEL), BF16)],
        compiler_params=_cp(("arbitrary",)),
        name="dispatch_scatter",
    )(fill_tile, n_act, pos, src, wg, wu, wd)


def _scatter_more_body(pos_ref, src_ref, dst_in_ref, dst_ref, sem, *, tm):
    del dst_in_ref
    _scatter_tile_rows(pos_ref, src_ref, dst_ref, sem, tm)


def _scatter_more_rows(pos, src, dst, tm):
    nt = src.shape[0] * SUB // tm
    return pl.pallas_call(
        functools.partial(_scatter_more_body, tm=tm),
        grid=(nt,),
        in_specs=[pl.BlockSpec((1, 1, tm), lambda i: (i, 0, 0), memory_space=pltpu.SMEM),
                  pl.BlockSpec((tm // SUB, SUB, ROW_W), lambda i: (i, 0, 0)),
                  pl.BlockSpec(memory_space=pl.ANY)],
        out_specs=pl.BlockSpec(memory_space=pl.ANY),
        out_shape=jax.ShapeDtypeStruct(dst.shape, dst.dtype),
        scratch_shapes=[pltpu.SemaphoreType.DMA(())],
        input_output_aliases={2: 0},
        compiler_params=_cp(("arbitrary",)),
        name="dispatch_scatter_more",
    )(pos, src, dst)


MOE_STEP_TILES = 2


def _moe_body(ta_ref, tb_ref, na_ref, xs_ref, nf_ref, *refs):
    del ta_ref, tb_ref
    w_refs, y_ref = refs[:-1], refs[-1]
    first_tile = pl.program_id(0) * MOE_STEP_TILES

    @pl.when(first_tile < na_ref[0])
    def _():
        staged = []
        for t in range(MOE_STEP_TILES):
            rows = slice(t * MOE_TM, (t + 1) * MOE_TM)
            x1 = xs_ref[rows, 0:D_MODEL]
            xb = ((x1 * xs_ref[rows, D_MODEL + 2:D_MODEL + 3]) * nf_ref[...]).astype(BF16)
            w = w_refs[6 * t:6 * t + 6]
            staged.append((x1, [(_dot(xb, w[3 * k][0]), _dot(xb, w[3 * k + 1][0])) for k in range(2)]))
        for t in range(MOE_STEP_TILES):
            rows = slice(t * MOE_TM, (t + 1) * MOE_TM)
            y, ups = staged[t]
            for k, (a, u) in enumerate(ups):
                hdn = (jax.nn.silu(a) * u) * xs_ref[rows, D_MODEL + k:D_MODEL + k + 1]
                y = y + _dot(hdn.astype(BF16), w_refs[6 * t + 3 * k + 2][0])
            y_ref[rows, :] = y

    @pl.when(first_tile >= na_ref[0])
    def _():
        y_ref[...] = jnp.zeros(y_ref.shape, F32)


def _moe_experts(l, tile_a, tile_b, n_act, xs, nf, wg, wu, wd):
    step_rows = MOE_STEP_TILES * MOE_TM
    assert xs.shape[0] % step_rows == 0
    last = lambda tile, na: jnp.minimum(tile, na[0] - 1)
    expert = lambda sel, t, i, ta, tb, na: (sel(ta, tb)[last(MOE_STEP_TILES * i + t, na)], 0, 0)
    w_up = lambda sel, t: pl.BlockSpec((1, D_MODEL, D_EXPERT), functools.partial(expert, sel, t))
    w_dn = lambda sel, t: pl.BlockSpec((1, D_EXPERT, D_MODEL), functools.partial(expert, sel, t))
    sa = lambda ta, tb: ta
    sb = lambda ta, tb: tb
    w_specs = []
    for t in range(MOE_STEP_TILES):
        w_specs += [w_up(sa, t), w_up(sa, t), w_dn(sa, t), w_up(sb, t), w_up(sb, t), w_dn(sb, t)]
    grid_spec = pltpu.PrefetchScalarGridSpec(
        num_scalar_prefetch=3,
        grid=(xs.shape[0] // step_rows,),
        in_specs=[pl.BlockSpec((step_rows, ROW_W), lambda i, ta, tb, na: (jnp.minimum(i, (na[0] - 1) // MOE_STEP_TILES), 0)),
                  _layer_spec(l, 1, D_MODEL)] + w_specs,
        out_specs=pl.BlockSpec((step_rows, D_MODEL), lambda i, ta, tb, na: (i, 0)),
    )
    return pl.pallas_call(
        _moe_body,
        grid_spec=grid_spec,
        out_shape=jax.ShapeDtypeStruct((xs.shape[0], D_MODEL), F32),
        compiler_params=_cp(("arbitrary",)),
        name="moe_experts",
    )(tile_a, tile_b, n_act, xs, nf, *((wg, wu, wd) * (2 * MOE_STEP_TILES)))


GATHER_SLOTS = 3


def _unpermute_body(pos_ref, ys_ref, o_ref, stage_ref, row_sems, out_sems, *, tm, n_steps):
    i = pl.program_id(0)
    groups = tm // SUB
    slot = lax.rem(i, GATHER_SLOTS)
    prev = lax.rem(i + GATHER_SLOTS - 1, GATHER_SLOTS)
    prev2 = lax.rem(i + GATHER_SLOTS - 2, GATHER_SLOTS)

    def writeback(step, frm):
        return pltpu.make_async_copy(stage_ref.at[frm], o_ref.at[pl.ds(step * groups, groups)], out_sems.at[frm])

    def group(g, c):
        for u in range(SUB):
            row = ys_ref.at[pl.ds(pos_ref[0, 0, g * SUB + u], 1), :]
            pltpu.make_async_copy(row, stage_ref.at[slot, g, pl.ds(u, 1), :], row_sems.at[slot]).start()
        return c

    lax.fori_loop(0, groups, group, 0)

    if n_steps > 1:
        @pl.when(i > 0)
        def _():
            _wait_rows(stage_ref.at[prev], row_sems.at[prev])
            writeback(i - 1, prev).start()

    if n_steps > 2:
        @pl.when(i > 1)
        def _():
            writeback(i - 2, prev2).wait()

    @pl.when(i == n_steps - 1)
    def _():
        _wait_rows(stage_ref.at[slot], row_sems.at[slot])
        writeback(i, slot).start()
        if n_steps > 1:
            writeback(i - 1, prev).wait()
        writeback(i, slot).wait()


def _unpermute(pos, ys, tm):
    t = pos.shape[0] * tm
    return pl.pallas_call(
        functools.partial(_unpermute_body, tm=tm, n_steps=t // tm),
        grid=(t // tm,),
        in_specs=[pl.BlockSpec((1, 1, tm), lambda i: (i, 0, 0), memory_space=pltpu.SMEM),
                  pl.BlockSpec(memory_space=pl.ANY)],
        out_specs=pl.BlockSpec(memory_space=pl.ANY),
        out_shape=jax.ShapeDtypeStruct((t // SUB, SUB, D_MODEL), F32),
        scratch_shapes=[pltpu.VMEM((GATHER_SLOTS, tm // SUB, SUB, D_MODEL), F32),
                        pltpu.SemaphoreType.DMA((GATHER_SLOTS,)), pltpu.SemaphoreType.DMA((GATHER_SLOTS,))],
        compiler_params=_cp(("arbitrary",)),
        name="moe_unpermute",
    )(pos, ys)


def _rope_lane_freq():
    half = ROT_DIM // 2
    inv_freq = jnp.float32(ROPE_THETA) ** (-jnp.arange(half, dtype=jnp.float32) * (2.0 / ROT_DIM))
    dim = np.arange(LANES) % HEAD_DIM
    return inv_freq[dim % half][None, :], dim < half, (dim >= half) & (dim < ROT_DIM)


def _rope_patterns(cos, sin, first, second):
    return (jnp.where(first | second, cos, 1.0), jnp.where(first, -sin, 0.0), jnp.where(second, sin, 0.0))


def _rope_tables(pos):
    freq, first, second = _rope_lane_freq()
    ang = pos.astype(jnp.float32)[:, None] * freq
    return _rope_patterns(jnp.cos(ang), jnp.sin(ang), first, second)


def _rope_tables_padded(n_blocks):
    freq, first, second = _rope_lane_freq()
    ang_a = (jnp.arange(n_blocks, dtype=jnp.int32) * BLOCK).astype(jnp.float32)[:, None] * freq
    ang_b = (jnp.arange(BLOCK, dtype=jnp.int32) - PAD).astype(jnp.float32)[:, None] * freq
    ca, sa = jnp.cos(ang_a)[:, None, :], jnp.sin(ang_a)[:, None, :]
    cb, sb = jnp.cos(ang_b)[None], jnp.sin(ang_b)[None]
    flat = lambda t: t.reshape(n_blocks * BLOCK, LANES)
    return _rope_patterns(flat(ca * cb - sa * sb), flat(sa * cb + ca * sb), first, second)


def _seg_ones(n):
    idx = np.arange(n) // HEAD_DIM
    return jnp.asarray(idx[:, None] == idx[None, :], BF16)


def _bucket_experts():
    ea, eb = [], []
    for g in range(N_GROUPS):
        for a in range(EXPERTS_PER_GROUP):
            for b in range(a + 1, EXPERTS_PER_GROUP):
                ea.append(g * EXPERTS_PER_GROUP + a)
                eb.append(g * EXPERTS_PER_GROUP + b)
    return np.asarray(ea, np.int32), np.asarray(eb, np.int32)


def _dispatch_plan(counts, n_tiles):
    padded = ((counts + MOE_TM - 1) // MOE_TM) * MOE_TM
    ends = jnp.cumsum(padded)
    offs = ends - padded
    n_act = jnp.maximum(ends[-1] // MOE_TM, 1)
    starts = jnp.arange(n_tiles, dtype=jnp.int32) * MOE_TM
    tile_bucket = jnp.minimum(jnp.sum(starts[:, None] >= ends[None, :], axis=1), N_BUCKETS - 1)
    ea, eb = _bucket_experts()
    onehot = tile_bucket[:, None] == jnp.arange(N_BUCKETS)[None, :]
    tile_a = jnp.sum(jnp.where(onehot, ea[None, :], 0), axis=1).astype(jnp.int32)
    tile_b = jnp.sum(jnp.where(onehot, eb[None, :], 0), axis=1).astype(jnp.int32)
    fill_tile = jnp.maximum(ends // MOE_TM - 1, 0).astype(jnp.int32)
    return offs, tile_a, tile_b, n_act.astype(jnp.int32).reshape(1), fill_tile


def _positions(offs, bucket, rank):
    onehot = bucket[..., None] == jnp.arange(N_BUCKETS, dtype=jnp.int32)
    return (jnp.sum(jnp.where(onehot, offs.astype(jnp.int32), 0), axis=-1) + rank).astype(jnp.int32)


def kernel(x_prompt, x_sample, cache_k, cache_v, state_conv, meta_tokens, norm_mix, w_in, conv_w, q_norm, k_norm,
           attn_sinks, w_conv_out, w_attn_out, w_o, norm_ffn, w_router_group, b_router_group, w_router_expert,
           b_router_expert, w_exp_gate, w_exp_up, w_exp_down):
    batch, seq, _ = x_prompt.shape
    depth = w_in.shape[0]
    n_dec = x_sample.shape[0]
    past_len = PAST_LEN
    lp = PAD + N_META + seq
    tm_in, tm_out, qb = 640, 640, 5
    tm_move, tm_last = 1664, 2048
    assert PAD + N_META == BLOCK and seq % BLOCK == 0 and tm_in % BLOCK == 0
    assert lp % tm_in == 0 and (batch * lp) % tm_out == 0 and lp % (qb * BLOCK) == 0
    assert (batch * lp) % tm_move == 0 and (batch * seq) % tm_last == 0
    assert x_sample.shape[1] == 1 and cache_k.shape[2] == WINDOW and past_len >= WINDOW

    t_prompt = batch * lp
    t_all = t_prompt + n_dec
    n_tiles = -(-(t_all + N_BUCKETS * (MOE_TM - 1)) // MOE_TM)
    n_tiles = -(-n_tiles // MOE_STEP_TILES) * MOE_STEP_TILES

    meta = jnp.broadcast_to(meta_tokens[None].astype(F32), (batch, N_META, D_MODEL))
    head = jnp.concatenate([jnp.zeros((batch, PAD, D_MODEL), F32), meta], axis=1)
    xs = x_sample.reshape(n_dec, D_MODEL)

    rope_p = _rope_tables_padded(lp // BLOCK)
    rope_s = _rope_tables(jnp.full((1,), past_len, jnp.int32))
    s256, s128 = _seg_ones(256), _seg_ones(LANES)
    tri_p = jnp.asarray(np.triu(np.ones((tm_out, tm_out)), 1), BF16)
    tri_s = jnp.asarray(np.triu(np.ones((n_dec, n_dec)), 1), BF16)
    zero_cnt = jnp.zeros((LANES, LANES), F32)

    w_mix = w_in.astype(BF16)
    w_gate = w_in[:, :, D_MIX:].astype(BF16)
    wco, wao, wo = w_conv_out.astype(BF16), w_attn_out.astype(BF16), w_o.astype(BF16)
    wg32 = w_exp_gate.reshape(depth * N_EXPERTS, D_MODEL, D_EXPERT)
    wu32 = w_exp_up.reshape(depth * N_EXPERTS, D_MODEL, D_EXPERT)
    wd32 = w_exp_down.reshape(depth * N_EXPERTS, D_EXPERT, D_MODEL)
    nm, nf = norm_mix.reshape(depth, 1, D_MODEL), norm_ffn.reshape(depth, 1, D_MODEL)
    qn = (jnp.tile(q_norm, (1, N_HEADS)) * Q_SCALE).reshape(depth, 1, D_Q)
    kn = jnp.tile(k_norm, (1, N_KV_HEADS)).reshape(depth, 1, D_KV)
    r_pad = LANES - N_GROUPS - N_EXPERTS
    wr = jnp.concatenate([w_router_group, w_router_expert, jnp.zeros((depth, D_MODEL, r_pad), F32)], axis=-1).astype(BF16)
    br = jnp.concatenate([b_router_group, b_router_expert, jnp.zeros((depth, r_pad), F32)], axis=-1).reshape(depth, 1, LANES)
    sinks = attn_sinks.astype(F32) * LOG2E
    sinkb = jnp.broadcast_to(sinks[:, :, None], (depth, N_HEADS, LANES))
    ck = cache_k.reshape(depth, n_dec, WINDOW, D_KV)
    cv = cache_v.reshape(depth, n_dec, WINDOW, D_KV)

    outs = {k: [] for k in ("kp", "vp", "cp", "ks", "vs", "cs")}
    for l in range(depth):
        if l == 0:
            cy, q, k, v, ulast, xp = _prompt_in(l, x_prompt, nm, w_mix, conv_w, qn, kn, rope_p, s256, s128, tm_in, head)
        else:
            cy, q, k, v, ulast = _prompt_in(l, xp, nm, w_mix, conv_w, qn, kn, rope_p, s256, s128, tm_in)
        ao = _prompt_attn(l, q, k, v, sinks, qb)
        xep, rankp, bktp, cnt = _mix_out(
            l, xp.reshape(t_prompt, D_MODEL), cy.reshape(t_prompt, D_CONV), ao.reshape(t_prompt, D_Q),
            nm, w_gate, wco, wao, wo, nf, wr, br, tri_p, zero_cnt, tm_out)
        outs["kp"].append(k[:, lp - WINDOW:].reshape(batch, WINDOW, N_KV_HEADS, HEAD_DIM))
        outs["vp"].append(v[:, lp - WINDOW:].reshape(batch, WINDOW, N_KV_HEADS, HEAD_DIM))
        outs["cp"].append(ulast[:, 8 - (CONV_W - 1):])

        c0, c1 = state_conv[l, :, 0, :], state_conv[l, :, 1, :]
        cys, qx, ksn, vsn, us = _sample_in(l, xs, c0, c1, nm, w_mix, conv_w, qn, kn, rope_s, s256, s128)
        ox, nk, nv = _sample_attn(l, jnp.transpose(qx, (1, 0, 2)), sinkb, ck, cv, ksn, vsn)
        ox = ox.reshape(n_dec, N_KV_HEADS, GQA, N_KV_HEADS, HEAD_DIM)
        aos = jnp.stack([ox[:, j, :, j, :] for j in range(N_KV_HEADS)], axis=1).reshape(n_dec, D_Q).astype(BF16)
        xes, ranks, bkts, cnt = _mix_out(l, xs, cys, aos, nm, w_gate, wco, wao, wo, nf, wr, br, tri_s, cnt, n_dec)
        outs["ks"].append(nk.reshape(n_dec, WINDOW, N_KV_HEADS, HEAD_DIM))
        outs["vs"].append(nv.reshape(n_dec, WINDOW, N_KV_HEADS, HEAD_DIM))
        outs["cs"].append(jnp.stack([c1, us], axis=1))

        counts = cnt[:N_BUCKETS, 0].astype(jnp.int32)
        offs, tile_a, tile_b, n_act, fill_tile = _dispatch_plan(counts, n_tiles)
        posp = _positions(offs, bktp, rankp)
        poss = _positions(offs, bkts, ranks)
        by8 = lambda a: a.reshape(a.shape[0] // SUB, SUB, a.shape[1])
        sorted_rows, wg, wu, wd = _scatter_rows(l, fill_tile, n_act, posp.reshape(N_EXPERTS, 1, t_prompt // N_EXPERTS),
                                                by8(xep), n_tiles, wg32, wu32, wd32)
        posp = posp.reshape(t_prompt // tm_move, 1, tm_move)
        sorted_rows = _scatter_more_rows(poss, by8(xes), sorted_rows, n_dec)
        ys = _moe_experts(l, tile_a, tile_b, n_act, sorted_rows, nf, wg, wu, wd)
        xs = _unpermute(poss, ys, n_dec).reshape(n_dec, D_MODEL)
        if l + 1 < depth:
            xp = _unpermute(posp, ys, tm_move).reshape(batch, lp, D_MODEL)
        else:
            pos_tok = posp.reshape(batch, lp)[:, PAD + N_META:].reshape(batch * seq // tm_last, 1, tm_last)
            y_prompt = _unpermute(pos_tok, ys, tm_last).reshape(batch, seq, D_MODEL)

    y_sample = xs.reshape(n_dec, 1, D_MODEL)
    st = lambda k: jnp.stack(outs[k])
    return (y_prompt, y_sample, st("kp"), st("vp"), st("cp"), st("ks"), st("vs"), st("cs"))
```

```python
import functools
import math

import jax
import jax.numpy as jnp
import numpy as np
from jax import lax
from jax.experimental import pallas as pl
from jax.experimental.pallas import tpu as pltpu

D_MODEL = 1024
N_META = 16
D_CONV = D_MODEL
CONV_W = 3
N_HEADS = 16
N_KV_HEADS = 2
HEAD_DIM = 64
GQA = N_HEADS // N_KV_HEADS
ROT_DIM = HEAD_DIM // 4
ROPE_THETA = 500000.0
WINDOW = 128
PAST_LEN = 8192
BLOCK = 128
N_GROUPS = 4
EXPERTS_PER_GROUP = 4
N_EXPERTS = N_GROUPS * EXPERTS_PER_GROUP
D_EXPERT = 512
EPS = 1e-6
NEG = -1e30
D_Q = N_HEADS * HEAD_DIM
D_KV = N_KV_HEADS * HEAD_DIM
C_B, C_C, C_HC = 0, D_CONV, 2 * D_CONV
C_Q = 3 * D_CONV
C_K = C_Q + D_Q
C_V = C_K + D_KV
C_G = C_V + D_KV
D_MIX = C_G
D_IN = C_G + 2 * D_MODEL

LANES = 128
SUBLANES = 8
PAD = (-N_META) % BLOCK
N_PAIRS = 6
N_BUCKETS = N_GROUPS * N_PAIRS
MOE_TM = 256
ROW_W = D_MODEL + LANES
LOG2E = math.log2(math.e)
Q_SCALE = HEAD_DIM ** -0.5 * LOG2E

F32 = jnp.float32
BF16 = jnp.bfloat16
VMEM_LIMIT = 56 * 1024 * 1024


def _cp(sem, vmem=VMEM_LIMIT):
    return pltpu.CompilerParams(dimension_semantics=sem, vmem_limit_bytes=vmem)


def _const_spec(shape):
    nd = len(shape)
    return pl.BlockSpec(shape, lambda *_: (0,) * nd, pipeline_mode=pl.Buffered(1))


def _layer_spec(l, *shape):
    n = len(shape)
    return pl.BlockSpec((None,) + shape, lambda *_: (l,) + (0,) * n, pipeline_mode=pl.Buffered(1))


def _w_mix_spec(l):
    return pl.BlockSpec((None, D_MODEL, D_MIX), lambda *_: (l, 0, 0), pipeline_mode=pl.Buffered(1))


def _dot(a, b):
    return jnp.dot(a, b, preferred_element_type=F32)


def _seg_mean_sq(x, seg_ones):
    return _dot((x * x).astype(BF16), seg_ones) * (1.0 / HEAD_DIM)


def _rope128(t, cos, sin_pm, first):
    partner = jnp.where(first, pltpu.roll(t, LANES - ROT_DIM // 2, 1), pltpu.roll(t, ROT_DIM // 2, 1))
    return t * cos + partner * sin_pm


def _rms_scale(x):
    return lax.rsqrt(jnp.mean(x * x, axis=-1, keepdims=True) + EPS)


def _rms_rows(x, g):
    return (x * _rms_scale(x)) * g


def _qk_project(hb, w_ref, s256, s128):
    qs = [_dot(hb, w_ref[:, C_Q + c * 256:C_Q + (c + 1) * 256]) for c in range(D_Q // 256)]
    kv = _dot(hb, w_ref[:, C_K:C_K + 2 * D_KV])
    kc, v = kv[:, 0:D_KV], kv[:, D_KV:]
    return qs, [_seg_mean_sq(qc, s256) for qc in qs], kc, _seg_mean_sq(kc, s128), v


def _qk_finish(proj, qn, kn, cos, sneg, spos, store_q):
    qs, q_ms, kc, k_ms, _ = proj
    sin_pm = sneg + spos
    lane = lax.broadcasted_iota(jnp.int32, cos.shape, 1)
    first = lax.bitwise_and(lane, HEAD_DIM - 1) < ROT_DIM // 2
    for c, (qc, ms) in enumerate(zip(qs, q_ms)):
        qc = (qc * lax.rsqrt(ms + EPS)) * qn[:, c * 256:(c + 1) * 256]
        for s in range(2):
            r = _rope128(qc[:, s * LANES:(s + 1) * LANES], cos, sin_pm, first)
            store_q(2 * c + s, r.astype(BF16))
    kc = (kc * lax.rsqrt(k_ms + EPS)) * kn
    return _rope128(kc, cos, sin_pm, first)


def _in_body(x_ref, nm_ref, w_ref, cw_ref, qn_ref, kn_ref, cos_ref, sneg_ref, spos_ref,
             s256_ref, s128_ref, cy_ref, q_ref, k_ref, v_ref, ul_ref, us_ref, *, tm, parts):
    i = pl.program_id(1)
    th = tm // parts

    @pl.when(i == 0)
    def _():
        us_ref[0:8, :] = jnp.zeros((8, D_CONV), F32)

    projs = []
    for h in range(parts):
        r0 = h * th
        hb = _rms_rows(x_ref[0, r0:r0 + th, :], nm_ref[...]).astype(BF16)
        projs.append(_qk_project(hb, w_ref, s256_ref[...], s128_ref[...]))
        u = _dot(hb, w_ref[:, C_C:C_C + D_CONV]) * _dot(hb, w_ref[:, C_HC:C_HC + D_CONV])
        row = lax.broadcasted_iota(jnp.int32, (th, 1), 0) + (i * tm + r0)
        u = jnp.where(row >= PAD, u, 0.0)
        us_ref[8 + r0:8 + r0 + th, :] = u
        conv = (us_ref[6 + r0:6 + r0 + th, :] * cw_ref[0:1, :] + us_ref[7 + r0:7 + r0 + th, :] * cw_ref[1:2, :]) \
            + u * cw_ref[2:3, :]
        cy_ref[0, r0:r0 + th, :] = (_dot(hb, w_ref[:, C_B:C_B + D_CONV]) * conv).astype(BF16)
        v_ref[0, r0:r0 + th, :] = projs[h][4]

    last = us_ref[tm:tm + 8, :]
    ul_ref[0] = last
    us_ref[0:8, :] = last

    for h in range(parts):
        rows = slice(h * th, (h + 1) * th)

        def store_q(slab, val, rows=rows):
            q_ref[0, rows, slab * LANES:(slab + 1) * LANES] = val

        k_ref[0, rows, :] = _qk_finish(projs[h], qn_ref[...], kn_ref[...], cos_ref[rows, :], sneg_ref[rows, :],
                                       spos_ref[rows, :], store_q)


def _in_first_body(head_ref, *refs, tm, parts):
    nb = tm // BLOCK
    blocks, rest, xpad_ref, us_ref = refs[:nb], refs[nb:-2], refs[-2], refs[-1]
    first = pl.program_id(1) == 0
    xpad_ref[0, 0:BLOCK, :] = jnp.where(first, head_ref[0], blocks[0][0])
    for j in range(1, nb):
        xpad_ref[0, j * BLOCK:(j + 1) * BLOCK, :] = blocks[j][0]
    _in_body(xpad_ref, *rest, us_ref, tm=tm, parts=parts)


def _prompt_in(l, x, nm, w_mix, cw, qn, kn, rope, s256, s128, tm, head=None):
    b = x.shape[0]
    lp = x.shape[1] if head is None else x.shape[1] + BLOCK
    nt = lp // tm
    nb = tm // BLOCK
    cos, sneg, spos = rope
    tok = lambda w: pl.BlockSpec((1, tm, w), lambda bi, i: (bi, i, 0))
    tab = pl.BlockSpec((tm, LANES), lambda bi, i: (i, 0))
    params = [_layer_spec(l, 1, D_MODEL), _w_mix_spec(l),
              _layer_spec(l, CONV_W, D_CONV), _layer_spec(l, 1, D_Q), _layer_spec(l, 1, D_KV),
              tab, tab, tab, _const_spec((256, 256)), _const_spec((LANES, LANES))]
    out_specs = [tok(D_CONV), tok(D_Q), tok(D_KV), tok(D_KV), pl.BlockSpec((1, 8, D_CONV), lambda bi, i: (bi, 0, 0))]
    out_shape = [jax.ShapeDtypeStruct((b, lp, D_CONV), BF16), jax.ShapeDtypeStruct((b, lp, D_Q), BF16),
                 jax.ShapeDtypeStruct((b, lp, D_KV), F32), jax.ShapeDtypeStruct((b, lp, D_KV), F32),
                 jax.ShapeDtypeStruct((b, 8, D_CONV), F32)]
    if head is None:
        body, x_specs, x_args = _in_body, [tok(D_MODEL)], (x,)
    else:
        blk = lambda j: pl.BlockSpec((1, BLOCK, D_MODEL), lambda bi, i: (bi, jnp.maximum(nb * i - 1 + j, 0), 0))
        body = _in_first_body
        x_specs = [pl.BlockSpec((1, BLOCK, D_MODEL), lambda bi, i: (bi, 0, 0))] + [blk(j) for j in range(nb)]
        x_args = (head,) + (x,) * nb
        out_specs.append(tok(D_MODEL))
        out_shape.append(jax.ShapeDtypeStruct((b, lp, D_MODEL), F32))
    return pl.pallas_call(
        functools.partial(body, tm=tm, parts=2),
        grid=(b, nt),
        in_specs=x_specs + params,
        out_specs=out_specs,
        out_shape=out_shape,
        scratch_shapes=[pltpu.VMEM((tm + 8, D_CONV), F32)],
        compiler_params=_cp(("arbitrary", "arbitrary")),
        name="prompt_in",
    )(*x_args, nm, w_mix, cw, qn, kn, cos, sneg, spos, s256, s128)


def _sin_body(x_ref, c0_ref, c1_ref, nm_ref, w_ref, cw_ref, qn_ref, kn_ref, cos_ref, sneg_ref,
              spos_ref, s256_ref, s128_ref, cy_ref, qx_ref, k_ref, v_ref, u_ref):
    hb = _rms_rows(x_ref[...], nm_ref[...]).astype(BF16)
    u = _dot(hb, w_ref[:, C_C:C_C + D_CONV]) * _dot(hb, w_ref[:, C_HC:C_HC + D_CONV])
    u_ref[...] = u
    conv = (c0_ref[...] * cw_ref[0:1, :] + c1_ref[...] * cw_ref[1:2, :]) + u * cw_ref[2:3, :]
    cy_ref[...] = (_dot(hb, w_ref[:, C_B:C_B + D_CONV]) * conv).astype(BF16)

    lane = lax.broadcasted_iota(jnp.int32, (x_ref.shape[0], LANES), 1)
    low = lane < HEAD_DIM

    def store_q(slab, val):
        valf = val.astype(F32)
        swapped = pltpu.roll(valf, HEAD_DIM, 1)
        zero = jnp.zeros_like(valf)
        for h in (2 * slab, 2 * slab + 1):
            src = valf if (h % 2) == (h // GQA) else swapped
            keep = low if (h // GQA) == 0 else jnp.logical_not(low)
            qx_ref[h] = jnp.where(keep, src, zero).astype(BF16)

    cos = jnp.broadcast_to(cos_ref[...], (x_ref.shape[0], LANES))
    sneg = jnp.broadcast_to(sneg_ref[...], (x_ref.shape[0], LANES))
    spos = jnp.broadcast_to(spos_ref[...], (x_ref.shape[0], LANES))
    proj = _qk_project(hb, w_ref, s256_ref[...], s128_ref[...])
    k_ref[...] = _qk_finish(proj, qn_ref[...], kn_ref[...], cos, sneg, spos, store_q)
    v_ref[...] = proj[4]


def _sample_in(l, x, c0, c1, nm, w_mix, cw, qn, kn, rope, s256, s128):
    n = x.shape[0]
    cos, sneg, spos = rope
    full = lambda *s: pl.BlockSpec(s, lambda i: (0,) * len(s))
    return pl.pallas_call(
        _sin_body,
        grid=(1,),
        in_specs=[full(n, D_MODEL), full(n, D_CONV), full(n, D_CONV), _layer_spec(l, 1, D_MODEL),
                  _w_mix_spec(l), _layer_spec(l, CONV_W, D_CONV), _layer_spec(l, 1, D_Q),
                  _layer_spec(l, 1, D_KV), full(1, LANES), full(1, LANES), full(1, LANES), full(256, 256),
                  full(LANES, LANES)],
        out_specs=[full(n, D_CONV), full(N_HEADS, n, LANES), full(n, D_KV), full(n, D_KV), full(n, D_CONV)],
        out_shape=[jax.ShapeDtypeStruct((n, D_CONV), BF16), jax.ShapeDtypeStruct((N_HEADS, n, LANES), BF16),
                   jax.ShapeDtypeStruct((n, D_KV), F32), jax.ShapeDtypeStruct((n, D_KV), F32),
                   jax.ShapeDtypeStruct((n, D_CONV), F32)],
        compiler_params=_cp(("arbitrary",)),
        name="sample_in",
    )(x, c0, c1, nm, w_mix, cw, qn, kn, cos, sneg, spos, s256, s128)


def _attn_body(sink_ref, q_ref, kp_ref, kc_ref, vp_ref, vc_ref, o_ref, ke_ref, ko_ref, vt_ref, *, qb, l):
    i = pl.program_id(1)
    lane = lax.broadcasted_iota(jnp.int32, (BLOCK, LANES), 1)
    low = lane < HEAD_DIM

    def prep_k(src, blk0, nblk):
        for t in range(nblk):
            blk = src[0, t * BLOCK:(t + 1) * BLOCK, :]
            swp = pltpu.roll(blk, HEAD_DIM, 1)
            zero = jnp.zeros_like(blk)
            rows = slice((blk0 + t) * BLOCK, (blk0 + t + 1) * BLOCK)
            ke_ref[0, rows, :] = jnp.where(low, blk, zero).astype(BF16)
            ko_ref[0, rows, :] = jnp.where(low, zero, swp).astype(BF16)
            ke_ref[1, rows, :] = jnp.where(low, swp, zero).astype(BF16)
            ko_ref[1, rows, :] = jnp.where(low, zero, blk).astype(BF16)

    def prep_v(src, blk0, nblk):
        for t in range(nblk):
            vt = jnp.transpose(src[0, t * BLOCK:(t + 1) * BLOCK, :]).astype(BF16)
            for j in range(N_KV_HEADS):
                vt_ref[blk0 + t, j] = vt[j * HEAD_DIM:(j + 1) * HEAD_DIM, :]

    prep_k(kp_ref, 0, 1)
    prep_k(kc_ref, 1, qb)
    prep_v(vp_ref, 0, 1)
    prep_v(vc_ref, 1, qb)

    c = lax.broadcasted_iota(jnp.int32, (2 * BLOCK, BLOCK), 0)
    r = lax.broadcasted_iota(jnp.int32, (2 * BLOCK, BLOCK), 1)
    diff = r - (c - BLOCK)
    band = (diff >= 0) & (diff < WINDOW)
    nt = (((1,), (1,)), ((), ()))

    def one_block(b, carry):
        r0 = pl.multiple_of(b * BLOCK, BLOCK)
        kpos = (i * qb + b) * BLOCK + c - BLOCK - PAD
        bias = jnp.where(band & (kpos >= 0), 0.0, NEG)
        for m in range(N_HEADS // 2):
            j = (2 * m) // GQA
            q2 = q_ref[0, pl.ds(r0, BLOCK), m * LANES:(m + 1) * LANES]
            halves = []
            for par, k_ref in ((0, ke_ref), (1, ko_ref)):
                st = lax.dot_general(k_ref[j, pl.ds(r0, 2 * BLOCK), :], q2, nt, preferred_element_type=F32)
                st = st + bias
                sink = sink_ref[l, 2 * m + par]
                mx = jnp.maximum(jnp.max(st, axis=0, keepdims=True), sink)
                p = jnp.exp2(st - mx)
                den = jnp.sum(p, axis=0, keepdims=True) + jnp.exp2(sink - mx)
                pb = p.astype(BF16)
                ot = _dot(vt_ref[b, j], pb[0:BLOCK]) + _dot(vt_ref[b + 1, j], pb[BLOCK:])
                halves.append(ot * (1.0 / den))
            o2 = jnp.transpose(jnp.concatenate(halves, axis=0))
            o_ref[0, pl.ds(r0, BLOCK), m * LANES:(m + 1) * LANES] = o2.astype(BF16)
        return carry

    lax.fori_loop(0, qb, one_block, 0, unroll=True)


def _prompt_attn(l, q, k, v, sinks, qb):
    b, lp, _ = q.shape
    nsteps = lp // (qb * BLOCK)
    cur = lambda w: pl.BlockSpec((1, qb * BLOCK, w), lambda bi, i: (bi, i, 0))
    prev = pl.BlockSpec((1, BLOCK, D_KV), lambda bi, i: (bi, jnp.maximum(i * qb - 1, 0), 0))
    ext = ((qb + 1) * BLOCK, LANES)
    return pl.pallas_call(
        functools.partial(_attn_body, qb=qb, l=l),
        grid=(b, nsteps),
        in_specs=[pl.BlockSpec(memory_space=pltpu.SMEM), cur(D_Q), prev, cur(D_KV), prev, cur(D_KV)],
        out_specs=cur(D_Q),
        out_shape=jax.ShapeDtypeStruct((b, lp, D_Q), BF16),
        scratch_shapes=[pltpu.VMEM((N_KV_HEADS,) + ext, BF16), pltpu.VMEM((N_KV_HEADS,) + ext, BF16),
                        pltpu.VMEM((qb + 1, N_KV_HEADS, HEAD_DIM, BLOCK), BF16)],
        compiler_params=_cp(("arbitrary", "arbitrary")),
        name="prompt_attn",
    )(sinks, q, k, k, v, v)


def _sattn_body(qx_ref, sink_ref, ck_ref, cv_ref, kn_ref, vn_ref, ox_ref, nk_ref, nv_ref, *, tb):
    def window(c_ref, n_ref, t):
        return jnp.concatenate([c_ref[t, 1:WINDOW, :], n_ref[t:t + 1, :]], axis=0)

    for t in range(tb):
        nk_ref[t] = window(ck_ref, kn_ref, t)
        nv_ref[t] = window(cv_ref, vn_ref, t)
    nt = (((1,), (1,)), ((), ()))
    s = jnp.concatenate([lax.dot_general(qx_ref[t], window(ck_ref, kn_ref, t).astype(BF16), nt,
                                         preferred_element_type=F32) for t in range(tb)], axis=0)
    sink = jnp.concatenate([sink_ref[...][:, 0:1]] * tb, axis=0)
    m = jnp.maximum(jnp.max(s, axis=-1, keepdims=True), sink)
    p = jnp.exp2(s - m)
    rden = 1.0 / (jnp.sum(p, axis=-1, keepdims=True) + jnp.exp2(sink - m))
    pb = p.astype(BF16)
    for t in range(tb):
        rows = slice(t * N_HEADS, (t + 1) * N_HEADS)
        ox_ref[t] = _dot(pb[rows], window(cv_ref, vn_ref, t).astype(BF16)) * rden[rows]


def _sample_attn(l, qx, sinkb, ck, cv, kn, vn, tb=16):
    n = qx.shape[0]
    blk3 = lambda a, c: pl.BlockSpec((tb, a, c), lambda i: (i, 0, 0))
    cache = pl.BlockSpec((None, tb, WINDOW, D_KV), lambda i: (l, i, 0, 0))
    row = pl.BlockSpec((tb, D_KV), lambda i: (i, 0))
    return pl.pallas_call(
        functools.partial(_sattn_body, tb=tb),
        grid=(n // tb,),
        in_specs=[blk3(N_HEADS, LANES), _layer_spec(l, N_HEADS, LANES), cache, cache, row, row],
        out_specs=[blk3(N_HEADS, LANES), blk3(WINDOW, D_KV), blk3(WINDOW, D_KV)],
        out_shape=[jax.ShapeDtypeStruct((n, N_HEADS, LANES), F32),
                   jax.ShapeDtypeStruct((n, WINDOW, D_KV), F32), jax.ShapeDtypeStruct((n, WINDOW, D_KV), F32)],
        compiler_params=_cp(("arbitrary",)),
        name="sample_attn",
    )(qx, sinkb, ck, cv, kn, vn)


ROUTE_ROWS = 24
RINV_LANE = LANES - 1


def _route(lt):
    top = lt[0:ROUTE_ROWS, :]
    rows = top.shape[1]
    rowf = lax.broadcasted_iota(jnp.int32, top.shape, 0).astype(F32)
    big = jnp.float32(3e38)
    far = jnp.float32(LANES)
    cmax = lambda a: jnp.max(a, axis=0, keepdims=True)
    cmin = lambda a: jnp.min(a, axis=0, keepdims=True)

    gmask = rowf < N_GROUPS
    gl = jnp.where(gmask, top, -big)
    gmax = cmax(gl)
    grp = cmin(jnp.where(gmask & (gl == gmax), rowf, far))
    p_grp = 1.0 / jnp.sum(jnp.where(gmask, jnp.exp(gl - gmax), 0.0), axis=0, keepdims=True)

    e_lo = N_GROUPS + EXPERTS_PER_GROUP * grp
    emask = (rowf >= e_lo) & (rowf < e_lo + EXPERTS_PER_GROUP)
    el = jnp.where(emask, top, -big)
    v1 = cmax(el)
    i1 = cmin(jnp.where(emask & (el == v1), rowf, far))
    rest = emask & (rowf != i1)
    el2 = jnp.where(rest, top, -big)
    v2 = cmax(el2)
    i2 = cmin(jnp.where(rest & (el2 == v2), rowf, far))
    e = jnp.exp(v2 - v1)
    w1 = (1.0 / (1.0 + e)) * p_grp
    w2 = (e / (1.0 + e)) * p_grp
    first_low = i1 < i2
    ea = jnp.where(first_low, i1, i2) - e_lo
    eb = jnp.where(first_low, i2, i1) - e_lo
    w_a = jnp.where(first_low, w1, w2)
    w_b = jnp.where(first_low, w2, w1)
    pair = jnp.where(ea == 0.0, 0.0, jnp.where(ea == 1.0, 3.0, 5.0)) + (eb - ea - 1.0)
    bucket = grp * N_PAIRS + pair

    r8 = lax.broadcasted_iota(jnp.int32, (SUBLANES, rows), 0)
    head = jnp.where(r8 == 0, w_a, jnp.where(r8 == 1, w_b, jnp.where(r8 == 2, lt[RINV_LANE:RINV_LANE + 1, :], 0.0)))
    return bucket, jnp.concatenate([head, jnp.zeros((LANES - SUBLANES, rows), F32)], axis=0)


def _out_body(x_ref, cy_ref, ao_ref, nm_ref, wg_ref, wco_ref, wao_ref, wo_ref, nf_ref, wr_ref, br_ref,
              tri_ref, cin_ref, xe_ref, rank_ref, bkt_ref, cnt_ref, run_ref, *, tm, parts):
    i = pl.program_id(0)
    th = tm // parts
    stage = []
    for h in range(parts):
        rows = slice(h * th, (h + 1) * th)
        hb = _rms_rows(x_ref[rows, :], nm_ref[...]).astype(BF16)
        stage.append((_dot(cy_ref[rows, :], wco_ref[...]), _dot(ao_ref[rows, :], wao_ref[...]),
                      _dot(hb, wg_ref[:, 0:D_MODEL]), _dot(hb, wg_ref[:, D_MODEL:])))
    for h, (ya, yb, gc, ga) in enumerate(stage):
        rows = slice(h * th, (h + 1) * th)
        mix = jax.nn.sigmoid(gc) * ya + jax.nn.sigmoid(ga) * yb
        xe_ref[rows, 0:D_MODEL] = x_ref[rows, :] + _dot(mix.astype(BF16), wo_ref[...])
    x1 = xe_ref[:, 0:D_MODEL]

    rinv = _rms_scale(x1)
    xnb = ((x1 * rinv) * nf_ref[...]).astype(BF16)
    logits = _dot(xnb, wr_ref[...]) + br_ref[...]
    lane = lax.broadcasted_iota(jnp.int32, (tm, LANES), 1)
    bucket, meta_t = _route(jnp.transpose(jnp.where(lane == RINV_LANE, rinv, logits)))
    xe_ref[:, D_MODEL:] = jnp.transpose(meta_t)

    @pl.when(i == 0)
    def _():
        run_ref[...] = cin_ref[...]

    sub = lax.broadcasted_iota(jnp.int32, (LANES, tm), 0).astype(F32)
    oht = (sub == bucket).astype(F32)
    before = _dot(oht.astype(BF16), tri_ref[...]) + run_ref[:, 0:1]
    rank_ref[0] = jnp.sum(oht * before, axis=0, keepdims=True).astype(jnp.int32)
    bkt_ref[0] = bucket.astype(jnp.int32)
    run_ref[...] = run_ref[...] + jnp.sum(oht, axis=-1, keepdims=True)
    cnt_ref[...] = run_ref[...]


def _mix_out(l, x, cy, ao, nm, w_gate, wco, wao, wo, nf, wr, br, tri, cnt_in, tm):
    t = x.shape[0]
    nt = t // tm
    tok = lambda w: pl.BlockSpec((tm, w), lambda i: (i, 0))
    rowi = pl.BlockSpec((1, 1, tm), lambda i: (i, 0, 0))
    sq = (D_MODEL, D_MODEL)
    return pl.pallas_call(
        functools.partial(_out_body, tm=tm, parts=2 if tm >= 4 * LANES else 1),
        grid=(nt,),
        in_specs=[tok(D_MODEL), tok(D_CONV), tok(D_Q), _layer_spec(l, 1, D_MODEL),
                  _layer_spec(l, D_MODEL, 2 * D_MODEL), _layer_spec(l, *sq), _layer_spec(l, *sq), _layer_spec(l, *sq),
                  _layer_spec(l, 1, D_MODEL), _layer_spec(l, D_MODEL, LANES), _layer_spec(l, 1, LANES),
                  _const_spec((tm, tm)), _const_spec((LANES, LANES))],
        out_specs=[tok(ROW_W), rowi, rowi, pl.BlockSpec((LANES, LANES), lambda i: (0, 0))],
        out_shape=[jax.ShapeDtypeStruct((t, ROW_W), F32),
                   jax.ShapeDtypeStruct((nt, 1, tm), jnp.int32), jax.ShapeDtypeStruct((nt, 1, tm), jnp.int32),
                   jax.ShapeDtypeStruct((LANES, LANES), F32)],
        scratch_shapes=[pltpu.VMEM((LANES, LANES), F32)],
        compiler_params=_cp(("arbitrary",)),
        name="mix_out",
    )(x, cy, ao, nm, w_gate, wco, wao, wo, nf, wr, br, tri, cnt_in)


SUB = SUBLANES


def _wait_rows(block_ref, sem):
    pltpu.make_async_copy(block_ref, block_ref, sem).wait()


def _scatter_tile_rows(pos_ref, src_ref, dst_ref, sem, tm):
    def group(g, c):
        for u in range(SUB):
            row = dst_ref.at[pl.ds(pos_ref[0, 0, g * SUB + u], 1), :]
            pltpu.make_async_copy(src_ref.at[g, pl.ds(u, 1), :], row, sem).start()
        return c

    lax.fori_loop(0, tm // SUB, group, 0)
    _wait_rows(src_ref, sem)


CAST_ROWS = 16


def _scatter_body(fill_ref, na_ref, pos_ref, src_ref, wg_ref, wu_ref, wd_ref, dst_ref, wgb_ref, wub_ref, wdb_ref,
                  zero_ref, sem, *, tm, n_tiles):
    tile_rows = lambda t: dst_ref.at[pl.ds(pl.multiple_of(t * MOE_TM, MOE_TM), MOE_TM), :]

    @pl.when(pl.program_id(0) == 0)
    def _():
        zero_ref[...] = jnp.zeros(zero_ref.shape, F32)
        for b in range(N_BUCKETS):
            pltpu.make_async_copy(zero_ref, tile_rows(fill_ref[b]), sem).start()

        def tail_start(t, c):
            pltpu.make_async_copy(zero_ref, tile_rows(t), sem).start()
            return c

        lax.fori_loop(na_ref[0], n_tiles, tail_start, 0)
        for b in range(N_BUCKETS):
            pltpu.make_async_copy(zero_ref, tile_rows(fill_ref[b]), sem).wait()

        def tail_wait(t, c):
            pltpu.make_async_copy(zero_ref, tile_rows(t), sem).wait()
            return c

        lax.fori_loop(na_ref[0], n_tiles, tail_wait, 0)

    n_up, n_down = D_MODEL // CAST_ROWS, D_EXPERT // CAST_ROWS

    def cast(src, dst, chunk):
        rows = pl.ds(pl.multiple_of(chunk * CAST_ROWS, CAST_ROWS), CAST_ROWS)
        dst[0, rows, :] = src[0, rows, :].astype(BF16)

    def step(it, c):
        for g in (2 * it, 2 * it + 1):
            for u in range(SUB):
                row = dst_ref.at[pl.ds(pos_ref[0, 0, g * SUB + u], 1), :]
                pltpu.make_async_copy(src_ref.at[g, pl.ds(u, 1), :], row, sem).start()
        cast(wg_ref, wgb_ref, jnp.minimum(it, n_up - 1))
        cast(wu_ref, wub_ref, jnp.minimum(it, n_up - 1))
        cast(wd_ref, wdb_ref, jnp.minimum(it, n_down - 1))
        return c

    assert tm % (2 * SUB) == 0 and tm // (2 * SUB) >= n_up
    lax.fori_loop(0, tm // (2 * SUB), step, 0)
    _wait_rows(src_ref, sem)


def _scatter_rows(l, fill_tile, n_act, pos, src, n_tiles, wg, wu, wd):
    nt = N_EXPERTS
    tm = src.shape[0] * SUB // nt
    w_in_spec = lambda r, c: pl.BlockSpec((1, r, c), lambda i, *_: (l * N_EXPERTS + i, 0, 0))
    w_out_spec = lambda r, c: pl.BlockSpec((1, r, c), lambda i, *_: (i, 0, 0))
    grid_spec = pltpu.PrefetchScalarGridSpec(
        num_scalar_prefetch=2,
        grid=(nt,),
        in_specs=[pl.BlockSpec((1, 1, tm), lambda i, *_: (i, 0, 0), memory_space=pltpu.SMEM),
                  pl.BlockSpec((tm // SUB, SUB, ROW_W), lambda i, *_: (i, 0, 0)),
                  w_in_spec(D_MODEL, D_EXPERT), w_in_spec(D_MODEL, D_EXPERT), w_in_spec(D_EXPERT, D_MODEL)],
        out_specs=[pl.BlockSpec(memory_space=pl.ANY),
                   w_out_spec(D_MODEL, D_EXPERT), w_out_spec(D_MODEL, D_EXPERT), w_out_spec(D_EXPERT, D_MODEL)],
        scratch_shapes=[pltpu.VMEM((MOE_TM, ROW_W), F32), pltpu.SemaphoreType.DMA(())],
    )
    return pl.pallas_call(
        functools.partial(_scatter_body, tm=tm, n_tiles=n_tiles),
        grid_spec=grid_spec,
        out_shape=[jax.ShapeDtypeStruct((n_tiles * MOE_TM, ROW_W), F32),
                   jax.ShapeDtypeStruct((N_EXPERTS, D_MODEL, D_EXPERT), BF16),
                   jax.ShapeDtypeStruct((N_EXPERTS, D_MODEL, D_EXPERT), BF16),
                   jax.ShapeDtypeStruct((N_EXPERTS, D_EXPERT, D_MODEL), BF16)],
        compiler_params=_cp(("arbitrary",)),
        name="dispatch_scatter",
    )(fill_tile, n_act, pos, src, wg, wu, wd)


def _scatter_more_body(pos_ref, src_ref, dst_in_ref, dst_ref, sem, *, tm):
    del dst_in_ref
    _scatter_tile_rows(pos_ref, src_ref, dst_ref, sem, tm)


def _scatter_more_rows(pos, src, dst, tm):
    nt = src.shape[0] * SUB // tm
    return pl.pallas_call(
        functools.partial(_scatter_more_body, tm=tm),
        grid=(nt,),
        in_specs=[pl.BlockSpec((1, 1, tm), lambda i: (i, 0, 0), memory_space=pltpu.SMEM),
                  pl.BlockSpec((tm // SUB, SUB, ROW_W), lambda i: (i, 0, 0)),
                  pl.BlockSpec(memory_space=pl.ANY)],
        out_specs=pl.BlockSpec(memory_space=pl.ANY),
        out_shape=jax.ShapeDtypeStruct(dst.shape, dst.dtype),
        scratch_shapes=[pltpu.SemaphoreType.DMA(())],
        input_output_aliases={2: 0},
        compiler_params=_cp(("arbitrary",)),
        name="dispatch_scatter_more",
    )(pos, src, dst)


MOE_STEP_TILES = 3


def _moe_body(ta_ref, tb_ref, na_ref, xs_ref, nf_ref, *refs):
    del ta_ref, tb_ref
    w_refs, y_ref = refs[:-1], refs[-1]
    first_tile = pl.program_id(0) * MOE_STEP_TILES

    @pl.when(first_tile < na_ref[0])
    def _():
        staged = []
        for t in range(MOE_STEP_TILES):
            rows = slice(t * MOE_TM, (t + 1) * MOE_TM)
            x1 = xs_ref[rows, 0:D_MODEL]
            xb = ((x1 * xs_ref[rows, D_MODEL + 2:D_MODEL + 3]) * nf_ref[...]).astype(BF16)
            w = w_refs[6 * t:6 * t + 6]
            staged.append((x1, [(_dot(xb, w[3 * k][0]), _dot(xb, w[3 * k + 1][0])) for k in range(2)]))
        for t in range(MOE_STEP_TILES):
            rows = slice(t * MOE_TM, (t + 1) * MOE_TM)
            y, ups = staged[t]
            for k, (a, u) in enumerate(ups):
                hdn = (jax.nn.silu(a) * u) * xs_ref[rows, D_MODEL + k:D_MODEL + k + 1]
                y = y + _dot(hdn.astype(BF16), w_refs[6 * t + 3 * k + 2][0])
            y_ref[rows, :] = y

    @pl.when(first_tile >= na_ref[0])
    def _():
        y_ref[...] = jnp.zeros(y_ref.shape, F32)


def _moe_experts(l, tile_a, tile_b, n_act, xs, nf, wg, wu, wd):
    step_rows = MOE_STEP_TILES * MOE_TM
    assert xs.shape[0] % step_rows == 0
    last = lambda tile, na: jnp.minimum(tile, na[0] - 1)
    expert = lambda sel, t, i, ta, tb, na: (sel(ta, tb)[last(MOE_STEP_TILES * i + t, na)], 0, 0)
    w_up = lambda sel, t: pl.BlockSpec((1, D_MODEL, D_EXPERT), functools.partial(expert, sel, t))
    w_dn = lambda sel, t: pl.BlockSpec((1, D_EXPERT, D_MODEL), functools.partial(expert, sel, t))
    sa = lambda ta, tb: ta
    sb = lambda ta, tb: tb
    w_specs = []
    for t in range(MOE_STEP_TILES):
        w_specs += [w_up(sa, t), w_up(sa, t), w_dn(sa, t), w_up(sb, t), w_up(sb, t), w_dn(sb, t)]
    grid_spec = pltpu.PrefetchScalarGridSpec(
        num_scalar_prefetch=3,
        grid=(xs.shape[0] // step_rows,),
        in_specs=[pl.BlockSpec((step_rows, ROW_W), lambda i, ta, tb, na: (jnp.minimum(i, (na[0] - 1) // MOE_STEP_TILES), 0)),
                  _layer_spec(l, 1, D_MODEL)] + w_specs,
        out_specs=pl.BlockSpec((step_rows, D_MODEL), lambda i, ta, tb, na: (i, 0)),
    )
    return pl.pallas_call(
        _moe_body,
        grid_spec=grid_spec,
        out_shape=jax.ShapeDtypeStruct((xs.shape[0], D_MODEL), F32),
        compiler_params=_cp(("arbitrary",)),
        name="moe_experts",
    )(tile_a, tile_b, n_act, xs, nf, *((wg, wu, wd) * (2 * MOE_STEP_TILES)))


def _unpermute_body(pos_ref, ys_ref, o_ref, sem, *, tm):
    def group(g, c):
        for u in range(SUB):
            row = ys_ref.at[pl.ds(pos_ref[0, 0, g * SUB + u], 1), :]
            pltpu.make_async_copy(row, o_ref.at[g, pl.ds(u, 1), :], sem).start()
        return c

    lax.fori_loop(0, tm // SUB, group, 0)
    _wait_rows(o_ref, sem)


def _unpermute(pos, ys, tm):
    t = pos.shape[0] * tm
    return pl.pallas_call(
        functools.partial(_unpermute_body, tm=tm),
        grid=(t // tm,),
        in_specs=[pl.BlockSpec((1, 1, tm), lambda i: (i, 0, 0), memory_space=pltpu.SMEM),
                  pl.BlockSpec(memory_space=pl.ANY)],
        out_specs=pl.BlockSpec((tm // SUB, SUB, D_MODEL), lambda i: (i, 0, 0)),
        out_shape=jax.ShapeDtypeStruct((t // SUB, SUB, D_MODEL), F32),
        scratch_shapes=[pltpu.SemaphoreType.DMA(())],
        compiler_params=_cp(("arbitrary",)),
        name="moe_unpermute",
    )(pos, ys)


def _rope_lane_freq():
    half = ROT_DIM // 2
    inv_freq = jnp.float32(ROPE_THETA) ** (-jnp.arange(half, dtype=jnp.float32) * (2.0 / ROT_DIM))
    dim = np.arange(LANES) % HEAD_DIM
    return inv_freq[dim % half][None, :], dim < half, (dim >= half) & (dim < ROT_DIM)


def _rope_patterns(cos, sin, first, second):
    return (jnp.where(first | second, cos, 1.0), jnp.where(first, -sin, 0.0), jnp.where(second, sin, 0.0))


def _rope_tables(pos):
    freq, first, second = _rope_lane_freq()
    ang = pos.astype(jnp.float32)[:, None] * freq
    return _rope_patterns(jnp.cos(ang), jnp.sin(ang), first, second)


def _rope_tables_padded(n_blocks):
    freq, first, second = _rope_lane_freq()
    ang_a = (jnp.arange(n_blocks, dtype=jnp.int32) * BLOCK).astype(jnp.float32)[:, None] * freq
    ang_b = (jnp.arange(BLOCK, dtype=jnp.int32) - PAD).astype(jnp.float32)[:, None] * freq
    ca, sa = jnp.cos(ang_a)[:, None, :], jnp.sin(ang_a)[:, None, :]
    cb, sb = jnp.cos(ang_b)[None], jnp.sin(ang_b)[None]
    flat = lambda t: t.reshape(n_blocks * BLOCK, LANES)
    return _rope_patterns(flat(ca * cb - sa * sb), flat(sa * cb + ca * sb), first, second)


def _seg_ones(n):
    idx = np.arange(n) // HEAD_DIM
    return jnp.asarray(idx[:, None] == idx[None, :], BF16)


def _bucket_experts():
    ea, eb = [], []
    for g in range(N_GROUPS):
        for a in range(EXPERTS_PER_GROUP):
            for b in range(a + 1, EXPERTS_PER_GROUP):
                ea.append(g * EXPERTS_PER_GROUP + a)
                eb.append(g * EXPERTS_PER_GROUP + b)
    return np.asarray(ea, np.int32), np.asarray(eb, np.int32)


def _dispatch_plan(counts, n_tiles):
    padded = ((counts + MOE_TM - 1) // MOE_TM) * MOE_TM
    ends = jnp.cumsum(padded)
    offs = ends - padded
    n_act = jnp.maximum(ends[-1] // MOE_TM, 1)
    starts = jnp.arange(n_tiles, dtype=jnp.int32) * MOE_TM
    tile_bucket = jnp.minimum(jnp.sum(starts[:, None] >= ends[None, :], axis=1), N_BUCKETS - 1)
    ea, eb = _bucket_experts()
    onehot = tile_bucket[:, None] == jnp.arange(N_BUCKETS)[None, :]
    tile_a = jnp.sum(jnp.where(onehot, ea[None, :], 0), axis=1).astype(jnp.int32)
    tile_b = jnp.sum(jnp.where(onehot, eb[None, :], 0), axis=1).astype(jnp.int32)
    fill_tile = jnp.maximum(ends // MOE_TM - 1, 0).astype(jnp.int32)
    return offs, tile_a, tile_b, n_act.astype(jnp.int32).reshape(1), fill_tile


def _positions(offs, bucket, rank):
    onehot = bucket[..., None] == jnp.arange(N_BUCKETS, dtype=jnp.int32)
    return (jnp.sum(jnp.where(onehot, offs.astype(jnp.int32), 0), axis=-1) + rank).astype(jnp.int32)


def kernel(x_prompt, x_sample, cache_k, cache_v, state_conv, meta_tokens, norm_mix, w_in, conv_w, q_norm, k_norm,
           attn_sinks, w_conv_out, w_attn_out, w_o, norm_ffn, w_router_group, b_router_group, w_router_expert,
           b_router_expert, w_exp_gate, w_exp_up, w_exp_down):
    batch, seq, _ = x_prompt.shape
    depth = w_in.shape[0]
    n_dec = x_sample.shape[0]
    past_len = PAST_LEN
    lp = PAD + N_META + seq
    tm_in, tm_out, qb = 640, 640, 13
    tm_move, tm_last = 3328, 4096
    assert PAD + N_META == BLOCK and seq % BLOCK == 0 and tm_in % BLOCK == 0
    assert lp % tm_in == 0 and (batch * lp) % tm_out == 0 and lp % (qb * BLOCK) == 0
    assert (batch * lp) % tm_move == 0 and (batch * seq) % tm_last == 0
    assert x_sample.shape[1] == 1 and cache_k.shape[2] == WINDOW and past_len >= WINDOW

    t_prompt = batch * lp
    t_all = t_prompt + n_dec
    n_tiles = -(-(t_all + N_BUCKETS * (MOE_TM - 1)) // MOE_TM)
    n_tiles = -(-n_tiles // MOE_STEP_TILES) * MOE_STEP_TILES

    meta = jnp.broadcast_to(meta_tokens[None].astype(F32), (batch, N_META, D_MODEL))
    head = jnp.concatenate([jnp.zeros((batch, PAD, D_MODEL), F32), meta], axis=1)
    xs = x_sample.reshape(n_dec, D_MODEL)

    rope_p = _rope_tables_padded(lp // BLOCK)
    rope_s = _rope_tables(jnp.full((1,), past_len, jnp.int32))
    s256, s128 = _seg_ones(256), _seg_ones(LANES)
    tri_p = jnp.asarray(np.triu(np.ones((tm_out, tm_out)), 1), BF16)
    tri_s = jnp.asarray(np.triu(np.ones((n_dec, n_dec)), 1), BF16)
    zero_cnt = jnp.zeros((LANES, LANES), F32)

    w_mix = w_in.astype(BF16)
    w_gate = w_in[:, :, D_MIX:].astype(BF16)
    wco, wao, wo = w_conv_out.astype(BF16), w_attn_out.astype(BF16), w_o.astype(BF16)
    wg32 = w_exp_gate.reshape(depth * N_EXPERTS, D_MODEL, D_EXPERT)
    wu32 = w_exp_up.reshape(depth * N_EXPERTS, D_MODEL, D_EXPERT)
    wd32 = w_exp_down.reshape(depth * N_EXPERTS, D_EXPERT, D_MODEL)
    nm, nf = norm_mix.reshape(depth, 1, D_MODEL), norm_ffn.reshape(depth, 1, D_MODEL)
    qn = (jnp.tile(q_norm, (1, N_HEADS)) * Q_SCALE).reshape(depth, 1, D_Q)
    kn = jnp.tile(k_norm, (1, N_KV_HEADS)).reshape(depth, 1, D_KV)
    r_pad = LANES - N_GROUPS - N_EXPERTS
    wr = jnp.concatenate([w_router_group, w_router_expert, jnp.zeros((depth, D_MODEL, r_pad), F32)], axis=-1).astype(BF16)
    br = jnp.concatenate([b_router_group, b_router_expert, jnp.zeros((depth, r_pad), F32)], axis=-1).reshape(depth, 1, LANES)
    sinks = attn_sinks.astype(F32) * LOG2E
    sinkb = jnp.broadcast_to(sinks[:, :, None], (depth, N_HEADS, LANES))
    ck = cache_k.reshape(depth, n_dec, WINDOW, D_KV)
    cv = cache_v.reshape(depth, n_dec, WINDOW, D_KV)

    outs = {k: [] for k in ("kp", "vp", "cp", "ks", "vs", "cs")}
    for l in range(depth):
        if l == 0:
            cy, q, k, v, ulast, xp = _prompt_in(l, x_prompt, nm, w_mix, conv_w, qn, kn, rope_p, s256, s128, tm_in, head)
        else:
            cy, q, k, v, ulast = _prompt_in(l, xp, nm, w_mix, conv_w, qn, kn, rope_p, s256, s128, tm_in)
        ao = _prompt_attn(l, q, k, v, sinks, qb)
        xep, rankp, bktp, cnt = _mix_out(
            l, xp.reshape(t_prompt, D_MODEL), cy.reshape(t_prompt, D_CONV), ao.reshape(t_prompt, D_Q),
            nm, w_gate, wco, wao, wo, nf, wr, br, tri_p, zero_cnt, tm_out)
        outs["kp"].append(k[:, lp - WINDOW:].reshape(batch, WINDOW, N_KV_HEADS, HEAD_DIM))
        outs["vp"].append(v[:, lp - WINDOW:].reshape(batch, WINDOW, N_KV_HEADS, HEAD_DIM))
        outs["cp"].append(ulast[:, 8 - (CONV_W - 1):])

        c0, c1 = state_conv[l, :, 0, :], state_conv[l, :, 1, :]
        cys, qx, ksn, vsn, us = _sample_in(l, xs, c0, c1, nm, w_mix, conv_w, qn, kn, rope_s, s256, s128)
        ox, nk, nv = _sample_attn(l, jnp.transpose(qx, (1, 0, 2)), sinkb, ck, cv, ksn, vsn)
        ox = ox.reshape(n_dec, N_KV_HEADS, GQA, N_KV_HEADS, HEAD_DIM)
        aos = jnp.stack([ox[:, j, :, j, :] for j in range(N_KV_HEADS)], axis=1).reshape(n_dec, D_Q).astype(BF16)
        xes, ranks, bkts, cnt = _mix_out(l, xs, cys, aos, nm, w_gate, wco, wao, wo, nf, wr, br, tri_s, cnt, n_dec)
        outs["ks"].append(nk.reshape(n_dec, WINDOW, N_KV_HEADS, HEAD_DIM))
        outs["vs"].append(nv.reshape(n_dec, WINDOW, N_KV_HEADS, HEAD_DIM))
        outs["cs"].append(jnp.stack([c1, us], axis=1))

        counts = cnt[:N_BUCKETS, 0].astype(jnp.int32)
        offs, tile_a, tile_b, n_act, fill_tile = _dispatch_plan(counts, n_tiles)
        posp = _positions(offs, bktp, rankp)
        poss = _positions(offs, bkts, ranks)
        by8 = lambda a: a.reshape(a.shape[0] // SUB, SUB, a.shape[1])
        sorted_rows, wg, wu, wd = _scatter_rows(l, fill_tile, n_act, posp.reshape(N_EXPERTS, 1, t_prompt // N_EXPERTS),
                                                by8(xep), n_tiles, wg32, wu32, wd32)
        posp = posp.reshape(t_prompt // tm_move, 1, tm_move)
        sorted_rows = _scatter_more_rows(poss, by8(xes), sorted_rows, n_dec)
        ys = _moe_experts(l, tile_a, tile_b, n_act, sorted_rows, nf, wg, wu, wd)
        xs = _unpermute(poss, ys, n_dec).reshape(n_dec, D_MODEL)
        if l + 1 < depth:
            xp = _unpermute(posp, ys, tm_move).reshape(batch, lp, D_MODEL)
        else:
            pos_tok = posp.reshape(batch, lp)[:, PAD + N_META:].reshape(batch * seq // tm_last, 1, tm_last)
            y_prompt = _unpermute(pos_tok, ys, tm_last).reshape(batch, seq, D_MODEL)

    y_sample = xs.reshape(n_dec, 1, D_MODEL)
    st = lambda k: jnp.stack(outs[k])
    return (y_prompt, y_sample, st("kp"), st("vp"), st("cp"), st("ks"), st("vs"), st("cs"))
```

```python
import functools
import math

import jax
import jax.numpy as jnp
import numpy as np
from jax import lax
from jax.experimental import pallas as pl
from jax.experimental.pallas import tpu as pltpu

D_MODEL = 1024
N_META = 16
D_CONV = D_MODEL
CONV_W = 3
N_HEADS = 16
N_KV_HEADS = 2
HEAD_DIM = 64
GQA = N_HEADS // N_KV_HEADS
ROT_DIM = HEAD_DIM // 4
ROPE_THETA = 500000.0
WINDOW = 128
PAST_LEN = 8192
BLOCK = 128
N_GROUPS = 4
EXPERTS_PER_GROUP = 4
N_EXPERTS = N_GROUPS * EXPERTS_PER_GROUP
D_EXPERT = 512
EPS = 1e-6
NEG = -1e30
D_Q = N_HEADS * HEAD_DIM
D_KV = N_KV_HEADS * HEAD_DIM
C_B, C_C, C_HC = 0, D_CONV, 2 * D_CONV
C_Q = 3 * D_CONV
C_K = C_Q + D_Q
C_V = C_K + D_KV
C_G = C_V + D_KV
D_MIX = C_G
D_IN = C_G + 2 * D_MODEL

LANES = 128
SUBLANES = 8
PAD = (-N_META) % BLOCK
N_PAIRS = 6
N_BUCKETS = N_GROUPS * N_PAIRS
MOE_TM = 256
ROW_W = D_MODEL + LANES
LOG2E = math.log2(math.e)
Q_SCALE = HEAD_DIM ** -0.5 * LOG2E

F32 = jnp.float32
BF16 = jnp.bfloat16
VMEM_LIMIT = 56 * 1024 * 1024


def _cp(sem, vmem=VMEM_LIMIT):
    return pltpu.CompilerParams(dimension_semantics=sem, vmem_limit_bytes=vmem)


def _const_spec(shape):
    nd = len(shape)
    return pl.BlockSpec(shape, lambda *_: (0,) * nd, pipeline_mode=pl.Buffered(1))


def _layer_spec(l, *shape):
    n = len(shape)
    return pl.BlockSpec((None,) + shape, lambda *_: (l,) + (0,) * n, pipeline_mode=pl.Buffered(1))


def _w_mix_spec(l):
    return pl.BlockSpec((None, D_MODEL, D_MIX), lambda *_: (l, 0, 0), pipeline_mode=pl.Buffered(1))


def _dot(a, b):
    return jnp.dot(a, b, preferred_element_type=F32)


def _seg_mean_sq(x, seg_ones):
    return _dot((x * x).astype(BF16), seg_ones) * (1.0 / HEAD_DIM)


def _rope128(t, cos, sin_pm, first):
    partner = jnp.where(first, pltpu.roll(t, LANES - ROT_DIM // 2, 1), pltpu.roll(t, ROT_DIM // 2, 1))
    return t * cos + partner * sin_pm


def _rms_scale(x):
    return lax.rsqrt(jnp.mean(x * x, axis=-1, keepdims=True) + EPS)


def _rms_rows(x, g):
    return (x * _rms_scale(x)) * g


def _qk_project(hb, w_ref, s256, s128):
    qs = [_dot(hb, w_ref[:, C_Q + c * 256:C_Q + (c + 1) * 256]) for c in range(D_Q // 256)]
    kv = _dot(hb, w_ref[:, C_K:C_K + 2 * D_KV])
    kc, v = kv[:, 0:D_KV], kv[:, D_KV:]
    return qs, [_seg_mean_sq(qc, s256) for qc in qs], kc, _seg_mean_sq(kc, s128), v


def _qk_finish(proj, qn, kn, cos, sneg, spos, store_q):
    qs, q_ms, kc, k_ms, _ = proj
    sin_pm = sneg + spos
    lane = lax.broadcasted_iota(jnp.int32, cos.shape, 1)
    first = lax.bitwise_and(lane, HEAD_DIM - 1) < ROT_DIM // 2
    for c, (qc, ms) in enumerate(zip(qs, q_ms)):
        qc = (qc * lax.rsqrt(ms + EPS)) * qn[:, c * 256:(c + 1) * 256]
        for s in range(2):
            r = _rope128(qc[:, s * LANES:(s + 1) * LANES], cos, sin_pm, first)
            store_q(2 * c + s, r.astype(BF16))
    kc = (kc * lax.rsqrt(k_ms + EPS)) * kn
    return _rope128(kc, cos, sin_pm, first)


def _in_body(x_ref, nm_ref, w_ref, cw_ref, qn_ref, kn_ref, cos_ref, sneg_ref, spos_ref,
             s256_ref, s128_ref, cy_ref, q_ref, k_ref, v_ref, ul_ref, us_ref, *, tm, parts):
    i = pl.program_id(1)
    th = tm // parts

    @pl.when(i == 0)
    def _():
        us_ref[0:8, :] = jnp.zeros((8, D_CONV), F32)

    projs = []
    for h in range(parts):
        r0 = h * th
        hb = _rms_rows(x_ref[0, r0:r0 + th, :], nm_ref[...]).astype(BF16)
        projs.append(_qk_project(hb, w_ref, s256_ref[...], s128_ref[...]))
        u = _dot(hb, w_ref[:, C_C:C_C + D_CONV]) * _dot(hb, w_ref[:, C_HC:C_HC + D_CONV])
        row = lax.broadcasted_iota(jnp.int32, (th, 1), 0) + (i * tm + r0)
        u = jnp.where(row >= PAD, u, 0.0)
        us_ref[8 + r0:8 + r0 + th, :] = u
        conv = (us_ref[6 + r0:6 + r0 + th, :] * cw_ref[0:1, :] + us_ref[7 + r0:7 + r0 + th, :] * cw_ref[1:2, :]) \
            + u * cw_ref[2:3, :]
        cy_ref[0, r0:r0 + th, :] = (_dot(hb, w_ref[:, C_B:C_B + D_CONV]) * conv).astype(BF16)
        v_ref[0, r0:r0 + th, :] = projs[h][4]

    last = us_ref[tm:tm + 8, :]
    ul_ref[0] = last
    us_ref[0:8, :] = last

    for h in range(parts):
        rows = slice(h * th, (h + 1) * th)

        def store_q(slab, val, rows=rows):
            q_ref[0, rows, slab * LANES:(slab + 1) * LANES] = val

        k_ref[0, rows, :] = _qk_finish(projs[h], qn_ref[...], kn_ref[...], cos_ref[rows, :], sneg_ref[rows, :],
                                       spos_ref[rows, :], store_q)


def _in_first_body(head_ref, *refs, tm, parts):
    nb = tm // BLOCK
    blocks, rest, xpad_ref, us_ref = refs[:nb], refs[nb:-2], refs[-2], refs[-1]
    first = pl.program_id(1) == 0
    xpad_ref[0, 0:BLOCK, :] = jnp.where(first, head_ref[0], blocks[0][0])
    for j in range(1, nb):
        xpad_ref[0, j * BLOCK:(j + 1) * BLOCK, :] = blocks[j][0]
    _in_body(xpad_ref, *rest, us_ref, tm=tm, parts=parts)


def _prompt_in(l, x, nm, w_mix, cw, qn, kn, rope, s256, s128, tm, head=None):
    b = x.shape[0]
    lp = x.shape[1] if head is None else x.shape[1] + BLOCK
    nt = lp // tm
    nb = tm // BLOCK
    cos, sneg, spos = rope
    tok = lambda w: pl.BlockSpec((1, tm, w), lambda bi, i: (bi, i, 0))
    tab = pl.BlockSpec((tm, LANES), lambda bi, i: (i, 0))
    params = [_layer_spec(l, 1, D_MODEL), _w_mix_spec(l),
              _layer_spec(l, CONV_W, D_CONV), _layer_spec(l, 1, D_Q), _layer_spec(l, 1, D_KV),
              tab, tab, tab, _const_spec((256, 256)), _const_spec((LANES, LANES))]
    out_specs = [tok(D_CONV), tok(D_Q), tok(D_KV), tok(D_KV), pl.BlockSpec((1, 8, D_CONV), lambda bi, i: (bi, 0, 0))]
    out_shape = [jax.ShapeDtypeStruct((b, lp, D_CONV), BF16), jax.ShapeDtypeStruct((b, lp, D_Q), BF16),
                 jax.ShapeDtypeStruct((b, lp, D_KV), F32), jax.ShapeDtypeStruct((b, lp, D_KV), F32),
                 jax.ShapeDtypeStruct((b, 8, D_CONV), F32)]
    if head is None:
        body, x_specs, x_args = _in_body, [tok(D_MODEL)], (x,)
    else:
        blk = lambda j: pl.BlockSpec((1, BLOCK, D_MODEL), lambda bi, i: (bi, jnp.maximum(nb * i - 1 + j, 0), 0))
        body = _in_first_body
        x_specs = [pl.BlockSpec((1, BLOCK, D_MODEL), lambda bi, i: (bi, 0, 0))] + [blk(j) for j in range(nb)]
        x_args = (head,) + (x,) * nb
        out_specs.append(tok(D_MODEL))
        out_shape.append(jax.ShapeDtypeStruct((b, lp, D_MODEL), F32))
    return pl.pallas_call(
        functools.partial(body, tm=tm, parts=2),
        grid=(b, nt),
        in_specs=x_specs + params,
        out_specs=out_specs,
        out_shape=out_shape,
        scratch_shapes=[pltpu.VMEM((tm + 8, D_CONV), F32)],
        compiler_params=_cp(("arbitrary", "arbitrary")),
        name="prompt_in",
    )(*x_args, nm, w_mix, cw, qn, kn, cos, sneg, spos, s256, s128)


def _sin_body(x_ref, c0_ref, c1_ref, nm_ref, w_ref, cw_ref, qn_ref, kn_ref, cos_ref, sneg_ref,
              spos_ref, s256_ref, s128_ref, cy_ref, qx_ref, k_ref, v_ref, u_ref):
    hb = _rms_rows(x_ref[...], nm_ref[...]).astype(BF16)
    u = _dot(hb, w_ref[:, C_C:C_C + D_CONV]) * _dot(hb, w_ref[:, C_HC:C_HC + D_CONV])
    u_ref[...] = u
    conv = (c0_ref[...] * cw_ref[0:1, :] + c1_ref[...] * cw_ref[1:2, :]) + u * cw_ref[2:3, :]
    cy_ref[...] = (_dot(hb, w_ref[:, C_B:C_B + D_CONV]) * conv).astype(BF16)

    lane = lax.broadcasted_iota(jnp.int32, (x_ref.shape[0], LANES), 1)
    low = lane < HEAD_DIM

    def store_q(slab, val):
        valf = val.astype(F32)
        swapped = pltpu.roll(valf, HEAD_DIM, 1)
        zero = jnp.zeros_like(valf)
        for h in (2 * slab, 2 * slab + 1):
            src = valf if (h % 2) == (h // GQA) else swapped
            keep = low if (h // GQA) == 0 else jnp.logical_not(low)
            qx_ref[h] = jnp.where(keep, src, zero).astype(BF16)

    cos = jnp.broadcast_to(cos_ref[...], (x_ref.shape[0], LANES))
    sneg = jnp.broadcast_to(sneg_ref[...], (x_ref.shape[0], LANES))
    spos = jnp.broadcast_to(spos_ref[...], (x_ref.shape[0], LANES))
    proj = _qk_project(hb, w_ref, s256_ref[...], s128_ref[...])
    k_ref[...] = _qk_finish(proj, qn_ref[...], kn_ref[...], cos, sneg, spos, store_q)
    v_ref[...] = proj[4]


def _sample_in(l, x, c0, c1, nm, w_mix, cw, qn, kn, rope, s256, s128):
    n = x.shape[0]
    cos, sneg, spos = rope
    full = lambda *s: pl.BlockSpec(s, lambda i: (0,) * len(s))
    return pl.pallas_call(
        _sin_body,
        grid=(1,),
        in_specs=[full(n, D_MODEL), full(n, D_CONV), full(n, D_CONV), _layer_spec(l, 1, D_MODEL),
                  _w_mix_spec(l), _layer_spec(l, CONV_W, D_CONV), _layer_spec(l, 1, D_Q),
                  _layer_spec(l, 1, D_KV), full(1, LANES), full(1, LANES), full(1, LANES), full(256, 256),
                  full(LANES, LANES)],
        out_specs=[full(n, D_CONV), full(N_HEADS, n, LANES), full(n, D_KV), full(n, D_KV), full(n, D_CONV)],
        out_shape=[jax.ShapeDtypeStruct((n, D_CONV), BF16), jax.ShapeDtypeStruct((N_HEADS, n, LANES), BF16),
                   jax.ShapeDtypeStruct((n, D_KV), F32), jax.ShapeDtypeStruct((n, D_KV), F32),
                   jax.ShapeDtypeStruct((n, D_CONV), F32)],
        compiler_params=_cp(("arbitrary",)),
        name="sample_in",
    )(x, c0, c1, nm, w_mix, cw, qn, kn, cos, sneg, spos, s256, s128)


def _attn_body(sink_ref, q_ref, kp_ref, kc_ref, vp_ref, vc_ref, o_ref, ke_ref, ko_ref, vt_ref, *, qb, l):
    i = pl.program_id(1)
    lane = lax.broadcasted_iota(jnp.int32, (BLOCK, LANES), 1)
    low = lane < HEAD_DIM

    def prep_k(src, blk0, nblk):
        for t in range(nblk):
            blk = src[0, t * BLOCK:(t + 1) * BLOCK, :]
            swp = pltpu.roll(blk, HEAD_DIM, 1)
            zero = jnp.zeros_like(blk)
            rows = slice((blk0 + t) * BLOCK, (blk0 + t + 1) * BLOCK)
            ke_ref[0, rows, :] = jnp.where(low, blk, zero).astype(BF16)
            ko_ref[0, rows, :] = jnp.where(low, zero, swp).astype(BF16)
            ke_ref[1, rows, :] = jnp.where(low, swp, zero).astype(BF16)
            ko_ref[1, rows, :] = jnp.where(low, zero, blk).astype(BF16)

    def prep_v(src, blk0, nblk):
        for t in range(nblk):
            vt = jnp.transpose(src[0, t * BLOCK:(t + 1) * BLOCK, :]).astype(BF16)
            for j in range(N_KV_HEADS):
                vt_ref[blk0 + t, j] = vt[j * HEAD_DIM:(j + 1) * HEAD_DIM, :]

    prep_k(kp_ref, 0, 1)
    prep_k(kc_ref, 1, qb)
    prep_v(vp_ref, 0, 1)
    prep_v(vc_ref, 1, qb)

    c = lax.broadcasted_iota(jnp.int32, (2 * BLOCK, BLOCK), 0)
    r = lax.broadcasted_iota(jnp.int32, (2 * BLOCK, BLOCK), 1)
    diff = r - (c - BLOCK)
    band = (diff >= 0) & (diff < WINDOW)
    nt = (((1,), (1,)), ((), ()))

    def one_block(b, carry):
        r0 = pl.multiple_of(b * BLOCK, BLOCK)
        kpos = (i * qb + b) * BLOCK + c - BLOCK - PAD
        bias = jnp.where(band & (kpos >= 0), 0.0, NEG)
        for m in range(N_HEADS // 2):
            j = (2 * m) // GQA
            q2 = q_ref[0, pl.ds(r0, BLOCK), m * LANES:(m + 1) * LANES]
            halves = []
            for par, k_ref in ((0, ke_ref), (1, ko_ref)):
                st = lax.dot_general(k_ref[j, pl.ds(r0, 2 * BLOCK), :], q2, nt, preferred_element_type=F32)
                st = st + bias
                sink = sink_ref[l, 2 * m + par]
                mx = jnp.maximum(jnp.max(st, axis=0, keepdims=True), sink)
                p = jnp.exp2(st - mx)
                den = jnp.sum(p, axis=0, keepdims=True) + jnp.exp2(sink - mx)
                pb = p.astype(BF16)
                ot = _dot(vt_ref[b, j], pb[0:BLOCK]) + _dot(vt_ref[b + 1, j], pb[BLOCK:])
                halves.append(ot * (1.0 / den))
            o2 = jnp.transpose(jnp.concatenate(halves, axis=0))
            o_ref[0, pl.ds(r0, BLOCK), m * LANES:(m + 1) * LANES] = o2.astype(BF16)
        return carry

    lax.fori_loop(0, qb, one_block, 0, unroll=True)


def _prompt_attn(l, q, k, v, sinks, qb):
    b, lp, _ = q.shape
    nsteps = lp // (qb * BLOCK)
    cur = lambda w: pl.BlockSpec((1, qb * BLOCK, w), lambda bi, i: (bi, i, 0))
    prev = pl.BlockSpec((1, BLOCK, D_KV), lambda bi, i: (bi, jnp.maximum(i * qb - 1, 0), 0))
    ext = ((qb + 1) * BLOCK, LANES)
    return pl.pallas_call(
        functools.partial(_attn_body, qb=qb, l=l),
        grid=(b, nsteps),
        in_specs=[pl.BlockSpec(memory_space=pltpu.SMEM), cur(D_Q), prev, cur(D_KV), prev, cur(D_KV)],
        out_specs=cur(D_Q),
        out_shape=jax.ShapeDtypeStruct((b, lp, D_Q), BF16),
        scratch_shapes=[pltpu.VMEM((N_KV_HEADS,) + ext, BF16), pltpu.VMEM((N_KV_HEADS,) + ext, BF16),
                        pltpu.VMEM((qb + 1, N_KV_HEADS, HEAD_DIM, BLOCK), BF16)],
        compiler_params=_cp(("arbitrary", "arbitrary")),
        name="prompt_attn",
    )(sinks, q, k, k, v, v)


def _sattn_body(qx_ref, sink_ref, ck_ref, cv_ref, kn_ref, vn_ref, ox_ref, nk_ref, nv_ref, *, tb):
    def window(c_ref, n_ref, t):
        return jnp.concatenate([c_ref[t, 1:WINDOW, :], n_ref[t:t + 1, :]], axis=0)

    for t in range(tb):
        nk_ref[t] = window(ck_ref, kn_ref, t)
        nv_ref[t] = window(cv_ref, vn_ref, t)
    nt = (((1,), (1,)), ((), ()))
    s = jnp.concatenate([lax.dot_general(qx_ref[t], window(ck_ref, kn_ref, t).astype(BF16), nt,
                                         preferred_element_type=F32) for t in range(tb)], axis=0)
    sink = jnp.concatenate([sink_ref[...][:, 0:1]] * tb, axis=0)
    m = jnp.maximum(jnp.max(s, axis=-1, keepdims=True), sink)
    p = jnp.exp2(s - m)
    rden = 1.0 / (jnp.sum(p, axis=-1, keepdims=True) + jnp.exp2(sink - m))
    pb = p.astype(BF16)
    for t in range(tb):
        rows = slice(t * N_HEADS, (t + 1) * N_HEADS)
        ox_ref[t] = _dot(pb[rows], window(cv_ref, vn_ref, t).astype(BF16)) * rden[rows]


def _sample_attn(l, qx, sinkb, ck, cv, kn, vn, tb=16):
    n = qx.shape[0]
    blk3 = lambda a, c: pl.BlockSpec((tb, a, c), lambda i: (i, 0, 0))
    cache = pl.BlockSpec((None, tb, WINDOW, D_KV), lambda i: (l, i, 0, 0))
    row = pl.BlockSpec((tb, D_KV), lambda i: (i, 0))
    return pl.pallas_call(
        functools.partial(_sattn_body, tb=tb),
        grid=(n // tb,),
        in_specs=[blk3(N_HEADS, LANES), _layer_spec(l, N_HEADS, LANES), cache, cache, row, row],
        out_specs=[blk3(N_HEADS, LANES), blk3(WINDOW, D_KV), blk3(WINDOW, D_KV)],
        out_shape=[jax.ShapeDtypeStruct((n, N_HEADS, LANES), F32),
                   jax.ShapeDtypeStruct((n, WINDOW, D_KV), F32), jax.ShapeDtypeStruct((n, WINDOW, D_KV), F32)],
        compiler_params=_cp(("arbitrary",)),
        name="sample_attn",
    )(qx, sinkb, ck, cv, kn, vn)


ROUTE_ROWS = 24
RINV_LANE = LANES - 1


def _route(lt):
    top = lt[0:ROUTE_ROWS, :]
    rows = top.shape[1]
    rowf = lax.broadcasted_iota(jnp.int32, top.shape, 0).astype(F32)
    big = jnp.float32(3e38)
    far = jnp.float32(LANES)
    cmax = lambda a: jnp.max(a, axis=0, keepdims=True)
    cmin = lambda a: jnp.min(a, axis=0, keepdims=True)

    gmask = rowf < N_GROUPS
    gl = jnp.where(gmask, top, -big)
    gmax = cmax(gl)
    grp = cmin(jnp.where(gmask & (gl == gmax), rowf, far))
    p_grp = 1.0 / jnp.sum(jnp.where(gmask, jnp.exp(gl - gmax), 0.0), axis=0, keepdims=True)

    e_lo = N_GROUPS + EXPERTS_PER_GROUP * grp
    emask = (rowf >= e_lo) & (rowf < e_lo + EXPERTS_PER_GROUP)
    el = jnp.where(emask, top, -big)
    v1 = cmax(el)
    i1 = cmin(jnp.where(emask & (el == v1), rowf, far))
    rest = emask & (rowf != i1)
    el2 = jnp.where(rest, top, -big)
    v2 = cmax(el2)
    i2 = cmin(jnp.where(rest & (el2 == v2), rowf, far))
    e = jnp.exp(v2 - v1)
    w1 = (1.0 / (1.0 + e)) * p_grp
    w2 = (e / (1.0 + e)) * p_grp
    first_low = i1 < i2
    ea = jnp.where(first_low, i1, i2) - e_lo
    eb = jnp.where(first_low, i2, i1) - e_lo
    w_a = jnp.where(first_low, w1, w2)
    w_b = jnp.where(first_low, w2, w1)
    pair = jnp.where(ea == 0.0, 0.0, jnp.where(ea == 1.0, 3.0, 5.0)) + (eb - ea - 1.0)
    bucket = grp * N_PAIRS + pair

    r8 = lax.broadcasted_iota(jnp.int32, (SUBLANES, rows), 0)
    head = jnp.where(r8 == 0, w_a, jnp.where(r8 == 1, w_b, jnp.where(r8 == 2, lt[RINV_LANE:RINV_LANE + 1, :], 0.0)))
    return bucket, jnp.concatenate([head, jnp.zeros((LANES - SUBLANES, rows), F32)], axis=0)


def _out_body(x_ref, cy_ref, ao_ref, nm_ref, wg_ref, wco_ref, wao_ref, wo_ref, nf_ref, wr_ref, br_ref,
              tri_ref, cin_ref, xe_ref, rank_ref, bkt_ref, cnt_ref, run_ref, *, tm, parts):
    i = pl.program_id(0)
    th = tm // parts
    stage = []
    for h in range(parts):
        rows = slice(h * th, (h + 1) * th)
        hb = _rms_rows(x_ref[rows, :], nm_ref[...]).astype(BF16)
        stage.append((_dot(cy_ref[rows, :], wco_ref[...]), _dot(ao_ref[rows, :], wao_ref[...]),
                      _dot(hb, wg_ref[:, 0:D_MODEL]), _dot(hb, wg_ref[:, D_MODEL:])))
    for h, (ya, yb, gc, ga) in enumerate(stage):
        rows = slice(h * th, (h + 1) * th)
        mix = jax.nn.sigmoid(gc) * ya + jax.nn.sigmoid(ga) * yb
        xe_ref[rows, 0:D_MODEL] = x_ref[rows, :] + _dot(mix.astype(BF16), wo_ref[...])
    x1 = xe_ref[:, 0:D_MODEL]

    rinv = _rms_scale(x1)
    xnb = ((x1 * rinv) * nf_ref[...]).astype(BF16)
    logits = _dot(xnb, wr_ref[...]) + br_ref[...]
    lane = lax.broadcasted_iota(jnp.int32, (tm, LANES), 1)
    bucket, meta_t = _route(jnp.transpose(jnp.where(lane == RINV_LANE, rinv, logits)))
    xe_ref[:, D_MODEL:] = jnp.transpose(meta_t)

    @pl.when(i == 0)
    def _():
        run_ref[...] = cin_ref[...]

    sub = lax.broadcasted_iota(jnp.int32, (LANES, tm), 0).astype(F32)
    oht = (sub == bucket).astype(F32)
    before = _dot(oht.astype(BF16), tri_ref[...]) + run_ref[:, 0:1]
    rank_ref[0] = jnp.sum(oht * before, axis=0, keepdims=True).astype(jnp.int32)
    bkt_ref[0] = bucket.astype(jnp.int32)
    run_ref[...] = run_ref[...] + jnp.sum(oht, axis=-1, keepdims=True)
    cnt_ref[...] = run_ref[...]


def _mix_out(l, x, cy, ao, nm, w_gate, wco, wao, wo, nf, wr, br, tri, cnt_in, tm):
    t = x.shape[0]
    nt = t // tm
    tok = lambda w: pl.BlockSpec((tm, w), lambda i: (i, 0))
    rowi = pl.BlockSpec((1, 1, tm), lambda i: (i, 0, 0))
    sq = (D_MODEL, D_MODEL)
    return pl.pallas_call(
        functools.partial(_out_body, tm=tm, parts=2 if tm >= 4 * LANES else 1),
        grid=(nt,),
        in_specs=[tok(D_MODEL), tok(D_CONV), tok(D_Q), _layer_spec(l, 1, D_MODEL),
                  _layer_spec(l, D_MODEL, 2 * D_MODEL), _layer_spec(l, *sq), _layer_spec(l, *sq), _layer_spec(l, *sq),
                  _layer_spec(l, 1, D_MODEL), _layer_spec(l, D_MODEL, LANES), _layer_spec(l, 1, LANES),
                  _const_spec((tm, tm)), _const_spec((LANES, LANES))],
        out_specs=[tok(ROW_W), rowi, rowi, pl.BlockSpec((LANES, LANES), lambda i: (0, 0))],
        out_shape=[jax.ShapeDtypeStruct((t, ROW_W), F32),
                   jax.ShapeDtypeStruct((nt, 1, tm), jnp.int32), jax.ShapeDtypeStruct((nt, 1, tm), jnp.int32),
                   jax.ShapeDtypeStruct((LANES, LANES), F32)],
        scratch_shapes=[pltpu.VMEM((LANES, LANES), F32)],
        compiler_params=_cp(("arbitrary",)),
        name="mix_out",
    )(x, cy, ao, nm, w_gate, wco, wao, wo, nf, wr, br, tri, cnt_in)


SUB = SUBLANES


def _wait_rows(block_ref, sem):
    pltpu.make_async_copy(block_ref, block_ref, sem).wait()


def _scatter_tile_rows(pos_ref, src_ref, dst_ref, sem, tm):
    def group(g, c):
        for u in range(SUB):
            row = dst_ref.at[pl.ds(pos_ref[0, 0, g * SUB + u], 1), :]
            pltpu.make_async_copy(src_ref.at[g, pl.ds(u, 1), :], row, sem).start()
        return c

    lax.fori_loop(0, tm // SUB, group, 0)
    _wait_rows(src_ref, sem)


CAST_ROWS = 16


def _scatter_body(fill_ref, na_ref, pos_ref, src_ref, wg_ref, wu_ref, wd_ref, dst_ref, wgb_ref, wub_ref, wdb_ref,
                  zero_ref, sem, *, tm, n_tiles):
    tile_rows = lambda t: dst_ref.at[pl.ds(pl.multiple_of(t * MOE_TM, MOE_TM), MOE_TM), :]

    @pl.when(pl.program_id(0) == 0)
    def _():
        zero_ref[...] = jnp.zeros(zero_ref.shape, F32)
        for b in range(N_BUCKETS):
            pltpu.make_async_copy(zero_ref, tile_rows(fill_ref[b]), sem).start()

        def tail_start(t, c):
            pltpu.make_async_copy(zero_ref, tile_rows(t), sem).start()
            return c

        lax.fori_loop(na_ref[0], n_tiles, tail_start, 0)
        for b in range(N_BUCKETS):
            pltpu.make_async_copy(zero_ref, tile_rows(fill_ref[b]), sem).wait()

        def tail_wait(t, c):
            pltpu.make_async_copy(zero_ref, tile_rows(t), sem).wait()
            return c

        lax.fori_loop(na_ref[0], n_tiles, tail_wait, 0)

    n_up, n_down = D_MODEL // CAST_ROWS, D_EXPERT // CAST_ROWS

    def cast(src, dst, chunk):
        rows = pl.ds(pl.multiple_of(chunk * CAST_ROWS, CAST_ROWS), CAST_ROWS)
        dst[0, rows, :] = src[0, rows, :].astype(BF16)

    def step(it, c):
        for g in (2 * it, 2 * it + 1):
            for u in range(SUB):
                row = dst_ref.at[pl.ds(pos_ref[0, 0, g * SUB + u], 1), :]
                pltpu.make_async_copy(src_ref.at[g, pl.ds(u, 1), :], row, sem).start()
        cast(wg_ref, wgb_ref, jnp.minimum(it, n_up - 1))
        cast(wu_ref, wub_ref, jnp.minimum(it, n_up - 1))
        cast(wd_ref, wdb_ref, jnp.minimum(it, n_down - 1))
        return c

    assert tm % (2 * SUB) == 0 and tm // (2 * SUB) >= n_up
    lax.fori_loop(0, tm // (2 * SUB), step, 0)
    _wait_rows(src_ref, sem)


def _scatter_rows(l, fill_tile, n_act, pos, src, n_tiles, wg, wu, wd):
    nt = N_EXPERTS
    tm = src.shape[0] * SUB // nt
    w_in_spec = lambda r, c: pl.BlockSpec((1, r, c), lambda i, *_: (l * N_EXPERTS + i, 0, 0))
    w_out_spec = lambda r, c: pl.BlockSpec((1, r, c), lambda i, *_: (i, 0, 0))
    grid_spec = pltpu.PrefetchScalarGridSpec(
        num_scalar_prefetch=2,
        grid=(nt,),
        in_specs=[pl.BlockSpec((1, 1, tm), lambda i, *_: (i, 0, 0), memory_space=pltpu.SMEM),
                  pl.BlockSpec((tm // SUB, SUB, ROW_W), lambda i, *_: (i, 0, 0)),
                  w_in_spec(D_MODEL, D_EXPERT), w_in_spec(D_MODEL, D_EXPERT), w_in_spec(D_EXPERT, D_MODEL)],
        out_specs=[pl.BlockSpec(memory_space=pl.ANY),
                   w_out_spec(D_MODEL, D_EXPERT), w_out_spec(D_MODEL, D_EXPERT), w_out_spec(D_EXPERT, D_MODEL)],
        scratch_shapes=[pltpu.VMEM((MOE_TM, ROW_W), F32), pltpu.SemaphoreType.DMA(())],
    )
    return pl.pallas_call(
        functools.partial(_scatter_body, tm=tm, n_tiles=n_tiles),
        grid_spec=grid_spec,
        out_shape=[jax.ShapeDtypeStruct((n_tiles * MOE_TM, ROW_W), F32),
                   jax.ShapeDtypeStruct((N_EXPERTS, D_MODEL, D_EXPERT), BF16),
                   jax.ShapeDtypeStruct((N_EXPERTS, D_MODEL, D_EXPERT), BF16),
                   jax.ShapeDtypeStruct((N_EXPERTS, D_EXPERT, D_MODEL), BF16)],
        compiler_params=_cp(("arbitrary",)),
        name="dispatch_scatter",
    )(fill_tile, n_act, pos, src, wg, wu, wd)


def _scatter_more_body(pos_ref, src_ref, dst_in_ref, dst_ref, sem, *, tm):
    del dst_in_ref
    _scatter_tile_rows(pos_ref, src_ref, dst_ref, sem, tm)


def _scatter_more_rows(pos, src, dst, tm):
    nt = src.shape[0] * SUB // tm
    return pl.pallas_call(
        functools.partial(_scatter_more_body, tm=tm),
        grid=(nt,),
        in_specs=[pl.BlockSpec((1, 1, tm), lambda i: (i, 0, 0), memory_space=pltpu.SMEM),
                  pl.BlockSpec((tm // SUB, SUB, ROW_W), lambda i: (i, 0, 0)),
                  pl.BlockSpec(memory_space=pl.ANY)],
        out_specs=pl.BlockSpec(memory_space=pl.ANY),
        out_shape=jax.ShapeDtypeStruct(dst.shape, dst.dtype),
        scratch_shapes=[pltpu.SemaphoreType.DMA(())],
        input_output_aliases={2: 0},
        compiler_params=_cp(("arbitrary",)),
        name="dispatch_scatter_more",
    )(pos, src, dst)


MOE_STEP_TILES = 2


def _moe_body(ta_ref, tb_ref, na_ref, xs_ref, nf_ref, *refs):
    del ta_ref, tb_ref
    w_refs, y_ref = refs[:-1], refs[-1]
    first_tile = pl.program_id(0) * MOE_STEP_TILES

    @pl.when(first_tile < na_ref[0])
    def _():
        staged = []
        for t in range(MOE_STEP_TILES):
            rows = slice(t * MOE_TM, (t + 1) * MOE_TM)
            x1 = xs_ref[rows, 0:D_MODEL]
            xb = ((x1 * xs_ref[rows, D_MODEL + 2:D_MODEL + 3]) * nf_ref[...]).astype(BF16)
            w = w_refs[6 * t:6 * t + 6]
            staged.append((x1, [(_dot(xb, w[3 * k][0]), _dot(xb, w[3 * k + 1][0])) for k in range(2)]))
        for t in range(MOE_STEP_TILES):
            rows = slice(t * MOE_TM, (t + 1) * MOE_TM)
            y, ups = staged[t]
            for k, (a, u) in enumerate(ups):
                hdn = (jax.nn.silu(a) * u) * xs_ref[rows, D_MODEL + k:D_MODEL + k + 1]
                y = y + _dot(hdn.astype(BF16), w_refs[6 * t + 3 * k + 2][0])
            y_ref[rows, :] = y

    @pl.when(first_tile >= na_ref[0])
    def _():
        y_ref[...] = jnp.zeros(y_ref.shape, F32)


def _moe_experts(l, tile_a, tile_b, n_act, xs, nf, wg, wu, wd):
    step_rows = MOE_STEP_TILES * MOE_TM
    assert xs.shape[0] % step_rows == 0
    last = lambda tile, na: jnp.minimum(tile, na[0] - 1)
    expert = lambda sel, t, i, ta, tb, na: (sel(ta, tb)[last(MOE_STEP_TILES * i + t, na)], 0, 0)
    w_up = lambda sel, t: pl.BlockSpec((1, D_MODEL, D_EXPERT), functools.partial(expert, sel, t))
    w_dn = lambda sel, t: pl.BlockSpec((1, D_EXPERT, D_MODEL), functools.partial(expert, sel, t))
    sa = lambda ta, tb: ta
    sb = lambda ta, tb: tb
    w_specs = []
    for t in range(MOE_STEP_TILES):
        w_specs += [w_up(sa, t), w_up(sa, t), w_dn(sa, t), w_up(sb, t), w_up(sb, t), w_dn(sb, t)]
    grid_spec = pltpu.PrefetchScalarGridSpec(
        num_scalar_prefetch=3,
        grid=(xs.shape[0] // step_rows,),
        in_specs=[pl.BlockSpec((step_rows, ROW_W), lambda i, ta, tb, na: (jnp.minimum(i, (na[0] - 1) // MOE_STEP_TILES), 0)),
                  _layer_spec(l, 1, D_MODEL)] + w_specs,
        out_specs=pl.BlockSpec((step_rows, D_MODEL), lambda i, ta, tb, na: (i, 0)),
    )
    return pl.pallas_call(
        _moe_body,
        grid_spec=grid_spec,
        out_shape=jax.ShapeDtypeStruct((xs.shape[0], D_MODEL), F32),
        compiler_params=_cp(("arbitrary",)),
        name="moe_experts",
    )(tile_a, tile_b, n_act, xs, nf, *((wg, wu, wd) * (2 * MOE_STEP_TILES)))


def _unpermute_body(pos_ref, ys_ref, o_ref, sem, *, tm):
    def group(g, c):
        for u in range(SUB):
            row = ys_ref.at[pl.ds(pos_ref[0, 0, g * SUB + u], 1), :]
            pltpu.make_async_copy(row, o_ref.at[g, pl.ds(u, 1), :], sem).start()
        return c

    lax.fori_loop(0, tm // SUB, group, 0)
    _wait_rows(o_ref, sem)


def _unpermute(pos, ys, tm):
    t = pos.shape[0] * tm
    return pl.pallas_call(
        functools.partial(_unpermute_body, tm=tm),
        grid=(t // tm,),
        in_specs=[pl.BlockSpec((1, 1, tm), lambda i: (i, 0, 0), memory_space=pltpu.SMEM),
                  pl.BlockSpec(memory_space=pl.ANY)],
        out_specs=pl.BlockSpec((tm // SUB, SUB, D_MODEL), lambda i: (i, 0, 0)),
        out_shape=jax.ShapeDtypeStruct((t // SUB, SUB, D_MODEL), F32),
        scratch_shapes=[pltpu.SemaphoreType.DMA(())],
        compiler_params=_cp(("arbitrary",)),
        name="moe_unpermute",
    )(pos, ys)


def _rope_lane_freq():
    half = ROT_DIM // 2
    inv_freq = jnp.float32(ROPE_THETA) ** (-jnp.arange(half, dtype=jnp.float32) * (2.0 / ROT_DIM))
    dim = np.arange(LANES) % HEAD_DIM
    return inv_freq[dim % half][None, :], dim < half, (dim >= half) & (dim < ROT_DIM)


def _rope_patterns(cos, sin, first, second):
    return (jnp.where(first | second, cos, 1.0), jnp.where(first, -sin, 0.0), jnp.where(second, sin, 0.0))


def _rope_tables(pos):
    freq, first, second = _rope_lane_freq()
    ang = pos.astype(jnp.float32)[:, None] * freq
    return _rope_patterns(jnp.cos(ang), jnp.sin(ang), first, second)


def _rope_tables_padded(n_blocks):
    freq, first, second = _rope_lane_freq()
    ang_a = (jnp.arange(n_blocks, dtype=jnp.int32) * BLOCK).astype(jnp.float32)[:, None] * freq
    ang_b = (jnp.arange(BLOCK, dtype=jnp.int32) - PAD).astype(jnp.float32)[:, None] * freq
    ca, sa = jnp.cos(ang_a)[:, None, :], jnp.sin(ang_a)[:, None, :]
    cb, sb = jnp.cos(ang_b)[None], jnp.sin(ang_b)[None]
    flat = lambda t: t.reshape(n_blocks * BLOCK, LANES)
    return _rope_patterns(flat(ca * cb - sa * sb), flat(sa * cb + ca * sb), first, second)


def _seg_ones(n):
    idx = np.arange(n) // HEAD_DIM
    return jnp.asarray(idx[:, None] == idx[None, :], BF16)


def _bucket_experts():
    ea, eb = [], []
    for g in range(N_GROUPS):
        for a in range(EXPERTS_PER_GROUP):
            for b in range(a + 1, EXPERTS_PER_GROUP):
                ea.append(g * EXPERTS_PER_GROUP + a)
                eb.append(g * EXPERTS_PER_GROUP + b)
    return np.asarray(ea, np.int32), np.asarray(eb, np.int32)


def _dispatch_plan(counts, n_tiles):
    padded = ((counts + MOE_TM - 1) // MOE_TM) * MOE_TM
    ends = jnp.cumsum(padded)
    offs = ends - padded
    n_act = jnp.maximum(ends[-1] // MOE_TM, 1)
    starts = jnp.arange(n_tiles, dtype=jnp.int32) * MOE_TM
    tile_bucket = jnp.minimum(jnp.sum(starts[:, None] >= ends[None, :], axis=1), N_BUCKETS - 1)
    ea, eb = _bucket_experts()
    onehot = tile_bucket[:, None] == jnp.arange(N_BUCKETS)[None, :]
    tile_a = jnp.sum(jnp.where(onehot, ea[None, :], 0), axis=1).astype(jnp.int32)
    tile_b = jnp.sum(jnp.where(onehot, eb[None, :], 0), axis=1).astype(jnp.int32)
    fill_tile = jnp.maximum(ends // MOE_TM - 1, 0).astype(jnp.int32)
    return offs, tile_a, tile_b, n_act.astype(jnp.int32).reshape(1), fill_tile


def _positions(offs, bucket, rank):
    onehot = bucket[..., None] == jnp.arange(N_BUCKETS, dtype=jnp.int32)
    return (jnp.sum(jnp.where(onehot, offs.astype(jnp.int32), 0), axis=-1) + rank).astype(jnp.int32)


def kernel(x_prompt, x_sample, cache_k, cache_v, state_conv, meta_tokens, norm_mix, w_in, conv_w, q_norm, k_norm,
           attn_sinks, w_conv_out, w_attn_out, w_o, norm_ffn, w_router_group, b_router_group, w_router_expert,
           b_router_expert, w_exp_gate, w_exp_up, w_exp_down):
    batch, seq, _ = x_prompt.shape
    depth = w_in.shape[0]
    n_dec = x_sample.shape[0]
    past_len = PAST_LEN
    lp = PAD + N_META + seq
    tm_in, tm_out, qb = 640, 640, 13
    tm_move, tm_last = 3328, 4096
    assert PAD + N_META == BLOCK and seq % BLOCK == 0 and tm_in % BLOCK == 0
    assert lp % tm_in == 0 and (batch * lp) % tm_out == 0 and lp % (qb * BLOCK) == 0
    assert (batch * lp) % tm_move == 0 and (batch * seq) % tm_last == 0
    assert x_sample.shape[1] == 1 and cache_k.shape[2] == WINDOW and past_len >= WINDOW

    t_prompt = batch * lp
    t_all = t_prompt + n_dec
    n_tiles = -(-(t_all + N_BUCKETS * (MOE_TM - 1)) // MOE_TM)
    n_tiles = -(-n_tiles // MOE_STEP_TILES) * MOE_STEP_TILES

    meta = jnp.broadcast_to(meta_tokens[None].astype(F32), (batch, N_META, D_MODEL))
    head = jnp.concatenate([jnp.zeros((batch, PAD, D_MODEL), F32), meta], axis=1)
    xs = x_sample.reshape(n_dec, D_MODEL)

    rope_p = _rope_tables_padded(lp // BLOCK)
    rope_s = _rope_tables(jnp.full((1,), past_len, jnp.int32))
    s256, s128 = _seg_ones(256), _seg_ones(LANES)
    tri_p = jnp.asarray(np.triu(np.ones((tm_out, tm_out)), 1), BF16)
    tri_s = jnp.asarray(np.triu(np.ones((n_dec, n_dec)), 1), BF16)
    zero_cnt = jnp.zeros((LANES, LANES), F32)

    w_mix = w_in.astype(BF16)
    w_gate = w_in[:, :, D_MIX:].astype(BF16)
    wco, wao, wo = w_conv_out.astype(BF16), w_attn_out.astype(BF16), w_o.astype(BF16)
    wg32 = w_exp_gate.reshape(depth * N_EXPERTS, D_MODEL, D_EXPERT)
    wu32 = w_exp_up.reshape(depth * N_EXPERTS, D_MODEL, D_EXPERT)
    wd32 = w_exp_down.reshape(depth * N_EXPERTS, D_EXPERT, D_MODEL)
    nm, nf = norm_mix.reshape(depth, 1, D_MODEL), norm_ffn.reshape(depth, 1, D_MODEL)
    qn = (jnp.tile(q_norm, (1, N_HEADS)) * Q_SCALE).reshape(depth, 1, D_Q)
    kn = jnp.tile(k_norm, (1, N_KV_HEADS)).reshape(depth, 1, D_KV)
    r_pad = LANES - N_GROUPS - N_EXPERTS
    wr = jnp.concatenate([w_router_group, w_router_expert, jnp.zeros((depth, D_MODEL, r_pad), F32)], axis=-1).astype(BF16)
    br = jnp.concatenate([b_router_group, b_router_expert, jnp.zeros((depth, r_pad), F32)], axis=-1).reshape(depth, 1, LANES)
    sinks = attn_sinks.astype(F32) * LOG2E
    sinkb = jnp.broadcast_to(sinks[:, :, None], (depth, N_HEADS, LANES))
    ck = cache_k.reshape(depth, n_dec, WINDOW, D_KV)
    cv = cache_v.reshape(depth, n_dec, WINDOW, D_KV)

    outs = {k: [] for k in ("kp", "vp", "cp", "ks", "vs", "cs")}
    for l in range(depth):
        if l == 0:
            cy, q, k, v, ulast, xp = _prompt_in(l, x_prompt, nm, w_mix, conv_w, qn, kn, rope_p, s256, s128, tm_in, head)
        else:
            cy, q, k, v, ulast = _prompt_in(l, xp, nm, w_mix, conv_w, qn, kn, rope_p, s256, s128, tm_in)
        ao = _prompt_attn(l, q, k, v, sinks, qb)
        xep, rankp, bktp, cnt = _mix_out(
            l, xp.reshape(t_prompt, D_MODEL), cy.reshape(t_prompt, D_CONV), ao.reshape(t_prompt, D_Q),
            nm, w_gate, wco, wao, wo, nf, wr, br, tri_p, zero_cnt, tm_out)
        outs["kp"].append(k[:, lp - WINDOW:].reshape(batch, WINDOW, N_KV_HEADS, HEAD_DIM))
        outs["vp"].append(v[:, lp - WINDOW:].reshape(batch, WINDOW, N_KV_HEADS, HEAD_DIM))
        outs["cp"].append(ulast[:, 8 - (CONV_W - 1):])

        c0, c1 = state_conv[l, :, 0, :], state_conv[l, :, 1, :]
        cys, qx, ksn, vsn, us = _sample_in(l, xs, c0, c1, nm, w_mix, conv_w, qn, kn, rope_s, s256, s128)
        ox, nk, nv = _sample_attn(l, jnp.transpose(qx, (1, 0, 2)), sinkb, ck, cv, ksn, vsn)
        ox = ox.reshape(n_dec, N_KV_HEADS, GQA, N_KV_HEADS, HEAD_DIM)
        aos = jnp.stack([ox[:, j, :, j, :] for j in range(N_KV_HEADS)], axis=1).reshape(n_dec, D_Q).astype(BF16)
        xes, ranks, bkts, cnt = _mix_out(l, xs, cys, aos, nm, w_gate, wco, wao, wo, nf, wr, br, tri_s, cnt, n_dec)
        outs["ks"].append(nk.reshape(n_dec, WINDOW, N_KV_HEADS, HEAD_DIM))
        outs["vs"].append(nv.reshape(n_dec, WINDOW, N_KV_HEADS, HEAD_DIM))
        outs["cs"].append(jnp.stack([c1, us], axis=1))

        counts = cnt[:N_BUCKETS, 0].astype(jnp.int32)
        offs, tile_a, tile_b, n_act, fill_tile = _dispatch_plan(counts, n_tiles)
        posp = _positions(offs, bktp, rankp)
        poss = _positions(offs, bkts, ranks)
        by8 = lambda a: a.reshape(a.shape[0] // SUB, SUB, a.shape[1])
        sorted_rows, wg, wu, wd = _scatter_rows(l, fill_tile, n_act, posp.reshape(N_EXPERTS, 1, t_prompt // N_EXPERTS),
                                                by8(xep), n_tiles, wg32, wu32, wd32)
        posp = posp.reshape(t_prompt // tm_move, 1, tm_move)
        sorted_rows = _scatter_more_rows(poss, by8(xes), sorted_rows, n_dec)
        ys = _moe_experts(l, tile_a, tile_b, n_act, sorted_rows, nf, wg, wu, wd)
        xs = _unpermute(poss, ys, n_dec).reshape(n_dec, D_MODEL)
        if l + 1 < depth:
            xp = _unpermute(posp, ys, tm_move).reshape(batch, lp, D_MODEL)
        else:
            pos_tok = posp.reshape(batch, lp)[:, PAD + N_META:].reshape(batch * seq // tm_last, 1, tm_last)
            y_prompt = _unpermute(pos_tok, ys, tm_last).reshape(batch, seq, D_MODEL)

    y_sample = xs.reshape(n_dec, 1, D_MODEL)
    st = lambda k: jnp.stack(outs[k])
    return (y_prompt, y_sample, st("kp"), st("vp"), st("cp"), st("ks"), st("vs"), st("cs"))
```

```python
import functools
import math

import jax
import jax.numpy as jnp
import numpy as np
from jax import lax
from jax.experimental import pallas as pl
from jax.experimental.pallas import tpu as pltpu

D_MODEL = 1024
N_META = 16
D_CONV = D_MODEL
CONV_W = 3
N_HEADS = 16
N_KV_HEADS = 2
HEAD_DIM = 64
GQA = N_HEADS // N_KV_HEADS
ROT_DIM = HEAD_DIM // 4
ROPE_THETA = 500000.0
WINDOW = 128
PAST_LEN = 8192
BLOCK = 128
N_GROUPS = 4
EXPERTS_PER_GROUP = 4
N_EXPERTS = N_GROUPS * EXPERTS_PER_GROUP
D_EXPERT = 512
EPS = 1e-6
NEG = -1e30
D_Q = N_HEADS * HEAD_DIM
D_KV = N_KV_HEADS * HEAD_DIM
C_B, C_C, C_HC = 0, D_CONV, 2 * D_CONV
C_Q = 3 * D_CONV
C_K = C_Q + D_Q
C_V = C_K + D_KV
C_G = C_V + D_KV
D_MIX = C_G
D_IN = C_G + 2 * D_MODEL

LANES = 128
SUBLANES = 8
PAD = (-N_META) % BLOCK
N_PAIRS = 6
N_BUCKETS = N_GROUPS * N_PAIRS
MOE_TM = 256
ROW_W = D_MODEL + LANES
LOG2E = math.log2(math.e)
Q_SCALE = HEAD_DIM ** -0.5 * LOG2E

F32 = jnp.float32
BF16 = jnp.bfloat16
VMEM_LIMIT = 56 * 1024 * 1024


def _cp(sem, vmem=VMEM_LIMIT):
    return pltpu.CompilerParams(dimension_semantics=sem, vmem_limit_bytes=vmem)


def _const_spec(shape):
    nd = len(shape)
    return pl.BlockSpec(shape, lambda *_: (0,) * nd, pipeline_mode=pl.Buffered(1))


def _layer_spec(l, *shape):
    n = len(shape)
    return pl.BlockSpec((None,) + shape, lambda *_: (l,) + (0,) * n, pipeline_mode=pl.Buffered(1))


def _w_mix_spec(l):
    return pl.BlockSpec((None, D_MODEL, D_MIX), lambda *_: (l, 0, 0), pipeline_mode=pl.Buffered(1))


def _dot(a, b):
    return jnp.dot(a, b, preferred_element_type=F32)


def _seg_mean_sq(x, seg_ones):
    return _dot((x * x).astype(BF16), seg_ones) * (1.0 / HEAD_DIM)


def _rope128(t, cos, sin_pm, first):
    partner = jnp.where(first, pltpu.roll(t, LANES - ROT_DIM // 2, 1), pltpu.roll(t, ROT_DIM // 2, 1))
    return t * cos + partner * sin_pm


def _rms_scale(x):
    return lax.rsqrt(jnp.mean(x * x, axis=-1, keepdims=True) + EPS)


def _rms_rows(x, g):
    return (x * _rms_scale(x)) * g


def _qk_project(hb, w_ref, s256, s128):
    qs = [_dot(hb, w_ref[:, C_Q + c * 256:C_Q + (c + 1) * 256]) for c in range(D_Q // 256)]
    kv = _dot(hb, w_ref[:, C_K:C_K + 2 * D_KV])
    kc, v = kv[:, 0:D_KV], kv[:, D_KV:]
    return qs, [_seg_mean_sq(qc, s256) for qc in qs], kc, _seg_mean_sq(kc, s128), v


def _qk_finish(proj, qn, kn, cos, sneg, spos, store_q):
    qs, q_ms, kc, k_ms, _ = proj
    sin_pm = sneg + spos
    lane = lax.broadcasted_iota(jnp.int32, cos.shape, 1)
    first = lax.bitwise_and(lane, HEAD_DIM - 1) < ROT_DIM // 2
    for c, (qc, ms) in enumerate(zip(qs, q_ms)):
        qc = (qc * lax.rsqrt(ms + EPS)) * qn[:, c * 256:(c + 1) * 256]
        for s in range(2):
            r = _rope128(qc[:, s * LANES:(s + 1) * LANES], cos, sin_pm, first)
            store_q(2 * c + s, r.astype(BF16))
    kc = (kc * lax.rsqrt(k_ms + EPS)) * kn
    return _rope128(kc, cos, sin_pm, first)


def _in_body(x_ref, nm_ref, w_ref, cw_ref, qn_ref, kn_ref, cos_ref, sneg_ref, spos_ref,
             s256_ref, s128_ref, cy_ref, q_ref, k_ref, v_ref, ul_ref, us_ref, *, tm, parts):
    i = pl.program_id(1)
    th = tm // parts

    @pl.when(i == 0)
    def _():
        us_ref[0:8, :] = jnp.zeros((8, D_CONV), F32)

    projs = []
    for h in range(parts):
        r0 = h * th
        hb = _rms_rows(x_ref[0, r0:r0 + th, :], nm_ref[...]).astype(BF16)
        projs.append(_qk_project(hb, w_ref, s256_ref[...], s128_ref[...]))
        u = _dot(hb, w_ref[:, C_C:C_C + D_CONV]) * _dot(hb, w_ref[:, C_HC:C_HC + D_CONV])
        row = lax.broadcasted_iota(jnp.int32, (th, 1), 0) + (i * tm + r0)
        u = jnp.where(row >= PAD, u, 0.0)
        us_ref[8 + r0:8 + r0 + th, :] = u
        conv = (us_ref[6 + r0:6 + r0 + th, :] * cw_ref[0:1, :] + us_ref[7 + r0:7 + r0 + th, :] * cw_ref[1:2, :]) \
            + u * cw_ref[2:3, :]
        cy_ref[0, r0:r0 + th, :] = (_dot(hb, w_ref[:, C_B:C_B + D_CONV]) * conv).astype(BF16)
        v_ref[0, r0:r0 + th, :] = projs[h][4]

    last = us_ref[tm:tm + 8, :]
    ul_ref[0] = last
    us_ref[0:8, :] = last

    for h in range(parts):
        rows = slice(h * th, (h + 1) * th)

        def store_q(slab, val, rows=rows):
            q_ref[0, rows, slab * LANES:(slab + 1) * LANES] = val

        k_ref[0, rows, :] = _qk_finish(projs[h], qn_ref[...], kn_ref[...], cos_ref[rows, :], sneg_ref[rows, :],
                                       spos_ref[rows, :], store_q)


def _in_first_body(head_ref, *refs, tm, parts):
    nb = tm // BLOCK
    blocks, rest, xpad_ref, us_ref = refs[:nb], refs[nb:-2], refs[-2], refs[-1]
    first = pl.program_id(1) == 0
    xpad_ref[0, 0:BLOCK, :] = jnp.where(first, head_ref[0], blocks[0][0])
    for j in range(1, nb):
        xpad_ref[0, j * BLOCK:(j + 1) * BLOCK, :] = blocks[j][0]
    _in_body(xpad_ref, *rest, us_ref, tm=tm, parts=parts)


def _prompt_in(l, x, nm, w_mix, cw, qn, kn, rope, s256, s128, tm, head=None):
    b = x.shape[0]
    lp = x.shape[1] if head is None else x.shape[1] + BLOCK
    nt = lp // tm
    nb = tm // BLOCK
    cos, sneg, spos = rope
    tok = lambda w: pl.BlockSpec((1, tm, w), lambda bi, i: (bi, i, 0))
    tab = pl.BlockSpec((tm, LANES), lambda bi, i: (i, 0))
    params = [_layer_spec(l, 1, D_MODEL), _w_mix_spec(l),
              _layer_spec(l, CONV_W, D_CONV), _layer_spec(l, 1, D_Q), _layer_spec(l, 1, D_KV),
              tab, tab, tab, _const_spec((256, 256)), _const_spec((LANES, LANES))]
    out_specs = [tok(D_CONV), tok(D_Q), tok(D_KV), tok(D_KV), pl.BlockSpec((1, 8, D_CONV), lambda bi, i: (bi, 0, 0))]
    out_shape = [jax.ShapeDtypeStruct((b, lp, D_CONV), BF16), jax.ShapeDtypeStruct((b, lp, D_Q), BF16),
                 jax.ShapeDtypeStruct((b, lp, D_KV), F32), jax.ShapeDtypeStruct((b, lp, D_KV), F32),
                 jax.ShapeDtypeStruct((b, 8, D_CONV), F32)]
    if head is None:
        body, x_specs, x_args = _in_body, [tok(D_MODEL)], (x,)
    else:
        blk = lambda j: pl.BlockSpec((1, BLOCK, D_MODEL), lambda bi, i: (bi, jnp.maximum(nb * i - 1 + j, 0), 0))
        body = _in_first_body
        x_specs = [pl.BlockSpec((1, BLOCK, D_MODEL), lambda bi, i: (bi, 0, 0))] + [blk(j) for j in range(nb)]
        x_args = (head,) + (x,) * nb
        out_specs.append(tok(D_MODEL))
        out_shape.append(jax.ShapeDtypeStruct((b, lp, D_MODEL), F32))
    return pl.pallas_call(
        functools.partial(body, tm=tm, parts=2),
        grid=(b, nt),
        in_specs=x_specs + params,
        out_specs=out_specs,
        out_shape=out_shape,
        scratch_shapes=[pltpu.VMEM((tm + 8, D_CONV), F32)],
        compiler_params=_cp(("arbitrary", "arbitrary")),
        name="prompt_in",
    )(*x_args, nm, w_mix, cw, qn, kn, cos, sneg, spos, s256, s128)


def _sin_body(x_ref, c0_ref, c1_ref, nm_ref, w_ref, cw_ref, qn_ref, kn_ref, cos_ref, sneg_ref,
              spos_ref, s256_ref, s128_ref, cy_ref, qx_ref, k_ref, v_ref, u_ref):
    hb = _rms_rows(x_ref[...], nm_ref[...]).astype(BF16)
    u = _dot(hb, w_ref[:, C_C:C_C + D_CONV]) * _dot(hb, w_ref[:, C_HC:C_HC + D_CONV])
    u_ref[...] = u
    conv = (c0_ref[...] * cw_ref[0:1, :] + c1_ref[...] * cw_ref[1:2, :]) + u * cw_ref[2:3, :]
    cy_ref[...] = (_dot(hb, w_ref[:, C_B:C_B + D_CONV]) * conv).astype(BF16)

    lane = lax.broadcasted_iota(jnp.int32, (x_ref.shape[0], LANES), 1)
    low = lane < HEAD_DIM

    def store_q(slab, val):
        valf = val.astype(F32)
        swapped = pltpu.roll(valf, HEAD_DIM, 1)
        zero = jnp.zeros_like(valf)
        for h in (2 * slab, 2 * slab + 1):
            src = valf if (h % 2) == (h // GQA) else swapped
            keep = low if (h // GQA) == 0 else jnp.logical_not(low)
            qx_ref[h] = jnp.where(keep, src, zero).astype(BF16)

    cos = jnp.broadcast_to(cos_ref[...], (x_ref.shape[0], LANES))
    sneg = jnp.broadcast_to(sneg_ref[...], (x_ref.shape[0], LANES))
    spos = jnp.broadcast_to(spos_ref[...], (x_ref.shape[0], LANES))
    proj = _qk_project(hb, w_ref, s256_ref[...], s128_ref[...])
    k_ref[...] = _qk_finish(proj, qn_ref[...], kn_ref[...], cos, sneg, spos, store_q)
    v_ref[...] = proj[4]


def _sample_in(l, x, c0, c1, nm, w_mix, cw, qn, kn, rope, s256, s128):
    n = x.shape[0]
    cos, sneg, spos = rope
    full = lambda *s: pl.BlockSpec(s, lambda i: (0,) * len(s))
    return pl.pallas_call(
        _sin_body,
        grid=(1,),
        in_specs=[full(n, D_MODEL), full(n, D_CONV), full(n, D_CONV), _layer_spec(l, 1, D_MODEL),
                  _w_mix_spec(l), _layer_spec(l, CONV_W, D_CONV), _layer_spec(l, 1, D_Q),
                  _layer_spec(l, 1, D_KV), full(1, LANES), full(1, LANES), full(1, LANES), full(256, 256),
                  full(LANES, LANES)],
        out_specs=[full(n, D_CONV), full(N_HEADS, n, LANES), full(n, D_KV), full(n, D_KV), full(n, D_CONV)],
        out_shape=[jax.ShapeDtypeStruct((n, D_CONV), BF16), jax.ShapeDtypeStruct((N_HEADS, n, LANES), BF16),
                   jax.ShapeDtypeStruct((n, D_KV), F32), jax.ShapeDtypeStruct((n, D_KV), F32),
                   jax.ShapeDtypeStruct((n, D_CONV), F32)],
        compiler_params=_cp(("arbitrary",)),
        name="sample_in",
    )(x, c0, c1, nm, w_mix, cw, qn, kn, cos, sneg, spos, s256, s128)


def _attn_body(sink_ref, q_ref, kp_ref, kc_ref, vp_ref, vc_ref, o_ref, ke_ref, ko_ref, vt_ref, *, qb, l):
    i = pl.program_id(1)
    lane = lax.broadcasted_iota(jnp.int32, (BLOCK, LANES), 1)
    low = lane < HEAD_DIM

    def prep_k(src, blk0, nblk):
        for t in range(nblk):
            blk = src[0, t * BLOCK:(t + 1) * BLOCK, :]
            swp = pltpu.roll(blk, HEAD_DIM, 1)
            zero = jnp.zeros_like(blk)
            rows = slice((blk0 + t) * BLOCK, (blk0 + t + 1) * BLOCK)
            ke_ref[0, rows, :] = jnp.where(low, blk, zero).astype(BF16)
            ko_ref[0, rows, :] = jnp.where(low, zero, swp).astype(BF16)
            ke_ref[1, rows, :] = jnp.where(low, swp, zero).astype(BF16)
            ko_ref[1, rows, :] = jnp.where(low, zero, blk).astype(BF16)

    def prep_v(src, blk0, nblk):
        for t in range(nblk):
            vt = jnp.transpose(src[0, t * BLOCK:(t + 1) * BLOCK, :]).astype(BF16)
            for j in range(N_KV_HEADS):
                vt_ref[blk0 + t, j] = vt[j * HEAD_DIM:(j + 1) * HEAD_DIM, :]

    prep_k(kp_ref, 0, 1)
    prep_k(kc_ref, 1, qb)
    prep_v(vp_ref, 0, 1)
    prep_v(vc_ref, 1, qb)

    c = lax.broadcasted_iota(jnp.int32, (2 * BLOCK, BLOCK), 0)
    r = lax.broadcasted_iota(jnp.int32, (2 * BLOCK, BLOCK), 1)
    diff = r - (c - BLOCK)
    band = (diff >= 0) & (diff < WINDOW)
    nt = (((1,), (1,)), ((), ()))

    def one_block(b, carry):
        r0 = pl.multiple_of(b * BLOCK, BLOCK)
        kpos = (i * qb + b) * BLOCK + c - BLOCK - PAD
        bias = jnp.where(band & (kpos >= 0), 0.0, NEG)
        for m in range(N_HEADS // 2):
            j = (2 * m) // GQA
            q2 = q_ref[0, pl.ds(r0, BLOCK), m * LANES:(m + 1) * LANES]
            halves = []
            for par, k_ref in ((0, ke_ref), (1, ko_ref)):
                st = lax.dot_general(k_ref[j, pl.ds(r0, 2 * BLOCK), :], q2, nt, preferred_element_type=F32)
                st = st + bias
                sink = sink_ref[l, 2 * m + par]
                mx = jnp.maximum(jnp.max(st, axis=0, keepdims=True), sink)
                p = jnp.exp2(st - mx)
                den = jnp.sum(p, axis=0, keepdims=True) + jnp.exp2(sink - mx)
                pb = p.astype(BF16)
                ot = _dot(vt_ref[b, j], pb[0:BLOCK]) + _dot(vt_ref[b + 1, j], pb[BLOCK:])
                halves.append(ot * (1.0 / den))
            o2 = jnp.transpose(jnp.concatenate(halves, axis=0))
            o_ref[0, pl.ds(r0, BLOCK), m * LANES:(m + 1) * LANES] = o2.astype(BF16)
        return carry

    lax.fori_loop(0, qb, one_block, 0, unroll=True)


def _prompt_attn(l, q, k, v, sinks, qb):
    b, lp, _ = q.shape
    nsteps = lp // (qb * BLOCK)
    cur = lambda w: pl.BlockSpec((1, qb * BLOCK, w), lambda bi, i: (bi, i, 0))
    prev = pl.BlockSpec((1, BLOCK, D_KV), lambda bi, i: (bi, jnp.maximum(i * qb - 1, 0), 0))
    ext = ((qb + 1) * BLOCK, LANES)
    return pl.pallas_call(
        functools.partial(_attn_body, qb=qb, l=l),
        grid=(b, nsteps),
        in_specs=[pl.BlockSpec(memory_space=pltpu.SMEM), cur(D_Q), prev, cur(D_KV), prev, cur(D_KV)],
        out_specs=cur(D_Q),
        out_shape=jax.ShapeDtypeStruct((b, lp, D_Q), BF16),
        scratch_shapes=[pltpu.VMEM((N_KV_HEADS,) + ext, BF16), pltpu.VMEM((N_KV_HEADS,) + ext, BF16),
                        pltpu.VMEM((qb + 1, N_KV_HEADS, HEAD_DIM, BLOCK), BF16)],
        compiler_params=_cp(("arbitrary", "arbitrary")),
        name="prompt_attn",
    )(sinks, q, k, k, v, v)


def _sattn_body(qx_ref, sink_ref, ck_ref, cv_ref, kn_ref, vn_ref, ox_ref, nk_ref, nv_ref, *, tb):
    def window(c_ref, n_ref, t):
        return jnp.concatenate([c_ref[t, 1:WINDOW, :], n_ref[t:t + 1, :]], axis=0)

    for t in range(tb):
        nk_ref[t] = window(ck_ref, kn_ref, t)
        nv_ref[t] = window(cv_ref, vn_ref, t)
    nt = (((1,), (1,)), ((), ()))
    s = jnp.concatenate([lax.dot_general(qx_ref[t], window(ck_ref, kn_ref, t).astype(BF16), nt,
                                         preferred_element_type=F32) for t in range(tb)], axis=0)
    sink = jnp.concatenate([sink_ref[...][:, 0:1]] * tb, axis=0)
    m = jnp.maximum(jnp.max(s, axis=-1, keepdims=True), sink)
    p = jnp.exp2(s - m)
    rden = 1.0 / (jnp.sum(p, axis=-1, keepdims=True) + jnp.exp2(sink - m))
    pb = p.astype(BF16)
    for t in range(tb):
        rows = slice(t * N_HEADS, (t + 1) * N_HEADS)
        ox_ref[t] = _dot(pb[rows], window(cv_ref, vn_ref, t).astype(BF16)) * rden[rows]


def _sample_attn(l, qx, sinkb, ck, cv, kn, vn, tb=16):
    n = qx.shape[0]
    blk3 = lambda a, c: pl.BlockSpec((tb, a, c), lambda i: (i, 0, 0))
    cache = pl.BlockSpec((None, tb, WINDOW, D_KV), lambda i: (l, i, 0, 0))
    row = pl.BlockSpec((tb, D_KV), lambda i: (i, 0))
    return pl.pallas_call(
        functools.partial(_sattn_body, tb=tb),
        grid=(n // tb,),
        in_specs=[blk3(N_HEADS, LANES), _layer_spec(l, N_HEADS, LANES), cache, cache, row, row],
        out_specs=[blk3(N_HEADS, LANES), blk3(WINDOW, D_KV), blk3(WINDOW, D_KV)],
        out_shape=[jax.ShapeDtypeStruct((n, N_HEADS, LANES), F32),
                   jax.ShapeDtypeStruct((n, WINDOW, D_KV), F32), jax.ShapeDtypeStruct((n, WINDOW, D_KV), F32)],
        compiler_params=_cp(("arbitrary",)),
        name="sample_attn",
    )(qx, sinkb, ck, cv, kn, vn)


ROUTE_ROWS = 24
RINV_LANE = LANES - 1


def _route(lt):
    top = lt[0:ROUTE_ROWS, :]
    rows = top.shape[1]
    rowf = lax.broadcasted_iota(jnp.int32, top.shape, 0).astype(F32)
    big = jnp.float32(3e38)
    far = jnp.float32(LANES)
    cmax = lambda a: jnp.max(a, axis=0, keepdims=True)
    cmin = lambda a: jnp.min(a, axis=0, keepdims=True)

    gmask = rowf < N_GROUPS
    gl = jnp.where(gmask, top, -big)
    gmax = cmax(gl)
    grp = cmin(jnp.where(gmask & (gl == gmax), rowf, far))
    p_grp = 1.0 / jnp.sum(jnp.where(gmask, jnp.exp(gl - gmax), 0.0), axis=0, keepdims=True)

    e_lo = N_GROUPS + EXPERTS_PER_GROUP * grp
    emask = (rowf >= e_lo) & (rowf < e_lo + EXPERTS_PER_GROUP)
    el = jnp.where(emask, top, -big)
    v1 = cmax(el)
    i1 = cmin(jnp.where(emask & (el == v1), rowf, far))
    rest = emask & (rowf != i1)
    el2 = jnp.where(rest, top, -big)
    v2 = cmax(el2)
    i2 = cmin(jnp.where(rest & (el2 == v2), rowf, far))
    e = jnp.exp(v2 - v1)
    w1 = (1.0 / (1.0 + e)) * p_grp
    w2 = (e / (1.0 + e)) * p_grp
    first_low = i1 < i2
    ea = jnp.where(first_low, i1, i2) - e_lo
    eb = jnp.where(first_low, i2, i1) - e_lo
    w_a = jnp.where(first_low, w1, w2)
    w_b = jnp.where(first_low, w2, w1)
    pair = jnp.where(ea == 0.0, 0.0, jnp.where(ea == 1.0, 3.0, 5.0)) + (eb - ea - 1.0)
    bucket = grp * N_PAIRS + pair

    r8 = lax.broadcasted_iota(jnp.int32, (SUBLANES, rows), 0)
    head = jnp.where(r8 == 0, w_a, jnp.where(r8 == 1, w_b, jnp.where(r8 == 2, lt[RINV_LANE:RINV_LANE + 1, :], 0.0)))
    return bucket, jnp.concatenate([head, jnp.zeros((LANES - SUBLANES, rows), F32)], axis=0)


def _out_body(x_ref, cy_ref, ao_ref, nm_ref, wg_ref, wco_ref, wao_ref, wo_ref, nf_ref, wr_ref, br_ref,
              tri_ref, cin_ref, xe_ref, rank_ref, bkt_ref, cnt_ref, run_ref, *, tm, parts):
    i = pl.program_id(0)
    th = tm // parts
    stage = []
    for h in range(parts):
        rows = slice(h * th, (h + 1) * th)
        hb = _rms_rows(x_ref[rows, :], nm_ref[...]).astype(BF16)
        stage.append((_dot(cy_ref[rows, :], wco_ref[...]), _dot(ao_ref[rows, :], wao_ref[...]),
                      _dot(hb, wg_ref[:, 0:D_MODEL]), _dot(hb, wg_ref[:, D_MODEL:])))
    for h, (ya, yb, gc, ga) in enumerate(stage):
        rows = slice(h * th, (h + 1) * th)
        mix = jax.nn.sigmoid(gc) * ya + jax.nn.sigmoid(ga) * yb
        xe_ref[rows, 0:D_MODEL] = x_ref[rows, :] + _dot(mix.astype(BF16), wo_ref[...])
    x1 = xe_ref[:, 0:D_MODEL]

    rinv = _rms_scale(x1)
    xnb = ((x1 * rinv) * nf_ref[...]).astype(BF16)
    logits = _dot(xnb, wr_ref[...]) + br_ref[...]
    lane = lax.broadcasted_iota(jnp.int32, (tm, LANES), 1)
    bucket, meta_t = _route(jnp.transpose(jnp.where(lane == RINV_LANE, rinv, logits)))
    xe_ref[:, D_MODEL:] = jnp.transpose(meta_t)

    @pl.when(i == 0)
    def _():
        run_ref[...] = cin_ref[...]

    sub = lax.broadcasted_iota(jnp.int32, (LANES, tm), 0).astype(F32)
    oht = (sub == bucket).astype(F32)
    before = _dot(oht.astype(BF16), tri_ref[...]) + run_ref[:, 0:1]
    rank_ref[0] = jnp.sum(oht * before, axis=0, keepdims=True).astype(jnp.int32)
    bkt_ref[0] = bucket.astype(jnp.int32)
    run_ref[...] = run_ref[...] + jnp.sum(oht, axis=-1, keepdims=True)
    cnt_ref[...] = run_ref[...]


def _mix_out(l, x, cy, ao, nm, w_gate, wco, wao, wo, nf, wr, br, tri, cnt_in, tm):
    t = x.shape[0]
    nt = t // tm
    tok = lambda w: pl.BlockSpec((tm, w), lambda i: (i, 0))
    rowi = pl.BlockSpec((1, 1, tm), lambda i: (i, 0, 0))
    sq = (D_MODEL, D_MODEL)
    return pl.pallas_call(
        functools.partial(_out_body, tm=tm, parts=2 if tm >= 4 * LANES else 1),
        grid=(nt,),
        in_specs=[tok(D_MODEL), tok(D_CONV), tok(D_Q), _layer_spec(l, 1, D_MODEL),
                  _layer_spec(l, D_MODEL, 2 * D_MODEL), _layer_spec(l, *sq), _layer_spec(l, *sq), _layer_spec(l, *sq),
                  _layer_spec(l, 1, D_MODEL), _layer_spec(l, D_MODEL, LANES), _layer_spec(l, 1, LANES),
                  _const_spec((tm, tm)), _const_spec((LANES, LANES))],
        out_specs=[tok(ROW_W), rowi, rowi, pl.BlockSpec((LANES, LANES), lambda i: (0, 0))],
        out_shape=[jax.ShapeDtypeStruct((t, ROW_W), F32),
                   jax.ShapeDtypeStruct((nt, 1, tm), jnp.int32), jax.ShapeDtypeStruct((nt, 1, tm), jnp.int32),
                   jax.ShapeDtypeStruct((LANES, LANES), F32)],
        scratch_shapes=[pltpu.VMEM((LANES, LANES), F32)],
        compiler_params=_cp(("arbitrary",)),
        name="mix_out",
    )(x, cy, ao, nm, w_gate, wco, wao, wo, nf, wr, br, tri, cnt_in)


SUB = SUBLANES


def _wait_rows(block_ref, sem):
    pltpu.make_async_copy(block_ref, block_ref, sem).wait()


def _scatter_tile_rows(pos_ref, src_ref, dst_ref, sem, tm):
    def group(g, c):
        for u in range(SUB):
            row = dst_ref.at[pl.ds(pos_ref[0, 0, g * SUB + u], 1), :]
            pltpu.make_async_copy(src_ref.at[g, pl.ds(u, 1), :], row, sem).start(priority=u % 2)
        return c

    lax.fori_loop(0, tm // SUB, group, 0)
    _wait_rows(src_ref, sem)


CAST_ROWS = 16


def _scatter_body(fill_ref, na_ref, pos_ref, src_ref, wg_ref, wu_ref, wd_ref, dst_ref, wgb_ref, wub_ref, wdb_ref,
                  zero_ref, sem, *, tm, n_tiles):
    tile_rows = lambda t: dst_ref.at[pl.ds(pl.multiple_of(t * MOE_TM, MOE_TM), MOE_TM), :]

    @pl.when(pl.program_id(0) == 0)
    def _():
        zero_ref[...] = jnp.zeros(zero_ref.shape, F32)
        for b in range(N_BUCKETS):
            pltpu.make_async_copy(zero_ref, tile_rows(fill_ref[b]), sem).start()

        def tail_start(t, c):
            pltpu.make_async_copy(zero_ref, tile_rows(t), sem).start()
            return c

        lax.fori_loop(na_ref[0], n_tiles, tail_start, 0)
        for b in range(N_BUCKETS):
            pltpu.make_async_copy(zero_ref, tile_rows(fill_ref[b]), sem).wait()

        def tail_wait(t, c):
            pltpu.make_async_copy(zero_ref, tile_rows(t), sem).wait()
            return c

        lax.fori_loop(na_ref[0], n_tiles, tail_wait, 0)

    n_up, n_down = D_MODEL // CAST_ROWS, D_EXPERT // CAST_ROWS

    def cast(src, dst, chunk):
        rows = pl.ds(pl.multiple_of(chunk * CAST_ROWS, CAST_ROWS), CAST_ROWS)
        dst[0, rows, :] = src[0, rows, :].astype(BF16)

    def step(it, c):
        for g in (2 * it, 2 * it + 1):
            for u in range(SUB):
                row = dst_ref.at[pl.ds(pos_ref[0, 0, g * SUB + u], 1), :]
                pltpu.make_async_copy(src_ref.at[g, pl.ds(u, 1), :], row, sem).start(priority=u % 2)
        cast(wg_ref, wgb_ref, jnp.minimum(it, n_up - 1))
        cast(wu_ref, wub_ref, jnp.minimum(it, n_up - 1))
        cast(wd_ref, wdb_ref, jnp.minimum(it, n_down - 1))
        return c

    assert tm % (2 * SUB) == 0 and tm // (2 * SUB) >= n_up
    lax.fori_loop(0, tm // (2 * SUB), step, 0)
    _wait_rows(src_ref, sem)


def _scatter_rows(l, fill_tile, n_act, pos, src, n_tiles, wg, wu, wd):
    nt = N_EXPERTS
    tm = src.shape[0] * SUB // nt
    w_in_spec = lambda r, c: pl.BlockSpec((1, r, c), lambda i, *_: (l * N_EXPERTS + i, 0, 0))
    w_out_spec = lambda r, c: pl.BlockSpec((1, r, c), lambda i, *_: (i, 0, 0))
    grid_spec = pltpu.PrefetchScalarGridSpec(
        num_scalar_prefetch=2,
        grid=(nt,),
        in_specs=[pl.BlockSpec((1, 1, tm), lambda i, *_: (i, 0, 0), memory_space=pltpu.SMEM),
                  pl.BlockSpec((tm // SUB, SUB, ROW_W), lambda i, *_: (i, 0, 0)),
                  w_in_spec(D_MODEL, D_EXPERT), w_in_spec(D_MODEL, D_EXPERT), w_in_spec(D_EXPERT, D_MODEL)],
        out_specs=[pl.BlockSpec(memory_space=pl.ANY),
                   w_out_spec(D_MODEL, D_EXPERT), w_out_spec(D_MODEL, D_EXPERT), w_out_spec(D_EXPERT, D_MODEL)],
        scratch_shapes=[pltpu.VMEM((MOE_TM, ROW_W), F32), pltpu.SemaphoreType.DMA(())],
    )
    return pl.pallas_call(
        functools.partial(_scatter_body, tm=tm, n_tiles=n_tiles),
        grid_spec=grid_spec,
        out_shape=[jax.ShapeDtypeStruct((n_tiles * MOE_TM, ROW_W), F32),
                   jax.ShapeDtypeStruct((N_EXPERTS, D_MODEL, D_EXPERT), BF16),
                   jax.ShapeDtypeStruct((N_EXPERTS, D_MODEL, D_EXPERT), BF16),
                   jax.ShapeDtypeStruct((N_EXPERTS, D_EXPERT, D_MODEL), BF16)],
        compiler_params=_cp(("arbitrary",)),
        name="dispatch_scatter",
    )(fill_tile, n_act, pos, src, wg, wu, wd)


def _scatter_more_body(pos_ref, src_ref, dst_in_ref, dst_ref, sem, *, tm):
    del dst_in_ref
    _scatter_tile_rows(pos_ref, src_ref, dst_ref, sem, tm)


def _scatter_more_rows(pos, src, dst, tm):
    nt = src.shape[0] * SUB // tm
    return pl.pallas_call(
        functools.partial(_scatter_more_body, tm=tm),
        grid=(nt,),
        in_specs=[pl.BlockSpec((1, 1, tm), lambda i: (i, 0, 0), memory_space=pltpu.SMEM),
                  pl.BlockSpec((tm // SUB, SUB, ROW_W), lambda i: (i, 0, 0)),
                  pl.BlockSpec(memory_space=pl.ANY)],
        out_specs=pl.BlockSpec(memory_space=pl.ANY),
        out_shape=jax.ShapeDtypeStruct(dst.shape, dst.dtype),
        scratch_shapes=[pltpu.SemaphoreType.DMA(())],
        input_output_aliases={2: 0},
        compiler_params=_cp(("arbitrary",)),
        name="dispatch_scatter_more",
    )(pos, src, dst)


MOE_STEP_TILES = 2


def _moe_body(ta_ref, tb_ref, na_ref, xs_ref, nf_ref, *refs):
    del ta_ref, tb_ref
    w_refs, y_ref = refs[:-1], refs[-1]
    first_tile = pl.program_id(0) * MOE_STEP_TILES

    @pl.when(first_tile < na_ref[0])
    def _():
        staged = []
        for t in range(MOE_STEP_TILES):
            rows = slice(t * MOE_TM, (t + 1) * MOE_TM)
            x1 = xs_ref[rows, 0:D_MODEL]
            xb = ((x1 * xs_ref[rows, D_MODEL + 2:D_MODEL + 3]) * nf_ref[...]).astype(BF16)
            w = w_refs[6 * t:6 * t + 6]
            staged.append((x1, [(_dot(xb, w[3 * k][0]), _dot(xb, w[3 * k + 1][0])) for k in range(2)]))
        for t in range(MOE_STEP_TILES):
            rows = slice(t * MOE_TM, (t + 1) * MOE_TM)
            y, ups = staged[t]
            for k, (a, u) in enumerate(ups):
                hdn = (jax.nn.silu(a) * u) * xs_ref[rows, D_MODEL + k:D_MODEL + k + 1]
                y = y + _dot(hdn.astype(BF16), w_refs[6 * t + 3 * k + 2][0])
            y_ref[rows, :] = y

    @pl.when(first_tile >= na_ref[0])
    def _():
        y_ref[...] = jnp.zeros(y_ref.shape, F32)


def _moe_experts(l, tile_a, tile_b, n_act, xs, nf, wg, wu, wd):
    step_rows = MOE_STEP_TILES * MOE_TM
    assert xs.shape[0] % step_rows == 0
    last = lambda tile, na: jnp.minimum(tile, na[0] - 1)
    expert = lambda sel, t, i, ta, tb, na: (sel(ta, tb)[last(MOE_STEP_TILES * i + t, na)], 0, 0)
    w_up = lambda sel, t: pl.BlockSpec((1, D_MODEL, D_EXPERT), functools.partial(expert, sel, t))
    w_dn = lambda sel, t: pl.BlockSpec((1, D_EXPERT, D_MODEL), functools.partial(expert, sel, t))
    sa = lambda ta, tb: ta
    sb = lambda ta, tb: tb
    w_specs = []
    for t in range(MOE_STEP_TILES):
        w_specs += [w_up(sa, t), w_up(sa, t), w_dn(sa, t), w_up(sb, t), w_up(sb, t), w_dn(sb, t)]
    grid_spec = pltpu.PrefetchScalarGridSpec(
        num_scalar_prefetch=3,
        grid=(xs.shape[0] // step_rows,),
        in_specs=[pl.BlockSpec((step_rows, ROW_W), lambda i, ta, tb, na: (jnp.minimum(i, (na[0] - 1) // MOE_STEP_TILES), 0)),
                  _layer_spec(l, 1, D_MODEL)] + w_specs,
        out_specs=pl.BlockSpec((step_rows, D_MODEL), lambda i, ta, tb, na: (i, 0)),
    )
    return pl.pallas_call(
        _moe_body,
        grid_spec=grid_spec,
        out_shape=jax.ShapeDtypeStruct((xs.shape[0], D_MODEL), F32),
        compiler_params=_cp(("arbitrary",)),
        name="moe_experts",
    )(tile_a, tile_b, n_act, xs, nf, *((wg, wu, wd) * (2 * MOE_STEP_TILES)))


def _unpermute_body(pos_ref, ys_ref, o_ref, sem, *, tm):
    def group(g, c):
        for u in range(SUB):
            row = ys_ref.at[pl.ds(pos_ref[0, 0, g * SUB + u], 1), :]
            pltpu.make_async_copy(row, o_ref.at[g, pl.ds(u, 1), :], sem).start(priority=u % 2)
        return c

    lax.fori_loop(0, tm // SUB, group, 0)
    _wait_rows(o_ref, sem)


def _unpermute(pos, ys, tm):
    t = pos.shape[0] * tm
    return pl.pallas_call(
        functools.partial(_unpermute_body, tm=tm),
        grid=(t // tm,),
        in_specs=[pl.BlockSpec((1, 1, tm), lambda i: (i, 0, 0), memory_space=pltpu.SMEM),
                  pl.BlockSpec(memory_space=pl.ANY)],
        out_specs=pl.BlockSpec((tm // SUB, SUB, D_MODEL), lambda i: (i, 0, 0)),
        out_shape=jax.ShapeDtypeStruct((t // SUB, SUB, D_MODEL), F32),
        scratch_shapes=[pltpu.SemaphoreType.DMA(())],
        compiler_params=_cp(("arbitrary",)),
        name="moe_unpermute",
    )(pos, ys)


def _rope_lane_freq():
    half = ROT_DIM // 2
    inv_freq = jnp.float32(ROPE_THETA) ** (-jnp.arange(half, dtype=jnp.float32) * (2.0 / ROT_DIM))
    dim = np.arange(LANES) % HEAD_DIM
    return inv_freq[dim % half][None, :], dim < half, (dim >= half) & (dim < ROT_DIM)


def _rope_patterns(cos, sin, first, second):
    return (jnp.where(first | second, cos, 1.0), jnp.where(first, -sin, 0.0), jnp.where(second, sin, 0.0))


def _rope_tables(pos):
    freq, first, second = _rope_lane_freq()
    ang = pos.astype(jnp.float32)[:, None] * freq
    return _rope_patterns(jnp.cos(ang), jnp.sin(ang), first, second)


def _rope_tables_padded(n_blocks):
    freq, first, second = _rope_lane_freq()
    ang_a = (jnp.arange(n_blocks, dtype=jnp.int32) * BLOCK).astype(jnp.float32)[:, None] * freq
    ang_b = (jnp.arange(BLOCK, dtype=jnp.int32) - PAD).astype(jnp.float32)[:, None] * freq
    ca, sa = jnp.cos(ang_a)[:, None, :], jnp.sin(ang_a)[:, None, :]
    cb, sb = jnp.cos(ang_b)[None], jnp.sin(ang_b)[None]
    flat = lambda t: t.reshape(n_blocks * BLOCK, LANES)
    return _rope_patterns(flat(ca * cb - sa * sb), flat(sa * cb + ca * sb), first, second)


def _seg_ones(n):
    idx = np.arange(n) // HEAD_DIM
    return jnp.asarray(idx[:, None] == idx[None, :], BF16)


def _bucket_experts():
    ea, eb = [], []
    for g in range(N_GROUPS):
        for a in range(EXPERTS_PER_GROUP):
            for b in range(a + 1, EXPERTS_PER_GROUP):
                ea.append(g * EXPERTS_PER_GROUP + a)
                eb.append(g * EXPERTS_PER_GROUP + b)
    return np.asarray(ea, np.int32), np.asarray(eb, np.int32)


def _dispatch_plan(counts, n_tiles):
    padded = ((counts + MOE_TM - 1) // MOE_TM) * MOE_TM
    ends = jnp.cumsum(padded)
    offs = ends - padded
    n_act = jnp.maximum(ends[-1] // MOE_TM, 1)
    starts = jnp.arange(n_tiles, dtype=jnp.int32) * MOE_TM
    tile_bucket = jnp.minimum(jnp.sum(starts[:, None] >= ends[None, :], axis=1), N_BUCKETS - 1)
    ea, eb = _bucket_experts()
    onehot = tile_bucket[:, None] == jnp.arange(N_BUCKETS)[None, :]
    tile_a = jnp.sum(jnp.where(onehot, ea[None, :], 0), axis=1).astype(jnp.int32)
    tile_b = jnp.sum(jnp.where(onehot, eb[None, :], 0), axis=1).astype(jnp.int32)
    fill_tile = jnp.maximum(ends // MOE_TM - 1, 0).astype(jnp.int32)
    return offs, tile_a, tile_b, n_act.astype(jnp.int32).reshape(1), fill_tile


def _positions(offs, bucket, rank):
    onehot = bucket[..., None] == jnp.arange(N_BUCKETS, dtype=jnp.int32)
    return (jnp.sum(jnp.where(onehot, offs.astype(jnp.int32), 0), axis=-1) + rank).astype(jnp.int32)


def kernel(x_prompt, x_sample, cache_k, cache_v, state_conv, meta_tokens, norm_mix, w_in, conv_w, q_norm, k_norm,
           attn_sinks, w_conv_out, w_attn_out, w_o, norm_ffn, w_router_group, b_router_group, w_router_expert,
           b_router_expert, w_exp_gate, w_exp_up, w_exp_down):
    batch, seq, _ = x_prompt.shape
    depth = w_in.shape[0]
    n_dec = x_sample.shape[0]
    past_len = PAST_LEN
    lp = PAD + N_META + seq
    tm_in, tm_out, qb = 640, 640, 13
    tm_move, tm_last = 3328, 4096
    assert PAD + N_META == BLOCK and seq % BLOCK == 0 and tm_in % BLOCK == 0
    assert lp % tm_in == 0 and (batch * lp) % tm_out == 0 and lp % (qb * BLOCK) == 0
    assert (batch * lp) % tm_move == 0 and (batch * seq) % tm_last == 0
    assert x_sample.shape[1] == 1 and cache_k.shape[2] == WINDOW and past_len >= WINDOW

    t_prompt = batch * lp
    t_all = t_prompt + n_dec
    n_tiles = -(-(t_all + N_BUCKETS * (MOE_TM - 1)) // MOE_TM)
    n_tiles = -(-n_tiles // MOE_STEP_TILES) * MOE_STEP_TILES

    meta = jnp.broadcast_to(meta_tokens[None].astype(F32), (batch, N_META, D_MODEL))
    head = jnp.concatenate([jnp.zeros((batch, PAD, D_MODEL), F32), meta], axis=1)
    xs = x_sample.reshape(n_dec, D_MODEL)

    rope_p = _rope_tables_padded(lp // BLOCK)
    rope_s = _rope_tables(jnp.full((1,), past_len, jnp.int32))
    s256, s128 = _seg_ones(256), _seg_ones(LANES)
    tri_p = jnp.asarray(np.triu(np.ones((tm_out, tm_out)), 1), BF16)
    tri_s = jnp.asarray(np.triu(np.ones((n_dec, n_dec)), 1), BF16)
    zero_cnt = jnp.zeros((LANES, LANES), F32)

    w_mix = w_in.astype(BF16)
    w_gate = w_in[:, :, D_MIX:].astype(BF16)
    wco, wao, wo = w_conv_out.astype(BF16), w_attn_out.astype(BF16), w_o.astype(BF16)
    wg32 = w_exp_gate.reshape(depth * N_EXPERTS, D_MODEL, D_EXPERT)
    wu32 = w_exp_up.reshape(depth * N_EXPERTS, D_MODEL, D_EXPERT)
    wd32 = w_exp_down.reshape(depth * N_EXPERTS, D_EXPERT, D_MODEL)
    nm, nf = norm_mix.reshape(depth, 1, D_MODEL), norm_ffn.reshape(depth, 1, D_MODEL)
    qn = (jnp.tile(q_norm, (1, N_HEADS)) * Q_SCALE).reshape(depth, 1, D_Q)
    kn = jnp.tile(k_norm, (1, N_KV_HEADS)).reshape(depth, 1, D_KV)
    r_pad = LANES - N_GROUPS - N_EXPERTS
    wr = jnp.concatenate([w_router_group, w_router_expert, jnp.zeros((depth, D_MODEL, r_pad), F32)], axis=-1).astype(BF16)
    br = jnp.concatenate([b_router_group, b_router_expert, jnp.zeros((depth, r_pad), F32)], axis=-1).reshape(depth, 1, LANES)
    sinks = attn_sinks.astype(F32) * LOG2E
    sinkb = jnp.broadcast_to(sinks[:, :, None], (depth, N_HEADS, LANES))
    ck = cache_k.reshape(depth, n_dec, WINDOW, D_KV)
    cv = cache_v.reshape(depth, n_dec, WINDOW, D_KV)

    outs = {k: [] for k in ("kp", "vp", "cp", "ks", "vs", "cs")}
    for l in range(depth):
        if l == 0:
            cy, q, k, v, ulast, xp = _prompt_in(l, x_prompt, nm, w_mix, conv_w, qn, kn, rope_p, s256, s128, tm_in, head)
        else:
            cy, q, k, v, ulast = _prompt_in(l, xp, nm, w_mix, conv_w, qn, kn, rope_p, s256, s128, tm_in)
        ao = _prompt_attn(l, q, k, v, sinks, qb)
        xep, rankp, bktp, cnt = _mix_out(
            l, xp.reshape(t_prompt, D_MODEL), cy.reshape(t_prompt, D_CONV), ao.reshape(t_prompt, D_Q),
            nm, w_gate, wco, wao, wo, nf, wr, br, tri_p, zero_cnt, tm_out)
        outs["kp"].append(k[:, lp - WINDOW:].reshape(batch, WINDOW, N_KV_HEADS, HEAD_DIM))
        outs["vp"].append(v[:, lp - WINDOW:].reshape(batch, WINDOW, N_KV_HEADS, HEAD_DIM))
        outs["cp"].append(ulast[:, 8 - (CONV_W - 1):])

        c0, c1 = state_conv[l, :, 0, :], state_conv[l, :, 1, :]
        cys, qx, ksn, vsn, us = _sample_in(l, xs, c0, c1, nm, w_mix, conv_w, qn, kn, rope_s, s256, s128)
        ox, nk, nv = _sample_attn(l, jnp.transpose(qx, (1, 0, 2)), sinkb, ck, cv, ksn, vsn)
        ox = ox.reshape(n_dec, N_KV_HEADS, GQA, N_KV_HEADS, HEAD_DIM)
        aos = jnp.stack([ox[:, j, :, j, :] for j in range(N_KV_HEADS)], axis=1).reshape(n_dec, D_Q).astype(BF16)
        xes, ranks, bkts, cnt = _mix_out(l, xs, cys, aos, nm, w_gate, wco, wao, wo, nf, wr, br, tri_s, cnt, n_dec)
        outs["ks"].append(nk.reshape(n_dec, WINDOW, N_KV_HEADS, HEAD_DIM))
        outs["vs"].append(nv.reshape(n_dec, WINDOW, N_KV_HEADS, HEAD_DIM))
        outs["cs"].append(jnp.stack([c1, us], axis=1))

        counts = cnt[:N_BUCKETS, 0].astype(jnp.int32)
        offs, tile_a, tile_b, n_act, fill_tile = _dispatch_plan(counts, n_tiles)
        posp = _positions(offs, bktp, rankp)
        poss = _positions(offs, bkts, ranks)
        by8 = lambda a: a.reshape(a.shape[0] // SUB, SUB, a.shape[1])
        sorted_rows, wg, wu, wd = _scatter_rows(l, fill_tile, n_act, posp.reshape(N_EXPERTS, 1, t_prompt // N_EXPERTS),
                                                by8(xep), n_tiles, wg32, wu32, wd32)
        posp = posp.reshape(t_prompt // tm_move, 1, tm_move)
        sorted_rows = _scatter_more_rows(poss, by8(xes), sorted_rows, n_dec)
        ys = _moe_experts(l, tile_a, tile_b, n_act, sorted_rows, nf, wg, wu, wd)
        xs = _unpermute(poss, ys, n_dec).reshape(n_dec, D_MODEL)
        if l + 1 < depth:
            xp = _unpermute(posp, ys, tm_move).reshape(batch, lp, D_MODEL)
        else:
            pos_tok = posp.reshape(batch, lp)[:, PAD + N_META:].reshape(batch * seq // tm_last, 1, tm_last)
            y_prompt = _unpermute(pos_tok, ys, tm_last).reshape(batch, seq, D_MODEL)

    y_sample = xs.reshape(n_dec, 1, D_MODEL)
    st = lambda k: jnp.stack(outs[k])
    return (y_prompt, y_sample, st("kp"), st("vp"), st("cp"), st("ks"), st("vs"), st("cs"))
```
